```python
import jax, jax.numpy as jnp
from jax import lax
import numpy as np

D_MODEL = 1024
BATCH = 8
SEQ = 2048
DEPTH = 2

CHUNK = 64
N_PREV_CHUNKS = 8
BAND = (N_PREV_CHUNKS + 1) * CHUNK
BAND_PAD = N_PREV_CHUNKS * CHUNK
D_CONV = D_MODEL // 2
D_ATT = D_MODEL - D_CONV
D_MIX = D_CONV + D_ATT
HEAD_DIM = 64
N_HEADS = D_ATT // HEAD_DIM
CONV_WIDTH = 31
MAX_REL = 128
N_REL = 2 * MAX_REL + 1
D_IN_COLS = 2 * D_CONV + 3 * D_ATT
D_FF_DENSE = 2816
N_EXPERTS = 8
TOP_K = 2
D_FF_EXPERT = 3584
N_DENSE = (DEPTH + 1) // 2
N_MOE = DEPTH // 2
EPS = 1e-6
NEG_INF = -1e30

kernel_name = "hybrid_conformer_chunkattn_moe_adaln"


def rmsnorm(x, g):
    xf = x.astype(jnp.float32)
    y = xf * lax.rsqrt(jnp.mean(xf * xf, axis=-1, keepdims=True) + EPS)
    return (y * g.astype(jnp.float32)).astype(x.dtype)


def layernorm(x, g, b):
    xf = x.astype(jnp.float32)
    mu = jnp.mean(xf, axis=-1, keepdims=True)
    xc = xf - mu
    var = jnp.mean(xc * xc, axis=-1, keepdims=True)
    y = xc * lax.rsqrt(var + EPS) * g.astype(jnp.float32) + b.astype(jnp.float32)
    return y.astype(x.dtype)


def modulate(h, shift, scale):
    return h * (1 + scale[:, None, :]) + shift[:, None, :]


def conformer_conv(u, conv_w, conv_b, ln_g, ln_b):
    a, g = jnp.split(u, 2, axis=-1)
    z = a * jax.nn.sigmoid(g)
    z = lax.conv_general_dilated(
        z, conv_w, window_strides=(1,), padding=[(CONV_WIDTH - 1, 0)],
        dimension_numbers=("NWC", "WIO", "NWC"),
        feature_group_count=D_CONV) + conv_b
    z = layernorm(z, ln_g, ln_b)
    return jax.nn.silu(z)


def chunk_band_attention(q, k, v, q_norm_g, k_norm_g, rel_bias):
    B, S = q.shape[0], q.shape[1]
    nc = S // CHUNK
    q = rmsnorm(q.reshape(B, S, N_HEADS, HEAD_DIM), q_norm_g)
    k = rmsnorm(k.reshape(B, S, N_HEADS, HEAD_DIM), k_norm_g)
    v = v.reshape(B, S, N_HEADS, HEAD_DIM)
    pad = ((0, 0), (BAND_PAD, 0), (0, 0), (0, 0))
    kp = jnp.pad(k, pad)
    vp = jnp.pad(v, pad)
    idx = jnp.arange(nc)[:, None] * CHUNK + jnp.arange(BAND)[None, :]
    kb = kp[:, idx]
    vb = vp[:, idx]
    qc = q.reshape(B, nc, CHUNK, N_HEADS, HEAD_DIM)
    s = jnp.einsum("bnqhd,bnkhd->bnhqk", qc, kb).astype(jnp.float32) * (HEAD_DIM ** -0.5)
    rel = jnp.arange(BAND)[None, :] - BAND_PAD - jnp.arange(CHUNK)[:, None]
    rel_idx = jnp.clip(rel, -MAX_REL, MAX_REL) + MAX_REL
    bias = rel_bias.astype(jnp.float32)[:, rel_idx]
    valid = idx >= BAND_PAD
    s = jnp.where(valid[None, :, None, None, :], s + bias[None, None], NEG_INF)
    p = jax.nn.softmax(s, axis=-1).astype(v.dtype)
    o = jnp.einsum("bnhqk,bnkhd->bnqhd", p, vb)
    return o.reshape(B, S, D_ATT)


def swiglu(h, w_gate, w_up, w_down):
    return (jax.nn.silu(h @ w_gate) * (h @ w_up)) @ w_down


def moe_swiglu(h, w_router, b_router, w_gate, w_up, w_down):
    B, S, D = h.shape
    t = h.reshape(B * S, D)
    logits = (t @ w_router).astype(jnp.float32) + b_router.astype(jnp.float32)
    top_val, top_idx = lax.top_k(logits, TOP_K)
    top_w = jax.nn.softmax(top_val, axis=-1)
    combine = jnp.sum(jax.nn.one_hot(top_idx, N_EXPERTS, dtype=jnp.float32)
                      * top_w[..., None], axis=1).astype(t.dtype)
    out = jnp.zeros_like(t)
    for e in range(N_EXPERTS):
        out = out + combine[:, e:e + 1] * swiglu(t, w_gate[e], w_up[e], w_down[e])
    return out.reshape(B, S, D)


def setup_inputs(seed: int = 0) -> dict:
    key = jax.random.key(seed)
    ks = jax.random.split(key, 24)
    f32 = jnp.float32

    def nrm(k, shape, scale):
        return jax.random.normal(k, shape, f32) * scale

    return {
        "x": nrm(ks[0], (BATCH, SEQ, D_MODEL), 1.0),
        "c": nrm(ks[1], (BATCH, D_MODEL), 1.0),
        "w_ada": nrm(ks[2], (DEPTH, D_MODEL, 6 * D_MODEL), 0.5 * D_MODEL ** -0.5),
        "b_ada": nrm(ks[3], (DEPTH, 6 * D_MODEL), 0.02),
        "norm_mix_g": 1.0 + nrm(ks[4], (DEPTH, D_MODEL), 0.02),
        "norm_ffn_g": 1.0 + nrm(ks[5], (DEPTH, D_MODEL), 0.02),
        "w_in": nrm(ks[6], (DEPTH, D_MODEL, D_IN_COLS), D_MODEL ** -0.5),
        "w_out": nrm(ks[7], (DEPTH, D_MIX, D_MODEL), D_MIX ** -0.5),
        "conv_w": nrm(ks[8], (DEPTH, CONV_WIDTH, 1, D_CONV), CONV_WIDTH ** -0.5),
        "conv_b": nrm(ks[9], (DEPTH, D_CONV), 0.02),
        "conv_ln_g": 1.0 + nrm(ks[10], (DEPTH, D_CONV), 0.02),
        "conv_ln_b": nrm(ks[11], (DEPTH, D_CONV), 0.02),
        "q_norm_g": 1.0 + nrm(ks[12], (DEPTH, HEAD_DIM), 0.02),
        "k_norm_g": 1.0 + nrm(ks[13], (DEPTH, HEAD_DIM), 0.02),
        "rel_bias": nrm(ks[14], (DEPTH, N_HEADS, N_REL), 0.5),
        "ffn_w_gate": nrm(ks[15], (N_DENSE, D_MODEL, D_FF_DENSE), D_MODEL ** -0.5),
        "ffn_w_up": nrm(ks[16], (N_DENSE, D_MODEL, D_FF_DENSE), D_MODEL ** -0.5),
        "ffn_w_down": nrm(ks[17], (N_DENSE, D_FF_DENSE, D_MODEL), D_FF_DENSE ** -0.5),
        "moe_w_router": nrm(ks[18], (N_MOE, D_MODEL, N_EXPERTS), D_MODEL ** -0.5),
        "moe_b_router": nrm(ks[19], (N_MOE, N_EXPERTS), 0.01),
        "moe_w_gate": nrm(ks[20], (N_MOE, N_EXPERTS, D_MODEL, D_FF_EXPERT), D_MODEL ** -0.5),
        "moe_w_up": nrm(ks[21], (N_MOE, N_EXPERTS, D_MODEL, D_FF_EXPERT), D_MODEL ** -0.5),
        "moe_w_down": nrm(ks[22], (N_MOE, N_EXPERTS, D_FF_EXPERT, D_MODEL), D_FF_EXPERT ** -0.5),
    }


def reference(x, c, w_ada, b_ada, norm_mix_g, norm_ffn_g, w_in, w_out,
              conv_w, conv_b, conv_ln_g, conv_ln_b, q_norm_g, k_norm_g, rel_bias,
              ffn_w_gate, ffn_w_up, ffn_w_down,
              moe_w_router, moe_b_router, moe_w_gate, moe_w_up, moe_w_down):
    c_act = jax.nn.silu(c)
    for l in range(DEPTH):
        mod = c_act @ w_ada[l] + b_ada[l]
        sh1, sc1, g1, sh2, sc2, g2 = jnp.split(mod, 6, axis=-1)

        h = modulate(rmsnorm(x, norm_mix_g[l]), sh1, sc1)
        proj = h @ w_in[l]
        u_conv = proj[..., :2 * D_CONV]
        q = proj[..., 2 * D_CONV:2 * D_CONV + D_ATT]
        k = proj[..., 2 * D_CONV + D_ATT:2 * D_CONV + 2 * D_ATT]
        v = proj[..., 2 * D_CONV + 2 * D_ATT:]
        y_conv = conformer_conv(u_conv, conv_w[l], conv_b[l], conv_ln_g[l], conv_ln_b[l])
        y_att = chunk_band_attention(q, k, v, q_norm_g[l], k_norm_g[l], rel_bias[l])
        y = jnp.concatenate([y_conv, y_att], axis=-1) @ w_out[l]
        x = x + g1[:, None, :] * y

        h = modulate(rmsnorm(x, norm_ffn_g[l]), sh2, sc2)
        if l % 2 == 0:
            i = l // 2
            f = swiglu(h, ffn_w_gate[i], ffn_w_up[i], ffn_w_down[i])
        else:
            i = l // 2
            f = moe_swiglu(h, moe_w_router[i], moe_b_router[i],
                           moe_w_gate[i], moe_w_up[i], moe_w_down[i])
        x = x + g2[:, None, :] * f
    return x
```

```python
import functools

import jax
import jax.numpy as jnp
from jax import lax
from jax.experimental import pallas as pl
from jax.experimental.pallas import tpu as pltpu

F32 = jnp.float32
BF16 = jnp.bfloat16

D_MODEL = 1024
CHUNK = 64
N_PREV_CHUNKS = 8
BAND_PAD = N_PREV_CHUNKS * CHUNK
D_CONV = 512
D_ATT = 512
HEAD_DIM = 64
N_HEADS = 8
CONV_WIDTH = 31
MAX_REL = 128
D_IN_COLS = 2 * D_CONV + 3 * D_ATT
N_EXPERTS = 8
EPS = 1e-6
NEG_INF = -1e30

LANES = 128
VMEM_LIMIT_BYTES = 56 * 1024 * 1024

ATT_HEADS = 4
ATT_GROUPS = N_HEADS // ATT_HEADS
ATT_W = ATT_HEADS * HEAD_DIM
ATT_Q = 2 * CHUNK
ATT_BAND = BAND_PAD + ATT_Q
ATT_L = ATT_HEADS * ATT_Q

CONV_HALO = 32
LANE_PAD_E = LANES

MOE_SUB = 256
MOE_TILE = 2048
MOE_FC = 512


def _cparams(n_axes, vmem=VMEM_LIMIT_BYTES):
    return pltpu.CompilerParams(
        dimension_semantics=("arbitrary",) * n_axes, vmem_limit_bytes=vmem)


def _silu(v):
    return v * jax.nn.sigmoid(v)


def _ada_kernel(c_ref, w_ref, b_ref, o_ref):
    ca = _silu(c_ref[...]).astype(BF16)
    w = w_ref[0].astype(BF16)
    o_ref[0] = jnp.dot(ca, w, preferred_element_type=F32) + b_ref[0]


def _ada_mod(c, w_ada, b_ada):
    depth, d, n6 = w_ada.shape
    b = c.shape[0]
    rows = 16
    c_pad = jnp.zeros((rows, d), F32).at[:b].set(c)
    tn = 1536
    out = pl.pallas_call(
        _ada_kernel,
        out_shape=jax.ShapeDtypeStruct((depth, rows, n6), F32),
        grid=(depth, n6 // tn),
        in_specs=[
            pl.BlockSpec((rows, d), lambda l, j: (0, 0)),
            pl.BlockSpec((1, d, tn), lambda l, j: (l, 0, j)),
            pl.BlockSpec((1, 1, tn), lambda l, j: (l, 0, j)),
        ],
        out_specs=pl.BlockSpec((1, rows, tn), lambda l, j: (l, 0, j)),
        compiler_params=_cparams(2),
        name="ada_mod",
    )(c_pad, w_ada, b_ada.reshape(depth, 1, n6))
    return out[:, :b]


def _rms_mod(xf, g, sc, sh):
    ms = jnp.mean(xf * xf, axis=-1, keepdims=True)
    return xf * lax.rsqrt(ms + EPS) * g * (1.0 + sc) + sh


def _mix_in_kernel(x_ref, sc_ref, sh_ref, g_ref, w_ref, gq_ref, gk_ref, ones_ref,
                   z_ref, q_ref, k_ref, vt_ref, wbf_ref):
    first = jnp.logical_and(pl.program_id(0) == 0, pl.program_id(1) == 0)

    @pl.when(first)
    def _():
        wbf_ref[...] = w_ref[...].astype(BF16)

    h = _rms_mod(x_ref[0], g_ref[...], sc_ref[0], sh_ref[0]).astype(BF16)
    proj = jnp.dot(h, wbf_ref[...], preferred_element_type=F32)

    a = proj[:, :D_CONV]
    gate = proj[:, D_CONV:2 * D_CONV]
    z_ref[0] = (a * jax.nn.sigmoid(gate)).astype(BF16)

    def head_norm(t, g):
        ss = jnp.dot((t * t).astype(BF16), ones_ref[...], preferred_element_type=F32)
        return (t * lax.rsqrt(ss * (1.0 / HEAD_DIM) + EPS) * g).astype(BF16)

    o = 2 * D_CONV
    q_ref[0] = head_norm(proj[:, o:o + D_ATT], gq_ref[...])
    k_ref[0] = head_norm(proj[:, o + D_ATT:o + 2 * D_ATT], gk_ref[...])
    v = proj[:, o + 2 * D_ATT:]
    tm = v.shape[0]
    for cidx in range(tm // ATT_Q):
        vt_ref[0, cidx] = v[cidx * ATT_Q:(cidx + 1) * ATT_Q, :].T.astype(BF16)


def _mix_in(x, sc, sh, g, w_in, gq, gk, tm):
    b, s, d = x.shape
    ones_bd = (jnp.arange(D_ATT)[:, None] // HEAD_DIM
               == jnp.arange(D_ATT)[None, :] // HEAD_DIM).astype(BF16)
    gq_t = (jnp.tile(gq, N_HEADS) * (HEAD_DIM ** -0.5)).reshape(1, D_ATT)
    gk_t = jnp.tile(gk, N_HEADS).reshape(1, D_ATT)
    row = lambda bi, ti: (bi, ti, 0)
    per_b = lambda bi, ti: (bi, 0, 0)
    const2 = lambda bi, ti: (0, 0)
    return pl.pallas_call(
        _mix_in_kernel,
        out_shape=(
            jax.ShapeDtypeStruct((b, s, D_CONV), BF16),
            jax.ShapeDtypeStruct((b, s, D_ATT), BF16),
            jax.ShapeDtypeStruct((b, s, D_ATT), BF16),
            jax.ShapeDtypeStruct((b, s // ATT_Q, D_ATT, ATT_Q), BF16),
        ),
        grid=(b, s // tm),
        in_specs=[
            pl.BlockSpec((1, tm, d), row),
            pl.BlockSpec((1, 1, d), per_b),
            pl.BlockSpec((1, 1, d), per_b),
            pl.BlockSpec((1, d), const2),
            pl.BlockSpec((d, D_IN_COLS), const2),
            pl.BlockSpec((1, D_ATT), const2),
            pl.BlockSpec((1, D_ATT), const2),
            pl.BlockSpec((D_ATT, D_ATT), const2),
        ],
        out_specs=(
            pl.BlockSpec((1, tm, D_CONV), row),
            pl.BlockSpec((1, tm, D_ATT), row),
            pl.BlockSpec((1, tm, D_ATT), row),
            pl.BlockSpec((1, tm // ATT_Q, D_ATT, ATT_Q), lambda bi, ti: (bi, ti, 0, 0)),
        ),
        scratch_shapes=[pltpu.VMEM((d, D_IN_COLS), BF16)],
        compiler_params=_cparams(2),
        name="mix_in",
    )(x, sc, sh, g.reshape(1, d), w_in, gq_t, gk_t, ones_bd)


def _conv_kernel(zc_ref, zp_ref, w_ref, cb_ref, lg_ref, lb_ref, o_ref, win_ref):
    tt = zc_ref.shape[1]
    t = pl.program_id(1)
    halo = zp_ref[0].astype(F32)
    win_ref[0:CONV_HALO, :] = jnp.where(t == 0, 0.0, halo)
    win_ref[CONV_HALO:, :] = zc_ref[0].astype(F32)
    acc = jnp.zeros((tt, D_CONV), F32) + cb_ref[...]
    base = CONV_HALO - (CONV_WIDTH - 1)
    for j in range(CONV_WIDTH):
        acc = acc + win_ref[base + j:base + j + tt, :] * w_ref[j:j + 1, :]
    mu = jnp.mean(acc, axis=-1, keepdims=True)
    xc = acc - mu
    var = jnp.mean(xc * xc, axis=-1, keepdims=True)
    y = xc * lax.rsqrt(var + EPS) * lg_ref[...] + lb_ref[...]
    o_ref[0] = _silu(y).astype(BF16)


def _conv_branch(z, conv_w, conv_b, ln_g, ln_b, tt):
    b, s, c = z.shape
    hb = tt // CONV_HALO
    const2 = lambda bi, ti: (0, 0)
    return pl.pallas_call(
        _conv_kernel,
        out_shape=jax.ShapeDtypeStruct((b, s, c), BF16),
        grid=(b, s // tt),
        in_specs=[
            pl.BlockSpec((1, tt, c), lambda bi, ti: (bi, ti, 0)),
            pl.BlockSpec((1, CONV_HALO, c),
                         lambda bi, ti: (bi, jnp.maximum(ti * hb - 1, 0), 0)),
            pl.BlockSpec((CONV_WIDTH, c), const2),
            pl.BlockSpec((1, c), const2),
            pl.BlockSpec((1, c), const2),
            pl.BlockSpec((1, c), const2),
        ],
        out_specs=pl.BlockSpec((1, tt, c), lambda bi, ti: (bi, ti, 0)),
        scratch_shapes=[pltpu.VMEM((tt + CONV_HALO, c), F32)],
        compiler_params=_cparams(2),
        name="conv_branch",
    )(z, z, conv_w.reshape(CONV_WIDTH, c), conv_b.reshape(1, c),
      ln_g.reshape(1, c), ln_b.reshape(1, c))


def _attn_kernel(q_ref, k_ref, vt_ref, bias_ref, o_ref, kpad_ref, vtpad_ref):
    s = q_ref.shape[1]
    npad = BAND_PAD // ATT_Q
    kpad_ref[0:BAND_PAD, :] = jnp.zeros((BAND_PAD, ATT_W), BF16)
    kpad_ref[BAND_PAD:, :] = k_ref[0]
    vtpad_ref[0:npad] = jnp.zeros((npad, ATT_W, ATT_Q), BF16)
    vtpad_ref[npad:] = vt_ref[0]

    iota = lambda shape, dim: lax.broadcasted_iota(jnp.int32, shape, dim)
    q_shift = ATT_Q.bit_length() - 1
    d_shift = HEAD_DIM.bit_length() - 1
    qb_mask = (iota((ATT_L, ATT_W), 0) >> q_shift) == (iota((ATT_L, ATT_W), 1) >> d_shift)
    ot_mask = (iota((ATT_W, ATT_L), 0) >> d_shift) == (iota((ATT_W, ATT_L), 1) >> q_shift)
    sel = jnp.where((iota((ATT_Q, ATT_L), 1) & (ATT_Q - 1)) == iota((ATT_Q, ATT_L), 0),
                    1.0, 0.0).astype(BF16)
    key_row = lax.broadcasted_iota(jnp.int32, (ATT_BAND, ATT_L), 0)
    contract_last = (((1,), (1,)), ((), ()))

    def step(m, carry):
        r0 = pl.multiple_of(m * ATT_Q, ATT_Q)
        qt = q_ref[0, pl.ds(r0, ATT_Q), :]
        qb = jnp.where(qb_mask, jnp.concatenate([qt] * ATT_HEADS, axis=0), 0)
        kb = kpad_ref[pl.ds(r0, ATT_BAND), :]
        st = lax.dot_general(kb, qb.astype(BF16), contract_last,
                             preferred_element_type=F32)
        st = st + bias_ref[0]
        st = jnp.where(key_row >= BAND_PAD - r0, st, NEG_INF)
        mx = jnp.max(st, axis=0, keepdims=True)
        p = jnp.exp(st - mx)
        den = jnp.sum(p, axis=0, keepdims=True)
        pb = p.astype(BF16)
        vb = jnp.concatenate([vtpad_ref[m + c] for c in range(ATT_BAND // ATT_Q)],
                             axis=1)
        ot = jnp.dot(vb, pb, preferred_element_type=F32)
        ot = jnp.where(ot_mask, ot / den, 0.0).astype(BF16)
        y = lax.dot_general(sel, ot, contract_last, preferred_element_type=F32)
        o_ref[0, pl.ds(r0, ATT_Q), :] = y.astype(BF16)
        return carry

    lax.fori_loop(0, s // ATT_Q, step, 0)


def _attn_bias_t(rel_bias):
    r = jnp.arange(ATT_BAND)[:, None]
    lane = jnp.arange(ATT_L)[None, :]
    hh = lane // ATT_Q
    qq = lane % ATT_Q
    rel = (r - BAND_PAD) - qq
    idx = jnp.clip(rel, -MAX_REL, MAX_REL) + MAX_REL
    first = (qq // CHUNK) * CHUNK
    valid = (r >= first) & (r < first + BAND_PAD + CHUNK)
    rb = rel_bias.astype(F32).reshape(ATT_GROUPS, ATT_HEADS, -1)
    bias = rb[:, hh, idx]
    return jnp.where(valid[None], bias, NEG_INF)


def _attention(qn, kn, vt, rel_bias):
    b, s, _ = qn.shape
    bias_t = _attn_bias_t(rel_bias)
    nck = s // ATT_Q
    return pl.pallas_call(
        _attn_kernel,
        out_shape=jax.ShapeDtypeStruct((b, s, D_ATT), BF16),
        grid=(b, ATT_GROUPS),
        in_specs=[
            pl.BlockSpec((1, s, ATT_W), lambda bi, gi: (bi, 0, gi)),
            pl.BlockSpec((1, s, ATT_W), lambda bi, gi: (bi, 0, gi)),
            pl.BlockSpec((1, nck, ATT_W, ATT_Q), lambda bi, gi: (bi, 0, gi, 0)),
            pl.BlockSpec((1, ATT_BAND, ATT_L), lambda bi, gi: (gi, 0, 0)),
        ],
        out_specs=pl.BlockSpec((1, s, ATT_W), lambda bi, gi: (bi, 0, gi)),
        scratch_shapes=[
            pltpu.VMEM((s + BAND_PAD, ATT_W), BF16),
            pltpu.VMEM((nck + BAND_PAD // ATT_Q, ATT_W, ATT_Q), BF16),
        ],
        compiler_params=_cparams(2),
        name="band_attention",
    )(qn, kn, vt, bias_t)


def _mix_out_kernel(x_ref, yc_ref, ya_ref, w_ref, g1_ref, gf_ref, sc_ref, sh_ref,
                    x1_ref, h_ref, wbf_ref):
    first = jnp.logical_and(pl.program_id(0) == 0, pl.program_id(1) == 0)

    @pl.when(first)
    def _():
        wbf_ref[...] = w_ref[...].astype(BF16)

    y = jnp.dot(yc_ref[0], wbf_ref[0:D_CONV, :], preferred_element_type=F32)
    y = y + jnp.dot(ya_ref[0], wbf_ref[D_CONV:, :], preferred_element_type=F32)
    x1 = x_ref[0] + g1_ref[0] * y
    x1_ref[0] = x1
    h_ref[0] = _rms_mod(x1, gf_ref[...], sc_ref[0], sh_ref[0]).astype(BF16)


def _mix_out(x, yc, ya, w_out, g1, gf, sc, sh, tm):
    b, s, d = x.shape
    row = lambda bi, ti: (bi, ti, 0)
    per_b = lambda bi, ti: (bi, 0, 0)
    const2 = lambda bi, ti: (0, 0)
    return pl.pallas_call(
        _mix_out_kernel,
        out_shape=(jax.ShapeDtypeStruct((b, s, d), F32),
                   jax.ShapeDtypeStruct((b, s, d), BF16)),
        grid=(b, s // tm),
        in_specs=[
            pl.BlockSpec((1, tm, d), row),
            pl.BlockSpec((1, tm, D_CONV), row),
            pl.BlockSpec((1, tm, D_ATT), row),
            pl.BlockSpec((d, d), const2),
            pl.BlockSpec((1, 1, d), per_b),
            pl.BlockSpec((1, d), const2),
            pl.BlockSpec((1, 1, d), per_b),
            pl.BlockSpec((1, 1, d), per_b),
        ],
        out_specs=(pl.BlockSpec((1, tm, d), row), pl.BlockSpec((1, tm, d), row)),
        scratch_shapes=[pltpu.VMEM((d, d), BF16)],
        compiler_params=_cparams(2),
        name="mix_out",
    )(x, yc, ya, w_out, g1, gf.reshape(1, d), sc, sh)


def _ffn_kernel(h_ref, x_ref, g2_ref, wg_ref, wu_ref, wd_ref, o_ref, acc_ref):
    f = pl.program_id(2)
    h = h_ref[0]
    a = jnp.dot(h, wg_ref[...].astype(BF16), preferred_element_type=F32)
    u = jnp.dot(h, wu_ref[...].astype(BF16), preferred_element_type=F32)
    t = (_silu(a) * u).astype(BF16)
    dn = jnp.dot(t, wd_ref[...].astype(BF16), preferred_element_type=F32)

    @pl.when(f == 0)
    def _():
        acc_ref[...] = dn

    @pl.when(f > 0)
    def _():
        acc_ref[...] += dn

    @pl.when(f == pl.num_programs(2) - 1)
    def _():
        o_ref[0] = x_ref[0] + g2_ref[0] * acc_ref[...]


def _dense_ffn(h, x1, g2, wg, wu, wd, tm, fc):
    b, s, d = x1.shape
    ff = wg.shape[1]
    row = lambda bi, ti, fi: (bi, ti, 0)
    return pl.pallas_call(
        _ffn_kernel,
        out_shape=jax.ShapeDtypeStruct((b, s, d), F32),
        grid=(b, s // tm, ff // fc),
        in_specs=[
            pl.BlockSpec((1, tm, d), row),
            pl.BlockSpec((1, tm, d), row),
            pl.BlockSpec((1, 1, d), lambda bi, ti, fi: (bi, 0, 0)),
            pl.BlockSpec((d, fc), lambda bi, ti, fi: (0, fi)),
            pl.BlockSpec((d, fc), lambda bi, ti, fi: (0, fi)),
            pl.BlockSpec((fc, d), lambda bi, ti, fi: (fi, 0)),
        ],
        out_specs=pl.BlockSpec((1, tm, d), row),
        scratch_shapes=[pltpu.VMEM((tm, d), F32)],
        compiler_params=_cparams(3),
        name="dense_ffn",
    )(h, x1, g2, wg, wu, wd)


def _router_kernel(h_ref, w_ref, b_ref, info_ref, cnt_ref, carry_ref):
    i = pl.program_id(0)

    @pl.when(i == 0)
    def _():
        carry_ref[...] = jnp.zeros_like(carry_ref)

    tr = h_ref.shape[0]
    logits = jnp.dot(h_ref[...], w_ref[...], preferred_element_type=F32) + b_ref[...]
    lane = lax.broadcasted_iota(jnp.int32, (tr, LANE_PAD_E), 1).astype(F32)
    no_lane = float(LANE_PAD_E)
    v0 = jnp.max(logits, axis=-1, keepdims=True)
    i0 = jnp.min(jnp.where(logits == v0, lane, no_lane), axis=-1, keepdims=True)
    rest = jnp.where(lane == i0, -jnp.inf, logits)
    v1 = jnp.max(rest, axis=-1, keepdims=True)
    i1 = jnp.min(jnp.where(rest == v1, lane, no_lane), axis=-1, keepdims=True)
    e1 = jnp.exp(v1 - v0)
    w0 = 1.0 / (1.0 + e1)
    w1 = e1 / (1.0 + e1)
    oh0 = lane == i0
    oh1 = lane == i1
    cnt = jnp.where(jnp.logical_or(oh0, oh1), 1.0, 0.0)
    tri = (lax.broadcasted_iota(jnp.int32, (tr, tr), 1)
           < lax.broadcasted_iota(jnp.int32, (tr, tr), 0)).astype(BF16)
    before = jnp.dot(tri, cnt.astype(BF16), preferred_element_type=F32) + carry_ref[0:1, :]
    r0 = jnp.sum(jnp.where(oh0, before, 0.0), axis=-1, keepdims=True)
    r1 = jnp.sum(jnp.where(oh1, before, 0.0), axis=-1, keepdims=True)
    carry_ref[...] = carry_ref[...] + jnp.sum(cnt, axis=0, keepdims=True)
    cnt_ref[...] = carry_ref[...]
    info = jnp.where(lane == 0, i0, 0.0)
    info = jnp.where(lane == 1, i1, info)
    info = jnp.where(lane == 2, w0, info)
    info = jnp.where(lane == 3, w1, info)
    info = jnp.where(lane == 4, r0, info)
    info = jnp.where(lane == 5, r1, info)
    info_ref[...] = info


def _router(h2, w_router, b_router, tr):
    n, d = h2.shape
    w_pad = jnp.zeros((d, LANE_PAD_E), BF16).at[:, :N_EXPERTS].set(w_router.astype(BF16))
    b_pad = jnp.full((1, LANE_PAD_E), -jnp.inf, F32).at[0, :N_EXPERTS].set(b_router.astype(F32))
    return pl.pallas_call(
        _router_kernel,
        out_shape=(jax.ShapeDtypeStruct((n, LANE_PAD_E), F32),
                   jax.ShapeDtypeStruct((8, LANE_PAD_E), F32)),
        grid=(n // tr,),
        in_specs=[
            pl.BlockSpec((tr, d), lambda i: (i, 0)),
            pl.BlockSpec((d, LANE_PAD_E), lambda i: (0, 0)),
            pl.BlockSpec((1, LANE_PAD_E), lambda i: (0, 0)),
        ],
        out_specs=(pl.BlockSpec((tr, LANE_PAD_E), lambda i: (i, 0)),
                   pl.BlockSpec((8, LANE_PAD_E), lambda i: (0, 0))),
        scratch_shapes=[pltpu.VMEM((8, LANE_PAD_E), F32)],
        compiler_params=_cparams(1),
        name="router",
    )(h2, w_pad, b_pad)


def _moe_kernel(te_ref, tb_ref, tn_ref, xs_ref, wg_ref, wu_ref, wd_ref, ys_ref,
                acc_ref, wgb_ref, wub_ref, wdb_ref):
    i = pl.program_id(0)
    f = pl.program_id(1)
    nsub = tn_ref[i]

    live = nsub > 0

    @pl.when(jnp.logical_and(live, f == 0))
    def _():
        acc_ref[...] = jnp.zeros_like(acc_ref)

    @pl.when(live)
    def _():
        wgb_ref[...] = wg_ref[0].astype(BF16)
        wub_ref[...] = wu_ref[0].astype(BF16)
        wdb_ref[...] = wd_ref[0].astype(BF16)

    def sub(sidx, carry):
        r = pl.multiple_of(sidx * MOE_SUB, MOE_SUB)
        xb = xs_ref[pl.ds(r, MOE_SUB), :]
        a = jnp.dot(xb, wgb_ref[...], preferred_element_type=F32)
        u = jnp.dot(xb, wub_ref[...], preferred_element_type=F32)
        t = (_silu(a) * u).astype(BF16)
        acc_ref[pl.ds(r, MOE_SUB), :] += jnp.dot(t, wdb_ref[...], preferred_element_type=F32)
        return carry

    lax.fori_loop(0, nsub, sub, 0)

    @pl.when(jnp.logical_and(live, f == pl.num_programs(1) - 1))
    def _():
        ys_ref[...] = acc_ref[...].astype(BF16)


def _moe_ffn(xs, tile_e, tile_blk, tile_nsub, wg, wu, wd):
    rpad, d = xs.shape
    ntiles = rpad // MOE_TILE
    ff = wg.shape[2]
    nf = ff // MOE_FC

    def fcol(i, f, tn):
        return jnp.where(tn[i] > 0, f, nf - 1)

    return pl.pallas_call(
        _moe_kernel,
        out_shape=jax.ShapeDtypeStruct((rpad, d), BF16),
        grid_spec=pltpu.PrefetchScalarGridSpec(
            num_scalar_prefetch=3,
            grid=(ntiles, nf),
            in_specs=[
                pl.BlockSpec((MOE_TILE, d), lambda i, f, te, tb, tn: (tb[i], 0)),
                pl.BlockSpec((1, d, MOE_FC), lambda i, f, te, tb, tn: (te[i], 0, fcol(i, f, tn))),
                pl.BlockSpec((1, d, MOE_FC), lambda i, f, te, tb, tn: (te[i], 0, fcol(i, f, tn))),
                pl.BlockSpec((1, MOE_FC, d), lambda i, f, te, tb, tn: (te[i], fcol(i, f, tn), 0)),
            ],
            out_specs=pl.BlockSpec((MOE_TILE, d), lambda i, f, te, tb, tn: (tb[i], 0)),
            scratch_shapes=[
                pltpu.VMEM((MOE_TILE, d), F32),
                pltpu.VMEM((d, MOE_FC), BF16),
                pltpu.VMEM((d, MOE_FC), BF16),
                pltpu.VMEM((MOE_FC, d), BF16),
            ],
        ),
        compiler_params=_cparams(2),
        name="moe_ffn",
    )(tile_e, tile_blk, tile_nsub, xs, wg, wu, wd)


def _combine_kernel(x_ref, y0_ref, y1_ref, info_ref, g2_ref, o_ref):
    info = info_ref[0]
    w0 = info[:, 2:3]
    w1 = info[:, 3:4]
    f = w0 * y0_ref[0].astype(F32) + w1 * y1_ref[0].astype(F32)
    o_ref[0] = x_ref[0] + g2_ref[0] * f


def _combine(x1, y0, y1, info, g2, tm):
    b, s, d = x1.shape
    row = lambda bi, ti: (bi, ti, 0)
    return pl.pallas_call(
        _combine_kernel,
        out_shape=jax.ShapeDtypeStruct((b, s, d), F32),
        grid=(b, s // tm),
        in_specs=[
            pl.BlockSpec((1, tm, d), row),
            pl.BlockSpec((1, tm, d), row),
            pl.BlockSpec((1, tm, d), row),
            pl.BlockSpec((1, tm, LANE_PAD_E), row),
            pl.BlockSpec((1, 1, d), lambda bi, ti: (bi, 0, 0)),
        ],
        out_specs=pl.BlockSpec((1, tm, d), row),
        compiler_params=_cparams(2),
        name="moe_combine",
    )(x1, y0, y1, info, g2)


def _moe_layer(h2, x1, g2, w_router, b_router, wg, wu, wd, tm):
    b, s, d = x1.shape
    n = b * s
    info, cnt = _router(h2.reshape(n, d), w_router, b_router, min(512, n))
    e0 = info[:, 0].astype(jnp.int32)
    e1 = info[:, 1].astype(jnp.int32)
    r0 = info[:, 4].astype(jnp.int32)
    r1 = info[:, 5].astype(jnp.int32)
    counts = cnt[0, :N_EXPERTS].astype(jnp.int32)

    ntiles = (2 * n) // MOE_TILE + N_EXPERTS
    tiles_per_e = (counts + MOE_TILE - 1) // MOE_TILE
    tile_end = jnp.cumsum(tiles_per_e)
    tile_start = tile_end - tiles_per_e
    total = tile_end[-1]
    tidx = jnp.arange(ntiles, dtype=jnp.int32)
    live = tidx < total
    tclip = jnp.minimum(tidx, total - 1)
    tile_e = jnp.minimum(jnp.sum(tclip[:, None] >= tile_end[None, :], axis=1),
                         N_EXPERTS - 1).astype(jnp.int32)
    rows_left = counts[tile_e] - (tclip - tile_start[tile_e]) * MOE_TILE
    rows_here = jnp.clip(rows_left, 0, MOE_TILE)
    tile_nsub = jnp.where(live, (rows_here + MOE_SUB - 1) // MOE_SUB, 0).astype(jnp.int32)
    tile_blk = tclip.astype(jnp.int32)

    row_start = tile_start * MOE_TILE
    pos0 = row_start[e0] + r0
    pos1 = row_start[e1] + r1

    hflat = h2.reshape(n, d)
    xs = jnp.zeros((ntiles * MOE_TILE, d), BF16).at[pos0].set(hflat).at[pos1].set(hflat)
    ys = _moe_ffn(xs, tile_e, tile_blk, tile_nsub, wg, wu, wd)
    y0 = ys[pos0].reshape(b, s, d)
    y1 = ys[pos1].reshape(b, s, d)
    return _combine(x1, y0, y1, info.reshape(b, s, LANE_PAD_E), g2, tm)


def kernel(x, c, w_ada, b_ada, norm_mix_g, norm_ffn_g, w_in, w_out, conv_w, conv_b,
           conv_ln_g, conv_ln_b, q_norm_g, k_norm_g, rel_bias, ffn_w_gate, ffn_w_up,
           ffn_w_down, moe_w_router, moe_b_router, moe_w_gate, moe_w_up, moe_w_down):
    b, s, d = x.shape
    depth = w_ada.shape[0]
    tm = min(512, s)
    mod = _ada_mod(c, w_ada, b_ada)
    for l in range(depth):
        sh1, sc1, g1, sh2, sc2, g2 = [
            mod[l, :, j * d:(j + 1) * d].reshape(b, 1, d) for j in range(6)]
        z, qn, kn, vt = _mix_in(x, sc1, sh1, norm_mix_g[l], w_in[l],
                                q_norm_g[l], k_norm_g[l], tm)
        yc = _conv_branch(z, conv_w[l], conv_b[l], conv_ln_g[l], conv_ln_b[l], min(256, s))
        ya = _attention(qn, kn, vt, rel_bias[l])
        x1, h2 = _mix_out(x, yc, ya, w_out[l], g1, norm_ffn_g[l], sc2, sh2, tm)
        i = l // 2
        if l % 2 == 0:
            x = _dense_ffn(h2, x1, g2, ffn_w_gate[i], ffn_w_up[i], ffn_w_down[i],
                           min(1024, s), 256)
        else:
            x = _moe_layer(h2, x1, g2, moe_w_router[i], moe_b_router[i],
                           moe_w_gate[i], moe_w_up[i], moe_w_down[i], tm)
    return x
```

```python
import functools

import jax
import jax.numpy as jnp
from jax import lax
from jax.experimental import pallas as pl
from jax.experimental.pallas import tpu as pltpu

F32 = jnp.float32
BF16 = jnp.bfloat16

D_MODEL = 1024
CHUNK = 64
N_PREV_CHUNKS = 8
BAND_PAD = N_PREV_CHUNKS * CHUNK
D_CONV = 512
D_ATT = 512
HEAD_DIM = 64
N_HEADS = 8
CONV_WIDTH = 31
MAX_REL = 128
D_IN_COLS = 2 * D_CONV + 3 * D_ATT
N_EXPERTS = 8
EPS = 1e-6
NEG_INF = -1e30

LANES = 128
VMEM_LIMIT_BYTES = 56 * 1024 * 1024

ATT_HEADS = 4
ATT_GROUPS = N_HEADS // ATT_HEADS
ATT_W = ATT_HEADS * HEAD_DIM
ATT_Q = 2 * CHUNK
ATT_BAND = BAND_PAD + ATT_Q
ATT_L = ATT_HEADS * ATT_Q

CONV_HALO = 32
LANE_PAD_E = LANES

MOE_SUB = 256
MOE_TILE = 2048
MOE_FC = 512


def _cparams(n_axes, vmem=VMEM_LIMIT_BYTES):
    return pltpu.CompilerParams(
        dimension_semantics=("arbitrary",) * n_axes, vmem_limit_bytes=vmem)


def _silu(v):
    return v * jax.nn.sigmoid(v)


def _ada_kernel(c_ref, w_ref, b_ref, o_ref):
    ca = _silu(c_ref[...]).astype(BF16)
    w = w_ref[0].astype(BF16)
    o_ref[0] = jnp.dot(ca, w, preferred_element_type=F32) + b_ref[0]


def _ada_mod(c, w_ada, b_ada):
    depth, d, n6 = w_ada.shape
    b = c.shape[0]
    rows = 16
    c_pad = jnp.zeros((rows, d), F32).at[:b].set(c)
    tn = 1536
    out = pl.pallas_call(
        _ada_kernel,
        out_shape=jax.ShapeDtypeStruct((depth, rows, n6), F32),
        grid=(depth, n6 // tn),
        in_specs=[
            pl.BlockSpec((rows, d), lambda l, j: (0, 0)),
            pl.BlockSpec((1, d, tn), lambda l, j: (l, 0, j)),
            pl.BlockSpec((1, 1, tn), lambda l, j: (l, 0, j)),
        ],
        out_specs=pl.BlockSpec((1, rows, tn), lambda l, j: (l, 0, j)),
        compiler_params=_cparams(2),
        name="ada_mod",
    )(c_pad, w_ada, b_ada.reshape(depth, 1, n6))
    return out[:, :b]


def _rms_mod(xf, g, sc, sh):
    ms = jnp.mean(xf * xf, axis=-1, keepdims=True)
    return xf * lax.rsqrt(ms + EPS) * g * (1.0 + sc) + sh


def _mix_in_kernel(x_ref, sc_ref, sh_ref, g_ref, w_ref, gq_ref, gk_ref, ones_ref,
                   z_ref, q_ref, k_ref, vt_ref, wbf_ref):
    first = jnp.logical_and(pl.program_id(0) == 0, pl.program_id(1) == 0)

    @pl.when(first)
    def _():
        wbf_ref[...] = w_ref[...].astype(BF16)

    h = _rms_mod(x_ref[0], g_ref[...], sc_ref[0], sh_ref[0]).astype(BF16)
    proj = jnp.dot(h, wbf_ref[...], preferred_element_type=F32)

    a = proj[:, :D_CONV]
    gate = proj[:, D_CONV:2 * D_CONV]
    z_ref[0] = (a * jax.nn.sigmoid(gate)).astype(BF16)

    def head_norm(t, g):
        ss = jnp.dot((t * t).astype(BF16), ones_ref[...], preferred_element_type=F32)
        return (t * lax.rsqrt(ss * (1.0 / HEAD_DIM) + EPS) * g).astype(BF16)

    o = 2 * D_CONV
    q_ref[0] = head_norm(proj[:, o:o + D_ATT], gq_ref[...])
    k_ref[0] = head_norm(proj[:, o + D_ATT:o + 2 * D_ATT], gk_ref[...])
    v = proj[:, o + 2 * D_ATT:]
    tm = v.shape[0]
    for cidx in range(tm // ATT_Q):
        vt_ref[0, cidx] = v[cidx * ATT_Q:(cidx + 1) * ATT_Q, :].T.astype(BF16)


def _mix_in(x, sc, sh, g, w_in, gq, gk, tm):
    b, s, d = x.shape
    ones_bd = (jnp.arange(D_ATT)[:, None] // HEAD_DIM
               == jnp.arange(D_ATT)[None, :] // HEAD_DIM).astype(BF16)
    gq_t = (jnp.tile(gq, N_HEADS) * (HEAD_DIM ** -0.5)).reshape(1, D_ATT)
    gk_t = jnp.tile(gk, N_HEADS).reshape(1, D_ATT)
    row = lambda bi, ti: (bi, ti, 0)
    per_b = lambda bi, ti: (bi, 0, 0)
    const2 = lambda bi, ti: (0, 0)
    return pl.pallas_call(
        _mix_in_kernel,
        out_shape=(
            jax.ShapeDtypeStruct((b, s, D_CONV), BF16),
            jax.ShapeDtypeStruct((b, s, D_ATT), BF16),
            jax.ShapeDtypeStruct((b, s, D_ATT), BF16),
            jax.ShapeDtypeStruct((b, s // ATT_Q, D_ATT, ATT_Q), BF16),
        ),
        grid=(b, s // tm),
        in_specs=[
            pl.BlockSpec((1, tm, d), row),
            pl.BlockSpec((1, 1, d), per_b),
            pl.BlockSpec((1, 1, d), per_b),
            pl.BlockSpec((1, d), const2),
            pl.BlockSpec((d, D_IN_COLS), const2),
            pl.BlockSpec((1, D_ATT), const2),
            pl.BlockSpec((1, D_ATT), const2),
            pl.BlockSpec((D_ATT, D_ATT), const2),
        ],
        out_specs=(
            pl.BlockSpec((1, tm, D_CONV), row),
            pl.BlockSpec((1, tm, D_ATT), row),
            pl.BlockSpec((1, tm, D_ATT), row),
            pl.BlockSpec((1, tm // ATT_Q, D_ATT, ATT_Q), lambda bi, ti: (bi, ti, 0, 0)),
        ),
        scratch_shapes=[pltpu.VMEM((d, D_IN_COLS), BF16)],
        compiler_params=_cparams(2),
        name="mix_in",
    )(x, sc, sh, g.reshape(1, d), w_in, gq_t, gk_t, ones_bd)


def _conv_kernel(zc_ref, zp_ref, w_ref, cb_ref, lg_ref, lb_ref, o_ref, win_ref):
    tt = zc_ref.shape[1]
    t = pl.program_id(1)
    halo = zp_ref[0].astype(F32)
    win_ref[0:CONV_HALO, :] = jnp.where(t == 0, 0.0, halo)
    win_ref[CONV_HALO:, :] = zc_ref[0].astype(F32)
    acc = jnp.zeros((tt, D_CONV), F32) + cb_ref[...]
    base = CONV_HALO - (CONV_WIDTH - 1)
    for j in range(CONV_WIDTH):
        acc = acc + win_ref[base + j:base + j + tt, :] * w_ref[j:j + 1, :]
    mu = jnp.mean(acc, axis=-1, keepdims=True)
    xc = acc - mu
    var = jnp.mean(xc * xc, axis=-1, keepdims=True)
    y = xc * lax.rsqrt(var + EPS) * lg_ref[...] + lb_ref[...]
    o_ref[0] = _silu(y).astype(BF16)


def _conv_branch(z, conv_w, conv_b, ln_g, ln_b, tt):
    b, s, c = z.shape
    hb = tt // CONV_HALO
    const2 = lambda bi, ti: (0, 0)
    return pl.pallas_call(
        _conv_kernel,
        out_shape=jax.ShapeDtypeStruct((b, s, c), BF16),
        grid=(b, s // tt),
        in_specs=[
            pl.BlockSpec((1, tt, c), lambda bi, ti: (bi, ti, 0)),
            pl.BlockSpec((1, CONV_HALO, c),
                         lambda bi, ti: (bi, jnp.maximum(ti * hb - 1, 0), 0)),
            pl.BlockSpec((CONV_WIDTH, c), const2),
            pl.BlockSpec((1, c), const2),
            pl.BlockSpec((1, c), const2),
            pl.BlockSpec((1, c), const2),
        ],
        out_specs=pl.BlockSpec((1, tt, c), lambda bi, ti: (bi, ti, 0)),
        scratch_shapes=[pltpu.VMEM((tt + CONV_HALO, c), F32)],
        compiler_params=_cparams(2),
        name="conv_branch",
    )(z, z, conv_w.reshape(CONV_WIDTH, c), conv_b.reshape(1, c),
      ln_g.reshape(1, c), ln_b.reshape(1, c))


def _attn_kernel(q_ref, k_ref, vt_ref, bias_ref, o_ref, kpad_ref, vtpad_ref):
    s = q_ref.shape[1]
    npad = BAND_PAD // ATT_Q
    kpad_ref[0:BAND_PAD, :] = jnp.zeros((BAND_PAD, ATT_W), BF16)
    kpad_ref[BAND_PAD:, :] = k_ref[0]
    vtpad_ref[0:npad] = jnp.zeros((npad, ATT_W, ATT_Q), BF16)
    vtpad_ref[npad:] = vt_ref[0]

    iota = lambda shape, dim: lax.broadcasted_iota(jnp.int32, shape, dim)
    q_shift = ATT_Q.bit_length() - 1
    d_shift = HEAD_DIM.bit_length() - 1
    qb_mask = (iota((ATT_L, ATT_W), 0) >> q_shift) == (iota((ATT_L, ATT_W), 1) >> d_shift)
    ot_mask = (iota((ATT_W, ATT_L), 0) >> d_shift) == (iota((ATT_W, ATT_L), 1) >> q_shift)
    sel = jnp.where((iota((ATT_Q, ATT_L), 1) & (ATT_Q - 1)) == iota((ATT_Q, ATT_L), 0),
                    1.0, 0.0).astype(BF16)
    key_row = lax.broadcasted_iota(jnp.int32, (ATT_BAND, ATT_L), 0)
    contract_last = (((1,), (1,)), ((), ()))

    def step(m, carry):
        r0 = pl.multiple_of(m * ATT_Q, ATT_Q)
        qt = q_ref[0, pl.ds(r0, ATT_Q), :]
        qb = jnp.where(qb_mask, jnp.concatenate([qt] * ATT_HEADS, axis=0), 0)
        kb = kpad_ref[pl.ds(r0, ATT_BAND), :]
        st = lax.dot_general(kb, qb.astype(BF16), contract_last,
                             preferred_element_type=F32)
        st = st + bias_ref[0]
        st = jnp.where(key_row >= BAND_PAD - r0, st, NEG_INF)
        mx = jnp.max(st, axis=0, keepdims=True)
        p = jnp.exp(st - mx)
        den = jnp.sum(p, axis=0, keepdims=True)
        pb = p.astype(BF16)
        vb = jnp.concatenate([vtpad_ref[m + c] for c in range(ATT_BAND // ATT_Q)],
                             axis=1)
        ot = jnp.dot(vb, pb, preferred_element_type=F32)
        ot = jnp.where(ot_mask, ot / den, 0.0).astype(BF16)
        y = lax.dot_general(sel, ot, contract_last, preferred_element_type=F32)
        o_ref[0, pl.ds(r0, ATT_Q), :] = y.astype(BF16)
        return carry

    lax.fori_loop(0, s // ATT_Q, step, 0)


def _attn_bias_t(rel_bias):
    rb = rel_bias.astype(F32)
    nu = ATT_BAND + ATT_Q - 1
    n_low = BAND_PAD - MAX_REL + ATT_Q
    t = jnp.concatenate([jnp.repeat(rb[:, :1], n_low, axis=1),
                         rb[:, 1:1 + nu - n_low]], axis=1)
    tp = jnp.pad(t, ((0, 0), (0, 1)))
    skew = jnp.tile(tp, (1, ATT_Q))[:, :ATT_Q * nu].reshape(N_HEADS, ATT_Q, nu)
    bias = skew[:, :, ATT_Q - 1:]
    bias = bias.reshape(ATT_GROUPS, ATT_HEADS, ATT_Q, ATT_BAND)
    bias = bias.transpose(0, 3, 1, 2).reshape(ATT_GROUPS, ATT_BAND, ATT_L)
    r = jnp.arange(ATT_BAND)[:, None]
    qq = jnp.arange(ATT_L)[None, :] % ATT_Q
    first = (qq // CHUNK) * CHUNK
    valid = (r >= first) & (r < first + BAND_PAD + CHUNK)
    return jnp.where(valid[None], bias, NEG_INF)


def _attention(qn, kn, vt, rel_bias):
    b, s, _ = qn.shape
    bias_t = _attn_bias_t(rel_bias)
    nck = s // ATT_Q
    return pl.pallas_call(
        _attn_kernel,
        out_shape=jax.ShapeDtypeStruct((b, s, D_ATT), BF16),
        grid=(b, ATT_GROUPS),
        in_specs=[
            pl.BlockSpec((1, s, ATT_W), lambda bi, gi: (bi, 0, gi)),
            pl.BlockSpec((1, s, ATT_W), lambda bi, gi: (bi, 0, gi)),
            pl.BlockSpec((1, nck, ATT_W, ATT_Q), lambda bi, gi: (bi, 0, gi, 0)),
            pl.BlockSpec((1, ATT_BAND, ATT_L), lambda bi, gi: (gi, 0, 0)),
        ],
        out_specs=pl.BlockSpec((1, s, ATT_W), lambda bi, gi: (bi, 0, gi)),
        scratch_shapes=[
            pltpu.VMEM((s + BAND_PAD, ATT_W), BF16),
            pltpu.VMEM((nck + BAND_PAD // ATT_Q, ATT_W, ATT_Q), BF16),
        ],
        compiler_params=_cparams(2),
        name="band_attention",
    )(qn, kn, vt, bias_t)


def _mix_out_kernel(x_ref, yc_ref, ya_ref, w_ref, g1_ref, gf_ref, sc_ref, sh_ref,
                    x1_ref, h_ref, wbf_ref):
    first = jnp.logical_and(pl.program_id(0) == 0, pl.program_id(1) == 0)

    @pl.when(first)
    def _():
        wbf_ref[...] = w_ref[...].astype(BF16)

    y = jnp.dot(yc_ref[0], wbf_ref[0:D_CONV, :], preferred_element_type=F32)
    y = y + jnp.dot(ya_ref[0], wbf_ref[D_CONV:, :], preferred_element_type=F32)
    x1 = x_ref[0] + g1_ref[0] * y
    x1_ref[0] = x1
    h_ref[0] = _rms_mod(x1, gf_ref[...], sc_ref[0], sh_ref[0]).astype(BF16)


def _mix_out(x, yc, ya, w_out, g1, gf, sc, sh, tm):
    b, s, d = x.shape
    row = lambda bi, ti: (bi, ti, 0)
    per_b = lambda bi, ti: (bi, 0, 0)
    const2 = lambda bi, ti: (0, 0)
    return pl.pallas_call(
        _mix_out_kernel,
        out_shape=(jax.ShapeDtypeStruct((b, s, d), F32),
                   jax.ShapeDtypeStruct((b, s, d), BF16)),
        grid=(b, s // tm),
        in_specs=[
            pl.BlockSpec((1, tm, d), row),
            pl.BlockSpec((1, tm, D_CONV), row),
            pl.BlockSpec((1, tm, D_ATT), row),
            pl.BlockSpec((d, d), const2),
            pl.BlockSpec((1, 1, d), per_b),
            pl.BlockSpec((1, d), const2),
            pl.BlockSpec((1, 1, d), per_b),
            pl.BlockSpec((1, 1, d), per_b),
        ],
        out_specs=(pl.BlockSpec((1, tm, d), row), pl.BlockSpec((1, tm, d), row)),
        scratch_shapes=[pltpu.VMEM((d, d), BF16)],
        compiler_params=_cparams(2),
        name="mix_out",
    )(x, yc, ya, w_out, g1, gf.reshape(1, d), sc, sh)


def _ffn_kernel(h_ref, x_ref, g2_ref, wg_ref, wu_ref, wd_ref, o_ref, acc_ref):
    f = pl.program_id(2)
    h = h_ref[0]
    a = jnp.dot(h, wg_ref[...].astype(BF16), preferred_element_type=F32)
    u = jnp.dot(h, wu_ref[...].astype(BF16), preferred_element_type=F32)
    t = (_silu(a) * u).astype(BF16)
    dn = jnp.dot(t, wd_ref[...].astype(BF16), preferred_element_type=F32)

    @pl.when(f == 0)
    def _():
        acc_ref[...] = dn

    @pl.when(f > 0)
    def _():
        acc_ref[...] += dn

    @pl.when(f == pl.num_programs(2) - 1)
    def _():
        o_ref[0] = x_ref[0] + g2_ref[0] * acc_ref[...]


def _dense_ffn(h, x1, g2, wg, wu, wd, tm, fc):
    b, s, d = x1.shape
    ff = wg.shape[1]
    row = lambda bi, ti, fi: (bi, ti, 0)
    return pl.pallas_call(
        _ffn_kernel,
        out_shape=jax.ShapeDtypeStruct((b, s, d), F32),
        grid=(b, s // tm, ff // fc),
        in_specs=[
            pl.BlockSpec((1, tm, d), row),
            pl.BlockSpec((1, tm, d), row),
            pl.BlockSpec((1, 1, d), lambda bi, ti, fi: (bi, 0, 0)),
            pl.BlockSpec((d, fc), lambda bi, ti, fi: (0, fi)),
            pl.BlockSpec((d, fc), lambda bi, ti, fi: (0, fi)),
            pl.BlockSpec((fc, d), lambda bi, ti, fi: (fi, 0)),
        ],
        out_specs=pl.BlockSpec((1, tm, d), row),
        scratch_shapes=[pltpu.VMEM((tm, d), F32)],
        compiler_params=_cparams(3),
        name="dense_ffn",
    )(h, x1, g2, wg, wu, wd)


def _router_kernel(h_ref, w_ref, b_ref, info_ref, cnt_ref, carry_ref):
    i = pl.program_id(0)

    @pl.when(i == 0)
    def _():
        carry_ref[...] = jnp.zeros_like(carry_ref)

    tr = h_ref.shape[0]
    logits = jnp.dot(h_ref[...], w_ref[...], preferred_element_type=F32) + b_ref[...]
    lane = lax.broadcasted_iota(jnp.int32, (tr, LANE_PAD_E), 1).astype(F32)
    no_lane = float(LANE_PAD_E)
    v0 = jnp.max(logits, axis=-1, keepdims=True)
    i0 = jnp.min(jnp.where(logits == v0, lane, no_lane), axis=-1, keepdims=True)
    rest = jnp.where(lane == i0, -jnp.inf, logits)
    v1 = jnp.max(rest, axis=-1, keepdims=True)
    i1 = jnp.min(jnp.where(rest == v1, lane, no_lane), axis=-1, keepdims=True)
    e1 = jnp.exp(v1 - v0)
    w0 = 1.0 / (1.0 + e1)
    w1 = e1 / (1.0 + e1)
    oh0 = lane == i0
    oh1 = lane == i1
    cnt = jnp.where(jnp.logical_or(oh0, oh1), 1.0, 0.0)
    tri = (lax.broadcasted_iota(jnp.int32, (tr, tr), 1)
           < lax.broadcasted_iota(jnp.int32, (tr, tr), 0)).astype(BF16)
    before = jnp.dot(tri, cnt.astype(BF16), preferred_element_type=F32) + carry_ref[0:1, :]
    r0 = jnp.sum(jnp.where(oh0, before, 0.0), axis=-1, keepdims=True)
    r1 = jnp.sum(jnp.where(oh1, before, 0.0), axis=-1, keepdims=True)
    carry_ref[...] = carry_ref[...] + jnp.sum(cnt, axis=0, keepdims=True)
    cnt_ref[...] = carry_ref[...]
    info = jnp.where(lane == 0, i0, 0.0)
    info = jnp.where(lane == 1, i1, info)
    info = jnp.where(lane == 2, w0, info)
    info = jnp.where(lane == 3, w1, info)
    info = jnp.where(lane == 4, r0, info)
    info = jnp.where(lane == 5, r1, info)
    info_ref[...] = info


def _router(h2, w_router, b_router, tr):
    n, d = h2.shape
    w_pad = jnp.zeros((d, LANE_PAD_E), BF16).at[:, :N_EXPERTS].set(w_router.astype(BF16))
    b_pad = jnp.full((1, LANE_PAD_E), -jnp.inf, F32).at[0, :N_EXPERTS].set(b_router.astype(F32))
    return pl.pallas_call(
        _router_kernel,
        out_shape=(jax.ShapeDtypeStruct((n, LANE_PAD_E), F32),
                   jax.ShapeDtypeStruct((8, LANE_PAD_E), F32)),
        grid=(n // tr,),
        in_specs=[
            pl.BlockSpec((tr, d), lambda i: (i, 0)),
            pl.BlockSpec((d, LANE_PAD_E), lambda i: (0, 0)),
            pl.BlockSpec((1, LANE_PAD_E), lambda i: (0, 0)),
        ],
        out_specs=(pl.BlockSpec((tr, LANE_PAD_E), lambda i: (i, 0)),
                   pl.BlockSpec((8, LANE_PAD_E), lambda i: (0, 0))),
        scratch_shapes=[pltpu.VMEM((8, LANE_PAD_E), F32)],
        compiler_params=_cparams(1),
        name="router",
    )(h2, w_pad, b_pad)


def _moe_kernel(te_ref, tb_ref, tn_ref, xs_ref, wg_ref, wu_ref, wd_ref, ys_ref,
                acc_ref, wgb_ref, wub_ref, wdb_ref):
    i = pl.program_id(0)
    f = pl.program_id(1)
    nsub = tn_ref[i]

    live = nsub > 0

    @pl.when(jnp.logical_and(live, f == 0))
    def _():
        acc_ref[...] = jnp.zeros_like(acc_ref)

    @pl.when(live)
    def _():
        wgb_ref[...] = wg_ref[0].astype(BF16)
        wub_ref[...] = wu_ref[0].astype(BF16)
        wdb_ref[...] = wd_ref[0].astype(BF16)

    def sub(sidx, carry):
        r = pl.multiple_of(sidx * MOE_SUB, MOE_SUB)
        xb = xs_ref[pl.ds(r, MOE_SUB), :]
        a = jnp.dot(xb, wgb_ref[...], preferred_element_type=F32)
        u = jnp.dot(xb, wub_ref[...], preferred_element_type=F32)
        t = (_silu(a) * u).astype(BF16)
        acc_ref[pl.ds(r, MOE_SUB), :] += jnp.dot(t, wdb_ref[...], preferred_element_type=F32)
        return carry

    lax.fori_loop(0, nsub, sub, 0)

    @pl.when(jnp.logical_and(live, f == pl.num_programs(1) - 1))
    def _():
        ys_ref[...] = acc_ref[...].astype(BF16)


def _moe_ffn(xs, tile_e, tile_blk, tile_nsub, wg, wu, wd):
    rpad, d = xs.shape
    ntiles = rpad // MOE_TILE
    ff = wg.shape[2]
    nf = ff // MOE_FC

    def fcol(i, f, tn):
        return jnp.where(tn[i] > 0, f, nf - 1)

    return pl.pallas_call(
        _moe_kernel,
        out_shape=jax.ShapeDtypeStruct((rpad, d), BF16),
        grid_spec=pltpu.PrefetchScalarGridSpec(
            num_scalar_prefetch=3,
            grid=(ntiles, nf),
            in_specs=[
                pl.BlockSpec((MOE_TILE, d), lambda i, f, te, tb, tn: (tb[i], 0)),
                pl.BlockSpec((1, d, MOE_FC), lambda i, f, te, tb, tn: (te[i], 0, fcol(i, f, tn))),
                pl.BlockSpec((1, d, MOE_FC), lambda i, f, te, tb, tn: (te[i], 0, fcol(i, f, tn))),
                pl.BlockSpec((1, MOE_FC, d), lambda i, f, te, tb, tn: (te[i], fcol(i, f, tn), 0)),
            ],
            out_specs=pl.BlockSpec((MOE_TILE, d), lambda i, f, te, tb, tn: (tb[i], 0)),
            scratch_shapes=[
                pltpu.VMEM((MOE_TILE, d), F32),
                pltpu.VMEM((d, MOE_FC), BF16),
                pltpu.VMEM((d, MOE_FC), BF16),
                pltpu.VMEM((MOE_FC, d), BF16),
            ],
        ),
        compiler_params=_cparams(2),
        name="moe_ffn",
    )(tile_e, tile_blk, tile_nsub, xs, wg, wu, wd)


def _combine_kernel(x_ref, y0_ref, y1_ref, info_ref, g2_ref, o_ref):
    info = info_ref[0]
    w0 = info[:, 2:3]
    w1 = info[:, 3:4]
    f = w0 * y0_ref[0].astype(F32) + w1 * y1_ref[0].astype(F32)
    o_ref[0] = x_ref[0] + g2_ref[0] * f


def _combine(x1, y0, y1, info, g2, tm):
    b, s, d = x1.shape
    row = lambda bi, ti: (bi, ti, 0)
    return pl.pallas_call(
        _combine_kernel,
        out_shape=jax.ShapeDtypeStruct((b, s, d), F32),
        grid=(b, s // tm),
        in_specs=[
            pl.BlockSpec((1, tm, d), row),
            pl.BlockSpec((1, tm, d), row),
            pl.BlockSpec((1, tm, d), row),
            pl.BlockSpec((1, tm, LANE_PAD_E), row),
            pl.BlockSpec((1, 1, d), lambda bi, ti: (bi, 0, 0)),
        ],
        out_specs=pl.BlockSpec((1, tm, d), row),
        compiler_params=_cparams(2),
        name="moe_combine",
    )(x1, y0, y1, info, g2)


def _moe_layer(h2, x1, g2, w_router, b_router, wg, wu, wd, tm):
    b, s, d = x1.shape
    n = b * s
    info, cnt = _router(h2.reshape(n, d), w_router, b_router, min(512, n))
    e0 = info[:, 0].astype(jnp.int32)
    e1 = info[:, 1].astype(jnp.int32)
    r0 = info[:, 4].astype(jnp.int32)
    r1 = info[:, 5].astype(jnp.int32)
    counts = cnt[0, :N_EXPERTS].astype(jnp.int32)

    ntiles = (2 * n) // MOE_TILE + N_EXPERTS
    tiles_per_e = (counts + MOE_TILE - 1) // MOE_TILE
    tile_end = jnp.cumsum(tiles_per_e)
    tile_start = tile_end - tiles_per_e
    total = tile_end[-1]
    tidx = jnp.arange(ntiles, dtype=jnp.int32)
    live = tidx < total
    tclip = jnp.minimum(tidx, total - 1)
    tile_e = jnp.minimum(jnp.sum(tclip[:, None] >= tile_end[None, :], axis=1),
                         N_EXPERTS - 1).astype(jnp.int32)
    rows_left = counts[tile_e] - (tclip - tile_start[tile_e]) * MOE_TILE
    rows_here = jnp.clip(rows_left, 0, MOE_TILE)
    tile_nsub = jnp.where(live, (rows_here + MOE_SUB - 1) // MOE_SUB, 0).astype(jnp.int32)
    tile_blk = tclip.astype(jnp.int32)

    row_start = tile_start * MOE_TILE
    eid = jnp.arange(N_EXPERTS, dtype=jnp.int32)[None, :]
    pos0 = jnp.sum(jnp.where(e0[:, None] == eid, row_start[None, :], 0), axis=1) + r0
    pos1 = jnp.sum(jnp.where(e1[:, None] == eid, row_start[None, :], 0), axis=1) + r1

    hflat = h2.reshape(n, d)
    xs = jnp.zeros((ntiles * MOE_TILE, d), BF16).at[pos0].set(hflat).at[pos1].set(hflat)
    ys = _moe_ffn(xs, tile_e, tile_blk, tile_nsub, wg, wu, wd)
    y0 = ys[pos0].reshape(b, s, d)
    y1 = ys[pos1].reshape(b, s, d)
    return _combine(x1, y0, y1, info.reshape(b, s, LANE_PAD_E), g2, tm)


def kernel(x, c, w_ada, b_ada, norm_mix_g, norm_ffn_g, w_in, w_out, conv_w, conv_b,
           conv_ln_g, conv_ln_b, q_norm_g, k_norm_g, rel_bias, ffn_w_gate, ffn_w_up,
           ffn_w_down, moe_w_router, moe_b_router, moe_w_gate, moe_w_up, moe_w_down):
    b, s, d = x.shape
    depth = w_ada.shape[0]
    tm = min(512, s)
    mod = _ada_mod(c, w_ada, b_ada)
    for l in range(depth):
        sh1, sc1, g1, sh2, sc2, g2 = [
            mod[l, :, j * d:(j + 1) * d].reshape(b, 1, d) for j in range(6)]
        z, qn, kn, vt = _mix_in(x, sc1, sh1, norm_mix_g[l], w_in[l],
                                q_norm_g[l], k_norm_g[l], tm)
        yc = _conv_branch(z, conv_w[l], conv_b[l], conv_ln_g[l], conv_ln_b[l], min(256, s))
        ya = _attention(qn, kn, vt, rel_bias[l])
        x1, h2 = _mix_out(x, yc, ya, w_out[l], g1, norm_ffn_g[l], sc2, sh2, tm)
        i = l // 2
        if l % 2 == 0:
            x = _dense_ffn(h2, x1, g2, ffn_w_gate[i], ffn_w_up[i], ffn_w_down[i],
                           min(1024, s), 256)
        else:
            x = _moe_layer(h2, x1, g2, moe_w_router[i], moe_b_router[i],
                           moe_w_gate[i], moe_w_up[i], moe_w_down[i], tm)
    return x
```

```python
import functools

import jax
import jax.numpy as jnp
from jax import lax
from jax.experimental import pallas as pl
from jax.experimental.pallas import tpu as pltpu
from jax.experimental.pallas import tpu_sc as plsc

F32 = jnp.float32
BF16 = jnp.bfloat16

D_MODEL = 1024
CHUNK = 64
N_PREV_CHUNKS = 8
BAND_PAD = N_PREV_CHUNKS * CHUNK
D_CONV = 512
D_ATT = 512
HEAD_DIM = 64
N_HEADS = 8
CONV_WIDTH = 31
MAX_REL = 128
D_IN_COLS = 2 * D_CONV + 3 * D_ATT
N_EXPERTS = 8
EPS = 1e-6
NEG_INF = -1e30

LANES = 128
VMEM_LIMIT_BYTES = 56 * 1024 * 1024

ATT_HEADS = 4
ATT_GROUPS = N_HEADS // ATT_HEADS
ATT_W = ATT_HEADS * HEAD_DIM
ATT_Q = 2 * CHUNK
ATT_BAND = BAND_PAD + ATT_Q
ATT_L = ATT_HEADS * ATT_Q

CONV_HALO = 32
LANE_PAD_E = LANES

MOE_SUB = 256
MOE_TILE = 2048
MOE_FC = 512


def _cparams(n_axes, vmem=VMEM_LIMIT_BYTES):
    return pltpu.CompilerParams(
        dimension_semantics=("arbitrary",) * n_axes, vmem_limit_bytes=vmem)


def _silu(v):
    return v * jax.nn.sigmoid(v)


def _pack_bf16_pairs(v):
    w = v.shape[1] // 2
    bits = lax.bitcast_convert_type(v.astype(BF16).astype(F32), jnp.uint32)
    packed = (bits[:, w:] & jnp.uint32(0xFFFF0000)) | (bits[:, :w] >> 16)
    return lax.bitcast_convert_type(packed, jnp.int32)


def _unpack_bf16_pairs(p):
    bits = lax.bitcast_convert_type(p, jnp.uint32)
    lo = lax.bitcast_convert_type(bits << 16, F32)
    hi = lax.bitcast_convert_type(bits & jnp.uint32(0xFFFF0000), F32)
    return jnp.concatenate([lo, hi], axis=1)


SC_CORES = 2
SC_SUBCORES = 16
SC_WORKERS = SC_CORES * SC_SUBCORES
SC_CHUNK = 64


def _sc_worker_id():
    return lax.axis_index("s") * SC_CORES + lax.axis_index("c")


def _sc_mesh():
    return plsc.VectorSubcoreMesh(core_axis_name="c", subcore_axis_name="s")


def _sc_gather_rows(table, idx):
    _, w = table.shape
    b = idx.shape[0]
    per_w = b // SC_WORKERS
    nch = per_w // SC_CHUNK

    def body(table_hbm, idx_hbm, out_hbm, idx_v, rows_v, gsem, wsem):
        wid = _sc_worker_id()
        base = wid * per_w
        pltpu.sync_copy(idx_hbm.at[wid], idx_v)
        gathers = [None] * nch
        writes = [None] * nch
        gathers[0] = pltpu.async_copy(table_hbm.at[idx_v.at[0]], rows_v.at[0], gsem.at[0])
        for c in range(nch):
            slot = c % 2
            gathers[c].wait()
            if c + 1 < nch:
                if c >= 1:
                    writes[c - 1].wait()
                gathers[c + 1] = pltpu.async_copy(
                    table_hbm.at[idx_v.at[c + 1]], rows_v.at[1 - slot], gsem.at[1 - slot])
            writes[c] = pltpu.async_copy(
                rows_v.at[slot], out_hbm.at[pl.ds(base + c * SC_CHUNK, SC_CHUNK)], wsem.at[slot])
        if nch >= 2:
            writes[nch - 2].wait()
        writes[nch - 1].wait()

    call = pl.kernel(
        body, mesh=_sc_mesh(),
        out_type=jax.ShapeDtypeStruct((b, w), jnp.int32),
        scratch_types=[pltpu.VMEM((nch, SC_CHUNK), jnp.int32),
                       pltpu.VMEM((2, SC_CHUNK, w), jnp.int32),
                       pltpu.SemaphoreType.DMA((2,)), pltpu.SemaphoreType.DMA((2,))],
        name="sc_gather_rows")
    return call(table, idx.reshape(SC_WORKERS, nch, SC_CHUNK))


def _sc_scatter_rows2(src, idx0, idx1, rows_out):
    n, w = src.shape
    per_w = n // SC_WORKERS
    nch = per_w // SC_CHUNK

    def body(src_hbm, i0_hbm, i1_hbm, out_hbm, i0_v, i1_v, rows_v, rsem, wsem):
        wid = _sc_worker_id()
        base = wid * per_w
        pltpu.sync_copy(i0_hbm.at[wid], i0_v)
        pltpu.sync_copy(i1_hbm.at[wid], i1_v)
        reads = [None] * nch
        writes = [None] * nch
        reads[0] = pltpu.async_copy(src_hbm.at[pl.ds(base, SC_CHUNK)], rows_v.at[0], rsem.at[0])
        for c in range(nch):
            slot = c % 2
            reads[c].wait()
            if c + 1 < nch:
                if c >= 1:
                    for wr in writes[c - 1]:
                        wr.wait()
                reads[c + 1] = pltpu.async_copy(
                    src_hbm.at[pl.ds(base + (c + 1) * SC_CHUNK, SC_CHUNK)],
                    rows_v.at[1 - slot], rsem.at[1 - slot])
            writes[c] = (
                pltpu.async_copy(rows_v.at[slot], out_hbm.at[i0_v.at[c]], wsem.at[slot, 0]),
                pltpu.async_copy(rows_v.at[slot], out_hbm.at[i1_v.at[c]], wsem.at[slot, 1]),
            )
        for c in range(max(nch - 2, 0), nch):
            for wr in writes[c]:
                wr.wait()

    call = pl.kernel(
        body, mesh=_sc_mesh(),
        out_type=jax.ShapeDtypeStruct((rows_out, w), jnp.int32),
        scratch_types=[pltpu.VMEM((nch, SC_CHUNK), jnp.int32),
                       pltpu.VMEM((nch, SC_CHUNK), jnp.int32),
                       pltpu.VMEM((2, SC_CHUNK, w), jnp.int32),
                       pltpu.SemaphoreType.DMA((2,)), pltpu.SemaphoreType.DMA((2, 2))],
        name="sc_scatter_rows")
    shape3 = (SC_WORKERS, nch, SC_CHUNK)
    return call(src, idx0.reshape(shape3), idx1.reshape(shape3))


def _ada_kernel(c_ref, w_ref, b_ref, o_ref):
    ca = _silu(c_ref[...]).astype(BF16)
    w = w_ref[0].astype(BF16)
    o_ref[0] = jnp.dot(ca, w, preferred_element_type=F32) + b_ref[0]


def _ada_mod(c, w_ada, b_ada):
    depth, d, n6 = w_ada.shape
    b = c.shape[0]
    rows = 16
    c_pad = jnp.zeros((rows, d), F32).at[:b].set(c)
    tn = 1536
    out = pl.pallas_call(
        _ada_kernel,
        out_shape=jax.ShapeDtypeStruct((depth, rows, n6), F32),
        grid=(depth, n6 // tn),
        in_specs=[
            pl.BlockSpec((rows, d), lambda l, j: (0, 0)),
            pl.BlockSpec((1, d, tn), lambda l, j: (l, 0, j)),
            pl.BlockSpec((1, 1, tn), lambda l, j: (l, 0, j)),
        ],
        out_specs=pl.BlockSpec((1, rows, tn), lambda l, j: (l, 0, j)),
        compiler_params=_cparams(2),
        name="ada_mod",
    )(c_pad, w_ada, b_ada.reshape(depth, 1, n6))
    return out[:, :b]


def _rms_mod(xf, g, sc, sh):
    ms = jnp.mean(xf * xf, axis=-1, keepdims=True)
    return xf * lax.rsqrt(ms + EPS) * g * (1.0 + sc) + sh


def _mix_in_kernel(x_ref, sc_ref, sh_ref, g_ref, w_ref, gq_ref, gk_ref, ones_ref,
                   z_ref, q_ref, k_ref, vt_ref, wbf_ref):
    first = jnp.logical_and(pl.program_id(0) == 0, pl.program_id(1) == 0)

    @pl.when(first)
    def _():
        wbf_ref[...] = w_ref[...].astype(BF16)

    h = _rms_mod(x_ref[0], g_ref[...], sc_ref[0], sh_ref[0]).astype(BF16)
    proj = jnp.dot(h, wbf_ref[...], preferred_element_type=F32)

    a = proj[:, :D_CONV]
    gate = proj[:, D_CONV:2 * D_CONV]
    z_ref[0] = (a * jax.nn.sigmoid(gate)).astype(BF16)

    def head_norm(t, g):
        ss = jnp.dot((t * t).astype(BF16), ones_ref[...], preferred_element_type=F32)
        return (t * lax.rsqrt(ss * (1.0 / HEAD_DIM) + EPS) * g).astype(BF16)

    o = 2 * D_CONV
    q_ref[0] = head_norm(proj[:, o:o + D_ATT], gq_ref[...])
    k_ref[0] = head_norm(proj[:, o + D_ATT:o + 2 * D_ATT], gk_ref[...])
    v = proj[:, o + 2 * D_ATT:]
    tm = v.shape[0]
    for cidx in range(tm // ATT_Q):
        vt_ref[0, cidx] = v[cidx * ATT_Q:(cidx + 1) * ATT_Q, :].T.astype(BF16)


def _mix_in(x, sc, sh, g, w_in, gq, gk, tm):
    b, s, d = x.shape
    ones_bd = (jnp.arange(D_ATT)[:, None] // HEAD_DIM
               == jnp.arange(D_ATT)[None, :] // HEAD_DIM).astype(BF16)
    gq_t = (jnp.tile(gq, N_HEADS) * (HEAD_DIM ** -0.5)).reshape(1, D_ATT)
    gk_t = jnp.tile(gk, N_HEADS).reshape(1, D_ATT)
    row = lambda bi, ti: (bi, ti, 0)
    per_b = lambda bi, ti: (bi, 0, 0)
    const2 = lambda bi, ti: (0, 0)
    return pl.pallas_call(
        _mix_in_kernel,
        out_shape=(
            jax.ShapeDtypeStruct((b, s, D_CONV), BF16),
            jax.ShapeDtypeStruct((b, s, D_ATT), BF16),
            jax.ShapeDtypeStruct((b, s, D_ATT), BF16),
            jax.ShapeDtypeStruct((b, s // ATT_Q, D_ATT, ATT_Q), BF16),
        ),
        grid=(b, s // tm),
        in_specs=[
            pl.BlockSpec((1, tm, d), row),
            pl.BlockSpec((1, 1, d), per_b),
            pl.BlockSpec((1, 1, d), per_b),
            pl.BlockSpec((1, d), const2),
            pl.BlockSpec((d, D_IN_COLS), const2),
            pl.BlockSpec((1, D_ATT), const2),
            pl.BlockSpec((1, D_ATT), const2),
            pl.BlockSpec((D_ATT, D_ATT), const2),
        ],
        out_specs=(
            pl.BlockSpec((1, tm, D_CONV), row),
            pl.BlockSpec((1, tm, D_ATT), row),
            pl.BlockSpec((1, tm, D_ATT), row),
            pl.BlockSpec((1, tm // ATT_Q, D_ATT, ATT_Q), lambda bi, ti: (bi, ti, 0, 0)),
        ),
        scratch_shapes=[pltpu.VMEM((d, D_IN_COLS), BF16)],
        compiler_params=_cparams(2),
        name="mix_in",
    )(x, sc, sh, g.reshape(1, d), w_in, gq_t, gk_t, ones_bd)


def _conv_kernel(zc_ref, zp_ref, w_ref, cb_ref, lg_ref, lb_ref, o_ref, win_ref):
    tt = zc_ref.shape[1]
    t = pl.program_id(1)
    halo = zp_ref[0].astype(F32)
    win_ref[0:CONV_HALO, :] = jnp.where(t == 0, 0.0, halo)
    win_ref[CONV_HALO:, :] = zc_ref[0].astype(F32)
    acc = jnp.zeros((tt, D_CONV), F32) + cb_ref[...]
    base = CONV_HALO - (CONV_WIDTH - 1)
    for j in range(CONV_WIDTH):
        acc = acc + win_ref[base + j:base + j + tt, :] * w_ref[j:j + 1, :]
    mu = jnp.mean(acc, axis=-1, keepdims=True)
    xc = acc - mu
    var = jnp.mean(xc * xc, axis=-1, keepdims=True)
    y = xc * lax.rsqrt(var + EPS) * lg_ref[...] + lb_ref[...]
    o_ref[0] = _silu(y).astype(BF16)


def _conv_branch(z, conv_w, conv_b, ln_g, ln_b, tt):
    b, s, c = z.shape
    hb = tt // CONV_HALO
    const2 = lambda bi, ti: (0, 0)
    return pl.pallas_call(
        _conv_kernel,
        out_shape=jax.ShapeDtypeStruct((b, s, c), BF16),
        grid=(b, s // tt),
        in_specs=[
            pl.BlockSpec((1, tt, c), lambda bi, ti: (bi, ti, 0)),
            pl.BlockSpec((1, CONV_HALO, c),
                         lambda bi, ti: (bi, jnp.maximum(ti * hb - 1, 0), 0)),
            pl.BlockSpec((CONV_WIDTH, c), const2),
            pl.BlockSpec((1, c), const2),
            pl.BlockSpec((1, c), const2),
            pl.BlockSpec((1, c), const2),
        ],
        out_specs=pl.BlockSpec((1, tt, c), lambda bi, ti: (bi, ti, 0)),
        scratch_shapes=[pltpu.VMEM((tt + CONV_HALO, c), F32)],
        compiler_params=_cparams(2),
        name="conv_branch",
    )(z, z, conv_w.reshape(CONV_WIDTH, c), conv_b.reshape(1, c),
      ln_g.reshape(1, c), ln_b.reshape(1, c))


def _attn_kernel(q_ref, k_ref, vt_ref, bias_ref, o_ref, kpad_ref, vtpad_ref):
    s = q_ref.shape[1]
    npad = BAND_PAD // ATT_Q
    kpad_ref[0:BAND_PAD, :] = jnp.zeros((BAND_PAD, ATT_W), BF16)
    kpad_ref[BAND_PAD:, :] = k_ref[0]
    vtpad_ref[0:npad] = jnp.zeros((npad, ATT_W, ATT_Q), BF16)
    vtpad_ref[npad:] = vt_ref[0]

    iota = lambda shape, dim: lax.broadcasted_iota(jnp.int32, shape, dim)
    q_shift = ATT_Q.bit_length() - 1
    d_shift = HEAD_DIM.bit_length() - 1
    qb_mask = (iota((ATT_L, ATT_W), 0) >> q_shift) == (iota((ATT_L, ATT_W), 1) >> d_shift)
    ot_mask = (iota((ATT_W, ATT_L), 0) >> d_shift) == (iota((ATT_W, ATT_L), 1) >> q_shift)
    sel = jnp.where((iota((ATT_Q, ATT_L), 1) & (ATT_Q - 1)) == iota((ATT_Q, ATT_L), 0),
                    1.0, 0.0).astype(BF16)
    key_row = lax.broadcasted_iota(jnp.int32, (ATT_BAND, ATT_L), 0)
    contract_last = (((1,), (1,)), ((), ()))

    def step(m, carry):
        r0 = pl.multiple_of(m * ATT_Q, ATT_Q)
        qt = q_ref[0, pl.ds(r0, ATT_Q), :]
        qb = jnp.where(qb_mask, jnp.concatenate([qt] * ATT_HEADS, axis=0), 0)
        kb = kpad_ref[pl.ds(r0, ATT_BAND), :]
        st = lax.dot_general(kb, qb.astype(BF16), contract_last,
                             preferred_element_type=F32)
        st = st + bias_ref[0]
        st = jnp.where(key_row >= BAND_PAD - r0, st, NEG_INF)
        mx = jnp.max(st, axis=0, keepdims=True)
        p = jnp.exp(st - mx)
        den = jnp.sum(p, axis=0, keepdims=True)
        pb = p.astype(BF16)
        vb = jnp.concatenate([vtpad_ref[m + c] for c in range(ATT_BAND // ATT_Q)],
                             axis=1)
        ot = jnp.dot(vb, pb, preferred_element_type=F32)
        ot = jnp.where(ot_mask, ot / den, 0.0).astype(BF16)
        y = lax.dot_general(sel, ot, contract_last, preferred_element_type=F32)
        o_ref[0, pl.ds(r0, ATT_Q), :] = y.astype(BF16)
        return carry

    lax.fori_loop(0, s // ATT_Q, step, 0)


def _attn_bias_t(rel_bias):
    rb = rel_bias.astype(F32)
    nu = ATT_BAND + ATT_Q - 1
    n_low = BAND_PAD - MAX_REL + ATT_Q
    t = jnp.concatenate([jnp.repeat(rb[:, :1], n_low, axis=1),
                         rb[:, 1:1 + nu - n_low]], axis=1)
    tp = jnp.pad(t, ((0, 0), (0, 1)))
    skew = jnp.tile(tp, (1, ATT_Q))[:, :ATT_Q * nu].reshape(N_HEADS, ATT_Q, nu)
    bias = skew[:, :, ATT_Q - 1:]
    bias = bias.reshape(ATT_GROUPS, ATT_HEADS, ATT_Q, ATT_BAND)
    bias = bias.transpose(0, 3, 1, 2).reshape(ATT_GROUPS, ATT_BAND, ATT_L)
    r = jnp.arange(ATT_BAND)[:, None]
    qq = jnp.arange(ATT_L)[None, :] % ATT_Q
    first = (qq // CHUNK) * CHUNK
    valid = (r >= first) & (r < first + BAND_PAD + CHUNK)
    return jnp.where(valid[None], bias, NEG_INF)


def _attention(qn, kn, vt, rel_bias):
    b, s, _ = qn.shape
    bias_t = _attn_bias_t(rel_bias)
    nck = s // ATT_Q
    return pl.pallas_call(
        _attn_kernel,
        out_shape=jax.ShapeDtypeStruct((b, s, D_ATT), BF16),
        grid=(b, ATT_GROUPS),
        in_specs=[
            pl.BlockSpec((1, s, ATT_W), lambda bi, gi: (bi, 0, gi)),
            pl.BlockSpec((1, s, ATT_W), lambda bi, gi: (bi, 0, gi)),
            pl.BlockSpec((1, nck, ATT_W, ATT_Q), lambda bi, gi: (bi, 0, gi, 0)),
            pl.BlockSpec((1, ATT_BAND, ATT_L), lambda bi, gi: (gi, 0, 0)),
        ],
        out_specs=pl.BlockSpec((1, s, ATT_W), lambda bi, gi: (bi, 0, gi)),
        scratch_shapes=[
            pltpu.VMEM((s + BAND_PAD, ATT_W), BF16),
            pltpu.VMEM((nck + BAND_PAD // ATT_Q, ATT_W, ATT_Q), BF16),
        ],
        compiler_params=_cparams(2),
        name="band_attention",
    )(qn, kn, vt, bias_t)


def _mix_out_kernel(routed, x_ref, yc_ref, ya_ref, w_ref, g1_ref, gf_ref, sc_ref, sh_ref,
                    *rest):
    if routed:
        wr_ref, br_ref, x1_ref, h_ref, lg_ref, wbf_ref = rest
    else:
        x1_ref, h_ref, wbf_ref = rest
    first = jnp.logical_and(pl.program_id(0) == 0, pl.program_id(1) == 0)

    @pl.when(first)
    def _():
        wbf_ref[...] = w_ref[...].astype(BF16)

    y = jnp.dot(yc_ref[0], wbf_ref[0:D_CONV, :], preferred_element_type=F32)
    y = y + jnp.dot(ya_ref[0], wbf_ref[D_CONV:, :], preferred_element_type=F32)
    x1 = x_ref[0] + g1_ref[0] * y
    x1_ref[0] = x1
    h = _rms_mod(x1, gf_ref[...], sc_ref[0], sh_ref[0])
    if routed:
        h_ref[0] = _pack_bf16_pairs(h)
        lg_ref[0] = jnp.dot(h.astype(BF16), wr_ref[...],
                            preferred_element_type=F32) + br_ref[...]
    else:
        h_ref[0] = h.astype(BF16)


def _mix_out(x, yc, ya, w_out, g1, gf, sc, sh, tm, router=None):
    b, s, d = x.shape
    row = lambda bi, ti: (bi, ti, 0)
    per_b = lambda bi, ti: (bi, 0, 0)
    const2 = lambda bi, ti: (0, 0)
    in_specs = [
        pl.BlockSpec((1, tm, d), row),
        pl.BlockSpec((1, tm, D_CONV), row),
        pl.BlockSpec((1, tm, D_ATT), row),
        pl.BlockSpec((d, d), const2),
        pl.BlockSpec((1, 1, d), per_b),
        pl.BlockSpec((1, d), const2),
        pl.BlockSpec((1, 1, d), per_b),
        pl.BlockSpec((1, 1, d), per_b),
    ]
    args = [x, yc, ya, w_out, g1, gf.reshape(1, d), sc, sh]
    if router is None:
        out_shape = (jax.ShapeDtypeStruct((b, s, d), F32),
                     jax.ShapeDtypeStruct((b, s, d), BF16))
        out_specs = (pl.BlockSpec((1, tm, d), row), pl.BlockSpec((1, tm, d), row))
    else:
        w_router, b_router = router
        w_pad = jnp.zeros((d, LANE_PAD_E), BF16).at[:, :N_EXPERTS].set(w_router.astype(BF16))
        b_pad = jnp.full((1, LANE_PAD_E), -jnp.inf, F32).at[0, :N_EXPERTS].set(
            b_router.astype(F32))
        in_specs += [pl.BlockSpec((d, LANE_PAD_E), const2), pl.BlockSpec((1, LANE_PAD_E), const2)]
        args += [w_pad, b_pad]
        out_shape = (jax.ShapeDtypeStruct((b, s, d), F32),
                     jax.ShapeDtypeStruct((b, s, d // 2), jnp.int32),
                     jax.ShapeDtypeStruct((b, s, LANE_PAD_E), F32))
        out_specs = (pl.BlockSpec((1, tm, d), row), pl.BlockSpec((1, tm, d // 2), row),
                     pl.BlockSpec((1, tm, LANE_PAD_E), row))
    return pl.pallas_call(
        functools.partial(_mix_out_kernel, router is not None),
        out_shape=out_shape,
        grid=(b, s // tm),
        in_specs=in_specs,
        out_specs=out_specs,
        scratch_shapes=[pltpu.VMEM((d, d), BF16)],
        compiler_params=_cparams(2),
        name="mix_out",
    )(*args)


def _ffn_kernel(h_ref, x_ref, g2_ref, wg_ref, wu_ref, wd_ref, o_ref, acc_ref):
    f = pl.program_id(2)
    h = h_ref[0]
    a = jnp.dot(h, wg_ref[...].astype(BF16), preferred_element_type=F32)
    u = jnp.dot(h, wu_ref[...].astype(BF16), preferred_element_type=F32)
    t = (_silu(a) * u).astype(BF16)
    dn = jnp.dot(t, wd_ref[...].astype(BF16), preferred_element_type=F32)

    @pl.when(f == 0)
    def _():
        acc_ref[...] = dn

    @pl.when(f > 0)
    def _():
        acc_ref[...] += dn

    @pl.when(f == pl.num_programs(2) - 1)
    def _():
        o_ref[0] = x_ref[0] + g2_ref[0] * acc_ref[...]


def _dense_ffn(h, x1, g2, wg, wu, wd, tm, fc):
    b, s, d = x1.shape
    ff = wg.shape[1]
    row = lambda bi, ti, fi: (bi, ti, 0)
    return pl.pallas_call(
        _ffn_kernel,
        out_shape=jax.ShapeDtypeStruct((b, s, d), F32),
        grid=(b, s // tm, ff // fc),
        in_specs=[
            pl.BlockSpec((1, tm, d), row),
            pl.BlockSpec((1, tm, d), row),
            pl.BlockSpec((1, 1, d), lambda bi, ti, fi: (bi, 0, 0)),
            pl.BlockSpec((d, fc), lambda bi, ti, fi: (0, fi)),
            pl.BlockSpec((d, fc), lambda bi, ti, fi: (0, fi)),
            pl.BlockSpec((fc, d), lambda bi, ti, fi: (fi, 0)),
        ],
        out_specs=pl.BlockSpec((1, tm, d), row),
        scratch_shapes=[pltpu.VMEM((tm, d), F32)],
        compiler_params=_cparams(3),
        name="dense_ffn",
    )(h, x1, g2, wg, wu, wd)


def _router_kernel(lg_ref, info_ref, cnt_ref, carry_ref):
    i = pl.program_id(0)

    @pl.when(i == 0)
    def _():
        carry_ref[...] = jnp.zeros_like(carry_ref)

    tr = lg_ref.shape[0]
    logits = lg_ref[...]
    lane = lax.broadcasted_iota(jnp.int32, (tr, LANE_PAD_E), 1).astype(F32)
    no_lane = float(LANE_PAD_E)
    v0 = jnp.max(logits, axis=-1, keepdims=True)
    i0 = jnp.min(jnp.where(logits == v0, lane, no_lane), axis=-1, keepdims=True)
    rest = jnp.where(lane == i0, -jnp.inf, logits)
    v1 = jnp.max(rest, axis=-1, keepdims=True)
    i1 = jnp.min(jnp.where(rest == v1, lane, no_lane), axis=-1, keepdims=True)
    e1 = jnp.exp(v1 - v0)
    w0 = 1.0 / (1.0 + e1)
    w1 = e1 / (1.0 + e1)
    oh0 = lane == i0
    oh1 = lane == i1
    cnt = jnp.where(jnp.logical_or(oh0, oh1), 1.0, 0.0)
    tri = (lax.broadcasted_iota(jnp.int32, (tr, tr), 1)
           < lax.broadcasted_iota(jnp.int32, (tr, tr), 0)).astype(BF16)
    before = jnp.dot(tri, cnt.astype(BF16), preferred_element_type=F32) + carry_ref[0:1, :]
    r0 = jnp.sum(jnp.where(oh0, before, 0.0), axis=-1, keepdims=True)
    r1 = jnp.sum(jnp.where(oh1, before, 0.0), axis=-1, keepdims=True)
    carry_ref[...] = carry_ref[...] + jnp.sum(cnt, axis=0, keepdims=True)
    cnt_ref[...] = carry_ref[...]
    info = jnp.where(lane == 0, i0, 0.0)
    info = jnp.where(lane == 1, i1, info)
    info = jnp.where(lane == 2, w0, info)
    info = jnp.where(lane == 3, w1, info)
    info = jnp.where(lane == 4, r0, info)
    info = jnp.where(lane == 5, r1, info)
    info_ref[...] = info


def _router(logits, tr):
    n = logits.shape[0]
    return pl.pallas_call(
        _router_kernel,
        out_shape=(jax.ShapeDtypeStruct((n, LANE_PAD_E), F32),
                   jax.ShapeDtypeStruct((8, LANE_PAD_E), F32)),
        grid=(n // tr,),
        in_specs=[pl.BlockSpec((tr, LANE_PAD_E), lambda i: (i, 0))],
        out_specs=(pl.BlockSpec((tr, LANE_PAD_E), lambda i: (i, 0)),
                   pl.BlockSpec((8, LANE_PAD_E), lambda i: (0, 0))),
        scratch_shapes=[pltpu.VMEM((8, LANE_PAD_E), F32)],
        compiler_params=_cparams(1),
        name="router",
    )(logits)


def _moe_kernel(te_ref, tn_ref, xs_ref, wg_ref, wu_ref, wd_ref, ys_ref,
                acc_ref, xb_ref, wgb_ref, wub_ref, wdb_ref):
    i = pl.program_id(0)
    f = pl.program_id(1)
    nsub = tn_ref[i]

    @pl.when(f == 0)
    def _():
        acc_ref[...] = jnp.zeros_like(acc_ref)

        def unpack(sidx, carry):
            r = pl.multiple_of(sidx * MOE_SUB, MOE_SUB)
            xb_ref[pl.ds(r, MOE_SUB), :] = _unpack_bf16_pairs(
                xs_ref[pl.ds(r, MOE_SUB), :]).astype(BF16)
            return carry

        lax.fori_loop(0, nsub, unpack, 0)

    @pl.when(nsub > 0)
    def _():
        wgb_ref[...] = wg_ref[0].astype(BF16)
        wub_ref[...] = wu_ref[0].astype(BF16)
        wdb_ref[...] = wd_ref[0].astype(BF16)

    def sub(sidx, carry):
        r = pl.multiple_of(sidx * MOE_SUB, MOE_SUB)
        xb = xb_ref[pl.ds(r, MOE_SUB), :]
        a = jnp.dot(xb, wgb_ref[...], preferred_element_type=F32)
        u = jnp.dot(xb, wub_ref[...], preferred_element_type=F32)
        t = (_silu(a) * u).astype(BF16)
        acc_ref[pl.ds(r, MOE_SUB), :] += jnp.dot(t, wdb_ref[...], preferred_element_type=F32)
        return carry

    lax.fori_loop(0, nsub, sub, 0)

    @pl.when(f == pl.num_programs(1) - 1)
    def _():
        ys_ref[...] = _pack_bf16_pairs(acc_ref[...])


def _moe_ffn(xs, tile_e, tile_nsub, wg, wu, wd):
    rpad, dw = xs.shape
    d = 2 * dw
    ntiles = rpad // MOE_TILE
    ff = wg.shape[2]
    nf = ff // MOE_FC

    def fcol(i, f, tn):
        return jnp.where(tn[i] > 0, f, nf - 1)

    return pl.pallas_call(
        _moe_kernel,
        out_shape=jax.ShapeDtypeStruct((rpad, dw), jnp.int32),
        grid_spec=pltpu.PrefetchScalarGridSpec(
            num_scalar_prefetch=2,
            grid=(ntiles, nf),
            in_specs=[
                pl.BlockSpec((MOE_TILE, dw), lambda i, f, te, tn: (i, 0)),
                pl.BlockSpec((1, d, MOE_FC), lambda i, f, te, tn: (te[i], 0, fcol(i, f, tn))),
                pl.BlockSpec((1, d, MOE_FC), lambda i, f, te, tn: (te[i], 0, fcol(i, f, tn))),
                pl.BlockSpec((1, MOE_FC, d), lambda i, f, te, tn: (te[i], fcol(i, f, tn), 0)),
            ],
            out_specs=pl.BlockSpec((MOE_TILE, dw), lambda i, f, te, tn: (i, 0)),
            scratch_shapes=[
                pltpu.VMEM((MOE_TILE, d), F32),
                pltpu.VMEM((MOE_TILE, d), BF16),
                pltpu.VMEM((d, MOE_FC), BF16),
                pltpu.VMEM((d, MOE_FC), BF16),
                pltpu.VMEM((MOE_FC, d), BF16),
            ],
        ),
        compiler_params=_cparams(2),
        name="moe_ffn",
    )(tile_e, tile_nsub, xs, wg, wu, wd)


def _combine_kernel(x_ref, y0_ref, y1_ref, info_ref, g2_ref, o_ref):
    info = info_ref[0]
    w0 = info[:, 2:3]
    w1 = info[:, 3:4]
    f = w0 * _unpack_bf16_pairs(y0_ref[0, 0]) + w1 * _unpack_bf16_pairs(y1_ref[0, 0])
    o_ref[0] = x_ref[0] + g2_ref[0] * f


def _combine(x1, y01, info, g2, tm):
    b, s, d = x1.shape
    row = lambda bi, ti: (bi, ti, 0)
    return pl.pallas_call(
        _combine_kernel,
        out_shape=jax.ShapeDtypeStruct((b, s, d), F32),
        grid=(b, s // tm),
        in_specs=[
            pl.BlockSpec((1, tm, d), row),
            pl.BlockSpec((1, 1, tm, d // 2), lambda bi, ti: (0, bi, ti, 0)),
            pl.BlockSpec((1, 1, tm, d // 2), lambda bi, ti: (1, bi, ti, 0)),
            pl.BlockSpec((1, tm, LANE_PAD_E), row),
            pl.BlockSpec((1, 1, d), lambda bi, ti: (bi, 0, 0)),
        ],
        out_specs=pl.BlockSpec((1, tm, d), row),
        compiler_params=_cparams(2),
        name="moe_combine",
    )(x1, y01, y01, info, g2)


def _moe_layer(hp, logits, x1, g2, wg, wu, wd, tm):
    b, s, d = x1.shape
    n = b * s
    info, cnt = _router(logits.reshape(n, LANE_PAD_E), min(512, n))
    e0 = info[:, 0].astype(jnp.int32)
    e1 = info[:, 1].astype(jnp.int32)
    r0 = info[:, 4].astype(jnp.int32)
    r1 = info[:, 5].astype(jnp.int32)
    counts = cnt[0, :N_EXPERTS].astype(jnp.int32)

    ntiles = (2 * n) // MOE_TILE + N_EXPERTS
    tiles_per_e = (counts + MOE_TILE - 1) // MOE_TILE
    tile_end = jnp.cumsum(tiles_per_e)
    tile_start = tile_end - tiles_per_e
    total = tile_end[-1]
    tidx = jnp.arange(ntiles, dtype=jnp.int32)
    live = tidx < total
    tclip = jnp.minimum(tidx, total - 1)
    tile_e = jnp.minimum(jnp.sum(tclip[:, None] >= tile_end[None, :], axis=1),
                         N_EXPERTS - 1).astype(jnp.int32)
    rows_left = counts[tile_e] - (tclip - tile_start[tile_e]) * MOE_TILE
    rows_here = jnp.clip(rows_left, 0, MOE_TILE)
    tile_nsub = jnp.where(live, (rows_here + MOE_SUB - 1) // MOE_SUB, 0).astype(jnp.int32)

    row_start = tile_start * MOE_TILE
    eid = jnp.arange(N_EXPERTS, dtype=jnp.int32)[None, :]
    pos0 = jnp.sum(jnp.where(e0[:, None] == eid, row_start[None, :], 0), axis=1) + r0
    pos1 = jnp.sum(jnp.where(e1[:, None] == eid, row_start[None, :], 0), axis=1) + r1

    xs = _sc_scatter_rows2(hp.reshape(n, d // 2), pos0, pos1, ntiles * MOE_TILE)
    ys = _moe_ffn(xs, tile_e, tile_nsub, wg, wu, wd)
    y01 = _sc_gather_rows(ys, jnp.concatenate([pos0, pos1]))
    return _combine(x1, y01.reshape(2, b, s, d // 2), info.reshape(b, s, LANE_PAD_E), g2, tm)


def kernel(x, c, w_ada, b_ada, norm_mix_g, norm_ffn_g, w_in, w_out, conv_w, conv_b,
           conv_ln_g, conv_ln_b, q_norm_g, k_norm_g, rel_bias, ffn_w_gate, ffn_w_up,
           ffn_w_down, moe_w_router, moe_b_router, moe_w_gate, moe_w_up, moe_w_down):
    b, s, d = x.shape
    depth = w_ada.shape[0]
    tm = min(512, s)
    mod = _ada_mod(c, w_ada, b_ada)
    for l in range(depth):
        sh1, sc1, g1, sh2, sc2, g2 = [
            mod[l, :, j * d:(j + 1) * d].reshape(b, 1, d) for j in range(6)]
        z, qn, kn, vt = _mix_in(x, sc1, sh1, norm_mix_g[l], w_in[l],
                                q_norm_g[l], k_norm_g[l], tm)
        yc = _conv_branch(z, conv_w[l], conv_b[l], conv_ln_g[l], conv_ln_b[l], min(256, s))
        ya = _attention(qn, kn, vt, rel_bias[l])
        i = l // 2
        if l % 2 == 0:
            x1, h2 = _mix_out(x, yc, ya, w_out[l], g1, norm_ffn_g[l], sc2, sh2, tm)
            x = _dense_ffn(h2, x1, g2, ffn_w_gate[i], ffn_w_up[i], ffn_w_down[i],
                           min(1024, s), 256)
        else:
            x1, hp, logits = _mix_out(x, yc, ya, w_out[l], g1, norm_ffn_g[l], sc2, sh2, tm,
                                      router=(moe_w_router[i], moe_b_router[i]))
            x = _moe_layer(hp, logits, x1, g2, moe_w_gate[i], moe_w_up[i], moe_w_down[i], tm)
    return x
```

```python
import functools

import jax
import jax.numpy as jnp
from jax import lax
from jax.experimental import pallas as pl
from jax.experimental.pallas import tpu as pltpu
from jax.experimental.pallas import tpu_sc as plsc

F32 = jnp.float32
BF16 = jnp.bfloat16

D_MODEL = 1024
CHUNK = 64
N_PREV_CHUNKS = 8
BAND_PAD = N_PREV_CHUNKS * CHUNK
D_CONV = 512
D_ATT = 512
HEAD_DIM = 64
N_HEADS = 8
CONV_WIDTH = 31
MAX_REL = 128
D_IN_COLS = 2 * D_CONV + 3 * D_ATT
N_EXPERTS = 8
EPS = 1e-6
NEG_INF = -1e30

LANES = 128
SUBLANES = 8
VMEM_LIMIT_BYTES = 56 * 1024 * 1024

ATT_HEADS = 4
ATT_GROUPS = N_HEADS // ATT_HEADS
ATT_W = ATT_HEADS * HEAD_DIM
ATT_Q = 2 * CHUNK
ATT_BAND = BAND_PAD + ATT_Q
ATT_L = ATT_HEADS * ATT_Q

CONV_HALO = 32
CONV_ROWS = 32
LANE_PAD_E = LANES

MOE_SUB = 512
MOE_HALF = MOE_SUB // 2
MOE_TILE = 9 * MOE_HALF
MOE_FC = 512


def _cparams(n_axes, vmem=VMEM_LIMIT_BYTES):
    return pltpu.CompilerParams(
        dimension_semantics=("arbitrary",) * n_axes, vmem_limit_bytes=vmem)


def _silu(v):
    return v * jax.nn.sigmoid(v)


def _pack_bf16_pairs(v):
    w = v.shape[1] // 2
    bits = lax.bitcast_convert_type(v.astype(BF16).astype(F32), jnp.uint32)
    packed = (bits[:, w:] & jnp.uint32(0xFFFF0000)) | (bits[:, :w] >> 16)
    return lax.bitcast_convert_type(packed, jnp.int32)


def _unpack_bf16_pairs(p):
    bits = lax.bitcast_convert_type(p, jnp.uint32)
    lo = lax.bitcast_convert_type(bits << 16, F32)
    hi = lax.bitcast_convert_type(bits & jnp.uint32(0xFFFF0000), F32)
    return jnp.concatenate([lo, hi], axis=1)


SC_CORES = 2
SC_SUBCORES = 16
SC_WORKERS = SC_CORES * SC_SUBCORES
SC_CHUNK = 64


def _sc_worker_id():
    return lax.axis_index("s") * SC_CORES + lax.axis_index("c")


def _sc_mesh():
    return plsc.VectorSubcoreMesh(core_axis_name="c", subcore_axis_name="s")


def _sc_gather_rows(table, idx):
    _, w = table.shape
    b = idx.shape[0]
    per_w = b // SC_WORKERS
    nch = per_w // SC_CHUNK

    def body(table_hbm, idx_hbm, out_hbm, idx_v, rows_v, gsem, wsem):
        wid = _sc_worker_id()
        base = wid * per_w
        pltpu.sync_copy(idx_hbm.at[wid], idx_v)
        gathers = [None] * nch
        writes = [None] * nch
        gathers[0] = pltpu.async_copy(table_hbm.at[idx_v.at[0]], rows_v.at[0], gsem.at[0])
        for c in range(nch):
            slot = c % 2
            gathers[c].wait()
            if c + 1 < nch:
                if c >= 1:
                    writes[c - 1].wait()
                gathers[c + 1] = pltpu.async_copy(
                    table_hbm.at[idx_v.at[c + 1]], rows_v.at[1 - slot], gsem.at[1 - slot])
            writes[c] = pltpu.async_copy(
                rows_v.at[slot], out_hbm.at[pl.ds(base + c * SC_CHUNK, SC_CHUNK)], wsem.at[slot])
        if nch >= 2:
            writes[nch - 2].wait()
        writes[nch - 1].wait()

    call = pl.kernel(
        body, mesh=_sc_mesh(),
        out_type=jax.ShapeDtypeStruct((b, w), jnp.int32),
        scratch_types=[pltpu.VMEM((nch, SC_CHUNK), jnp.int32),
                       pltpu.VMEM((2, SC_CHUNK, w), jnp.int32),
                       pltpu.SemaphoreType.DMA((2,)), pltpu.SemaphoreType.DMA((2,))],
        name="sc_gather_rows")
    return call(table, idx.reshape(SC_WORKERS, nch, SC_CHUNK))


def _sc_scatter_rows2(src, idx0, idx1, rows_out):
    n, w = src.shape
    per_w = n // SC_WORKERS
    nch = per_w // SC_CHUNK

    def body(src_hbm, i0_hbm, i1_hbm, out_hbm, i0_v, i1_v, rows_v, rsem, wsem):
        wid = _sc_worker_id()
        base = wid * per_w
        pltpu.sync_copy(i0_hbm.at[wid], i0_v)
        pltpu.sync_copy(i1_hbm.at[wid], i1_v)
        reads = [None] * nch
        writes = [None] * nch
        reads[0] = pltpu.async_copy(src_hbm.at[pl.ds(base, SC_CHUNK)], rows_v.at[0], rsem.at[0])
        for c in range(nch):
            slot = c % 2
            reads[c].wait()
            if c + 1 < nch:
                if c >= 1:
                    for wr in writes[c - 1]:
                        wr.wait()
                reads[c + 1] = pltpu.async_copy(
                    src_hbm.at[pl.ds(base + (c + 1) * SC_CHUNK, SC_CHUNK)],
                    rows_v.at[1 - slot], rsem.at[1 - slot])
            writes[c] = (
                pltpu.async_copy(rows_v.at[slot], out_hbm.at[i0_v.at[c]], wsem.at[slot, 0]),
                pltpu.async_copy(rows_v.at[slot], out_hbm.at[i1_v.at[c]], wsem.at[slot, 1]),
            )
        for c in range(max(nch - 2, 0), nch):
            for wr in writes[c]:
                wr.wait()

    call = pl.kernel(
        body, mesh=_sc_mesh(),
        out_type=jax.ShapeDtypeStruct((rows_out, w), jnp.int32),
        scratch_types=[pltpu.VMEM((nch, SC_CHUNK), jnp.int32),
                       pltpu.VMEM((nch, SC_CHUNK), jnp.int32),
                       pltpu.VMEM((2, SC_CHUNK, w), jnp.int32),
                       pltpu.SemaphoreType.DMA((2,)), pltpu.SemaphoreType.DMA((2, 2))],
        name="sc_scatter_rows")
    shape3 = (SC_WORKERS, nch, SC_CHUNK)
    return call(src, idx0.reshape(shape3), idx1.reshape(shape3))


def _ada_kernel(c_ref, w_ref, b_ref, o_ref):
    ca = _silu(c_ref[...]).astype(BF16)
    w = w_ref[0].astype(BF16)
    o_ref[0] = jnp.dot(ca, w, preferred_element_type=F32) + b_ref[0]


def _ada_mod(c, w_ada, b_ada):
    depth, d, n6 = w_ada.shape
    b = c.shape[0]
    rows = 16
    c_pad = jnp.zeros((rows, d), F32).at[:b].set(c)
    tn = 1536
    out = pl.pallas_call(
        _ada_kernel,
        out_shape=jax.ShapeDtypeStruct((depth, rows, n6), F32),
        grid=(depth, n6 // tn),
        in_specs=[
            pl.BlockSpec((rows, d), lambda l, j: (0, 0)),
            pl.BlockSpec((1, d, tn), lambda l, j: (l, 0, j)),
            pl.BlockSpec((1, 1, tn), lambda l, j: (l, 0, j)),
        ],
        out_specs=pl.BlockSpec((1, rows, tn), lambda l, j: (l, 0, j)),
        compiler_params=_cparams(2),
        name="ada_mod",
    )(c_pad, w_ada, b_ada.reshape(depth, 1, n6))
    return out[:, :b]


def _rms_mod(xf, g, sc, sh):
    ms = jnp.mean(xf * xf, axis=-1, keepdims=True)
    return xf * lax.rsqrt(ms + EPS) * g * (1.0 + sc) + sh


def _mix_in_kernel(x_ref, sc_ref, sh_ref, g_ref, w_ref, gq_ref, gk_ref, ones_ref,
                   z_ref, q_ref, k_ref, vt_ref, wbf_ref):
    first = jnp.logical_and(pl.program_id(0) == 0, pl.program_id(1) == 0)

    @pl.when(first)
    def _():
        wbf_ref[...] = w_ref[...].astype(BF16)

    h = _rms_mod(x_ref[0], g_ref[...], sc_ref[0], sh_ref[0]).astype(BF16)
    proj = jnp.dot(h, wbf_ref[...], preferred_element_type=F32)

    a = proj[:, :D_CONV]
    gate = proj[:, D_CONV:2 * D_CONV]
    z_ref[0] = (a * jax.nn.sigmoid(gate)).astype(BF16)

    def head_norm(t, g):
        ss = jnp.dot((t * t).astype(BF16), ones_ref[...], preferred_element_type=F32)
        return (t * lax.rsqrt(ss * (1.0 / HEAD_DIM) + EPS) * g).astype(BF16)

    o = 2 * D_CONV
    q_ref[0] = head_norm(proj[:, o:o + D_ATT], gq_ref[...])
    k_ref[0] = head_norm(proj[:, o + D_ATT:o + 2 * D_ATT], gk_ref[...])
    v = proj[:, o + 2 * D_ATT:]
    tm = v.shape[0]
    for cidx in range(tm // ATT_Q):
        vt_ref[0, cidx] = v[cidx * ATT_Q:(cidx + 1) * ATT_Q, :].T.astype(BF16)


def _mix_in(x, sc, sh, g, w_in, gq, gk, tm):
    b, s, d = x.shape
    ones_bd = (jnp.arange(D_ATT)[:, None] // HEAD_DIM
               == jnp.arange(D_ATT)[None, :] // HEAD_DIM).astype(BF16)
    gq_t = (jnp.tile(gq, N_HEADS) * (HEAD_DIM ** -0.5)).reshape(1, D_ATT)
    gk_t = jnp.tile(gk, N_HEADS).reshape(1, D_ATT)
    row = lambda bi, ti: (bi, ti, 0)
    per_b = lambda bi, ti: (bi, 0, 0)
    const2 = lambda bi, ti: (0, 0)
    return pl.pallas_call(
        _mix_in_kernel,
        out_shape=(
            jax.ShapeDtypeStruct((b, s, D_CONV), BF16),
            jax.ShapeDtypeStruct((b, s, D_ATT), BF16),
            jax.ShapeDtypeStruct((b, s, D_ATT), BF16),
            jax.ShapeDtypeStruct((b, s // ATT_Q, D_ATT, ATT_Q), BF16),
        ),
        grid=(b, s // tm),
        in_specs=[
            pl.BlockSpec((1, tm, d), row),
            pl.BlockSpec((1, 1, d), per_b),
            pl.BlockSpec((1, 1, d), per_b),
            pl.BlockSpec((1, d), const2),
            pl.BlockSpec((d, D_IN_COLS), const2),
            pl.BlockSpec((1, D_ATT), const2),
            pl.BlockSpec((1, D_ATT), const2),
            pl.BlockSpec((D_ATT, D_ATT), const2),
        ],
        out_specs=(
            pl.BlockSpec((1, tm, D_CONV), row),
            pl.BlockSpec((1, tm, D_ATT), row),
            pl.BlockSpec((1, tm, D_ATT), row),
            pl.BlockSpec((1, tm // ATT_Q, D_ATT, ATT_Q), lambda bi, ti: (bi, ti, 0, 0)),
        ),
        scratch_shapes=[pltpu.VMEM((d, D_IN_COLS), BF16)],
        compiler_params=_cparams(2),
        name="mix_in",
    )(x, sc, sh, g.reshape(1, d), w_in, gq_t, gk_t, ones_bd)


def _conv_kernel(zc_ref, zp_ref, w_ref, cb_ref, lg_ref, lb_ref, o_ref, win_ref, sh_ref,
                 acc_ref):
    tt = zc_ref.shape[1]
    t = pl.program_id(1)
    halo = zp_ref[0].astype(F32)
    win_ref[0:CONV_HALO, :] = jnp.where(t == 0, 0.0, halo)
    win_ref[CONV_HALO:, :] = zc_ref[0].astype(F32)
    span = tt + CONV_HALO - SUBLANES
    for sft in range(1, SUBLANES):
        sh_ref[sft - 1, 0:span, :] = win_ref[sft:sft + span, :]
    base = CONV_HALO - (CONV_WIDTH - 1)

    def group(gidx, carry):
        r = pl.multiple_of(gidx * CONV_ROWS, CONV_ROWS)
        acc = jnp.zeros((CONV_ROWS, D_CONV), F32) + cb_ref[...]
        for j in range(CONV_WIDTH):
            whole, sft = divmod(base + j, SUBLANES)
            start = pl.multiple_of(r + whole * SUBLANES, SUBLANES)
            if sft == 0:
                tap = win_ref[pl.ds(start, CONV_ROWS), :]
            else:
                tap = sh_ref[sft - 1, pl.ds(start, CONV_ROWS), :]
            acc = acc + tap * w_ref[j:j + 1, :]
        acc_ref[pl.ds(r, CONV_ROWS), :] = acc
        return carry

    lax.fori_loop(0, tt // CONV_ROWS, group, 0)
    acc = acc_ref[...]
    mu = jnp.mean(acc, axis=-1, keepdims=True)
    xc = acc - mu
    var = jnp.mean(xc * xc, axis=-1, keepdims=True)
    y = xc * lax.rsqrt(var + EPS) * lg_ref[...] + lb_ref[...]
    o_ref[0] = _silu(y).astype(BF16)


def _conv_branch(z, conv_w, conv_b, ln_g, ln_b, tt):
    b, s, c = z.shape
    hb = tt // CONV_HALO
    const2 = lambda bi, ti: (0, 0)
    return pl.pallas_call(
        _conv_kernel,
        out_shape=jax.ShapeDtypeStruct((b, s, c), BF16),
        grid=(b, s // tt),
        in_specs=[
            pl.BlockSpec((1, tt, c), lambda bi, ti: (bi, ti, 0)),
            pl.BlockSpec((1, CONV_HALO, c),
                         lambda bi, ti: (bi, jnp.maximum(ti * hb - 1, 0), 0)),
            pl.BlockSpec((CONV_WIDTH, c), const2),
            pl.BlockSpec((1, c), const2),
            pl.BlockSpec((1, c), const2),
            pl.BlockSpec((1, c), const2),
        ],
        out_specs=pl.BlockSpec((1, tt, c), lambda bi, ti: (bi, ti, 0)),
        scratch_shapes=[pltpu.VMEM((tt + CONV_HALO, c), F32),
                        pltpu.VMEM((SUBLANES - 1, tt + CONV_HALO, c), F32),
                        pltpu.VMEM((tt, c), F32)],
        compiler_params=_cparams(2),
        name="conv_branch",
    )(z, z, conv_w.reshape(CONV_WIDTH, c), conv_b.reshape(1, c),
      ln_g.reshape(1, c), ln_b.reshape(1, c))


def _attn_kernel(q_ref, k_ref, vt_ref, bias_ref, o_ref, kpad_ref, vtpad_ref,
                 st0_ref, st1_ref, pb0_ref, pb1_ref, den0_ref, den1_ref):
    st_refs = (st0_ref, st1_ref)
    pb_refs = (pb0_ref, pb1_ref)
    den_refs = (den0_ref, den1_ref)
    s = q_ref.shape[1]
    npad = BAND_PAD // ATT_Q
    kpad_ref[0:BAND_PAD, :] = jnp.zeros((BAND_PAD, ATT_W), BF16)
    kpad_ref[BAND_PAD:, :] = k_ref[0]
    vtpad_ref[0:npad] = jnp.zeros((npad, ATT_W, ATT_Q), BF16)
    vtpad_ref[npad:] = vt_ref[0]

    iota = lambda shape, dim: lax.broadcasted_iota(jnp.int32, shape, dim)
    q_shift = ATT_Q.bit_length() - 1
    d_shift = HEAD_DIM.bit_length() - 1
    qb_mask = (iota((ATT_L, ATT_W), 0) >> q_shift) == (iota((ATT_L, ATT_W), 1) >> d_shift)
    ot_mask = (iota((ATT_W, ATT_L), 0) >> d_shift) == (iota((ATT_W, ATT_L), 1) >> q_shift)
    sel = jnp.where((iota((ATT_Q, ATT_L), 1) & (ATT_Q - 1)) == iota((ATT_Q, ATT_L), 0),
                    1.0, 0.0).astype(BF16)
    key_row = lax.broadcasted_iota(jnp.int32, (ATT_BAND, ATT_L), 0)
    contract_last = (((1,), (1,)), ((), ()))

    def scores(m, p):
        r0 = pl.multiple_of(m * ATT_Q, ATT_Q)
        qt = q_ref[0, pl.ds(r0, ATT_Q), :]
        qb = jnp.where(qb_mask, jnp.concatenate([qt] * ATT_HEADS, axis=0), 0)
        kb = kpad_ref[pl.ds(r0, ATT_BAND), :]
        st_refs[p][...] = lax.dot_general(kb, qb.astype(BF16), contract_last,
                                          preferred_element_type=F32)

    def softmax(m, p, masked):
        st = st_refs[p][...] + bias_ref[0]
        if masked:
            st = jnp.where(key_row >= BAND_PAD - m * ATT_Q, st, NEG_INF)
        mx = jnp.max(st, axis=0, keepdims=True)
        e = jnp.exp(st - mx)
        den_refs[p][...] = jnp.sum(e, axis=0, keepdims=True)
        pb_refs[p][...] = e.astype(BF16)

    def values(m, p):
        r0 = pl.multiple_of(m * ATT_Q, ATT_Q)
        vb = jnp.concatenate([vtpad_ref[m + c] for c in range(ATT_BAND // ATT_Q)],
                             axis=1)
        ot = jnp.dot(vb, pb_refs[p][...], preferred_element_type=F32)
        ot = jnp.where(ot_mask, ot / den_refs[p][...], 0.0).astype(BF16)
        y = lax.dot_general(sel, ot, contract_last, preferred_element_type=F32)
        o_ref[0, pl.ds(r0, ATT_Q), :] = y.astype(BF16)

    n = s // ATT_Q
    n_masked = BAND_PAD // ATT_Q
    scores(0, 0)
    scores(1, 1)
    softmax(0, 0, True)

    def pair(masked, i, carry):
        m = 2 * i
        scores(m, 0)
        softmax(m - 1, 1, masked)
        values(m - 2, 0)
        scores(m + 1, 1)
        softmax(m, 0, masked)
        values(m - 1, 1)
        return carry

    split = min(n // 2, n_masked // 2 + 1)
    lax.fori_loop(1, split, functools.partial(pair, True), 0)
    lax.fori_loop(split, n // 2, functools.partial(pair, False), 0)
    softmax(n - 1, 1, n - 1 < n_masked)
    values(n - 2, 0)
    values(n - 1, 1)


def _attn_bias_t(rel_bias):
    rb = rel_bias.astype(F32)
    nu = ATT_BAND + ATT_Q - 1
    n_low = BAND_PAD - MAX_REL + ATT_Q
    t = jnp.concatenate([jnp.repeat(rb[:, :1], n_low, axis=1),
                         rb[:, 1:1 + nu - n_low]], axis=1)
    tp = jnp.pad(t, ((0, 0), (0, 1)))
    skew = jnp.tile(tp, (1, ATT_Q))[:, :ATT_Q * nu].reshape(N_HEADS, ATT_Q, nu)
    bias = skew[:, :, ATT_Q - 1:]
    bias = bias.reshape(ATT_GROUPS, ATT_HEADS, ATT_Q, ATT_BAND)
    bias = bias.transpose(0, 3, 1, 2).reshape(ATT_GROUPS, ATT_BAND, ATT_L)
    r = jnp.arange(ATT_BAND)[:, None]
    qq = jnp.arange(ATT_L)[None, :] % ATT_Q
    first = (qq // CHUNK) * CHUNK
    valid = (r >= first) & (r < first + BAND_PAD + CHUNK)
    return jnp.where(valid[None], bias, NEG_INF)


def _attention(qn, kn, vt, rel_bias):
    b, s, _ = qn.shape
    bias_t = _attn_bias_t(rel_bias)
    nck = s // ATT_Q
    assert nck >= 4 and nck % 2 == 0, "the attention pipeline runs query steps in pairs"
    return pl.pallas_call(
        _attn_kernel,
        out_shape=jax.ShapeDtypeStruct((b, s, D_ATT), BF16),
        grid=(b, ATT_GROUPS),
        in_specs=[
            pl.BlockSpec((1, s, ATT_W), lambda bi, gi: (bi, 0, gi)),
            pl.BlockSpec((1, s, ATT_W), lambda bi, gi: (bi, 0, gi)),
            pl.BlockSpec((1, nck, ATT_W, ATT_Q), lambda bi, gi: (bi, 0, gi, 0)),
            pl.BlockSpec((1, ATT_BAND, ATT_L), lambda bi, gi: (gi, 0, 0)),
        ],
        out_specs=pl.BlockSpec((1, s, ATT_W), lambda bi, gi: (bi, 0, gi)),
        scratch_shapes=[
            pltpu.VMEM((s + BAND_PAD, ATT_W), BF16),
            pltpu.VMEM((nck + BAND_PAD // ATT_Q, ATT_W, ATT_Q), BF16),
            pltpu.VMEM((ATT_BAND, ATT_L), F32), pltpu.VMEM((ATT_BAND, ATT_L), F32),
            pltpu.VMEM((ATT_BAND, ATT_L), BF16), pltpu.VMEM((ATT_BAND, ATT_L), BF16),
            pltpu.VMEM((1, ATT_L), F32), pltpu.VMEM((1, ATT_L), F32),
        ],
        compiler_params=_cparams(2),
        name="band_attention",
    )(qn, kn, vt, bias_t)


def _mix_out_kernel(routed, x_ref, yc_ref, ya_ref, w_ref, g1_ref, gf_ref, sc_ref, sh_ref,
                    *rest):
    if routed:
        wr_ref, br_ref, x1_ref, h_ref, lg_ref, wbf_ref = rest
    else:
        x1_ref, h_ref, wbf_ref = rest
    first = jnp.logical_and(pl.program_id(0) == 0, pl.program_id(1) == 0)

    @pl.when(first)
    def _():
        wbf_ref[...] = w_ref[...].astype(BF16)

    y = jnp.dot(yc_ref[0], wbf_ref[0:D_CONV, :], preferred_element_type=F32)
    y = y + jnp.dot(ya_ref[0], wbf_ref[D_CONV:, :], preferred_element_type=F32)
    x1 = x_ref[0] + g1_ref[0] * y
    x1_ref[0] = x1
    h = _rms_mod(x1, gf_ref[...], sc_ref[0], sh_ref[0])
    if routed:
        h_ref[0] = _pack_bf16_pairs(h)
        lg_ref[0] = jnp.dot(h.astype(BF16), wr_ref[...],
                            preferred_element_type=F32) + br_ref[...]
    else:
        h_ref[0] = h.astype(BF16)


def _mix_out(x, yc, ya, w_out, g1, gf, sc, sh, tm, router=None):
    b, s, d = x.shape
    row = lambda bi, ti: (bi, ti, 0)
    per_b = lambda bi, ti: (bi, 0, 0)
    const2 = lambda bi, ti: (0, 0)
    in_specs = [
        pl.BlockSpec((1, tm, d), row),
        pl.BlockSpec((1, tm, D_CONV), row),
        pl.BlockSpec((1, tm, D_ATT), row),
        pl.BlockSpec((d, d), const2),
        pl.BlockSpec((1, 1, d), per_b),
        pl.BlockSpec((1, d), const2),
        pl.BlockSpec((1, 1, d), per_b),
        pl.BlockSpec((1, 1, d), per_b),
    ]
    args = [x, yc, ya, w_out, g1, gf.reshape(1, d), sc, sh]
    if router is None:
        out_shape = (jax.ShapeDtypeStruct((b, s, d), F32),
                     jax.ShapeDtypeStruct((b, s, d), BF16))
        out_specs = (pl.BlockSpec((1, tm, d), row), pl.BlockSpec((1, tm, d), row))
    else:
        w_router, b_router = router
        w_pad = jnp.zeros((d, LANE_PAD_E), BF16).at[:, :N_EXPERTS].set(w_router.astype(BF16))
        b_pad = jnp.full((1, LANE_PAD_E), -jnp.inf, F32).at[0, :N_EXPERTS].set(
            b_router.astype(F32))
        in_specs += [pl.BlockSpec((d, LANE_PAD_E), const2), pl.BlockSpec((1, LANE_PAD_E), const2)]
        args += [w_pad, b_pad]
        out_shape = (jax.ShapeDtypeStruct((b, s, d), F32),
                     jax.ShapeDtypeStruct((b, s, d // 2), jnp.int32),
                     jax.ShapeDtypeStruct((b, s, LANE_PAD_E), F32))
        out_specs = (pl.BlockSpec((1, tm, d), row), pl.BlockSpec((1, tm, d // 2), row),
                     pl.BlockSpec((1, tm, LANE_PAD_E), row))
    return pl.pallas_call(
        functools.partial(_mix_out_kernel, router is not None),
        out_shape=out_shape,
        grid=(b, s // tm),
        in_specs=in_specs,
        out_specs=out_specs,
        scratch_shapes=[pltpu.VMEM((d, d), BF16)],
        compiler_params=_cparams(2),
        name="mix_out",
    )(*args)


def _ffn_kernel(h_ref, x_ref, g2_ref, wg_ref, wu_ref, wd_ref, o_ref, acc_ref):
    f = pl.program_id(2)

    @pl.when(f == 0)
    def _():
        acc_ref[...] = jnp.zeros_like(acc_ref)

    h = h_ref[0]
    a = jnp.dot(h, wg_ref[...].astype(BF16), preferred_element_type=F32)
    u = jnp.dot(h, wu_ref[...].astype(BF16), preferred_element_type=F32)
    t = (_silu(a) * u).astype(BF16)
    acc_ref[...] += jnp.dot(t, wd_ref[...].astype(BF16), preferred_element_type=F32)

    @pl.when(f == pl.num_programs(2) - 1)
    def _():
        o_ref[0] = x_ref[0] + g2_ref[0] * acc_ref[...]


def _dense_ffn(h, x1, g2, wg, wu, wd, tm, fc):
    b, s, d = x1.shape
    ff = wg.shape[1]
    row = lambda bi, ti, fi: (bi, ti, 0)
    return pl.pallas_call(
        _ffn_kernel,
        out_shape=jax.ShapeDtypeStruct((b, s, d), F32),
        grid=(b, s // tm, ff // fc),
        in_specs=[
            pl.BlockSpec((1, tm, d), row),
            pl.BlockSpec((1, tm, d), row),
            pl.BlockSpec((1, 1, d), lambda bi, ti, fi: (bi, 0, 0)),
            pl.BlockSpec((d, fc), lambda bi, ti, fi: (0, fi)),
            pl.BlockSpec((d, fc), lambda bi, ti, fi: (0, fi)),
            pl.BlockSpec((fc, d), lambda bi, ti, fi: (fi, 0)),
        ],
        out_specs=pl.BlockSpec((1, tm, d), row),
        scratch_shapes=[pltpu.VMEM((tm, d), F32)],
        compiler_params=_cparams(3),
        name="dense_ffn",
    )(h, x1, g2, wg, wu, wd)


def _router_kernel(lg_ref, info_ref, cnt_ref, carry_ref):
    i = pl.program_id(0)

    @pl.when(i == 0)
    def _():
        carry_ref[...] = jnp.zeros_like(carry_ref)

    tr = lg_ref.shape[0]
    logits = lg_ref[...]
    lane = lax.broadcasted_iota(jnp.int32, (tr, LANE_PAD_E), 1).astype(F32)
    no_lane = float(LANE_PAD_E)
    v0 = jnp.max(logits, axis=-1, keepdims=True)
    i0 = jnp.min(jnp.where(logits == v0, lane, no_lane), axis=-1, keepdims=True)
    rest = jnp.where(lane == i0, -jnp.inf, logits)
    v1 = jnp.max(rest, axis=-1, keepdims=True)
    i1 = jnp.min(jnp.where(rest == v1, lane, no_lane), axis=-1, keepdims=True)
    e1 = jnp.exp(v1 - v0)
    w0 = 1.0 / (1.0 + e1)
    w1 = e1 / (1.0 + e1)
    oh0 = lane == i0
    oh1 = lane == i1
    cnt = jnp.where(jnp.logical_or(oh0, oh1), 1.0, 0.0)
    tri = (lax.broadcasted_iota(jnp.int32, (tr, tr), 1)
           < lax.broadcasted_iota(jnp.int32, (tr, tr), 0)).astype(BF16)
    before = jnp.dot(tri, cnt.astype(BF16), preferred_element_type=F32) + carry_ref[0:1, :]
    r0 = jnp.sum(jnp.where(oh0, before, 0.0), axis=-1, keepdims=True)
    r1 = jnp.sum(jnp.where(oh1, before, 0.0), axis=-1, keepdims=True)
    carry_ref[...] = carry_ref[...] + jnp.sum(cnt, axis=0, keepdims=True)
    cnt_ref[...] = carry_ref[...]
    info = jnp.where(lane == 0, i0, 0.0)
    info = jnp.where(lane == 1, i1, info)
    info = jnp.where(lane == 2, w0, info)
    info = jnp.where(lane == 3, w1, info)
    info = jnp.where(lane == 4, r0, info)
    info = jnp.where(lane == 5, r1, info)
    info_ref[...] = info


def _router(logits, tr):
    n = logits.shape[0]
    return pl.pallas_call(
        _router_kernel,
        out_shape=(jax.ShapeDtypeStruct((n, LANE_PAD_E), F32),
                   jax.ShapeDtypeStruct((8, LANE_PAD_E), F32)),
        grid=(n // tr,),
        in_specs=[pl.BlockSpec((tr, LANE_PAD_E), lambda i: (i, 0))],
        out_specs=(pl.BlockSpec((tr, LANE_PAD_E), lambda i: (i, 0)),
                   pl.BlockSpec((8, LANE_PAD_E), lambda i: (0, 0))),
        scratch_shapes=[pltpu.VMEM((8, LANE_PAD_E), F32)],
        compiler_params=_cparams(1),
        name="router",
    )(logits)


def _moe_kernel(te_ref, tn_ref, xs_ref, wg_ref, wu_ref, wd_ref, ys_ref,
                acc_ref, xb_ref, wgb_ref, wub_ref, wdb_ref):
    i = pl.program_id(0)
    f = pl.program_id(1)
    nhalf = tn_ref[i]
    nfull = nhalf >> 1

    @pl.when(f == 0)
    def _():
        acc_ref[...] = jnp.zeros_like(acc_ref)

        def unpack(sidx, carry):
            r = pl.multiple_of(sidx * MOE_HALF, MOE_HALF)
            xb_ref[pl.ds(r, MOE_HALF), :] = _unpack_bf16_pairs(
                xs_ref[pl.ds(r, MOE_HALF), :]).astype(BF16)
            return carry

        lax.fori_loop(0, nhalf, unpack, 0)

    @pl.when(nhalf > 0)
    def _():
        wgb_ref[...] = wg_ref[0].astype(BF16)
        wub_ref[...] = wu_ref[0].astype(BF16)
        wdb_ref[...] = wd_ref[0].astype(BF16)

    def block(r, rows):
        xb = xb_ref[pl.ds(r, rows), :]
        a = jnp.dot(xb, wgb_ref[...], preferred_element_type=F32)
        u = jnp.dot(xb, wub_ref[...], preferred_element_type=F32)
        t = (_silu(a) * u).astype(BF16)
        acc_ref[pl.ds(r, rows), :] += jnp.dot(t, wdb_ref[...], preferred_element_type=F32)

    def sub(sidx, carry):
        block(pl.multiple_of(sidx * MOE_SUB, MOE_SUB), MOE_SUB)
        return carry

    lax.fori_loop(0, nfull, sub, 0)

    @pl.when((nhalf & 1) == 1)
    def _():
        block(pl.multiple_of(nfull * MOE_SUB, MOE_SUB), MOE_HALF)

    @pl.when(f == pl.num_programs(1) - 1)
    def _():
        ys_ref[...] = _pack_bf16_pairs(acc_ref[...])


def _moe_ffn(xs, tile_e, tile_nsub, wg, wu, wd):
    rpad, dw = xs.shape
    d = 2 * dw
    ntiles = rpad // MOE_TILE
    ff = wg.shape[2]
    nf = ff // MOE_FC

    def fcol(i, f, tn):
        return jnp.where(tn[i] > 0, f, nf - 1)

    return pl.pallas_call(
        _moe_kernel,
        out_shape=jax.ShapeDtypeStruct((rpad, dw), jnp.int32),
        grid_spec=pltpu.PrefetchScalarGridSpec(
            num_scalar_prefetch=2,
            grid=(ntiles, nf),
            in_specs=[
                pl.BlockSpec((MOE_TILE, dw), lambda i, f, te, tn: (i, 0)),
                pl.BlockSpec((1, d, MOE_FC), lambda i, f, te, tn: (te[i], 0, fcol(i, f, tn))),
                pl.BlockSpec((1, d, MOE_FC), lambda i, f, te, tn: (te[i], 0, fcol(i, f, tn))),
                pl.BlockSpec((1, MOE_FC, d), lambda i, f, te, tn: (te[i], fcol(i, f, tn), 0)),
            ],
            out_specs=pl.BlockSpec((MOE_TILE, dw), lambda i, f, te, tn: (i, 0)),
            scratch_shapes=[
                pltpu.VMEM((MOE_TILE, d), F32),
                pltpu.VMEM((MOE_TILE, d), BF16),
                pltpu.VMEM((d, MOE_FC), BF16),
                pltpu.VMEM((d, MOE_FC), BF16),
                pltpu.VMEM((MOE_FC, d), BF16),
            ],
        ),
        compiler_params=_cparams(2),
        name="moe_ffn",
    )(tile_e, tile_nsub, xs, wg, wu, wd)


def _combine_kernel(x_ref, y0_ref, y1_ref, info_ref, g2_ref, o_ref):
    info = info_ref[0]
    w0 = info[:, 2:3]
    w1 = info[:, 3:4]
    f = w0 * _unpack_bf16_pairs(y0_ref[0, 0]) + w1 * _unpack_bf16_pairs(y1_ref[0, 0])
    o_ref[0] = x_ref[0] + g2_ref[0] * f


def _combine(x1, y01, info, g2, tm):
    b, s, d = x1.shape
    row = lambda bi, ti: (bi, ti, 0)
    return pl.pallas_call(
        _combine_kernel,
        out_shape=jax.ShapeDtypeStruct((b, s, d), F32),
        grid=(b, s // tm),
        in_specs=[
            pl.BlockSpec((1, tm, d), row),
            pl.BlockSpec((1, 1, tm, d // 2), lambda bi, ti: (0, bi, ti, 0)),
            pl.BlockSpec((1, 1, tm, d // 2), lambda bi, ti: (1, bi, ti, 0)),
            pl.BlockSpec((1, tm, LANE_PAD_E), row),
            pl.BlockSpec((1, 1, d), lambda bi, ti: (bi, 0, 0)),
        ],
        out_specs=pl.BlockSpec((1, tm, d), row),
        compiler_params=_cparams(2),
        name="moe_combine",
    )(x1, y01, y01, info, g2)


def _moe_layer(hp, logits, x1, g2, wg, wu, wd, tm):
    b, s, d = x1.shape
    n = b * s
    info, cnt = _router(logits.reshape(n, LANE_PAD_E), min(512, n))
    e0 = info[:, 0].astype(jnp.int32)
    e1 = info[:, 1].astype(jnp.int32)
    r0 = info[:, 4].astype(jnp.int32)
    r1 = info[:, 5].astype(jnp.int32)
    counts = cnt[0, :N_EXPERTS].astype(jnp.int32)

    ntiles = (2 * n) // MOE_TILE + N_EXPERTS
    tiles_per_e = (counts + MOE_TILE - 1) // MOE_TILE
    tile_end = jnp.cumsum(tiles_per_e)
    tile_start = tile_end - tiles_per_e
    total = tile_end[-1]
    tidx = jnp.arange(ntiles, dtype=jnp.int32)
    live = tidx < total
    tclip = jnp.minimum(tidx, total - 1)
    tile_e = jnp.minimum(jnp.sum(tclip[:, None] >= tile_end[None, :], axis=1),
                         N_EXPERTS - 1).astype(jnp.int32)
    rows_left = counts[tile_e] - (tclip - tile_start[tile_e]) * MOE_TILE
    rows_here = jnp.clip(rows_left, 0, MOE_TILE)
    tile_nsub = jnp.where(live, (rows_here + MOE_HALF - 1) // MOE_HALF, 0).astype(jnp.int32)

    row_start = tile_start * MOE_TILE
    eid = jnp.arange(N_EXPERTS, dtype=jnp.int32)[None, :]
    pos0 = jnp.sum(jnp.where(e0[:, None] == eid, row_start[None, :], 0), axis=1) + r0
    pos1 = jnp.sum(jnp.where(e1[:, None] == eid, row_start[None, :], 0), axis=1) + r1

    xs = _sc_scatter_rows2(hp.reshape(n, d // 2), pos0, pos1, ntiles * MOE_TILE)
    ys = _moe_ffn(xs, tile_e, tile_nsub, wg, wu, wd)
    y01 = _sc_gather_rows(ys, jnp.concatenate([pos0, pos1]))
    return _combine(x1, y01.reshape(2, b, s, d // 2), info.reshape(b, s, LANE_PAD_E), g2, tm)


def kernel(x, c, w_ada, b_ada, norm_mix_g, norm_ffn_g, w_in, w_out, conv_w, conv_b,
           conv_ln_g, conv_ln_b, q_norm_g, k_norm_g, rel_bias, ffn_w_gate, ffn_w_up,
           ffn_w_down, moe_w_router, moe_b_router, moe_w_gate, moe_w_up, moe_w_down):
    b, s, d = x.shape
    depth = w_ada.shape[0]
    tm = min(512, s)
    mod = _ada_mod(c, w_ada, b_ada)
    for l in range(depth):
        sh1, sc1, g1, sh2, sc2, g2 = [
            mod[l, :, j * d:(j + 1) * d].reshape(b, 1, d) for j in range(6)]
        z, qn, kn, vt = _mix_in(x, sc1, sh1, norm_mix_g[l], w_in[l],
                                q_norm_g[l], k_norm_g[l], tm)
        yc = _conv_branch(z, conv_w[l], conv_b[l], conv_ln_g[l], conv_ln_b[l], min(512, s))
        ya = _attention(qn, kn, vt, rel_bias[l])
        i = l // 2
        if l % 2 == 0:
            x1, h2 = _mix_out(x, yc, ya, w_out[l], g1, norm_ffn_g[l], sc2, sh2, tm)
            x = _dense_ffn(h2, x1, g2, ffn_w_gate[i], ffn_w_up[i], ffn_w_down[i],
                           min(1024, s), 256)
        else:
            x1, hp, logits = _mix_out(x, yc, ya, w_out[l], g1, norm_ffn_g[l], sc2, sh2, tm,
                                      router=(moe_w_router[i], moe_b_router[i]))
            x = _moe_layer(hp, logits, x1, g2, moe_w_gate[i], moe_w_up[i], moe_w_down[i], tm)
    return x
```

```python
import functools

import jax
import jax.numpy as jnp
from jax import lax
from jax.experimental import pallas as pl
from jax.experimental.pallas import tpu as pltpu
from jax.experimental.pallas import tpu_sc as plsc

F32 = jnp.float32
BF16 = jnp.bfloat16

D_MODEL = 1024
CHUNK = 64
N_PREV_CHUNKS = 8
BAND_PAD = N_PREV_CHUNKS * CHUNK
D_CONV = 512
D_ATT = 512
HEAD_DIM = 64
N_HEADS = 8
CONV_WIDTH = 31
MAX_REL = 128
D_IN_COLS = 2 * D_CONV + 3 * D_ATT
N_EXPERTS = 8
EPS = 1e-6
NEG_INF = -1e30

LANES = 128
SUBLANES = 8
VMEM_LIMIT_BYTES = 56 * 1024 * 1024

ATT_HEADS = 4
ATT_GROUPS = N_HEADS // ATT_HEADS
ATT_W = ATT_HEADS * HEAD_DIM
ATT_Q = 2 * CHUNK
ATT_BAND = BAND_PAD + ATT_Q
ATT_L = ATT_HEADS * ATT_Q

CONV_HALO = 32
CONV_ROWS = 32
LANE_PAD_E = LANES

MOE_SUB = 512
MOE_HALF = MOE_SUB // 2
MOE_TILE = 9 * MOE_HALF
MOE_FC = 512


def _cparams(n_axes, vmem=VMEM_LIMIT_BYTES):
    return pltpu.CompilerParams(
        dimension_semantics=("arbitrary",) * n_axes, vmem_limit_bytes=vmem)


def _silu(v):
    return v * jax.nn.sigmoid(v)


def _pack_bf16_pairs(v):
    w = v.shape[1] // 2
    bits = lax.bitcast_convert_type(v.astype(BF16).astype(F32), jnp.uint32)
    packed = (bits[:, w:] & jnp.uint32(0xFFFF0000)) | (bits[:, :w] >> 16)
    return lax.bitcast_convert_type(packed, jnp.int32)


def _unpack_bf16_pairs(p):
    bits = lax.bitcast_convert_type(p, jnp.uint32)
    lo = lax.bitcast_convert_type(bits << 16, F32)
    hi = lax.bitcast_convert_type(bits & jnp.uint32(0xFFFF0000), F32)
    return jnp.concatenate([lo, hi], axis=1)


SC_CORES = 2
SC_SUBCORES = 16
SC_WORKERS = SC_CORES * SC_SUBCORES
SC_CHUNK = 64


def _sc_worker_id():
    return lax.axis_index("s") * SC_CORES + lax.axis_index("c")


def _sc_mesh():
    return plsc.VectorSubcoreMesh(core_axis_name="c", subcore_axis_name="s")


def _sc_gather_rows(table, idx):
    _, w = table.shape
    b = idx.shape[0]
    per_w = b // SC_WORKERS
    nch = per_w // SC_CHUNK

    def body(table_hbm, idx_hbm, out_hbm, idx_v, rows_v, gsem, wsem):
        wid = _sc_worker_id()
        base = wid * per_w
        pltpu.sync_copy(idx_hbm.at[wid], idx_v)
        gathers = [None] * nch
        writes = [None] * nch
        gathers[0] = pltpu.async_copy(table_hbm.at[idx_v.at[0]], rows_v.at[0], gsem.at[0])
        for c in range(nch):
            slot = c % 2
            gathers[c].wait()
            if c + 1 < nch:
                if c >= 1:
                    writes[c - 1].wait()
                gathers[c + 1] = pltpu.async_copy(
                    table_hbm.at[idx_v.at[c + 1]], rows_v.at[1 - slot], gsem.at[1 - slot])
            writes[c] = pltpu.async_copy(
                rows_v.at[slot], out_hbm.at[pl.ds(base + c * SC_CHUNK, SC_CHUNK)], wsem.at[slot])
        if nch >= 2:
            writes[nch - 2].wait()
        writes[nch - 1].wait()

    call = pl.kernel(
        body, mesh=_sc_mesh(),
        out_type=jax.ShapeDtypeStruct((b, w), jnp.int32),
        scratch_types=[pltpu.VMEM((nch, SC_CHUNK), jnp.int32),
                       pltpu.VMEM((2, SC_CHUNK, w), jnp.int32),
                       pltpu.SemaphoreType.DMA((2,)), pltpu.SemaphoreType.DMA((2,))],
        name="sc_gather_rows")
    return call(table, idx.reshape(SC_WORKERS, nch, SC_CHUNK))


def _sc_scatter_rows2(src, idx0, idx1, rows_out):
    n, w = src.shape
    per_w = n // SC_WORKERS
    nch = per_w // SC_CHUNK

    def body(src_hbm, i0_hbm, i1_hbm, out_hbm, i0_v, i1_v, rows_v, rsem, wsem):
        wid = _sc_worker_id()
        base = wid * per_w
        pltpu.sync_copy(i0_hbm.at[wid], i0_v)
        pltpu.sync_copy(i1_hbm.at[wid], i1_v)
        reads = [None] * nch
        writes = [None] * nch
        reads[0] = pltpu.async_copy(src_hbm.at[pl.ds(base, SC_CHUNK)], rows_v.at[0], rsem.at[0])
        for c in range(nch):
            slot = c % 2
            reads[c].wait()
            if c + 1 < nch:
                if c >= 1:
                    for wr in writes[c - 1]:
                        wr.wait()
                reads[c + 1] = pltpu.async_copy(
                    src_hbm.at[pl.ds(base + (c + 1) * SC_CHUNK, SC_CHUNK)],
                    rows_v.at[1 - slot], rsem.at[1 - slot])
            writes[c] = (
                pltpu.async_copy(rows_v.at[slot], out_hbm.at[i0_v.at[c]], wsem.at[slot, 0]),
                pltpu.async_copy(rows_v.at[slot], out_hbm.at[i1_v.at[c]], wsem.at[slot, 1]),
            )
        for c in range(max(nch - 2, 0), nch):
            for wr in writes[c]:
                wr.wait()

    call = pl.kernel(
        body, mesh=_sc_mesh(),
        out_type=jax.ShapeDtypeStruct((rows_out, w), jnp.int32),
        scratch_types=[pltpu.VMEM((nch, SC_CHUNK), jnp.int32),
                       pltpu.VMEM((nch, SC_CHUNK), jnp.int32),
                       pltpu.VMEM((2, SC_CHUNK, w), jnp.int32),
                       pltpu.SemaphoreType.DMA((2,)), pltpu.SemaphoreType.DMA((2, 2))],
        name="sc_scatter_rows")
    shape3 = (SC_WORKERS, nch, SC_CHUNK)
    return call(src, idx0.reshape(shape3), idx1.reshape(shape3))


def _ada_kernel(c_ref, w_ref, b_ref, o_ref):
    ca = _silu(c_ref[...]).astype(BF16)
    w = w_ref[0].astype(BF16)
    o_ref[0] = jnp.dot(ca, w, preferred_element_type=F32) + b_ref[0]


def _ada_mod(c, w_ada, b_ada):
    depth, d, n6 = w_ada.shape
    b = c.shape[0]
    rows = 16
    c_pad = jnp.zeros((rows, d), F32).at[:b].set(c)
    tn = 1536
    out = pl.pallas_call(
        _ada_kernel,
        out_shape=jax.ShapeDtypeStruct((depth, rows, n6), F32),
        grid=(depth, n6 // tn),
        in_specs=[
            pl.BlockSpec((rows, d), lambda l, j: (0, 0)),
            pl.BlockSpec((1, d, tn), lambda l, j: (l, 0, j)),
            pl.BlockSpec((1, 1, tn), lambda l, j: (l, 0, j)),
        ],
        out_specs=pl.BlockSpec((1, rows, tn), lambda l, j: (l, 0, j)),
        compiler_params=_cparams(2),
        name="ada_mod",
    )(c_pad, w_ada, b_ada.reshape(depth, 1, n6))
    return out[:, :b]


def _rms_mod(xf, g, sc, sh):
    ms = jnp.mean(xf * xf, axis=-1, keepdims=True)
    return xf * lax.rsqrt(ms + EPS) * g * (1.0 + sc) + sh


def _mix_in_kernel(x_ref, sc_ref, sh_ref, g_ref, w_ref, gq_ref, gk_ref, ones_ref,
                   z_ref, q_ref, k_ref, vt_ref, wbf_ref):
    first = jnp.logical_and(pl.program_id(0) == 0, pl.program_id(1) == 0)

    @pl.when(first)
    def _():
        wbf_ref[...] = w_ref[...].astype(BF16)

    h = _rms_mod(x_ref[0], g_ref[...], sc_ref[0], sh_ref[0]).astype(BF16)
    proj = jnp.dot(h, wbf_ref[...], preferred_element_type=F32)

    a = proj[:, :D_CONV]
    gate = proj[:, D_CONV:2 * D_CONV]
    z_ref[0] = (a * jax.nn.sigmoid(gate)).astype(BF16)

    def head_norm(t, g):
        sq = (t * t).astype(BF16)
        ss = jnp.concatenate(
            [jnp.dot(sq[:, c:c + ATT_W], ones_ref[...], preferred_element_type=F32)
             for c in range(0, D_ATT, ATT_W)], axis=1)
        return (t * lax.rsqrt(ss * (1.0 / HEAD_DIM) + EPS) * g).astype(BF16)

    o = 2 * D_CONV
    q_ref[0] = head_norm(proj[:, o:o + D_ATT], gq_ref[...])
    k_ref[0] = head_norm(proj[:, o + D_ATT:o + 2 * D_ATT], gk_ref[...])
    v = proj[:, o + 2 * D_ATT:]
    tm = v.shape[0]
    for cidx in range(tm // ATT_Q):
        vt_ref[0, cidx] = v[cidx * ATT_Q:(cidx + 1) * ATT_Q, :].T.astype(BF16)


def _mix_in(x, sc, sh, g, w_in, gq, gk, tm):
    b, s, d = x.shape
    ones_bd = (jnp.arange(ATT_W)[:, None] // HEAD_DIM
               == jnp.arange(ATT_W)[None, :] // HEAD_DIM).astype(BF16)
    gq_t = (jnp.tile(gq, N_HEADS) * (HEAD_DIM ** -0.5)).reshape(1, D_ATT)
    gk_t = jnp.tile(gk, N_HEADS).reshape(1, D_ATT)
    row = lambda bi, ti: (bi, ti, 0)
    per_b = lambda bi, ti: (bi, 0, 0)
    const2 = lambda bi, ti: (0, 0)
    return pl.pallas_call(
        _mix_in_kernel,
        out_shape=(
            jax.ShapeDtypeStruct((b, s, D_CONV), BF16),
            jax.ShapeDtypeStruct((b, s, D_ATT), BF16),
            jax.ShapeDtypeStruct((b, s, D_ATT), BF16),
            jax.ShapeDtypeStruct((b, s // ATT_Q, D_ATT, ATT_Q), BF16),
        ),
        grid=(b, s // tm),
        in_specs=[
            pl.BlockSpec((1, tm, d), row),
            pl.BlockSpec((1, 1, d), per_b),
            pl.BlockSpec((1, 1, d), per_b),
            pl.BlockSpec((1, d), const2),
            pl.BlockSpec((d, D_IN_COLS), const2),
            pl.BlockSpec((1, D_ATT), const2),
            pl.BlockSpec((1, D_ATT), const2),
            pl.BlockSpec((ATT_W, ATT_W), const2),
        ],
        out_specs=(
            pl.BlockSpec((1, tm, D_CONV), row),
            pl.BlockSpec((1, tm, D_ATT), row),
            pl.BlockSpec((1, tm, D_ATT), row),
            pl.BlockSpec((1, tm // ATT_Q, D_ATT, ATT_Q), lambda bi, ti: (bi, ti, 0, 0)),
        ),
        scratch_shapes=[pltpu.VMEM((d, D_IN_COLS), BF16)],
        compiler_params=_cparams(2),
        name="mix_in",
    )(x, sc, sh, g.reshape(1, d), w_in, gq_t, gk_t, ones_bd)


def _conv_kernel(zc_ref, zp_ref, w_ref, cb_ref, lg_ref, lb_ref, o_ref, win_ref, sh_ref,
                 acc_ref):
    tt = zc_ref.shape[1]
    t = pl.program_id(1)
    halo = zp_ref[0].astype(F32)
    win_ref[0:CONV_HALO, :] = jnp.where(t == 0, 0.0, halo)
    win_ref[CONV_HALO:, :] = zc_ref[0].astype(F32)
    span = tt + CONV_HALO - SUBLANES
    for sft in range(1, SUBLANES):
        sh_ref[sft - 1, 0:span, :] = win_ref[sft:sft + span, :]
    base = CONV_HALO - (CONV_WIDTH - 1)

    def group(gidx, carry):
        r = pl.multiple_of(gidx * CONV_ROWS, CONV_ROWS)
        acc = jnp.zeros((CONV_ROWS, D_CONV), F32) + cb_ref[...]
        for j in range(CONV_WIDTH):
            whole, sft = divmod(base + j, SUBLANES)
            start = pl.multiple_of(r + whole * SUBLANES, SUBLANES)
            if sft == 0:
                tap = win_ref[pl.ds(start, CONV_ROWS), :]
            else:
                tap = sh_ref[sft - 1, pl.ds(start, CONV_ROWS), :]
            acc = acc + tap * w_ref[j:j + 1, :]
        acc_ref[pl.ds(r, CONV_ROWS), :] = acc
        return carry

    lax.fori_loop(0, tt // CONV_ROWS, group, 0)
    acc = acc_ref[...]
    mu = jnp.mean(acc, axis=-1, keepdims=True)
    xc = acc - mu
    var = jnp.mean(xc * xc, axis=-1, keepdims=True)
    y = xc * lax.rsqrt(var + EPS) * lg_ref[...] + lb_ref[...]
    o_ref[0] = _silu(y).astype(BF16)


def _conv_branch(z, conv_w, conv_b, ln_g, ln_b, tt):
    b, s, c = z.shape
    hb = tt // CONV_HALO
    const2 = lambda bi, ti: (0, 0)
    return pl.pallas_call(
        _conv_kernel,
        out_shape=jax.ShapeDtypeStruct((b, s, c), BF16),
        grid=(b, s // tt),
        in_specs=[
            pl.BlockSpec((1, tt, c), lambda bi, ti: (bi, ti, 0)),
            pl.BlockSpec((1, CONV_HALO, c),
                         lambda bi, ti: (bi, jnp.maximum(ti * hb - 1, 0), 0)),
            pl.BlockSpec((CONV_WIDTH, c), const2),
            pl.BlockSpec((1, c), const2),
            pl.BlockSpec((1, c), const2),
            pl.BlockSpec((1, c), const2),
        ],
        out_specs=pl.BlockSpec((1, tt, c), lambda bi, ti: (bi, ti, 0)),
        scratch_shapes=[pltpu.VMEM((tt + CONV_HALO, c), F32),
                        pltpu.VMEM((SUBLANES - 1, tt + CONV_HALO, c), F32),
                        pltpu.VMEM((tt, c), F32)],
        compiler_params=_cparams(2),
        name="conv_branch",
    )(z, z, conv_w.reshape(CONV_WIDTH, c), conv_b.reshape(1, c),
      ln_g.reshape(1, c), ln_b.reshape(1, c))


def _attn_kernel(q_ref, k_ref, vt_ref, bias_ref, o_ref, kpad_ref, vtpad_ref,
                 st0_ref, st1_ref, pb0_ref, pb1_ref, den0_ref, den1_ref):
    st_refs = (st0_ref, st1_ref)
    pb_refs = (pb0_ref, pb1_ref)
    den_refs = (den0_ref, den1_ref)
    s = q_ref.shape[1]
    npad = BAND_PAD // ATT_Q
    kpad_ref[0:BAND_PAD, :] = jnp.zeros((BAND_PAD, ATT_W), BF16)
    kpad_ref[BAND_PAD:, :] = k_ref[0]
    vtpad_ref[0:npad] = jnp.zeros((npad, ATT_W, ATT_Q), BF16)
    vtpad_ref[npad:] = vt_ref[0]

    iota = lambda shape, dim: lax.broadcasted_iota(jnp.int32, shape, dim)
    q_shift = ATT_Q.bit_length() - 1
    d_shift = HEAD_DIM.bit_length() - 1
    qb_mask = (iota((ATT_L, ATT_W), 0) >> q_shift) == (iota((ATT_L, ATT_W), 1) >> d_shift)
    ot_mask = (iota((ATT_W, ATT_L), 0) >> d_shift) == (iota((ATT_W, ATT_L), 1) >> q_shift)
    sel = jnp.where((iota((ATT_Q, ATT_L), 1) & (ATT_Q - 1)) == iota((ATT_Q, ATT_L), 0),
                    1.0, 0.0).astype(BF16)
    key_row = lax.broadcasted_iota(jnp.int32, (ATT_BAND, ATT_L), 0)
    contract_last = (((1,), (1,)), ((), ()))

    def scores(m, p):
        r0 = pl.multiple_of(m * ATT_Q, ATT_Q)
        qt = q_ref[0, pl.ds(r0, ATT_Q), :]
        qb = jnp.where(qb_mask, jnp.concatenate([qt] * ATT_HEADS, axis=0), 0)
        kb = kpad_ref[pl.ds(r0, ATT_BAND), :]
        st_refs[p][...] = lax.dot_general(kb, qb.astype(BF16), contract_last,
                                          preferred_element_type=F32)

    def softmax(m, p, masked):
        st = st_refs[p][...] + bias_ref[0]
        if masked:
            st = jnp.where(key_row >= BAND_PAD - m * ATT_Q, st, NEG_INF)
        mx = jnp.max(st, axis=0, keepdims=True)
        e = jnp.exp(st - mx)
        den_refs[p][...] = jnp.sum(e, axis=0, keepdims=True)
        pb_refs[p][...] = e.astype(BF16)

    def values(m, p):
        r0 = pl.multiple_of(m * ATT_Q, ATT_Q)
        vb = jnp.concatenate([vtpad_ref[m + c] for c in range(ATT_BAND // ATT_Q)],
                             axis=1)
        ot = jnp.dot(vb, pb_refs[p][...], preferred_element_type=F32)
        ot = jnp.where(ot_mask, ot / den_refs[p][...], 0.0).astype(BF16)
        y = lax.dot_general(sel, ot, contract_last, preferred_element_type=F32)
        o_ref[0, pl.ds(r0, ATT_Q), :] = y.astype(BF16)

    n = s // ATT_Q
    n_masked = BAND_PAD // ATT_Q
    scores(0, 0)
    scores(1, 1)
    softmax(0, 0, True)

    def pair(masked, i, carry):
        m = 2 * i
        scores(m, 0)
        softmax(m - 1, 1, masked)
        values(m - 2, 0)
        scores(m + 1, 1)
        softmax(m, 0, masked)
        values(m - 1, 1)
        return carry

    split = min(n // 2, n_masked // 2 + 1)
    lax.fori_loop(1, split, functools.partial(pair, True), 0)
    lax.fori_loop(split, n // 2, functools.partial(pair, False), 0)
    softmax(n - 1, 1, n - 1 < n_masked)
    values(n - 2, 0)
    values(n - 1, 1)


def _attn_bias_t(rel_bias):
    rb = rel_bias.astype(F32)
    nu = ATT_BAND + ATT_Q - 1
    n_low = BAND_PAD - MAX_REL + ATT_Q
    t = jnp.concatenate([jnp.repeat(rb[:, :1], n_low, axis=1),
                         rb[:, 1:1 + nu - n_low]], axis=1)
    tp = jnp.pad(t, ((0, 0), (0, 1)))
    skew = jnp.tile(tp, (1, ATT_Q))[:, :ATT_Q * nu].reshape(N_HEADS, ATT_Q, nu)
    bias = skew[:, :, ATT_Q - 1:]
    bias = bias.reshape(ATT_GROUPS, ATT_HEADS, ATT_Q, ATT_BAND)
    bias = bias.transpose(0, 3, 1, 2).reshape(ATT_GROUPS, ATT_BAND, ATT_L)
    r = jnp.arange(ATT_BAND)[:, None]
    qq = jnp.arange(ATT_L)[None, :] % ATT_Q
    first = (qq // CHUNK) * CHUNK
    valid = (r >= first) & (r < first + BAND_PAD + CHUNK)
    return jnp.where(valid[None], bias, NEG_INF)


def _attention(qn, kn, vt, rel_bias):
    b, s, _ = qn.shape
    bias_t = _attn_bias_t(rel_bias)
    nck = s // ATT_Q
    assert nck >= 4 and nck % 2 == 0, "the attention pipeline runs query steps in pairs"
    return pl.pallas_call(
        _attn_kernel,
        out_shape=jax.ShapeDtypeStruct((b, s, D_ATT), BF16),
        grid=(b, ATT_GROUPS),
        in_specs=[
            pl.BlockSpec((1, s, ATT_W), lambda bi, gi: (bi, 0, gi)),
            pl.BlockSpec((1, s, ATT_W), lambda bi, gi: (bi, 0, gi)),
            pl.BlockSpec((1, nck, ATT_W, ATT_Q), lambda bi, gi: (bi, 0, gi, 0)),
            pl.BlockSpec((1, ATT_BAND, ATT_L), lambda bi, gi: (gi, 0, 0)),
        ],
        out_specs=pl.BlockSpec((1, s, ATT_W), lambda bi, gi: (bi, 0, gi)),
        scratch_shapes=[
            pltpu.VMEM((s + BAND_PAD, ATT_W), BF16),
            pltpu.VMEM((nck + BAND_PAD // ATT_Q, ATT_W, ATT_Q), BF16),
            pltpu.VMEM((ATT_BAND, ATT_L), F32), pltpu.VMEM((ATT_BAND, ATT_L), F32),
            pltpu.VMEM((ATT_BAND, ATT_L), BF16), pltpu.VMEM((ATT_BAND, ATT_L), BF16),
            pltpu.VMEM((1, ATT_L), F32), pltpu.VMEM((1, ATT_L), F32),
        ],
        compiler_params=_cparams(2),
        name="band_attention",
    )(qn, kn, vt, bias_t)


def _mix_out_kernel(x_ref, yc_ref, ya_ref, w_ref, g1_ref, gf_ref, sc_ref, sh_ref,
                    wr_ref, br_ref, x1_ref, h_ref, lg_ref, wbf_ref):
    first = jnp.logical_and(pl.program_id(0) == 0, pl.program_id(1) == 0)

    @pl.when(first)
    def _():
        wbf_ref[...] = w_ref[...].astype(BF16)

    y = jnp.dot(yc_ref[0], wbf_ref[0:D_CONV, :], preferred_element_type=F32)
    y = y + jnp.dot(ya_ref[0], wbf_ref[D_CONV:, :], preferred_element_type=F32)
    x1 = x_ref[0] + g1_ref[0] * y
    x1_ref[0] = x1
    h = _rms_mod(x1, gf_ref[...], sc_ref[0], sh_ref[0])
    h_ref[0] = _pack_bf16_pairs(h)
    lg_ref[0] = jnp.dot(h.astype(BF16), wr_ref[...], preferred_element_type=F32) + br_ref[...]


def _mix_out_routed(x, yc, ya, w_out, g1, gf, sc, sh, w_router, b_router, tm):
    b, s, d = x.shape
    row = lambda bi, ti: (bi, ti, 0)
    per_b = lambda bi, ti: (bi, 0, 0)
    const2 = lambda bi, ti: (0, 0)
    w_pad = jnp.zeros((d, LANE_PAD_E), BF16).at[:, :N_EXPERTS].set(w_router.astype(BF16))
    b_pad = jnp.full((1, LANE_PAD_E), -jnp.inf, F32).at[0, :N_EXPERTS].set(
        b_router.astype(F32))
    return pl.pallas_call(
        _mix_out_kernel,
        out_shape=(jax.ShapeDtypeStruct((b, s, d), F32),
                   jax.ShapeDtypeStruct((b, s, d // 2), jnp.int32),
                   jax.ShapeDtypeStruct((b, s, LANE_PAD_E), F32)),
        grid=(b, s // tm),
        in_specs=[
            pl.BlockSpec((1, tm, d), row),
            pl.BlockSpec((1, tm, D_CONV), row),
            pl.BlockSpec((1, tm, D_ATT), row),
            pl.BlockSpec((d, d), const2),
            pl.BlockSpec((1, 1, d), per_b),
            pl.BlockSpec((1, d), const2),
            pl.BlockSpec((1, 1, d), per_b),
            pl.BlockSpec((1, 1, d), per_b),
            pl.BlockSpec((d, LANE_PAD_E), const2),
            pl.BlockSpec((1, LANE_PAD_E), const2),
        ],
        out_specs=(pl.BlockSpec((1, tm, d), row), pl.BlockSpec((1, tm, d // 2), row),
                   pl.BlockSpec((1, tm, LANE_PAD_E), row)),
        scratch_shapes=[pltpu.VMEM((d, d), BF16)],
        compiler_params=_cparams(2),
        name="mix_out",
    )(x, yc, ya, w_out, g1, gf.reshape(1, d), sc, sh, w_pad, b_pad)


def _cast_ffn_kernel(wg_ref, wu_ref, wd_ref, wo_ref, wg3_ref, wu3_ref, wd3_ref, wob_ref):
    wg3_ref[0] = wg_ref[...].astype(BF16)
    wu3_ref[0] = wu_ref[...].astype(BF16)
    wd3_ref[0] = wd_ref[...].astype(BF16)

    @pl.when(pl.program_id(0) == 0)
    def _():
        wob_ref[...] = wo_ref[...].astype(BF16)


def _cast_ffn_weights(wg, wu, wd, w_out, fc):
    d, ff = wg.shape
    nf = ff // fc
    return pl.pallas_call(
        _cast_ffn_kernel,
        out_shape=(jax.ShapeDtypeStruct((nf, d, fc), BF16),
                   jax.ShapeDtypeStruct((nf, d, fc), BF16),
                   jax.ShapeDtypeStruct((nf, fc, d), BF16),
                   jax.ShapeDtypeStruct((d, d), BF16)),
        grid=(nf,),
        in_specs=[
            pl.BlockSpec((d, fc), lambda f: (0, f)),
            pl.BlockSpec((d, fc), lambda f: (0, f)),
            pl.BlockSpec((fc, d), lambda f: (f, 0)),
            pl.BlockSpec((d, d), lambda f: (0, 0)),
        ],
        out_specs=(pl.BlockSpec((1, d, fc), lambda f: (f, 0, 0)),
                   pl.BlockSpec((1, d, fc), lambda f: (f, 0, 0)),
                   pl.BlockSpec((1, fc, d), lambda f: (f, 0, 0)),
                   pl.BlockSpec((d, d), lambda f: (0, 0))),
        compiler_params=_cparams(1),
        name="cast_ffn_weights",
    )(wg, wu, wd, w_out)


def _mix_ffn_kernel(x_ref, yc_ref, ya_ref, wo_ref, g1_ref, gf_ref, sc_ref, sh_ref, g2_ref,
                    wg_ref, wu_ref, wd_ref, o_ref, acc_ref, x1_ref, h_ref):
    f = pl.program_id(2)

    @pl.when(f == 0)
    def _():
        y = jnp.dot(yc_ref[0], wo_ref[0:D_CONV, :], preferred_element_type=F32)
        y = y + jnp.dot(ya_ref[0], wo_ref[D_CONV:, :], preferred_element_type=F32)
        x1 = x_ref[0] + g1_ref[0] * y
        x1_ref[...] = x1
        h_ref[...] = _rms_mod(x1, gf_ref[...], sc_ref[0], sh_ref[0]).astype(BF16)
        acc_ref[...] = jnp.zeros_like(acc_ref)

    h = h_ref[...]
    a = jnp.dot(h, wg_ref[0], preferred_element_type=F32)
    u = jnp.dot(h, wu_ref[0], preferred_element_type=F32)
    t = (_silu(a) * u).astype(BF16)
    acc_ref[...] += jnp.dot(t, wd_ref[0], preferred_element_type=F32)

    @pl.when(f == pl.num_programs(2) - 1)
    def _():
        o_ref[0] = x1_ref[...] + g2_ref[0] * acc_ref[...]


def _mix_out_dense_ffn(x, yc, ya, w_out, g1, gf, sc, sh, g2, wg, wu, wd, tm, fc):
    b, s, d = x.shape
    wg3, wu3, wd3, wob = _cast_ffn_weights(wg, wu, wd, w_out, fc)
    nf = wg3.shape[0]
    row = lambda bi, ti, fi: (bi, ti, 0)
    per_b = lambda bi, ti, fi: (bi, 0, 0)
    const2 = lambda bi, ti, fi: (0, 0)
    chunk = lambda bi, ti, fi: (fi, 0, 0)
    return pl.pallas_call(
        _mix_ffn_kernel,
        out_shape=jax.ShapeDtypeStruct((b, s, d), F32),
        grid=(b, s // tm, nf),
        in_specs=[
            pl.BlockSpec((1, tm, d), row),
            pl.BlockSpec((1, tm, D_CONV), row),
            pl.BlockSpec((1, tm, D_ATT), row),
            pl.BlockSpec((d, d), const2),
            pl.BlockSpec((1, 1, d), per_b),
            pl.BlockSpec((1, d), const2),
            pl.BlockSpec((1, 1, d), per_b),
            pl.BlockSpec((1, 1, d), per_b),
            pl.BlockSpec((1, 1, d), per_b),
            pl.BlockSpec((1, d, fc), chunk),
            pl.BlockSpec((1, d, fc), chunk),
            pl.BlockSpec((1, fc, d), chunk),
        ],
        out_specs=pl.BlockSpec((1, tm, d), row),
        scratch_shapes=[pltpu.VMEM((tm, d), F32), pltpu.VMEM((tm, d), F32),
                        pltpu.VMEM((tm, d), BF16)],
        compiler_params=_cparams(3),
        name="mix_out_dense_ffn",
    )(x, yc, ya, wob, g1, gf.reshape(1, d), sc, sh, g2, wg3, wu3, wd3)


def _router_kernel(lg_ref, info_ref, info_t_ref, cnt_ref, carry_ref):
    i = pl.program_id(0)

    @pl.when(i == 0)
    def _():
        carry_ref[...] = jnp.zeros_like(carry_ref)

    tr = lg_ref.shape[0]
    logits = lg_ref[...]
    lane = lax.broadcasted_iota(jnp.int32, (tr, LANE_PAD_E), 1).astype(F32)
    no_lane = float(LANE_PAD_E)
    v0 = jnp.max(logits, axis=-1, keepdims=True)
    i0 = jnp.min(jnp.where(logits == v0, lane, no_lane), axis=-1, keepdims=True)
    rest = jnp.where(lane == i0, -jnp.inf, logits)
    v1 = jnp.max(rest, axis=-1, keepdims=True)
    i1 = jnp.min(jnp.where(rest == v1, lane, no_lane), axis=-1, keepdims=True)
    e1 = jnp.exp(v1 - v0)
    w0 = 1.0 / (1.0 + e1)
    w1 = e1 / (1.0 + e1)
    oh0 = lane == i0
    oh1 = lane == i1
    cnt = jnp.where(jnp.logical_or(oh0, oh1), 1.0, 0.0)
    tri = (lax.broadcasted_iota(jnp.int32, (tr, tr), 1)
           < lax.broadcasted_iota(jnp.int32, (tr, tr), 0)).astype(BF16)
    before = jnp.dot(tri, cnt.astype(BF16), preferred_element_type=F32) + carry_ref[0:1, :]
    r0 = jnp.sum(jnp.where(oh0, before, 0.0), axis=-1, keepdims=True)
    r1 = jnp.sum(jnp.where(oh1, before, 0.0), axis=-1, keepdims=True)
    carry_ref[...] = carry_ref[...] + jnp.sum(cnt, axis=0, keepdims=True)
    cnt_ref[...] = carry_ref[...]
    info = jnp.where(lane == 0, i0, 0.0)
    info = jnp.where(lane == 1, i1, info)
    info = jnp.where(lane == 2, w0, info)
    info = jnp.where(lane == 3, w1, info)
    info = jnp.where(lane == 4, r0, info)
    info = jnp.where(lane == 5, r1, info)
    info_ref[...] = info
    for c in range(tr // LANES):
        blk = info[c * LANES:(c + 1) * LANES, :].T
        info_t_ref[:, c * LANES:(c + 1) * LANES] = blk[0:SUBLANES, :]


def _router(logits, tr):
    n = logits.shape[0]
    return pl.pallas_call(
        _router_kernel,
        out_shape=(jax.ShapeDtypeStruct((n, LANE_PAD_E), F32),
                   jax.ShapeDtypeStruct((SUBLANES, n), F32),
                   jax.ShapeDtypeStruct((8, LANE_PAD_E), F32)),
        grid=(n // tr,),
        in_specs=[pl.BlockSpec((tr, LANE_PAD_E), lambda i: (i, 0))],
        out_specs=(pl.BlockSpec((tr, LANE_PAD_E), lambda i: (i, 0)),
                   pl.BlockSpec((SUBLANES, tr), lambda i: (0, i)),
                   pl.BlockSpec((8, LANE_PAD_E), lambda i: (0, 0))),
        scratch_shapes=[pltpu.VMEM((8, LANE_PAD_E), F32)],
        compiler_params=_cparams(1),
        name="router",
    )(logits)


def _moe_kernel(te_ref, tn_ref, xs_ref, wg_ref, wu_ref, wd_ref, ys_ref,
                acc_ref, xb_ref, wgb_ref, wub_ref, wdb_ref):
    i = pl.program_id(0)
    f = pl.program_id(1)
    nhalf = tn_ref[i]
    nfull = nhalf >> 1

    @pl.when(f == 0)
    def _():
        acc_ref[...] = jnp.zeros_like(acc_ref)

        def unpack(sidx, carry):
            r = pl.multiple_of(sidx * MOE_HALF, MOE_HALF)
            xb_ref[pl.ds(r, MOE_HALF), :] = _unpack_bf16_pairs(
                xs_ref[pl.ds(r, MOE_HALF), :]).astype(BF16)
            return carry

        lax.fori_loop(0, nhalf, unpack, 0)

    @pl.when(nhalf > 0)
    def _():
        wgb_ref[...] = wg_ref[0].astype(BF16)
        wub_ref[...] = wu_ref[0].astype(BF16)
        wdb_ref[...] = wd_ref[0].astype(BF16)

    def block(r, rows):
        xb = xb_ref[pl.ds(r, rows), :]
        a = jnp.dot(xb, wgb_ref[...], preferred_element_type=F32)
        u = jnp.dot(xb, wub_ref[...], preferred_element_type=F32)
        t = (_silu(a) * u).astype(BF16)
        acc_ref[pl.ds(r, rows), :] += jnp.dot(t, wdb_ref[...], preferred_element_type=F32)

    def sub(sidx, carry):
        block(pl.multiple_of(sidx * MOE_SUB, MOE_SUB), MOE_SUB)
        return carry

    lax.fori_loop(0, nfull, sub, 0)

    @pl.when((nhalf & 1) == 1)
    def _():
        block(pl.multiple_of(nfull * MOE_SUB, MOE_SUB), MOE_HALF)

    @pl.when(f == pl.num_programs(1) - 1)
    def _():
        ys_ref[...] = _pack_bf16_pairs(acc_ref[...])


def _moe_ffn(xs, tile_e, tile_nsub, wg, wu, wd):
    rpad, dw = xs.shape
    d = 2 * dw
    ntiles = rpad // MOE_TILE
    ff = wg.shape[2]
    nf = ff // MOE_FC

    def fcol(i, f, tn):
        return jnp.where(tn[i] > 0, f, nf - 1)

    return pl.pallas_call(
        _moe_kernel,
        out_shape=jax.ShapeDtypeStruct((rpad, dw), jnp.int32),
        grid_spec=pltpu.PrefetchScalarGridSpec(
            num_scalar_prefetch=2,
            grid=(ntiles, nf),
            in_specs=[
                pl.BlockSpec((MOE_TILE, dw), lambda i, f, te, tn: (i, 0)),
                pl.BlockSpec((1, d, MOE_FC), lambda i, f, te, tn: (te[i], 0, fcol(i, f, tn))),
                pl.BlockSpec((1, d, MOE_FC), lambda i, f, te, tn: (te[i], 0, fcol(i, f, tn))),
                pl.BlockSpec((1, MOE_FC, d), lambda i, f, te, tn: (te[i], fcol(i, f, tn), 0)),
            ],
            out_specs=pl.BlockSpec((MOE_TILE, dw), lambda i, f, te, tn: (i, 0)),
            scratch_shapes=[
                pltpu.VMEM((MOE_TILE, d), F32),
                pltpu.VMEM((MOE_TILE, d), BF16),
                pltpu.VMEM((d, MOE_FC), BF16),
                pltpu.VMEM((d, MOE_FC), BF16),
                pltpu.VMEM((MOE_FC, d), BF16),
            ],
        ),
        compiler_params=_cparams(2),
        name="moe_ffn",
    )(tile_e, tile_nsub, xs, wg, wu, wd)


def _combine_kernel(x_ref, y0_ref, y1_ref, info_ref, g2_ref, o_ref):
    info = info_ref[0]
    w0 = info[:, 2:3]
    w1 = info[:, 3:4]
    f = w0 * _unpack_bf16_pairs(y0_ref[0, 0]) + w1 * _unpack_bf16_pairs(y1_ref[0, 0])
    o_ref[0] = x_ref[0] + g2_ref[0] * f


def _combine(x1, y01, info, g2, tm):
    b, s, d = x1.shape
    row = lambda bi, ti: (bi, ti, 0)
    return pl.pallas_call(
        _combine_kernel,
        out_shape=jax.ShapeDtypeStruct((b, s, d), F32),
        grid=(b, s // tm),
        in_specs=[
            pl.BlockSpec((1, tm, d), row),
            pl.BlockSpec((1, 1, tm, d // 2), lambda bi, ti: (0, bi, ti, 0)),
            pl.BlockSpec((1, 1, tm, d // 2), lambda bi, ti: (1, bi, ti, 0)),
            pl.BlockSpec((1, tm, LANE_PAD_E), row),
            pl.BlockSpec((1, 1, d), lambda bi, ti: (bi, 0, 0)),
        ],
        out_specs=pl.BlockSpec((1, tm, d), row),
        compiler_params=_cparams(2),
        name="moe_combine",
    )(x1, y01, y01, info, g2)


def _moe_layer(hp, logits, x1, g2, wg, wu, wd, tm):
    b, s, d = x1.shape
    n = b * s
    info, info_t, cnt = _router(logits.reshape(n, LANE_PAD_E), min(512, n))
    e0 = info_t[0].astype(jnp.int32)
    e1 = info_t[1].astype(jnp.int32)
    r0 = info_t[4].astype(jnp.int32)
    r1 = info_t[5].astype(jnp.int32)
    counts = cnt[0, :N_EXPERTS].astype(jnp.int32)

    ntiles = (2 * n) // MOE_TILE + N_EXPERTS
    tiles_per_e = (counts + MOE_TILE - 1) // MOE_TILE
    tile_end = jnp.cumsum(tiles_per_e)
    tile_start = tile_end - tiles_per_e
    total = tile_end[-1]
    tidx = jnp.arange(ntiles, dtype=jnp.int32)
    live = tidx < total
    tclip = jnp.minimum(tidx, total - 1)
    tile_e = jnp.minimum(jnp.sum(tclip[:, None] >= tile_end[None, :], axis=1),
                         N_EXPERTS - 1).astype(jnp.int32)
    rows_left = counts[tile_e] - (tclip - tile_start[tile_e]) * MOE_TILE
    rows_here = jnp.clip(rows_left, 0, MOE_TILE)
    tile_nsub = jnp.where(live, (rows_here + MOE_HALF - 1) // MOE_HALF, 0).astype(jnp.int32)

    row_start = tile_start * MOE_TILE
    eid = jnp.arange(N_EXPERTS, dtype=jnp.int32)[None, :]
    pos0 = jnp.sum(jnp.where(e0[:, None] == eid, row_start[None, :], 0), axis=1) + r0
    pos1 = jnp.sum(jnp.where(e1[:, None] == eid, row_start[None, :], 0), axis=1) + r1

    xs = _sc_scatter_rows2(hp.reshape(n, d // 2), pos0, pos1, ntiles * MOE_TILE)
    ys = _moe_ffn(xs, tile_e, tile_nsub, wg, wu, wd)
    y01 = _sc_gather_rows(ys, jnp.concatenate([pos0, pos1]))
    return _combine(x1, y01.reshape(2, b, s, d // 2), info.reshape(b, s, LANE_PAD_E), g2, tm)


def kernel(x, c, w_ada, b_ada, norm_mix_g, norm_ffn_g, w_in, w_out, conv_w, conv_b,
           conv_ln_g, conv_ln_b, q_norm_g, k_norm_g, rel_bias, ffn_w_gate, ffn_w_up,
           ffn_w_down, moe_w_router, moe_b_router, moe_w_gate, moe_w_up, moe_w_down):
    b, s, d = x.shape
    depth = w_ada.shape[0]
    tm = min(512, s)
    mod = _ada_mod(c, w_ada, b_ada)
    for l in range(depth):
        sh1, sc1, g1, sh2, sc2, g2 = [
            mod[l, :, j * d:(j + 1) * d].reshape(b, 1, d) for j in range(6)]
        z, qn, kn, vt = _mix_in(x, sc1, sh1, norm_mix_g[l], w_in[l],
                                q_norm_g[l], k_norm_g[l], tm)
        yc = _conv_branch(z, conv_w[l], conv_b[l], conv_ln_g[l], conv_ln_b[l], min(512, s))
        ya = _attention(qn, kn, vt, rel_bias[l])
        i = l // 2
        if l % 2 == 0:
            x = _mix_out_dense_ffn(x, yc, ya, w_out[l], g1, norm_ffn_g[l], sc2, sh2, g2,
                                   ffn_w_gate[i], ffn_w_up[i], ffn_w_down[i],
                                   min(1024, s), 256)
        else:
            x1, hp, logits = _mix_out_routed(x, yc, ya, w_out[l], g1, norm_ffn_g[l], sc2, sh2,
                                             moe_w_router[i], moe_b_router[i], tm)
            x = _moe_layer(hp, logits, x1, g2, moe_w_gate[i], moe_w_up[i], moe_w_down[i], tm)
    return x
```

```python
import functools

import jax
import jax.numpy as jnp
from jax import lax
from jax.experimental import pallas as pl
from jax.experimental.pallas import tpu as pltpu
from jax.experimental.pallas import tpu_sc as plsc

F32 = jnp.float32
BF16 = jnp.bfloat16

D_MODEL = 1024
CHUNK = 64
N_PREV_CHUNKS = 8
BAND_PAD = N_PREV_CHUNKS * CHUNK
D_CONV = 512
D_ATT = 512
HEAD_DIM = 64
N_HEADS = 8
CONV_WIDTH = 31
MAX_REL = 128
D_IN_COLS = 2 * D_CONV + 3 * D_ATT
N_EXPERTS = 8
EPS = 1e-6
NEG_INF = -1e30

LANES = 128
SUBLANES = 8
VMEM_LIMIT_BYTES = 56 * 1024 * 1024

ATT_HEADS = 4
ATT_GROUPS = N_HEADS // ATT_HEADS
ATT_W = ATT_HEADS * HEAD_DIM
ATT_Q = 2 * CHUNK
ATT_BAND = BAND_PAD + ATT_Q
ATT_L = ATT_HEADS * ATT_Q

CONV_HALO = 32
CONV_ROWS = 32
LANE_PAD_E = LANES

MOE_SUB = 512
MOE_HALF = MOE_SUB // 2
MOE_TILE = 9 * MOE_HALF
MOE_FC = 512


def _cparams(n_axes, vmem=VMEM_LIMIT_BYTES):
    return pltpu.CompilerParams(
        dimension_semantics=("arbitrary",) * n_axes, vmem_limit_bytes=vmem)


def _silu(v):
    return v * jax.nn.sigmoid(v)


def _pack_bf16_pairs(v):
    w = v.shape[1] // 2
    bits = lax.bitcast_convert_type(v.astype(BF16).astype(F32), jnp.uint32)
    packed = (bits[:, w:] & jnp.uint32(0xFFFF0000)) | (bits[:, :w] >> 16)
    return lax.bitcast_convert_type(packed, jnp.int32)


def _unpack_bf16_pairs(p):
    bits = lax.bitcast_convert_type(p, jnp.uint32)
    lo = lax.bitcast_convert_type(bits << 16, F32)
    hi = lax.bitcast_convert_type(bits & jnp.uint32(0xFFFF0000), F32)
    return jnp.concatenate([lo, hi], axis=1)


SC_CORES = 2
SC_SUBCORES = 16
SC_WORKERS = SC_CORES * SC_SUBCORES
SC_CHUNK = 64


def _sc_worker_id():
    return lax.axis_index("s") * SC_CORES + lax.axis_index("c")


def _sc_mesh():
    return plsc.VectorSubcoreMesh(core_axis_name="c", subcore_axis_name="s")


def _sc_gather_rows(table, idx):
    _, w = table.shape
    b = idx.shape[0]
    per_w = b // SC_WORKERS
    nch = per_w // SC_CHUNK

    def body(table_hbm, idx_hbm, out_hbm, idx_v, rows_v, gsem, wsem):
        wid = _sc_worker_id()
        base = wid * per_w
        pltpu.sync_copy(idx_hbm.at[wid], idx_v)
        gathers = [None] * nch
        writes = [None] * nch
        gathers[0] = pltpu.async_copy(table_hbm.at[idx_v.at[0]], rows_v.at[0], gsem.at[0])
        for c in range(nch):
            slot = c % 2
            gathers[c].wait()
            if c + 1 < nch:
                if c >= 1:
                    writes[c - 1].wait()
                gathers[c + 1] = pltpu.async_copy(
                    table_hbm.at[idx_v.at[c + 1]], rows_v.at[1 - slot], gsem.at[1 - slot])
            writes[c] = pltpu.async_copy(
                rows_v.at[slot], out_hbm.at[pl.ds(base + c * SC_CHUNK, SC_CHUNK)], wsem.at[slot])
        if nch >= 2:
            writes[nch - 2].wait()
        writes[nch - 1].wait()

    call = pl.kernel(
        body, mesh=_sc_mesh(),
        out_type=jax.ShapeDtypeStruct((b, w), jnp.int32),
        scratch_types=[pltpu.VMEM((nch, SC_CHUNK), jnp.int32),
                       pltpu.VMEM((2, SC_CHUNK, w), jnp.int32),
                       pltpu.SemaphoreType.DMA((2,)), pltpu.SemaphoreType.DMA((2,))],
        name="sc_gather_rows")
    return call(table, idx.reshape(SC_WORKERS, nch, SC_CHUNK))


def _sc_scatter_rows2(src, idx0, idx1, rows_out):
    n, w = src.shape
    per_w = n // SC_WORKERS
    nch = per_w // SC_CHUNK

    def body(src_hbm, i0_hbm, i1_hbm, out_hbm, i0_v, i1_v, rows_v, rsem, wsem):
        wid = _sc_worker_id()
        base = wid * per_w
        pltpu.sync_copy(i0_hbm.at[wid], i0_v)
        pltpu.sync_copy(i1_hbm.at[wid], i1_v)
        reads = [None] * nch
        writes = [None] * nch
        reads[0] = pltpu.async_copy(src_hbm.at[pl.ds(base, SC_CHUNK)], rows_v.at[0], rsem.at[0])
        for c in range(nch):
            slot = c % 2
            reads[c].wait()
            if c + 1 < nch:
                if c >= 1:
                    for wr in writes[c - 1]:
                        wr.wait()
                reads[c + 1] = pltpu.async_copy(
                    src_hbm.at[pl.ds(base + (c + 1) * SC_CHUNK, SC_CHUNK)],
                    rows_v.at[1 - slot], rsem.at[1 - slot])
            writes[c] = (
                pltpu.async_copy(rows_v.at[slot], out_hbm.at[i0_v.at[c]], wsem.at[slot, 0]),
                pltpu.async_copy(rows_v.at[slot], out_hbm.at[i1_v.at[c]], wsem.at[slot, 1]),
            )
        for c in range(max(nch - 2, 0), nch):
            for wr in writes[c]:
                wr.wait()

    call = pl.kernel(
        body, mesh=_sc_mesh(),
        out_type=jax.ShapeDtypeStruct((rows_out, w), jnp.int32),
        scratch_types=[pltpu.VMEM((nch, SC_CHUNK), jnp.int32),
                       pltpu.VMEM((nch, SC_CHUNK), jnp.int32),
                       pltpu.VMEM((2, SC_CHUNK, w), jnp.int32),
                       pltpu.SemaphoreType.DMA((2,)), pltpu.SemaphoreType.DMA((2, 2))],
        name="sc_scatter_rows")
    shape3 = (SC_WORKERS, nch, SC_CHUNK)
    return call(src, idx0.reshape(shape3), idx1.reshape(shape3))


def _ada_kernel(c_ref, w_ref, b_ref, o_ref):
    ca = _silu(c_ref[...]).astype(BF16)
    w = w_ref[0].astype(BF16)
    o_ref[0] = jnp.dot(ca, w, preferred_element_type=F32) + b_ref[0]


def _ada_mod(c, w_ada, b_ada):
    depth, d, n6 = w_ada.shape
    b = c.shape[0]
    rows = 16
    c_pad = jnp.zeros((rows, d), F32).at[:b].set(c)
    tn = 1536
    out = pl.pallas_call(
        _ada_kernel,
        out_shape=jax.ShapeDtypeStruct((depth, rows, n6), F32),
        grid=(depth, n6 // tn),
        in_specs=[
            pl.BlockSpec((rows, d), lambda l, j: (0, 0)),
            pl.BlockSpec((1, d, tn), lambda l, j: (l, 0, j)),
            pl.BlockSpec((1, 1, tn), lambda l, j: (l, 0, j)),
        ],
        out_specs=pl.BlockSpec((1, rows, tn), lambda l, j: (l, 0, j)),
        compiler_params=_cparams(2),
        name="ada_mod",
    )(c_pad, w_ada, b_ada.reshape(depth, 1, n6))
    return out[:, :b]


def _rms_mod(xf, g, sc, sh):
    ms = jnp.mean(xf * xf, axis=-1, keepdims=True)
    return xf * lax.rsqrt(ms + EPS) * g * (1.0 + sc) + sh


def _mix_in_kernel(x_ref, sc_ref, sh_ref, g_ref, w_ref, gq_ref, gk_ref, ones_ref,
                   z_ref, q_ref, k_ref, vt_ref, wbf_ref):
    first = jnp.logical_and(pl.program_id(0) == 0, pl.program_id(1) == 0)

    @pl.when(first)
    def _():
        wbf_ref[...] = w_ref[...].astype(BF16)

    h = _rms_mod(x_ref[0], g_ref[...], sc_ref[0], sh_ref[0]).astype(BF16)
    proj = jnp.dot(h, wbf_ref[...], preferred_element_type=F32)

    a = proj[:, :D_CONV]
    gate = proj[:, D_CONV:2 * D_CONV]
    z_ref[0] = (a * jax.nn.sigmoid(gate)).astype(BF16)

    def head_norm(t, g):
        sq = (t * t).astype(BF16)
        ss = jnp.concatenate(
            [jnp.dot(sq[:, c:c + ATT_W], ones_ref[...], preferred_element_type=F32)
             for c in range(0, D_ATT, ATT_W)], axis=1)
        return (t * lax.rsqrt(ss * (1.0 / HEAD_DIM) + EPS) * g).astype(BF16)

    o = 2 * D_CONV
    q_ref[0] = head_norm(proj[:, o:o + D_ATT], gq_ref[...])
    k_ref[0] = head_norm(proj[:, o + D_ATT:o + 2 * D_ATT], gk_ref[...])
    v = proj[:, o + 2 * D_ATT:]
    tm = v.shape[0]
    for cidx in range(tm // ATT_Q):
        vt_ref[0, cidx] = v[cidx * ATT_Q:(cidx + 1) * ATT_Q, :].T.astype(BF16)


def _mix_in(x, sc, sh, g, w_in, gq, gk, tm):
    b, s, d = x.shape
    ones_bd = (jnp.arange(ATT_W)[:, None] // HEAD_DIM
               == jnp.arange(ATT_W)[None, :] // HEAD_DIM).astype(BF16)
    gq_t = (jnp.tile(gq, N_HEADS) * (HEAD_DIM ** -0.5)).reshape(1, D_ATT)
    gk_t = jnp.tile(gk, N_HEADS).reshape(1, D_ATT)
    row = lambda bi, ti: (bi, ti, 0)
    per_b = lambda bi, ti: (bi, 0, 0)
    const2 = lambda bi, ti: (0, 0)
    return pl.pallas_call(
        _mix_in_kernel,
        out_shape=(
            jax.ShapeDtypeStruct((b, s, D_CONV), BF16),
            jax.ShapeDtypeStruct((b, s, D_ATT), BF16),
            jax.ShapeDtypeStruct((b, s, D_ATT), BF16),
            jax.ShapeDtypeStruct((b, s // ATT_Q, D_ATT, ATT_Q), BF16),
        ),
        grid=(b, s // tm),
        in_specs=[
            pl.BlockSpec((1, tm, d), row),
            pl.BlockSpec((1, 1, d), per_b),
            pl.BlockSpec((1, 1, d), per_b),
            pl.BlockSpec((1, d), const2),
            pl.BlockSpec((d, D_IN_COLS), const2),
            pl.BlockSpec((1, D_ATT), const2),
            pl.BlockSpec((1, D_ATT), const2),
            pl.BlockSpec((ATT_W, ATT_W), const2),
        ],
        out_specs=(
            pl.BlockSpec((1, tm, D_CONV), row),
            pl.BlockSpec((1, tm, D_ATT), row),
            pl.BlockSpec((1, tm, D_ATT), row),
            pl.BlockSpec((1, tm // ATT_Q, D_ATT, ATT_Q), lambda bi, ti: (bi, ti, 0, 0)),
        ),
        scratch_shapes=[pltpu.VMEM((d, D_IN_COLS), BF16)],
        compiler_params=_cparams(2),
        name="mix_in",
    )(x, sc, sh, g.reshape(1, d), w_in, gq_t, gk_t, ones_bd)


def _conv_kernel(zc_ref, zp_ref, w_ref, cb_ref, lg_ref, lb_ref, o_ref, win_ref, sh_ref,
                 acc_ref):
    tt = zc_ref.shape[1]
    t = pl.program_id(1)
    halo = zp_ref[0].astype(F32)
    win_ref[0:CONV_HALO, :] = jnp.where(t == 0, 0.0, halo)
    win_ref[CONV_HALO:, :] = zc_ref[0].astype(F32)
    span = tt + CONV_HALO - SUBLANES
    for sft in range(1, SUBLANES):
        sh_ref[sft - 1, 0:span, :] = win_ref[sft:sft + span, :]
    base = CONV_HALO - (CONV_WIDTH - 1)
    tiles = CONV_ROWS // SUBLANES

    def group(gidx, carry):
        r = pl.multiple_of(gidx * CONV_ROWS, CONV_ROWS)
        acc = jnp.zeros((tiles, SUBLANES, D_CONV), F32) + cb_ref[...]
        for j in range(CONV_WIDTH):
            whole, sft = divmod(base + j, SUBLANES)
            start = pl.multiple_of(r + whole * SUBLANES, SUBLANES)
            if sft == 0:
                tap = win_ref[pl.ds(start, CONV_ROWS), :]
            else:
                tap = sh_ref[sft - 1, pl.ds(start, CONV_ROWS), :]
            acc = acc + tap.reshape(tiles, SUBLANES, D_CONV) * w_ref[j]
        acc_ref[pl.ds(r, CONV_ROWS), :] = acc.reshape(CONV_ROWS, D_CONV)
        return carry

    lax.fori_loop(0, tt // CONV_ROWS, group, 0)
    acc = acc_ref[...]
    mu = jnp.mean(acc, axis=-1, keepdims=True)
    xc = acc - mu
    var = jnp.mean(xc * xc, axis=-1, keepdims=True)
    y = xc * lax.rsqrt(var + EPS) * lg_ref[...] + lb_ref[...]
    o_ref[0] = _silu(y).astype(BF16)


def _conv_branch(z, conv_w, conv_b, ln_g, ln_b, tt):
    b, s, c = z.shape
    hb = tt // CONV_HALO
    w_tiles = jnp.broadcast_to(conv_w.reshape(CONV_WIDTH, 1, c), (CONV_WIDTH, SUBLANES, c))
    const2 = lambda bi, ti: (0, 0)
    return pl.pallas_call(
        _conv_kernel,
        out_shape=jax.ShapeDtypeStruct((b, s, c), BF16),
        grid=(b, s // tt),
        in_specs=[
            pl.BlockSpec((1, tt, c), lambda bi, ti: (bi, ti, 0)),
            pl.BlockSpec((1, CONV_HALO, c),
                         lambda bi, ti: (bi, jnp.maximum(ti * hb - 1, 0), 0)),
            pl.BlockSpec((CONV_WIDTH, SUBLANES, c), lambda bi, ti: (0, 0, 0)),
            pl.BlockSpec((1, c), const2),
            pl.BlockSpec((1, c), const2),
            pl.BlockSpec((1, c), const2),
        ],
        out_specs=pl.BlockSpec((1, tt, c), lambda bi, ti: (bi, ti, 0)),
        scratch_shapes=[pltpu.VMEM((tt + CONV_HALO, c), F32),
                        pltpu.VMEM((SUBLANES - 1, tt + CONV_HALO, c), F32),
                        pltpu.VMEM((tt, c), F32)],
        compiler_params=_cparams(2),
        name="conv_branch",
    )(z, z, w_tiles, conv_b.reshape(1, c),
      ln_g.reshape(1, c), ln_b.reshape(1, c))


def _attn_kernel(q_ref, k_ref, vt_ref, bias_ref, o_ref, kpad_ref, vtpad_ref,
                 st0_ref, st1_ref, pb0_ref, pb1_ref, den0_ref, den1_ref):
    st_refs = (st0_ref, st1_ref)
    pb_refs = (pb0_ref, pb1_ref)
    den_refs = (den0_ref, den1_ref)
    s = q_ref.shape[1]
    npad = BAND_PAD // ATT_Q
    kpad_ref[0:BAND_PAD, :] = jnp.zeros((BAND_PAD, ATT_W), BF16)
    kpad_ref[BAND_PAD:, :] = k_ref[0]
    vtpad_ref[0:npad] = jnp.zeros((npad, ATT_W, ATT_Q), BF16)
    vtpad_ref[npad:] = vt_ref[0]

    iota = lambda shape, dim: lax.broadcasted_iota(jnp.int32, shape, dim)
    q_shift = ATT_Q.bit_length() - 1
    d_shift = HEAD_DIM.bit_length() - 1
    qb_mask = (iota((ATT_L, ATT_W), 0) >> q_shift) == (iota((ATT_L, ATT_W), 1) >> d_shift)
    ot_mask = (iota((ATT_W, ATT_L), 0) >> d_shift) == (iota((ATT_W, ATT_L), 1) >> q_shift)
    sel = jnp.where((iota((ATT_Q, ATT_L), 1) & (ATT_Q - 1)) == iota((ATT_Q, ATT_L), 0),
                    1.0, 0.0).astype(BF16)
    key_row = lax.broadcasted_iota(jnp.int32, (ATT_BAND, ATT_L), 0)
    contract_last = (((1,), (1,)), ((), ()))

    def scores(m, p):
        r0 = pl.multiple_of(m * ATT_Q, ATT_Q)
        qt = q_ref[0, pl.ds(r0, ATT_Q), :]
        qb = jnp.where(qb_mask, jnp.concatenate([qt] * ATT_HEADS, axis=0), 0)
        kb = kpad_ref[pl.ds(r0, ATT_BAND), :]
        st_refs[p][...] = lax.dot_general(kb, qb.astype(BF16), contract_last,
                                          preferred_element_type=F32)

    def softmax(m, p, masked):
        st = st_refs[p][...] + bias_ref[0]
        if masked:
            st = jnp.where(key_row >= BAND_PAD - m * ATT_Q, st, NEG_INF)
        mx = jnp.max(st, axis=0, keepdims=True)
        e = jnp.exp(st - mx)
        den_refs[p][...] = jnp.sum(e, axis=0, keepdims=True)
        pb_refs[p][...] = e.astype(BF16)

    def values(m, p):
        r0 = pl.multiple_of(m * ATT_Q, ATT_Q)
        vb = jnp.concatenate([vtpad_ref[m + c] for c in range(ATT_BAND // ATT_Q)],
                             axis=1)
        ot = jnp.dot(vb, pb_refs[p][...], preferred_element_type=F32)
        ot = jnp.where(ot_mask, ot / den_refs[p][...], 0.0).astype(BF16)
        y = lax.dot_general(sel, ot, contract_last, preferred_element_type=F32)
        o_ref[0, pl.ds(r0, ATT_Q), :] = y.astype(BF16)

    n = s // ATT_Q
    n_masked = BAND_PAD // ATT_Q
    scores(0, 0)
    scores(1, 1)
    softmax(0, 0, True)

    def pair(masked, i, carry):
        m = 2 * i
        scores(m, 0)
        softmax(m - 1, 1, masked)
        values(m - 2, 0)
        scores(m + 1, 1)
        softmax(m, 0, masked)
        values(m - 1, 1)
        return carry

    split = min(n // 2, n_masked // 2 + 1)
    lax.fori_loop(1, split, functools.partial(pair, True), 0)
    lax.fori_loop(split, n // 2, functools.partial(pair, False), 0)
    softmax(n - 1, 1, n - 1 < n_masked)
    values(n - 2, 0)
    values(n - 1, 1)


def _attn_bias_t(rel_bias):
    rb = rel_bias.astype(F32)
    nu = ATT_BAND + ATT_Q - 1
    n_low = BAND_PAD - MAX_REL + ATT_Q
    t = jnp.concatenate([jnp.repeat(rb[:, :1], n_low, axis=1),
                         rb[:, 1:1 + nu - n_low]], axis=1)
    starts = (ATT_Q - 1) - jnp.arange(ATT_Q)
    win = jax.vmap(lambda st: lax.dynamic_slice_in_dim(t, st, ATT_BAND, axis=1))(starts)
    bias = win.reshape(ATT_Q, ATT_GROUPS, ATT_HEADS, ATT_BAND)
    bias = bias.transpose(1, 3, 2, 0).reshape(ATT_GROUPS, ATT_BAND, ATT_L)
    r = jnp.arange(ATT_BAND)[:, None]
    qq = jnp.arange(ATT_L)[None, :] % ATT_Q
    first = (qq // CHUNK) * CHUNK
    valid = (r >= first) & (r < first + BAND_PAD + CHUNK)
    return jnp.where(valid[None], bias, NEG_INF)


def _attention(qn, kn, vt, rel_bias):
    b, s, _ = qn.shape
    bias_t = _attn_bias_t(rel_bias)
    nck = s // ATT_Q
    assert nck >= 4 and nck % 2 == 0, "the attention pipeline runs query steps in pairs"
    return pl.pallas_call(
        _attn_kernel,
        out_shape=jax.ShapeDtypeStruct((b, s, D_ATT), BF16),
        grid=(b, ATT_GROUPS),
        in_specs=[
            pl.BlockSpec((1, s, ATT_W), lambda bi, gi: (bi, 0, gi)),
            pl.BlockSpec((1, s, ATT_W), lambda bi, gi: (bi, 0, gi)),
            pl.BlockSpec((1, nck, ATT_W, ATT_Q), lambda bi, gi: (bi, 0, gi, 0)),
            pl.BlockSpec((1, ATT_BAND, ATT_L), lambda bi, gi: (gi, 0, 0)),
        ],
        out_specs=pl.BlockSpec((1, s, ATT_W), lambda bi, gi: (bi, 0, gi)),
        scratch_shapes=[
            pltpu.VMEM((s + BAND_PAD, ATT_W), BF16),
            pltpu.VMEM((nck + BAND_PAD // ATT_Q, ATT_W, ATT_Q), BF16),
            pltpu.VMEM((ATT_BAND, ATT_L), F32), pltpu.VMEM((ATT_BAND, ATT_L), F32),
            pltpu.VMEM((ATT_BAND, ATT_L), BF16), pltpu.VMEM((ATT_BAND, ATT_L), BF16),
            pltpu.VMEM((1, ATT_L), F32), pltpu.VMEM((1, ATT_L), F32),
        ],
        compiler_params=_cparams(2),
        name="band_attention",
    )(qn, kn, vt, bias_t)


def _mix_out_kernel(x_ref, yc_ref, ya_ref, w_ref, g1_ref, gf_ref, sc_ref, sh_ref,
                    wr_ref, br_ref, x1_ref, h_ref, info_ref, info_t_ref, cnt_ref,
                    wbf_ref, carry_ref):
    first = jnp.logical_and(pl.program_id(0) == 0, pl.program_id(1) == 0)

    @pl.when(first)
    def _():
        wbf_ref[...] = w_ref[...].astype(BF16)
        carry_ref[...] = jnp.zeros_like(carry_ref)

    y = jnp.dot(yc_ref[0], wbf_ref[0:D_CONV, :], preferred_element_type=F32)
    y = y + jnp.dot(ya_ref[0], wbf_ref[D_CONV:, :], preferred_element_type=F32)
    x1 = x_ref[0] + g1_ref[0] * y
    x1_ref[0] = x1
    h = _rms_mod(x1, gf_ref[...], sc_ref[0], sh_ref[0])
    h_ref[0] = _pack_bf16_pairs(h)
    logits = jnp.dot(h.astype(BF16), wr_ref[...], preferred_element_type=F32) + br_ref[...]
    _route_tile(logits, info_ref, info_t_ref, cnt_ref, carry_ref)


def _mix_out_routed(x, yc, ya, w_out, g1, gf, sc, sh, w_router, b_router, tm):
    b, s, d = x.shape
    nt = s // tm
    row = lambda bi, ti: (bi, ti, 0)
    per_b = lambda bi, ti: (bi, 0, 0)
    const2 = lambda bi, ti: (0, 0)
    w_pad = jnp.zeros((d, LANE_PAD_E), BF16).at[:, :N_EXPERTS].set(w_router.astype(BF16))
    b_pad = jnp.full((1, LANE_PAD_E), -jnp.inf, F32).at[0, :N_EXPERTS].set(
        b_router.astype(F32))
    return pl.pallas_call(
        _mix_out_kernel,
        out_shape=(jax.ShapeDtypeStruct((b, s, d), F32),
                   jax.ShapeDtypeStruct((b, s, d // 2), jnp.int32),
                   jax.ShapeDtypeStruct((b, s, LANE_PAD_E), F32),
                   jax.ShapeDtypeStruct((SUBLANES, b * s), F32),
                   jax.ShapeDtypeStruct((SUBLANES, LANE_PAD_E), F32)),
        grid=(b, nt),
        in_specs=[
            pl.BlockSpec((1, tm, d), row),
            pl.BlockSpec((1, tm, D_CONV), row),
            pl.BlockSpec((1, tm, D_ATT), row),
            pl.BlockSpec((d, d), const2),
            pl.BlockSpec((1, 1, d), per_b),
            pl.BlockSpec((1, d), const2),
            pl.BlockSpec((1, 1, d), per_b),
            pl.BlockSpec((1, 1, d), per_b),
            pl.BlockSpec((d, LANE_PAD_E), const2),
            pl.BlockSpec((1, LANE_PAD_E), const2),
        ],
        out_specs=(pl.BlockSpec((1, tm, d), row), pl.BlockSpec((1, tm, d // 2), row),
                   pl.BlockSpec((1, tm, LANE_PAD_E), row),
                   pl.BlockSpec((SUBLANES, tm), lambda bi, ti: (0, bi * nt + ti)),
                   pl.BlockSpec((SUBLANES, LANE_PAD_E), const2)),
        scratch_shapes=[pltpu.VMEM((d, d), BF16), pltpu.VMEM((SUBLANES, LANE_PAD_E), F32)],
        compiler_params=_cparams(2),
        name="mix_out",
    )(x, yc, ya, w_out, g1, gf.reshape(1, d), sc, sh, w_pad, b_pad)


def _cast_ffn_kernel(wg_ref, wu_ref, wd_ref, wo_ref, wg3_ref, wu3_ref, wd3_ref, wob_ref):
    wg3_ref[0] = wg_ref[...].astype(BF16)
    wu3_ref[0] = wu_ref[...].astype(BF16)
    wd3_ref[0] = wd_ref[...].astype(BF16)

    @pl.when(pl.program_id(0) == 0)
    def _():
        wob_ref[...] = wo_ref[...].astype(BF16)


def _cast_ffn_weights(wg, wu, wd, w_out, fc):
    d, ff = wg.shape
    nf = ff // fc
    return pl.pallas_call(
        _cast_ffn_kernel,
        out_shape=(jax.ShapeDtypeStruct((nf, d, fc), BF16),
                   jax.ShapeDtypeStruct((nf, d, fc), BF16),
                   jax.ShapeDtypeStruct((nf, fc, d), BF16),
                   jax.ShapeDtypeStruct((d, d), BF16)),
        grid=(nf,),
        in_specs=[
            pl.BlockSpec((d, fc), lambda f: (0, f)),
            pl.BlockSpec((d, fc), lambda f: (0, f)),
            pl.BlockSpec((fc, d), lambda f: (f, 0)),
            pl.BlockSpec((d, d), lambda f: (0, 0)),
        ],
        out_specs=(pl.BlockSpec((1, d, fc), lambda f: (f, 0, 0)),
                   pl.BlockSpec((1, d, fc), lambda f: (f, 0, 0)),
                   pl.BlockSpec((1, fc, d), lambda f: (f, 0, 0)),
                   pl.BlockSpec((d, d), lambda f: (0, 0))),
        compiler_params=_cparams(1),
        name="cast_ffn_weights",
    )(wg, wu, wd, w_out)


def _mix_ffn_kernel(x_ref, yc_ref, ya_ref, wo_ref, g1_ref, gf_ref, sc_ref, sh_ref, g2_ref,
                    wg_ref, wu_ref, wd_ref, o_ref, acc_ref, x1_ref, h_ref):
    f = pl.program_id(2)

    @pl.when(f == 0)
    def _():
        y = jnp.dot(yc_ref[0], wo_ref[0:D_CONV, :], preferred_element_type=F32)
        y = y + jnp.dot(ya_ref[0], wo_ref[D_CONV:, :], preferred_element_type=F32)
        x1 = x_ref[0] + g1_ref[0] * y
        x1_ref[...] = x1
        h_ref[...] = _rms_mod(x1, gf_ref[...], sc_ref[0], sh_ref[0]).astype(BF16)
        acc_ref[...] = jnp.zeros_like(acc_ref)

    h = h_ref[...]
    a = jnp.dot(h, wg_ref[0], preferred_element_type=F32)
    u = jnp.dot(h, wu_ref[0], preferred_element_type=F32)
    t = (_silu(a) * u).astype(BF16)
    acc_ref[...] += jnp.dot(t, wd_ref[0], preferred_element_type=F32)

    @pl.when(f == pl.num_programs(2) - 1)
    def _():
        o_ref[0] = x1_ref[...] + g2_ref[0] * acc_ref[...]


def _mix_out_dense_ffn(x, yc, ya, w_out, g1, gf, sc, sh, g2, wg, wu, wd, tm, fc):
    b, s, d = x.shape
    wg3, wu3, wd3, wob = _cast_ffn_weights(wg, wu, wd, w_out, fc)
    nf = wg3.shape[0]
    row = lambda bi, ti, fi: (bi, ti, 0)
    per_b = lambda bi, ti, fi: (bi, 0, 0)
    const2 = lambda bi, ti, fi: (0, 0)
    chunk = lambda bi, ti, fi: (fi, 0, 0)
    return pl.pallas_call(
        _mix_ffn_kernel,
        out_shape=jax.ShapeDtypeStruct((b, s, d), F32),
        grid=(b, s // tm, nf),
        in_specs=[
            pl.BlockSpec((1, tm, d), row),
            pl.BlockSpec((1, tm, D_CONV), row),
            pl.BlockSpec((1, tm, D_ATT), row),
            pl.BlockSpec((d, d), const2),
            pl.BlockSpec((1, 1, d), per_b),
            pl.BlockSpec((1, d), const2),
            pl.BlockSpec((1, 1, d), per_b),
            pl.BlockSpec((1, 1, d), per_b),
            pl.BlockSpec((1, 1, d), per_b),
            pl.BlockSpec((1, d, fc), chunk),
            pl.BlockSpec((1, d, fc), chunk),
            pl.BlockSpec((1, fc, d), chunk),
        ],
        out_specs=pl.BlockSpec((1, tm, d), row),
        scratch_shapes=[pltpu.VMEM((tm, d), F32), pltpu.VMEM((tm, d), F32),
                        pltpu.VMEM((tm, d), BF16)],
        compiler_params=_cparams(3),
        name="mix_out_dense_ffn",
    )(x, yc, ya, wob, g1, gf.reshape(1, d), sc, sh, g2, wg3, wu3, wd3)


def _route_tile(logits, info_ref, info_t_ref, cnt_ref, carry_ref):
    tr = logits.shape[0]
    lane = lax.broadcasted_iota(jnp.int32, (tr, LANE_PAD_E), 1).astype(F32)
    no_lane = float(LANE_PAD_E)
    v0 = jnp.max(logits, axis=-1, keepdims=True)
    i0 = jnp.min(jnp.where(logits == v0, lane, no_lane), axis=-1, keepdims=True)
    rest = jnp.where(lane == i0, -jnp.inf, logits)
    v1 = jnp.max(rest, axis=-1, keepdims=True)
    i1 = jnp.min(jnp.where(rest == v1, lane, no_lane), axis=-1, keepdims=True)
    e1 = jnp.exp(v1 - v0)
    w0 = 1.0 / (1.0 + e1)
    w1 = e1 / (1.0 + e1)
    oh0 = lane == i0
    oh1 = lane == i1
    cnt = jnp.where(jnp.logical_or(oh0, oh1), 1.0, 0.0)
    tri = (lax.broadcasted_iota(jnp.int32, (tr, tr), 1)
           < lax.broadcasted_iota(jnp.int32, (tr, tr), 0)).astype(BF16)
    before = jnp.dot(tri, cnt.astype(BF16), preferred_element_type=F32) + carry_ref[0:1, :]
    r0 = jnp.sum(jnp.where(oh0, before, 0.0), axis=-1, keepdims=True)
    r1 = jnp.sum(jnp.where(oh1, before, 0.0), axis=-1, keepdims=True)
    carry_ref[...] = carry_ref[...] + jnp.sum(cnt, axis=0, keepdims=True)
    cnt_ref[...] = carry_ref[...]
    info = jnp.where(lane == 0, i0, 0.0)
    info = jnp.where(lane == 1, i1, info)
    info = jnp.where(lane == 2, w0, info)
    info = jnp.where(lane == 3, w1, info)
    info = jnp.where(lane == 4, r0, info)
    info = jnp.where(lane == 5, r1, info)
    info_ref[0] = info
    for c in range(tr // LANES):
        blk = info[c * LANES:(c + 1) * LANES, :].T
        info_t_ref[:, c * LANES:(c + 1) * LANES] = blk[0:SUBLANES, :]


def _moe_kernel(te_ref, tn_ref, xs_ref, wg_ref, wu_ref, wd_ref, ys_ref,
                acc_ref, xb_ref, wgb_ref, wub_ref, wdb_ref):
    i = pl.program_id(0)
    f = pl.program_id(1)
    nhalf = tn_ref[i]
    nfull = nhalf >> 1

    @pl.when(f == 0)
    def _():
        acc_ref[...] = jnp.zeros_like(acc_ref)

        def unpack(sidx, carry):
            r = pl.multiple_of(sidx * MOE_HALF, MOE_HALF)
            xb_ref[pl.ds(r, MOE_HALF), :] = _unpack_bf16_pairs(
                xs_ref[pl.ds(r, MOE_HALF), :]).astype(BF16)
            return carry

        lax.fori_loop(0, nhalf, unpack, 0)

    @pl.when(nhalf > 0)
    def _():
        wgb_ref[...] = wg_ref[0].astype(BF16)
        wub_ref[...] = wu_ref[0].astype(BF16)
        wdb_ref[...] = wd_ref[0].astype(BF16)

    def block(r, rows):
        xb = xb_ref[pl.ds(r, rows), :]
        a = jnp.dot(xb, wgb_ref[...], preferred_element_type=F32)
        u = jnp.dot(xb, wub_ref[...], preferred_element_type=F32)
        t = (_silu(a) * u).astype(BF16)
        acc_ref[pl.ds(r, rows), :] += jnp.dot(t, wdb_ref[...], preferred_element_type=F32)

    def sub(sidx, carry):
        block(pl.multiple_of(sidx * MOE_SUB, MOE_SUB), MOE_SUB)
        return carry

    lax.fori_loop(0, nfull, sub, 0)

    @pl.when((nhalf & 1) == 1)
    def _():
        block(pl.multiple_of(nfull * MOE_SUB, MOE_SUB), MOE_HALF)

    @pl.when(f == pl.num_programs(1) - 1)
    def _():
        ys_ref[...] = _pack_bf16_pairs(acc_ref[...])


def _moe_ffn(xs, tile_e, tile_nsub, wg, wu, wd):
    rpad, dw = xs.shape
    d = 2 * dw
    ntiles = rpad // MOE_TILE
    ff = wg.shape[2]
    nf = ff // MOE_FC

    def fcol(i, f, tn):
        return jnp.where(tn[i] > 0, f, nf - 1)

    return pl.pallas_call(
        _moe_kernel,
        out_shape=jax.ShapeDtypeStruct((rpad, dw), jnp.int32),
        grid_spec=pltpu.PrefetchScalarGridSpec(
            num_scalar_prefetch=2,
            grid=(ntiles, nf),
            in_specs=[
                pl.BlockSpec((MOE_TILE, dw), lambda i, f, te, tn: (i, 0)),
                pl.BlockSpec((1, d, MOE_FC), lambda i, f, te, tn: (te[i], 0, fcol(i, f, tn))),
                pl.BlockSpec((1, d, MOE_FC), lambda i, f, te, tn: (te[i], 0, fcol(i, f, tn))),
                pl.BlockSpec((1, MOE_FC, d), lambda i, f, te, tn: (te[i], fcol(i, f, tn), 0)),
            ],
            out_specs=pl.BlockSpec((MOE_TILE, dw), lambda i, f, te, tn: (i, 0)),
            scratch_shapes=[
                pltpu.VMEM((MOE_TILE, d), F32),
                pltpu.VMEM((MOE_TILE, d), BF16),
                pltpu.VMEM((d, MOE_FC), BF16),
                pltpu.VMEM((d, MOE_FC), BF16),
                pltpu.VMEM((MOE_FC, d), BF16),
            ],
        ),
        compiler_params=_cparams(2),
        name="moe_ffn",
    )(tile_e, tile_nsub, xs, wg, wu, wd)


def _combine_kernel(x_ref, y0_ref, y1_ref, info_ref, g2_ref, o_ref):
    info = info_ref[0]
    w0 = info[:, 2:3]
    w1 = info[:, 3:4]
    f = w0 * _unpack_bf16_pairs(y0_ref[0, 0]) + w1 * _unpack_bf16_pairs(y1_ref[0, 0])
    o_ref[0] = x_ref[0] + g2_ref[0] * f


def _combine(x1, y01, info, g2, tm):
    b, s, d = x1.shape
    row = lambda bi, ti: (bi, ti, 0)
    return pl.pallas_call(
        _combine_kernel,
        out_shape=jax.ShapeDtypeStruct((b, s, d), F32),
        grid=(b, s // tm),
        in_specs=[
            pl.BlockSpec((1, tm, d), row),
            pl.BlockSpec((1, 1, tm, d // 2), lambda bi, ti: (0, bi, ti, 0)),
            pl.BlockSpec((1, 1, tm, d // 2), lambda bi, ti: (1, bi, ti, 0)),
            pl.BlockSpec((1, tm, LANE_PAD_E), row),
            pl.BlockSpec((1, 1, d), lambda bi, ti: (bi, 0, 0)),
        ],
        out_specs=pl.BlockSpec((1, tm, d), row),
        compiler_params=_cparams(2),
        name="moe_combine",
    )(x1, y01, y01, info, g2)


def _moe_layer(hp, info, info_t, cnt, x1, g2, wg, wu, wd, tm):
    b, s, d = x1.shape
    n = b * s
    e0 = info_t[0].astype(jnp.int32)
    e1 = info_t[1].astype(jnp.int32)
    r0 = info_t[4].astype(jnp.int32)
    r1 = info_t[5].astype(jnp.int32)
    counts = cnt[0, :N_EXPERTS].astype(jnp.int32)

    ntiles = (2 * n) // MOE_TILE + N_EXPERTS
    tiles_per_e = (counts + MOE_TILE - 1) // MOE_TILE
    tile_end = jnp.cumsum(tiles_per_e)
    tile_start = tile_end - tiles_per_e
    total = tile_end[-1]
    tidx = jnp.arange(ntiles, dtype=jnp.int32)
    live = tidx < total
    tclip = jnp.minimum(tidx, total - 1)
    tile_e = jnp.minimum(jnp.sum(tclip[:, None] >= tile_end[None, :], axis=1),
                         N_EXPERTS - 1).astype(jnp.int32)
    rows_left = counts[tile_e] - (tclip - tile_start[tile_e]) * MOE_TILE
    rows_here = jnp.clip(rows_left, 0, MOE_TILE)
    tile_nsub = jnp.where(live, (rows_here + MOE_HALF - 1) // MOE_HALF, 0).astype(jnp.int32)

    row_start = tile_start * MOE_TILE
    eid = jnp.arange(N_EXPERTS, dtype=jnp.int32)[None, :]
    pos0 = jnp.sum(jnp.where(e0[:, None] == eid, row_start[None, :], 0), axis=1) + r0
    pos1 = jnp.sum(jnp.where(e1[:, None] == eid, row_start[None, :], 0), axis=1) + r1

    xs = _sc_scatter_rows2(hp.reshape(n, d // 2), pos0, pos1, ntiles * MOE_TILE)
    ys = _moe_ffn(xs, tile_e, tile_nsub, wg, wu, wd)
    y01 = _sc_gather_rows(ys, jnp.concatenate([pos0, pos1]))
    return _combine(x1, y01.reshape(2, b, s, d // 2), info.reshape(b, s, LANE_PAD_E), g2, tm)


def kernel(x, c, w_ada, b_ada, norm_mix_g, norm_ffn_g, w_in, w_out, conv_w, conv_b,
           conv_ln_g, conv_ln_b, q_norm_g, k_norm_g, rel_bias, ffn_w_gate, ffn_w_up,
           ffn_w_down, moe_w_router, moe_b_router, moe_w_gate, moe_w_up, moe_w_down):
    b, s, d = x.shape
    depth = w_ada.shape[0]
    tm = min(512, s)
    mod = _ada_mod(c, w_ada, b_ada)
    for l in range(depth):
        sh1, sc1, g1, sh2, sc2, g2 = [
            mod[l, :, j * d:(j + 1) * d].reshape(b, 1, d) for j in range(6)]
        z, qn, kn, vt = _mix_in(x, sc1, sh1, norm_mix_g[l], w_in[l],
                                q_norm_g[l], k_norm_g[l], tm)
        yc = _conv_branch(z, conv_w[l], conv_b[l], conv_ln_g[l], conv_ln_b[l], min(512, s))
        ya = _attention(qn, kn, vt, rel_bias[l])
        i = l // 2
        if l % 2 == 0:
            x = _mix_out_dense_ffn(x, yc, ya, w_out[l], g1, norm_ffn_g[l], sc2, sh2, g2,
                                   ffn_w_gate[i], ffn_w_up[i], ffn_w_down[i],
                                   min(1024, s), 256)
        else:
            x1, hp, info, info_t, cnt = _mix_out_routed(
                x, yc, ya, w_out[l], g1, norm_ffn_g[l], sc2, sh2,
                moe_w_router[i], moe_b_router[i], tm)
            x = _moe_layer(hp, info, info_t, cnt, x1, g2,
                           moe_w_gate[i], moe_w_up[i], moe_w_down[i], tm)
    return x
```

```python
import functools

import jax
import jax.numpy as jnp
from jax import lax
from jax.experimental import pallas as pl
from jax.experimental.pallas import tpu as pltpu
from jax.experimental.pallas import tpu_sc as plsc

F32 = jnp.float32
BF16 = jnp.bfloat16

D_MODEL = 1024
CHUNK = 64
N_PREV_CHUNKS = 8
BAND_PAD = N_PREV_CHUNKS * CHUNK
D_CONV = 512
D_ATT = 512
HEAD_DIM = 64
N_HEADS = 8
CONV_WIDTH = 31
MAX_REL = 128
D_IN_COLS = 2 * D_CONV + 3 * D_ATT
N_EXPERTS = 8
EPS = 1e-6
NEG_INF = -1e30

LANES = 128
SUBLANES = 8
VMEM_LIMIT_BYTES = 56 * 1024 * 1024

ATT_HEADS = 4
ATT_GROUPS = N_HEADS // ATT_HEADS
ATT_W = ATT_HEADS * HEAD_DIM
ATT_Q = 2 * CHUNK
ATT_BAND = BAND_PAD + ATT_Q
ATT_L = ATT_HEADS * ATT_Q

CONV_HALO = 32
CONV_ROWS = 32
LANE_PAD_E = LANES

MOE_SUB = 512
MOE_HALF = MOE_SUB // 2
MOE_TILE = 9 * MOE_HALF
MOE_FC = 512


def _cparams(n_axes, vmem=VMEM_LIMIT_BYTES):
    return pltpu.CompilerParams(
        dimension_semantics=("arbitrary",) * n_axes, vmem_limit_bytes=vmem)


def _silu(v):
    return v * jax.nn.sigmoid(v)


def _pack_bf16_pairs(v):
    w = v.shape[1] // 2
    bits = lax.bitcast_convert_type(v.astype(BF16).astype(F32), jnp.uint32)
    packed = (bits[:, w:] & jnp.uint32(0xFFFF0000)) | (bits[:, :w] >> 16)
    return lax.bitcast_convert_type(packed, jnp.int32)


def _unpack_bf16_pairs(p):
    bits = lax.bitcast_convert_type(p, jnp.uint32)
    lo = lax.bitcast_convert_type(bits << 16, F32)
    hi = lax.bitcast_convert_type(bits & jnp.uint32(0xFFFF0000), F32)
    return jnp.concatenate([lo, hi], axis=1)


SC_CORES = 2
SC_SUBCORES = 16
SC_WORKERS = SC_CORES * SC_SUBCORES
SC_CHUNK = 64


def _sc_worker_id():
    return lax.axis_index("s") * SC_CORES + lax.axis_index("c")


def _sc_mesh():
    return plsc.VectorSubcoreMesh(core_axis_name="c", subcore_axis_name="s")


def _sc_gather_rows(table, idx):
    _, w = table.shape
    b = idx.shape[0]
    per_w = b // SC_WORKERS
    nch = per_w // SC_CHUNK

    def body(table_hbm, idx_hbm, out_hbm, idx_v, rows_v, gsem, wsem):
        wid = _sc_worker_id()
        base = wid * per_w
        pltpu.sync_copy(idx_hbm.at[wid], idx_v)
        gathers = [None] * nch
        writes = [None] * nch
        gathers[0] = pltpu.async_copy(table_hbm.at[idx_v.at[0]], rows_v.at[0], gsem.at[0])
        for c in range(nch):
            slot = c % 2
            gathers[c].wait()
            if c + 1 < nch:
                if c >= 1:
                    writes[c - 1].wait()
                gathers[c + 1] = pltpu.async_copy(
                    table_hbm.at[idx_v.at[c + 1]], rows_v.at[1 - slot], gsem.at[1 - slot])
            writes[c] = pltpu.async_copy(
                rows_v.at[slot], out_hbm.at[pl.ds(base + c * SC_CHUNK, SC_CHUNK)], wsem.at[slot])
        if nch >= 2:
            writes[nch - 2].wait()
        writes[nch - 1].wait()

    call = pl.kernel(
        body, mesh=_sc_mesh(),
        out_type=jax.ShapeDtypeStruct((b, w), jnp.int32),
        scratch_types=[pltpu.VMEM((nch, SC_CHUNK), jnp.int32),
                       pltpu.VMEM((2, SC_CHUNK, w), jnp.int32),
                       pltpu.SemaphoreType.DMA((2,)), pltpu.SemaphoreType.DMA((2,))],
        name="sc_gather_rows")
    return call(table, idx.reshape(SC_WORKERS, nch, SC_CHUNK))


def _sc_scatter_rows2(src, idx0, idx1, rows_out):
    n, w = src.shape
    per_w = n // SC_WORKERS
    nch = per_w // SC_CHUNK

    def body(src_hbm, i0_hbm, i1_hbm, out_hbm, i0_v, i1_v, rows_v, rsem, wsem):
        wid = _sc_worker_id()
        base = wid * per_w
        pltpu.sync_copy(i0_hbm.at[wid], i0_v)
        pltpu.sync_copy(i1_hbm.at[wid], i1_v)
        reads = [None] * nch
        writes = [None] * nch
        reads[0] = pltpu.async_copy(src_hbm.at[pl.ds(base, SC_CHUNK)], rows_v.at[0], rsem.at[0])
        for c in range(nch):
            slot = c % 2
            reads[c].wait()
            if c + 1 < nch:
                if c >= 1:
                    for wr in writes[c - 1]:
                        wr.wait()
                reads[c + 1] = pltpu.async_copy(
                    src_hbm.at[pl.ds(base + (c + 1) * SC_CHUNK, SC_CHUNK)],
                    rows_v.at[1 - slot], rsem.at[1 - slot])
            writes[c] = (
                pltpu.async_copy(rows_v.at[slot], out_hbm.at[i0_v.at[c]], wsem.at[slot, 0]),
                pltpu.async_copy(rows_v.at[slot], out_hbm.at[i1_v.at[c]], wsem.at[slot, 1]),
            )
        for c in range(max(nch - 2, 0), nch):
            for wr in writes[c]:
                wr.wait()

    call = pl.kernel(
        body, mesh=_sc_mesh(),
        out_type=jax.ShapeDtypeStruct((rows_out, w), jnp.int32),
        scratch_types=[pltpu.VMEM((nch, SC_CHUNK), jnp.int32),
                       pltpu.VMEM((nch, SC_CHUNK), jnp.int32),
                       pltpu.VMEM((2, SC_CHUNK, w), jnp.int32),
                       pltpu.SemaphoreType.DMA((2,)), pltpu.SemaphoreType.DMA((2, 2))],
        name="sc_scatter_rows")
    shape3 = (SC_WORKERS, nch, SC_CHUNK)
    return call(src, idx0.reshape(shape3), idx1.reshape(shape3))


def _ada_kernel(c_ref, w_ref, b_ref, o_ref):
    ca = _silu(c_ref[...]).astype(BF16)
    w = w_ref[0].astype(BF16)
    o_ref[0] = jnp.dot(ca, w, preferred_element_type=F32) + b_ref[0]


def _ada_mod(c, w_ada, b_ada):
    depth, d, n6 = w_ada.shape
    b = c.shape[0]
    rows = 16
    c_pad = jnp.zeros((rows, d), F32).at[:b].set(c)
    tn = 1536
    out = pl.pallas_call(
        _ada_kernel,
        out_shape=jax.ShapeDtypeStruct((depth, rows, n6), F32),
        grid=(depth, n6 // tn),
        in_specs=[
            pl.BlockSpec((rows, d), lambda l, j: (0, 0)),
            pl.BlockSpec((1, d, tn), lambda l, j: (l, 0, j)),
            pl.BlockSpec((1, 1, tn), lambda l, j: (l, 0, j)),
        ],
        out_specs=pl.BlockSpec((1, rows, tn), lambda l, j: (l, 0, j)),
        compiler_params=_cparams(2),
        name="ada_mod",
    )(c_pad, w_ada, b_ada.reshape(depth, 1, n6))
    return out[:, :b]


def _rms_mod(xf, g, sc, sh):
    ms = jnp.mean(xf * xf, axis=-1, keepdims=True)
    return xf * lax.rsqrt(ms + EPS) * g * (1.0 + sc) + sh


def _mix_in_kernel(x_ref, sc_ref, sh_ref, g_ref, w_ref, gq_ref, gk_ref, ones_ref,
                   z_ref, q_ref, k_ref, vt_ref, wbf_ref):
    first = jnp.logical_and(pl.program_id(0) == 0, pl.program_id(1) == 0)

    @pl.when(first)
    def _():
        wbf_ref[...] = w_ref[...].astype(BF16)

    h = _rms_mod(x_ref[0], g_ref[...], sc_ref[0], sh_ref[0]).astype(BF16)
    proj = jnp.dot(h, wbf_ref[...], preferred_element_type=F32)

    a = proj[:, :D_CONV]
    gate = proj[:, D_CONV:2 * D_CONV]
    z_ref[0] = (a * jax.nn.sigmoid(gate)).astype(BF16)

    def head_norm(t, g):
        sq = (t * t).astype(BF16)
        ss = jnp.concatenate(
            [jnp.dot(sq[:, c:c + ATT_W], ones_ref[...], preferred_element_type=F32)
             for c in range(0, D_ATT, ATT_W)], axis=1)
        return (t * lax.rsqrt(ss * (1.0 / HEAD_DIM) + EPS) * g).astype(BF16)

    o = 2 * D_CONV
    q_ref[0] = head_norm(proj[:, o:o + D_ATT], gq_ref[...])
    k_ref[0] = head_norm(proj[:, o + D_ATT:o + 2 * D_ATT], gk_ref[...])
    v = proj[:, o + 2 * D_ATT:]
    tm = v.shape[0]
    for cidx in range(tm // ATT_Q):
        vt_ref[0, cidx] = v[cidx * ATT_Q:(cidx + 1) * ATT_Q, :].T.astype(BF16)


def _mix_in(x, sc, sh, g, w_in, layer, gq, gk, tm):
    b, s, d = x.shape
    ones_bd = (jnp.arange(ATT_W)[:, None] // HEAD_DIM
               == jnp.arange(ATT_W)[None, :] // HEAD_DIM).astype(BF16)
    gq_t = (jnp.tile(gq, N_HEADS) * (HEAD_DIM ** -0.5)).reshape(1, D_ATT)
    gk_t = jnp.tile(gk, N_HEADS).reshape(1, D_ATT)
    row = lambda bi, ti: (bi, ti, 0)
    per_b = lambda bi, ti: (bi, 0, 0)
    const2 = lambda bi, ti: (0, 0)
    return pl.pallas_call(
        _mix_in_kernel,
        out_shape=(
            jax.ShapeDtypeStruct((b, s, D_CONV), BF16),
            jax.ShapeDtypeStruct((b, s, D_ATT), BF16),
            jax.ShapeDtypeStruct((b, s, D_ATT), BF16),
            jax.ShapeDtypeStruct((b, s // ATT_Q, D_ATT, ATT_Q), BF16),
        ),
        grid=(b, s // tm),
        in_specs=[
            pl.BlockSpec((1, tm, d), row),
            pl.BlockSpec((1, 1, d), per_b),
            pl.BlockSpec((1, 1, d), per_b),
            pl.BlockSpec((1, d), const2),
            pl.BlockSpec((None, d, D_IN_COLS), lambda bi, ti: (layer, 0, 0)),
            pl.BlockSpec((1, D_ATT), const2),
            pl.BlockSpec((1, D_ATT), const2),
            pl.BlockSpec((ATT_W, ATT_W), const2),
        ],
        out_specs=(
            pl.BlockSpec((1, tm, D_CONV), row),
            pl.BlockSpec((1, tm, D_ATT), row),
            pl.BlockSpec((1, tm, D_ATT), row),
            pl.BlockSpec((1, tm // ATT_Q, D_ATT, ATT_Q), lambda bi, ti: (bi, ti, 0, 0)),
        ),
        scratch_shapes=[pltpu.VMEM((d, D_IN_COLS), BF16)],
        compiler_params=_cparams(2),
        name="mix_in",
    )(x, sc, sh, g.reshape(1, d), w_in, gq_t, gk_t, ones_bd)


def _conv_kernel(zc_ref, zp_ref, w_ref, cb_ref, lg_ref, lb_ref, o_ref, win_ref, sh_ref,
                 acc_ref):
    tt = zc_ref.shape[1]
    t = pl.program_id(1)
    halo = zp_ref[0].astype(F32)
    win_ref[0:CONV_HALO, :] = jnp.where(t == 0, 0.0, halo)
    win_ref[CONV_HALO:, :] = zc_ref[0].astype(F32)
    span = tt + CONV_HALO - SUBLANES
    for sft in range(1, SUBLANES):
        sh_ref[sft - 1, 0:span, :] = win_ref[sft:sft + span, :]
    base = CONV_HALO - (CONV_WIDTH - 1)
    tiles = CONV_ROWS // SUBLANES

    def group(gidx, carry):
        r = pl.multiple_of(gidx * CONV_ROWS, CONV_ROWS)
        acc = jnp.zeros((tiles, SUBLANES, D_CONV), F32) + cb_ref[...]
        for j in range(CONV_WIDTH):
            whole, sft = divmod(base + j, SUBLANES)
            start = pl.multiple_of(r + whole * SUBLANES, SUBLANES)
            if sft == 0:
                tap = win_ref[pl.ds(start, CONV_ROWS), :]
            else:
                tap = sh_ref[sft - 1, pl.ds(start, CONV_ROWS), :]
            acc = acc + tap.reshape(tiles, SUBLANES, D_CONV) * w_ref[j]
        acc_ref[pl.ds(r, CONV_ROWS), :] = acc.reshape(CONV_ROWS, D_CONV)
        return carry

    lax.fori_loop(0, tt // CONV_ROWS, group, 0)
    acc = acc_ref[...]
    mu = jnp.mean(acc, axis=-1, keepdims=True)
    xc = acc - mu
    var = jnp.mean(xc * xc, axis=-1, keepdims=True)
    y = xc * lax.rsqrt(var + EPS) * lg_ref[...] + lb_ref[...]
    o_ref[0] = _silu(y).astype(BF16)


def _conv_branch(z, conv_w, conv_b, ln_g, ln_b, tt):
    b, s, c = z.shape
    hb = tt // CONV_HALO
    w_tiles = jnp.broadcast_to(conv_w.reshape(CONV_WIDTH, 1, c), (CONV_WIDTH, SUBLANES, c))
    const2 = lambda bi, ti: (0, 0)
    return pl.pallas_call(
        _conv_kernel,
        out_shape=jax.ShapeDtypeStruct((b, s, c), BF16),
        grid=(b, s // tt),
        in_specs=[
            pl.BlockSpec((1, tt, c), lambda bi, ti: (bi, ti, 0)),
            pl.BlockSpec((1, CONV_HALO, c),
                         lambda bi, ti: (bi, jnp.maximum(ti * hb - 1, 0), 0)),
            pl.BlockSpec((CONV_WIDTH, SUBLANES, c), lambda bi, ti: (0, 0, 0)),
            pl.BlockSpec((1, c), const2),
            pl.BlockSpec((1, c), const2),
            pl.BlockSpec((1, c), const2),
        ],
        out_specs=pl.BlockSpec((1, tt, c), lambda bi, ti: (bi, ti, 0)),
        scratch_shapes=[pltpu.VMEM((tt + CONV_HALO, c), F32),
                        pltpu.VMEM((SUBLANES - 1, tt + CONV_HALO, c), F32),
                        pltpu.VMEM((tt, c), F32)],
        compiler_params=_cparams(2),
        name="conv_branch",
    )(z, z, w_tiles, conv_b.reshape(1, c),
      ln_g.reshape(1, c), ln_b.reshape(1, c))


def _attn_kernel(q_ref, k_ref, vt_ref, bias_ref, o_ref, kpad_ref, vtpad_ref,
                 st0_ref, st1_ref, pb0_ref, pb1_ref, den0_ref, den1_ref):
    st_refs = (st0_ref, st1_ref)
    pb_refs = (pb0_ref, pb1_ref)
    den_refs = (den0_ref, den1_ref)
    s = q_ref.shape[1]
    npad = BAND_PAD // ATT_Q
    kpad_ref[0:BAND_PAD, :] = jnp.zeros((BAND_PAD, ATT_W), BF16)
    kpad_ref[BAND_PAD:, :] = k_ref[0]
    vtpad_ref[0:npad] = jnp.zeros((npad, ATT_W, ATT_Q), BF16)
    vtpad_ref[npad:] = vt_ref[0]

    iota = lambda shape, dim: lax.broadcasted_iota(jnp.int32, shape, dim)
    q_shift = ATT_Q.bit_length() - 1
    d_shift = HEAD_DIM.bit_length() - 1
    qb_mask = (iota((ATT_L, ATT_W), 0) >> q_shift) == (iota((ATT_L, ATT_W), 1) >> d_shift)
    ot_mask = (iota((ATT_W, ATT_L), 0) >> d_shift) == (iota((ATT_W, ATT_L), 1) >> q_shift)
    sel = jnp.where((iota((ATT_Q, ATT_L), 1) & (ATT_Q - 1)) == iota((ATT_Q, ATT_L), 0),
                    1.0, 0.0).astype(BF16)
    key_row = lax.broadcasted_iota(jnp.int32, (ATT_BAND, ATT_L), 0)
    contract_last = (((1,), (1,)), ((), ()))

    def scores(m, p):
        r0 = pl.multiple_of(m * ATT_Q, ATT_Q)
        qt = q_ref[0, pl.ds(r0, ATT_Q), :]
        qb = jnp.where(qb_mask, jnp.concatenate([qt] * ATT_HEADS, axis=0), 0)
        kb = kpad_ref[pl.ds(r0, ATT_BAND), :]
        st_refs[p][...] = lax.dot_general(kb, qb.astype(BF16), contract_last,
                                          preferred_element_type=F32)

    def softmax(m, p, masked):
        st = st_refs[p][...] + bias_ref[0]
        if masked:
            st = jnp.where(key_row >= BAND_PAD - m * ATT_Q, st, NEG_INF)
        mx = jnp.max(st, axis=0, keepdims=True)
        e = jnp.exp(st - mx)
        den_refs[p][...] = jnp.sum(e, axis=0, keepdims=True)
        pb_refs[p][...] = e.astype(BF16)

    def values(m, p):
        r0 = pl.multiple_of(m * ATT_Q, ATT_Q)
        vb = jnp.concatenate([vtpad_ref[m + c] for c in range(ATT_BAND // ATT_Q)],
                             axis=1)
        ot = jnp.dot(vb, pb_refs[p][...], preferred_element_type=F32)
        ot = jnp.where(ot_mask, ot / den_refs[p][...], 0.0).astype(BF16)
        y = lax.dot_general(sel, ot, contract_last, preferred_element_type=F32)
        o_ref[0, pl.ds(r0, ATT_Q), :] = y.astype(BF16)

    n = s // ATT_Q
    n_masked = BAND_PAD // ATT_Q
    scores(0, 0)
    scores(1, 1)
    softmax(0, 0, True)

    def pair(masked, i, carry):
        m = 2 * i
        scores(m, 0)
        softmax(m - 1, 1, masked)
        values(m - 2, 0)
        scores(m + 1, 1)
        softmax(m, 0, masked)
        values(m - 1, 1)
        return carry

    split = min(n // 2, n_masked // 2 + 1)
    lax.fori_loop(1, split, functools.partial(pair, True), 0)
    lax.fori_loop(split, n // 2, functools.partial(pair, False), 0)
    softmax(n - 1, 1, n - 1 < n_masked)
    values(n - 2, 0)
    values(n - 1, 1)


def _attn_bias_t(rel_bias):
    rb = rel_bias.astype(F32)
    nu = ATT_BAND + ATT_Q - 1
    n_low = BAND_PAD - MAX_REL + ATT_Q
    t = jnp.concatenate([jnp.repeat(rb[:, :1], n_low, axis=1),
                         rb[:, 1:1 + nu - n_low]], axis=1)
    tp = jnp.pad(t, ((0, 0), (0, 1)))
    skew = jnp.tile(tp, (1, ATT_Q))[:, :ATT_Q * nu].reshape(N_HEADS, ATT_Q, nu)
    bias = skew[:, :, ATT_Q - 1:]
    bias = bias.reshape(ATT_GROUPS, ATT_HEADS, ATT_Q, ATT_BAND)
    bias = bias.transpose(0, 3, 1, 2).reshape(ATT_GROUPS, ATT_BAND, ATT_L)
    r = jnp.arange(ATT_BAND)[:, None]
    qq = jnp.arange(ATT_L)[None, :] % ATT_Q
    first = (qq // CHUNK) * CHUNK
    valid = (r >= first) & (r < first + BAND_PAD + CHUNK)
    return jnp.where(valid[None], bias, NEG_INF)


def _attention(qn, kn, vt, rel_bias):
    b, s, _ = qn.shape
    bias_t = _attn_bias_t(rel_bias)
    nck = s // ATT_Q
    assert nck >= 4 and nck % 2 == 0, "the attention pipeline runs query steps in pairs"
    return pl.pallas_call(
        _attn_kernel,
        out_shape=jax.ShapeDtypeStruct((b, s, D_ATT), BF16),
        grid=(b, ATT_GROUPS),
        in_specs=[
            pl.BlockSpec((1, s, ATT_W), lambda bi, gi: (bi, 0, gi)),
            pl.BlockSpec((1, s, ATT_W), lambda bi, gi: (bi, 0, gi)),
            pl.BlockSpec((1, nck, ATT_W, ATT_Q), lambda bi, gi: (bi, 0, gi, 0)),
            pl.BlockSpec((1, ATT_BAND, ATT_L), lambda bi, gi: (gi, 0, 0)),
        ],
        out_specs=pl.BlockSpec((1, s, ATT_W), lambda bi, gi: (bi, 0, gi)),
        scratch_shapes=[
            pltpu.VMEM((s + BAND_PAD, ATT_W), BF16),
            pltpu.VMEM((nck + BAND_PAD // ATT_Q, ATT_W, ATT_Q), BF16),
            pltpu.VMEM((ATT_BAND, ATT_L), F32), pltpu.VMEM((ATT_BAND, ATT_L), F32),
            pltpu.VMEM((ATT_BAND, ATT_L), BF16), pltpu.VMEM((ATT_BAND, ATT_L), BF16),
            pltpu.VMEM((1, ATT_L), F32), pltpu.VMEM((1, ATT_L), F32),
        ],
        compiler_params=_cparams(2),
        name="band_attention",
    )(qn, kn, vt, bias_t)


def _mix_out_kernel(x_ref, yc_ref, ya_ref, w_ref, g1_ref, gf_ref, sc_ref, sh_ref,
                    wr_ref, br_ref, x1_ref, h_ref, info_ref, info_t_ref, cnt_ref,
                    wbf_ref, carry_ref):
    first = jnp.logical_and(pl.program_id(0) == 0, pl.program_id(1) == 0)

    @pl.when(first)
    def _():
        wbf_ref[...] = w_ref[...].astype(BF16)
        carry_ref[...] = jnp.zeros_like(carry_ref)

    y = jnp.dot(yc_ref[0], wbf_ref[0:D_CONV, :], preferred_element_type=F32)
    y = y + jnp.dot(ya_ref[0], wbf_ref[D_CONV:, :], preferred_element_type=F32)
    x1 = x_ref[0] + g1_ref[0] * y
    x1_ref[0] = x1
    h = _rms_mod(x1, gf_ref[...], sc_ref[0], sh_ref[0])
    h_ref[0] = _pack_bf16_pairs(h)
    logits = jnp.dot(h.astype(BF16), wr_ref[...], preferred_element_type=F32) + br_ref[...]
    _route_tile(logits, info_ref, info_t_ref, cnt_ref, carry_ref)


def _mix_out_routed(x, yc, ya, w_out, layer, g1, gf, sc, sh, w_router, b_router, tm):
    b, s, d = x.shape
    nt = s // tm
    row = lambda bi, ti: (bi, ti, 0)
    per_b = lambda bi, ti: (bi, 0, 0)
    const2 = lambda bi, ti: (0, 0)
    w_pad = jnp.zeros((d, LANE_PAD_E), BF16).at[:, :N_EXPERTS].set(w_router.astype(BF16))
    b_pad = jnp.full((1, LANE_PAD_E), -jnp.inf, F32).at[0, :N_EXPERTS].set(
        b_router.astype(F32))
    return pl.pallas_call(
        _mix_out_kernel,
        out_shape=(jax.ShapeDtypeStruct((b, s, d), F32),
                   jax.ShapeDtypeStruct((b, s, d // 2), jnp.int32),
                   jax.ShapeDtypeStruct((b, s, LANE_PAD_E), F32),
                   jax.ShapeDtypeStruct((SUBLANES, b * s), F32),
                   jax.ShapeDtypeStruct((SUBLANES, LANE_PAD_E), F32)),
        grid=(b, nt),
        in_specs=[
            pl.BlockSpec((1, tm, d), row),
            pl.BlockSpec((1, tm, D_CONV), row),
            pl.BlockSpec((1, tm, D_ATT), row),
            pl.BlockSpec((None, d, d), lambda bi, ti: (layer, 0, 0)),
            pl.BlockSpec((1, 1, d), per_b),
            pl.BlockSpec((1, d), const2),
            pl.BlockSpec((1, 1, d), per_b),
            pl.BlockSpec((1, 1, d), per_b),
            pl.BlockSpec((d, LANE_PAD_E), const2),
            pl.BlockSpec((1, LANE_PAD_E), const2),
        ],
        out_specs=(pl.BlockSpec((1, tm, d), row), pl.BlockSpec((1, tm, d // 2), row),
                   pl.BlockSpec((1, tm, LANE_PAD_E), row),
                   pl.BlockSpec((SUBLANES, tm), lambda bi, ti: (0, bi * nt + ti)),
                   pl.BlockSpec((SUBLANES, LANE_PAD_E), const2)),
        scratch_shapes=[pltpu.VMEM((d, d), BF16), pltpu.VMEM((SUBLANES, LANE_PAD_E), F32)],
        compiler_params=_cparams(2),
        name="mix_out",
    )(x, yc, ya, w_out, g1, gf.reshape(1, d), sc, sh, w_pad, b_pad)


def _cast_ffn_kernel(wg_ref, wu_ref, wd_ref, wo_ref, wg3_ref, wu3_ref, wd3_ref, wob_ref):
    wg3_ref[0] = wg_ref[...].astype(BF16)
    wu3_ref[0] = wu_ref[...].astype(BF16)
    wd3_ref[0] = wd_ref[...].astype(BF16)

    @pl.when(pl.program_id(0) == 0)
    def _():
        wob_ref[...] = wo_ref[...].astype(BF16)


def _cast_ffn_weights(wg, wu, wd, w_out, layer, fc):
    d, ff = wg.shape
    nf = ff // fc
    return pl.pallas_call(
        _cast_ffn_kernel,
        out_shape=(jax.ShapeDtypeStruct((nf, d, fc), BF16),
                   jax.ShapeDtypeStruct((nf, d, fc), BF16),
                   jax.ShapeDtypeStruct((nf, fc, d), BF16),
                   jax.ShapeDtypeStruct((d, d), BF16)),
        grid=(nf,),
        in_specs=[
            pl.BlockSpec((d, fc), lambda f: (0, f)),
            pl.BlockSpec((d, fc), lambda f: (0, f)),
            pl.BlockSpec((fc, d), lambda f: (f, 0)),
            pl.BlockSpec((None, d, d), lambda f: (layer, 0, 0)),
        ],
        out_specs=(pl.BlockSpec((1, d, fc), lambda f: (f, 0, 0)),
                   pl.BlockSpec((1, d, fc), lambda f: (f, 0, 0)),
                   pl.BlockSpec((1, fc, d), lambda f: (f, 0, 0)),
                   pl.BlockSpec((d, d), lambda f: (0, 0))),
        compiler_params=_cparams(1),
        name="cast_ffn_weights",
    )(wg, wu, wd, w_out)


def _mix_ffn_kernel(x_ref, yc_ref, ya_ref, wo_ref, g1_ref, gf_ref, sc_ref, sh_ref, g2_ref,
                    wg_ref, wu_ref, wd_ref, o_ref, acc_ref, x1_ref, h_ref):
    f = pl.program_id(2)

    @pl.when(f == 0)
    def _():
        y = jnp.dot(yc_ref[0], wo_ref[0:D_CONV, :], preferred_element_type=F32)
        y = y + jnp.dot(ya_ref[0], wo_ref[D_CONV:, :], preferred_element_type=F32)
        x1 = x_ref[0] + g1_ref[0] * y
        x1_ref[...] = x1
        h_ref[...] = _rms_mod(x1, gf_ref[...], sc_ref[0], sh_ref[0]).astype(BF16)
        acc_ref[...] = jnp.zeros_like(acc_ref)

    h = h_ref[...]
    a = jnp.dot(h, wg_ref[0], preferred_element_type=F32)
    u = jnp.dot(h, wu_ref[0], preferred_element_type=F32)
    t = (_silu(a) * u).astype(BF16)
    acc_ref[...] += jnp.dot(t, wd_ref[0], preferred_element_type=F32)

    @pl.when(f == pl.num_programs(2) - 1)
    def _():
        o_ref[0] = x1_ref[...] + g2_ref[0] * acc_ref[...]


def _mix_out_dense_ffn(x, yc, ya, w_out, layer, g1, gf, sc, sh, g2, wg, wu, wd, tm, fc):
    b, s, d = x.shape
    wg3, wu3, wd3, wob = _cast_ffn_weights(wg, wu, wd, w_out, layer, fc)
    nf = wg3.shape[0]
    row = lambda bi, ti, fi: (bi, ti, 0)
    per_b = lambda bi, ti, fi: (bi, 0, 0)
    const2 = lambda bi, ti, fi: (0, 0)
    chunk = lambda bi, ti, fi: (fi, 0, 0)
    return pl.pallas_call(
        _mix_ffn_kernel,
        out_shape=jax.ShapeDtypeStruct((b, s, d), F32),
        grid=(b, s // tm, nf),
        in_specs=[
            pl.BlockSpec((1, tm, d), row),
            pl.BlockSpec((1, tm, D_CONV), row),
            pl.BlockSpec((1, tm, D_ATT), row),
            pl.BlockSpec((d, d), const2),
            pl.BlockSpec((1, 1, d), per_b),
            pl.BlockSpec((1, d), const2),
            pl.BlockSpec((1, 1, d), per_b),
            pl.BlockSpec((1, 1, d), per_b),
            pl.BlockSpec((1, 1, d), per_b),
            pl.BlockSpec((1, d, fc), chunk),
            pl.BlockSpec((1, d, fc), chunk),
            pl.BlockSpec((1, fc, d), chunk),
        ],
        out_specs=pl.BlockSpec((1, tm, d), row),
        scratch_shapes=[pltpu.VMEM((tm, d), F32), pltpu.VMEM((tm, d), F32),
                        pltpu.VMEM((tm, d), BF16)],
        compiler_params=_cparams(3),
        name="mix_out_dense_ffn",
    )(x, yc, ya, wob, g1, gf.reshape(1, d), sc, sh, g2, wg3, wu3, wd3)


def _route_tile(logits, info_ref, info_t_ref, cnt_ref, carry_ref):
    tr = logits.shape[0]
    lane = lax.broadcasted_iota(jnp.int32, (tr, LANE_PAD_E), 1).astype(F32)
    no_lane = float(LANE_PAD_E)
    v0 = jnp.max(logits, axis=-1, keepdims=True)
    i0 = jnp.min(jnp.where(logits == v0, lane, no_lane), axis=-1, keepdims=True)
    rest = jnp.where(lane == i0, -jnp.inf, logits)
    v1 = jnp.max(rest, axis=-1, keepdims=True)
    i1 = jnp.min(jnp.where(rest == v1, lane, no_lane), axis=-1, keepdims=True)
    e1 = jnp.exp(v1 - v0)
    w0 = 1.0 / (1.0 + e1)
    w1 = e1 / (1.0 + e1)
    oh0 = lane == i0
    oh1 = lane == i1
    cnt = jnp.where(jnp.logical_or(oh0, oh1), 1.0, 0.0)
    tri = (lax.broadcasted_iota(jnp.int32, (tr, tr), 1)
           < lax.broadcasted_iota(jnp.int32, (tr, tr), 0)).astype(BF16)
    before = jnp.dot(tri, cnt.astype(BF16), preferred_element_type=F32) + carry_ref[0:1, :]
    r0 = jnp.sum(jnp.where(oh0, before, 0.0), axis=-1, keepdims=True)
    r1 = jnp.sum(jnp.where(oh1, before, 0.0), axis=-1, keepdims=True)
    carry_ref[...] = carry_ref[...] + jnp.sum(cnt, axis=0, keepdims=True)
    cnt_ref[...] = carry_ref[...]
    info = jnp.where(lane == 0, i0, 0.0)
    info = jnp.where(lane == 1, i1, info)
    info = jnp.where(lane == 2, w0, info)
    info = jnp.where(lane == 3, w1, info)
    info = jnp.where(lane == 4, r0, info)
    info = jnp.where(lane == 5, r1, info)
    info_ref[0] = info
    for c in range(tr // LANES):
        blk = info[c * LANES:(c + 1) * LANES, :].T
        info_t_ref[:, c * LANES:(c + 1) * LANES] = blk[0:SUBLANES, :]


def _moe_kernel(te_ref, tn_ref, xs_ref, wg_ref, wu_ref, wd_ref, ys_ref,
                acc_ref, xb_ref, wgb_ref, wub_ref, wdb_ref):
    i = pl.program_id(0)
    f = pl.program_id(1)
    nhalf = tn_ref[i]
    nfull = nhalf >> 1

    @pl.when(f == 0)
    def _():
        acc_ref[...] = jnp.zeros_like(acc_ref)

        def unpack(sidx, carry):
            r = pl.multiple_of(sidx * MOE_HALF, MOE_HALF)
            xb_ref[pl.ds(r, MOE_HALF), :] = _unpack_bf16_pairs(
                xs_ref[pl.ds(r, MOE_HALF), :]).astype(BF16)
            return carry

        lax.fori_loop(0, nhalf, unpack, 0)

    @pl.when(nhalf > 0)
    def _():
        wgb_ref[...] = wg_ref[0].astype(BF16)
        wub_ref[...] = wu_ref[0].astype(BF16)
        wdb_ref[...] = wd_ref[0].astype(BF16)

    def block(r, rows):
        xb = xb_ref[pl.ds(r, rows), :]
        a = jnp.dot(xb, wgb_ref[...], preferred_element_type=F32)
        u = jnp.dot(xb, wub_ref[...], preferred_element_type=F32)
        t = (_silu(a) * u).astype(BF16)
        acc_ref[pl.ds(r, rows), :] += jnp.dot(t, wdb_ref[...], preferred_element_type=F32)

    def sub(sidx, carry):
        block(pl.multiple_of(sidx * MOE_SUB, MOE_SUB), MOE_SUB)
        return carry

    lax.fori_loop(0, nfull, sub, 0)

    @pl.when((nhalf & 1) == 1)
    def _():
        block(pl.multiple_of(nfull * MOE_SUB, MOE_SUB), MOE_HALF)

    @pl.when(f == pl.num_programs(1) - 1)
    def _():
        ys_ref[...] = _pack_bf16_pairs(acc_ref[...])


def _moe_ffn(xs, tile_e, tile_nsub, wg, wu, wd):
    rpad, dw = xs.shape
    d = 2 * dw
    ntiles = rpad // MOE_TILE
    ff = wg.shape[2]
    nf = ff // MOE_FC

    def fcol(i, f, tn):
        return jnp.where(tn[i] > 0, f, nf - 1)

    return pl.pallas_call(
        _moe_kernel,
        out_shape=jax.ShapeDtypeStruct((rpad, dw), jnp.int32),
        grid_spec=pltpu.PrefetchScalarGridSpec(
            num_scalar_prefetch=2,
            grid=(ntiles, nf),
            in_specs=[
                pl.BlockSpec((MOE_TILE, dw), lambda i, f, te, tn: (i, 0)),
                pl.BlockSpec((1, d, MOE_FC), lambda i, f, te, tn: (te[i], 0, fcol(i, f, tn))),
                pl.BlockSpec((1, d, MOE_FC), lambda i, f, te, tn: (te[i], 0, fcol(i, f, tn))),
                pl.BlockSpec((1, MOE_FC, d), lambda i, f, te, tn: (te[i], fcol(i, f, tn), 0)),
            ],
            out_specs=pl.BlockSpec((MOE_TILE, dw), lambda i, f, te, tn: (i, 0)),
            scratch_shapes=[
                pltpu.VMEM((MOE_TILE, d), F32),
                pltpu.VMEM((MOE_TILE, d), BF16),
                pltpu.VMEM((d, MOE_FC), BF16),
                pltpu.VMEM((d, MOE_FC), BF16),
                pltpu.VMEM((MOE_FC, d), BF16),
            ],
        ),
        compiler_params=_cparams(2),
        name="moe_ffn",
    )(tile_e, tile_nsub, xs, wg, wu, wd)


def _combine_kernel(x_ref, y0_ref, y1_ref, info_ref, g2_ref, o_ref):
    info = info_ref[0]
    w0 = info[:, 2:3]
    w1 = info[:, 3:4]
    f = w0 * _unpack_bf16_pairs(y0_ref[0, 0]) + w1 * _unpack_bf16_pairs(y1_ref[0, 0])
    o_ref[0] = x_ref[0] + g2_ref[0] * f


def _combine(x1, y01, info, g2, tm):
    b, s, d = x1.shape
    row = lambda bi, ti: (bi, ti, 0)
    return pl.pallas_call(
        _combine_kernel,
        out_shape=jax.ShapeDtypeStruct((b, s, d), F32),
        grid=(b, s // tm),
        in_specs=[
            pl.BlockSpec((1, tm, d), row),
            pl.BlockSpec((1, 1, tm, d // 2), lambda bi, ti: (0, bi, ti, 0)),
            pl.BlockSpec((1, 1, tm, d // 2), lambda bi, ti: (1, bi, ti, 0)),
            pl.BlockSpec((1, tm, LANE_PAD_E), row),
            pl.BlockSpec((1, 1, d), lambda bi, ti: (bi, 0, 0)),
        ],
        out_specs=pl.BlockSpec((1, tm, d), row),
        compiler_params=_cparams(2),
        name="moe_combine",
    )(x1, y01, y01, info, g2)


def _moe_layer(hp, info, info_t, cnt, x1, g2, wg, wu, wd, tm):
    b, s, d = x1.shape
    n = b * s
    e0 = info_t[0].astype(jnp.int32)
    e1 = info_t[1].astype(jnp.int32)
    r0 = info_t[4].astype(jnp.int32)
    r1 = info_t[5].astype(jnp.int32)
    counts = cnt[0, :N_EXPERTS].astype(jnp.int32)

    ntiles = (2 * n) // MOE_TILE + N_EXPERTS
    tiles_per_e = (counts + MOE_TILE - 1) // MOE_TILE
    tile_end = jnp.cumsum(tiles_per_e)
    tile_start = tile_end - tiles_per_e
    total = tile_end[-1]
    tidx = jnp.arange(ntiles, dtype=jnp.int32)
    live = tidx < total
    tclip = jnp.minimum(tidx, total - 1)
    tile_e = jnp.minimum(jnp.sum(tclip[:, None] >= tile_end[None, :], axis=1),
                         N_EXPERTS - 1).astype(jnp.int32)
    rows_left = counts[tile_e] - (tclip - tile_start[tile_e]) * MOE_TILE
    rows_here = jnp.clip(rows_left, 0, MOE_TILE)
    tile_nsub = jnp.where(live, (rows_here + MOE_HALF - 1) // MOE_HALF, 0).astype(jnp.int32)

    row_start = tile_start * MOE_TILE
    eid = jnp.arange(N_EXPERTS, dtype=jnp.int32)[None, :]
    pos0 = jnp.sum(jnp.where(e0[:, None] == eid, row_start[None, :], 0), axis=1) + r0
    pos1 = jnp.sum(jnp.where(e1[:, None] == eid, row_start[None, :], 0), axis=1) + r1

    xs = _sc_scatter_rows2(hp.reshape(n, d // 2), pos0, pos1, ntiles * MOE_TILE)
    ys = _moe_ffn(xs, tile_e, tile_nsub, wg, wu, wd)
    y01 = _sc_gather_rows(ys, jnp.concatenate([pos0, pos1]))
    return _combine(x1, y01.reshape(2, b, s, d // 2), info.reshape(b, s, LANE_PAD_E), g2, tm)


def kernel(x, c, w_ada, b_ada, norm_mix_g, norm_ffn_g, w_in, w_out, conv_w, conv_b,
           conv_ln_g, conv_ln_b, q_norm_g, k_norm_g, rel_bias, ffn_w_gate, ffn_w_up,
           ffn_w_down, moe_w_router, moe_b_router, moe_w_gate, moe_w_up, moe_w_down):
    b, s, d = x.shape
    depth = w_ada.shape[0]
    tm = min(512, s)
    mod = _ada_mod(c, w_ada, b_ada)
    for l in range(depth):
        sh1, sc1, g1, sh2, sc2, g2 = [
            mod[l, :, j * d:(j + 1) * d].reshape(b, 1, d) for j in range(6)]
        z, qn, kn, vt = _mix_in(x, sc1, sh1, norm_mix_g[l], w_in, l,
                                q_norm_g[l], k_norm_g[l], tm)
        yc = _conv_branch(z, conv_w[l], conv_b[l], conv_ln_g[l], conv_ln_b[l], min(512, s))
        ya = _attention(qn, kn, vt, rel_bias[l])
        i = l // 2
        if l % 2 == 0:
            x = _mix_out_dense_ffn(x, yc, ya, w_out, l, g1, norm_ffn_g[l], sc2, sh2, g2,
                                   ffn_w_gate[i], ffn_w_up[i], ffn_w_down[i],
                                   min(1024, s), 256)
        else:
            x1, hp, info, info_t, cnt = _mix_out_routed(
                x, yc, ya, w_out, l, g1, norm_ffn_g[l], sc2, sh2,
                moe_w_router[i], moe_b_router[i], tm)
            x = _moe_layer(hp, info, info_t, cnt, x1, g2,
                           moe_w_gate[i], moe_w_up[i], moe_w_down[i], tm)
    return x
```

```python
import functools

import jax
import jax.numpy as jnp
from jax import lax
from jax.experimental import pallas as pl
from jax.experimental.pallas import tpu as pltpu
from jax.experimental.pallas import tpu_sc as plsc

F32 = jnp.float32
BF16 = jnp.bfloat16

D_MODEL = 1024
CHUNK = 64
N_PREV_CHUNKS = 8
BAND_PAD = N_PREV_CHUNKS * CHUNK
D_CONV = 512
D_ATT = 512
HEAD_DIM = 64
N_HEADS = 8
CONV_WIDTH = 31
MAX_REL = 128
D_IN_COLS = 2 * D_CONV + 3 * D_ATT
N_EXPERTS = 8
EPS = 1e-6
NEG_INF = -1e30

LANES = 128
SUBLANES = 8
VMEM_LIMIT_BYTES = 56 * 1024 * 1024

ATT_HEADS = 4
ATT_GROUPS = N_HEADS // ATT_HEADS
ATT_W = ATT_HEADS * HEAD_DIM
ATT_Q = 2 * CHUNK
ATT_BAND = BAND_PAD + ATT_Q
ATT_L = ATT_HEADS * ATT_Q

CONV_HALO = 32
CONV_ROWS = 32
LANE_PAD_E = LANES

MOE_HALF = 256
MOE_SUB = 4 * MOE_HALF
MOE_TAILS = (2 * MOE_HALF, MOE_HALF)
MOE_TILE = 9 * MOE_HALF
MOE_FC = 512


def _cparams(n_axes, vmem=VMEM_LIMIT_BYTES):
    return pltpu.CompilerParams(
        dimension_semantics=("arbitrary",) * n_axes, vmem_limit_bytes=vmem)


def _silu(v):
    return v * jax.nn.sigmoid(v)


def _pack_bf16_pairs(v):
    w = v.shape[1] // 2
    bits = lax.bitcast_convert_type(v.astype(BF16).astype(F32), jnp.uint32)
    packed = (bits[:, w:] & jnp.uint32(0xFFFF0000)) | (bits[:, :w] >> 16)
    return lax.bitcast_convert_type(packed, jnp.int32)


def _unpack_bf16_pairs(p):
    bits = lax.bitcast_convert_type(p, jnp.uint32)
    lo = lax.bitcast_convert_type(bits << 16, F32)
    hi = lax.bitcast_convert_type(bits & jnp.uint32(0xFFFF0000), F32)
    return jnp.concatenate([lo, hi], axis=1)


SC_CORES = 2
SC_SUBCORES = 16
SC_WORKERS = SC_CORES * SC_SUBCORES
SC_CHUNK = 64


def _sc_worker_id():
    return lax.axis_index("s") * SC_CORES + lax.axis_index("c")


def _sc_mesh():
    return plsc.VectorSubcoreMesh(core_axis_name="c", subcore_axis_name="s")


def _sc_gather_rows(table, idx):
    _, w = table.shape
    b = idx.shape[0]
    per_w = b // SC_WORKERS
    nch = per_w // SC_CHUNK

    def body(table_hbm, idx_hbm, out_hbm, idx_v, rows_v, gsem, wsem):
        wid = _sc_worker_id()
        base = wid * per_w
        pltpu.sync_copy(idx_hbm.at[wid], idx_v)
        gathers = [None] * nch
        writes = [None] * nch
        gathers[0] = pltpu.async_copy(table_hbm.at[idx_v.at[0]], rows_v.at[0], gsem.at[0])
        for c in range(nch):
            slot = c % 2
            gathers[c].wait()
            if c + 1 < nch:
                if c >= 1:
                    writes[c - 1].wait()
                gathers[c + 1] = pltpu.async_copy(
                    table_hbm.at[idx_v.at[c + 1]], rows_v.at[1 - slot], gsem.at[1 - slot])
            writes[c] = pltpu.async_copy(
                rows_v.at[slot], out_hbm.at[pl.ds(base + c * SC_CHUNK, SC_CHUNK)], wsem.at[slot])
        if nch >= 2:
            writes[nch - 2].wait()
        writes[nch - 1].wait()

    call = pl.kernel(
        body, mesh=_sc_mesh(),
        out_type=jax.ShapeDtypeStruct((b, w), jnp.int32),
        scratch_types=[pltpu.VMEM((nch, SC_CHUNK), jnp.int32),
                       pltpu.VMEM((2, SC_CHUNK, w), jnp.int32),
                       pltpu.SemaphoreType.DMA((2,)), pltpu.SemaphoreType.DMA((2,))],
        name="sc_gather_rows")
    return call(table, idx.reshape(SC_WORKERS, nch, SC_CHUNK))


def _sc_scatter_rows2(src, idx0, idx1, rows_out):
    n, w = src.shape
    per_w = n // SC_WORKERS
    nch = per_w // SC_CHUNK

    def body(src_hbm, i0_hbm, i1_hbm, out_hbm, i0_v, i1_v, rows_v, rsem, wsem):
        wid = _sc_worker_id()
        base = wid * per_w
        pltpu.sync_copy(i0_hbm.at[wid], i0_v)
        pltpu.sync_copy(i1_hbm.at[wid], i1_v)
        reads = [None] * nch
        writes = [None] * nch
        reads[0] = pltpu.async_copy(src_hbm.at[pl.ds(base, SC_CHUNK)], rows_v.at[0], rsem.at[0])
        for c in range(nch):
            slot = c % 2
            reads[c].wait()
            if c + 1 < nch:
                if c >= 1:
                    for wr in writes[c - 1]:
                        wr.wait()
                reads[c + 1] = pltpu.async_copy(
                    src_hbm.at[pl.ds(base + (c + 1) * SC_CHUNK, SC_CHUNK)],
                    rows_v.at[1 - slot], rsem.at[1 - slot])
            writes[c] = (
                pltpu.async_copy(rows_v.at[slot], out_hbm.at[i0_v.at[c]], wsem.at[slot, 0]),
                pltpu.async_copy(rows_v.at[slot], out_hbm.at[i1_v.at[c]], wsem.at[slot, 1]),
            )
        for c in range(max(nch - 2, 0), nch):
            for wr in writes[c]:
                wr.wait()

    call = pl.kernel(
        body, mesh=_sc_mesh(),
        out_type=jax.ShapeDtypeStruct((rows_out, w), jnp.int32),
        scratch_types=[pltpu.VMEM((nch, SC_CHUNK), jnp.int32),
                       pltpu.VMEM((nch, SC_CHUNK), jnp.int32),
                       pltpu.VMEM((2, SC_CHUNK, w), jnp.int32),
                       pltpu.SemaphoreType.DMA((2,)), pltpu.SemaphoreType.DMA((2, 2))],
        name="sc_scatter_rows")
    shape3 = (SC_WORKERS, nch, SC_CHUNK)
    return call(src, idx0.reshape(shape3), idx1.reshape(shape3))


def _ada_kernel(c_ref, w_ref, b_ref, o_ref):
    ca = _silu(c_ref[...]).astype(BF16)
    w = w_ref[0].astype(BF16)
    o_ref[0] = jnp.dot(ca, w, preferred_element_type=F32) + b_ref[0]


def _ada_mod(c, w_ada, b_ada):
    depth, d, n6 = w_ada.shape
    b = c.shape[0]
    rows = 16
    c_pad = jnp.zeros((rows, d), F32).at[:b].set(c)
    tn = 1536
    out = pl.pallas_call(
        _ada_kernel,
        out_shape=jax.ShapeDtypeStruct((depth, rows, n6), F32),
        grid=(depth, n6 // tn),
        in_specs=[
            pl.BlockSpec((rows, d), lambda l, j: (0, 0)),
            pl.BlockSpec((1, d, tn), lambda l, j: (l, 0, j)),
            pl.BlockSpec((1, 1, tn), lambda l, j: (l, 0, j)),
        ],
        out_specs=pl.BlockSpec((1, rows, tn), lambda l, j: (l, 0, j)),
        compiler_params=_cparams(2),
        name="ada_mod",
    )(c_pad, w_ada, b_ada.reshape(depth, 1, n6))
    return out[:, :b]


def _rms_mod(xf, g, sc, sh):
    ms = jnp.mean(xf * xf, axis=-1, keepdims=True)
    return xf * lax.rsqrt(ms + EPS) * g * (1.0 + sc) + sh


def _mix_in_kernel(x_ref, sc_ref, sh_ref, g_ref, w_ref, gq_ref, gk_ref, ones_ref,
                   z_ref, q_ref, k_ref, vt_ref, wbf_ref):
    first = jnp.logical_and(pl.program_id(0) == 0, pl.program_id(1) == 0)

    @pl.when(first)
    def _():
        wbf_ref[...] = w_ref[...].astype(BF16)

    h = _rms_mod(x_ref[0], g_ref[...], sc_ref[0], sh_ref[0]).astype(BF16)
    proj = jnp.dot(h, wbf_ref[...], preferred_element_type=F32)

    a = proj[:, :D_CONV]
    gate = proj[:, D_CONV:2 * D_CONV]
    z_ref[0] = (a * jax.nn.sigmoid(gate)).astype(BF16)

    def head_norm(t, g):
        sq = (t * t).astype(BF16)
        ss = jnp.concatenate(
            [jnp.dot(sq[:, c:c + ATT_W], ones_ref[...], preferred_element_type=F32)
             for c in range(0, D_ATT, ATT_W)], axis=1)
        return (t * lax.rsqrt(ss * (1.0 / HEAD_DIM) + EPS) * g).astype(BF16)

    o = 2 * D_CONV
    q_ref[0] = head_norm(proj[:, o:o + D_ATT], gq_ref[...])
    k_ref[0] = head_norm(proj[:, o + D_ATT:o + 2 * D_ATT], gk_ref[...])
    v = proj[:, o + 2 * D_ATT:]
    tm = v.shape[0]
    for cidx in range(tm // ATT_Q):
        vt_ref[0, cidx] = v[cidx * ATT_Q:(cidx + 1) * ATT_Q, :].T.astype(BF16)


def _mix_in(x, sc, sh, g, w_in, layer, gq, gk, tm):
    b, s, d = x.shape
    ones_bd = (jnp.arange(ATT_W)[:, None] // HEAD_DIM
               == jnp.arange(ATT_W)[None, :] // HEAD_DIM).astype(BF16)
    gq_t = (jnp.tile(gq, N_HEADS) * (HEAD_DIM ** -0.5)).reshape(1, D_ATT)
    gk_t = jnp.tile(gk, N_HEADS).reshape(1, D_ATT)
    row = lambda bi, ti: (bi, ti, 0)
    per_b = lambda bi, ti: (bi, 0, 0)
    const2 = lambda bi, ti: (0, 0)
    return pl.pallas_call(
        _mix_in_kernel,
        out_shape=(
            jax.ShapeDtypeStruct((b, s, D_CONV), BF16),
            jax.ShapeDtypeStruct((b, s, D_ATT), BF16),
            jax.ShapeDtypeStruct((b, s, D_ATT), BF16),
            jax.ShapeDtypeStruct((b, s // ATT_Q, D_ATT, ATT_Q), BF16),
        ),
        grid=(b, s // tm),
        in_specs=[
            pl.BlockSpec((1, tm, d), row),
            pl.BlockSpec((1, 1, d), per_b),
            pl.BlockSpec((1, 1, d), per_b),
            pl.BlockSpec((1, d), const2),
            pl.BlockSpec((None, d, D_IN_COLS), lambda bi, ti: (layer, 0, 0)),
            pl.BlockSpec((1, D_ATT), const2),
            pl.BlockSpec((1, D_ATT), const2),
            pl.BlockSpec((ATT_W, ATT_W), const2),
        ],
        out_specs=(
            pl.BlockSpec((1, tm, D_CONV), row),
            pl.BlockSpec((1, tm, D_ATT), row),
            pl.BlockSpec((1, tm, D_ATT), row),
            pl.BlockSpec((1, tm // ATT_Q, D_ATT, ATT_Q), lambda bi, ti: (bi, ti, 0, 0)),
        ),
        scratch_shapes=[pltpu.VMEM((d, D_IN_COLS), BF16)],
        compiler_params=_cparams(2),
        name="mix_in",
    )(x, sc, sh, g.reshape(1, d), w_in, gq_t, gk_t, ones_bd)


def _conv_kernel(zc_ref, zp_ref, w_ref, cb_ref, lg_ref, lb_ref, o_ref, win_ref, sh_ref,
                 acc_ref):
    tt = zc_ref.shape[1]
    t = pl.program_id(1)
    halo = zp_ref[0].astype(F32)
    win_ref[0:CONV_HALO, :] = jnp.where(t == 0, 0.0, halo)
    win_ref[CONV_HALO:, :] = zc_ref[0].astype(F32)
    span = tt + CONV_HALO - SUBLANES
    for sft in range(1, SUBLANES):
        sh_ref[sft - 1, 0:span, :] = win_ref[sft:sft + span, :]
    base = CONV_HALO - (CONV_WIDTH - 1)
    tiles = CONV_ROWS // SUBLANES

    def group(gidx, carry):
        r = pl.multiple_of(gidx * CONV_ROWS, CONV_ROWS)
        acc = jnp.zeros((tiles, SUBLANES, D_CONV), F32) + cb_ref[...]
        for j in range(CONV_WIDTH):
            whole, sft = divmod(base + j, SUBLANES)
            start = pl.multiple_of(r + whole * SUBLANES, SUBLANES)
            if sft == 0:
                tap = win_ref[pl.ds(start, CONV_ROWS), :]
            else:
                tap = sh_ref[sft - 1, pl.ds(start, CONV_ROWS), :]
            acc = acc + tap.reshape(tiles, SUBLANES, D_CONV) * w_ref[j]
        acc_ref[pl.ds(r, CONV_ROWS), :] = acc.reshape(CONV_ROWS, D_CONV)
        return carry

    lax.fori_loop(0, tt // CONV_ROWS, group, 0)
    acc = acc_ref[...]
    mu = jnp.mean(acc, axis=-1, keepdims=True)
    xc = acc - mu
    var = jnp.mean(xc * xc, axis=-1, keepdims=True)
    y = xc * lax.rsqrt(var + EPS) * lg_ref[...] + lb_ref[...]
    o_ref[0] = _silu(y).astype(BF16)


def _conv_branch(z, conv_w, conv_b, ln_g, ln_b, tt):
    b, s, c = z.shape
    hb = tt // CONV_HALO
    w_tiles = jnp.broadcast_to(conv_w.reshape(CONV_WIDTH, 1, c), (CONV_WIDTH, SUBLANES, c))
    const2 = lambda bi, ti: (0, 0)
    return pl.pallas_call(
        _conv_kernel,
        out_shape=jax.ShapeDtypeStruct((b, s, c), BF16),
        grid=(b, s // tt),
        in_specs=[
            pl.BlockSpec((1, tt, c), lambda bi, ti: (bi, ti, 0)),
            pl.BlockSpec((1, CONV_HALO, c),
                         lambda bi, ti: (bi, jnp.maximum(ti * hb - 1, 0), 0)),
            pl.BlockSpec((CONV_WIDTH, SUBLANES, c), lambda bi, ti: (0, 0, 0)),
            pl.BlockSpec((1, c), const2),
            pl.BlockSpec((1, c), const2),
            pl.BlockSpec((1, c), const2),
        ],
        out_specs=pl.BlockSpec((1, tt, c), lambda bi, ti: (bi, ti, 0)),
        scratch_shapes=[pltpu.VMEM((tt + CONV_HALO, c), F32),
                        pltpu.VMEM((SUBLANES - 1, tt + CONV_HALO, c), F32),
                        pltpu.VMEM((tt, c), F32)],
        compiler_params=_cparams(2),
        name="conv_branch",
    )(z, z, w_tiles, conv_b.reshape(1, c),
      ln_g.reshape(1, c), ln_b.reshape(1, c))


def _attn_kernel(q_ref, k_ref, vt_ref, bias_ref, o_ref, kpad_ref, vtpad_ref,
                 st0_ref, st1_ref, pb0_ref, pb1_ref, den0_ref, den1_ref):
    st_refs = (st0_ref, st1_ref)
    pb_refs = (pb0_ref, pb1_ref)
    den_refs = (den0_ref, den1_ref)
    s = q_ref.shape[1]
    npad = BAND_PAD // ATT_Q
    kpad_ref[0:BAND_PAD, :] = jnp.zeros((BAND_PAD, ATT_W), BF16)
    kpad_ref[BAND_PAD:, :] = k_ref[0]
    vtpad_ref[0:npad] = jnp.zeros((npad, ATT_W, ATT_Q), BF16)
    vtpad_ref[npad:] = vt_ref[0]

    iota = lambda shape, dim: lax.broadcasted_iota(jnp.int32, shape, dim)
    q_shift = ATT_Q.bit_length() - 1
    d_shift = HEAD_DIM.bit_length() - 1
    qb_mask = (iota((ATT_L, ATT_W), 0) >> q_shift) == (iota((ATT_L, ATT_W), 1) >> d_shift)
    ot_mask = (iota((ATT_W, ATT_L), 0) >> d_shift) == (iota((ATT_W, ATT_L), 1) >> q_shift)
    sel = jnp.where((iota((ATT_Q, ATT_L), 1) & (ATT_Q - 1)) == iota((ATT_Q, ATT_L), 0),
                    1.0, 0.0).astype(BF16)
    key_row = lax.broadcasted_iota(jnp.int32, (ATT_BAND, ATT_L), 0)
    contract_last = (((1,), (1,)), ((), ()))

    def scores(m, p):
        r0 = pl.multiple_of(m * ATT_Q, ATT_Q)
        qt = q_ref[0, pl.ds(r0, ATT_Q), :]
        qb = jnp.where(qb_mask, jnp.concatenate([qt] * ATT_HEADS, axis=0), 0)
        kb = kpad_ref[pl.ds(r0, ATT_BAND), :]
        st_refs[p][...] = lax.dot_general(kb, qb.astype(BF16), contract_last,
                                          preferred_element_type=F32)

    def softmax(m, p, masked):
        st = st_refs[p][...] + bias_ref[0]
        if masked:
            st = jnp.where(key_row >= BAND_PAD - m * ATT_Q, st, NEG_INF)
        mx = jnp.max(st, axis=0, keepdims=True)
        e = jnp.exp(st - mx)
        den_refs[p][...] = jnp.sum(e, axis=0, keepdims=True)
        pb_refs[p][...] = e.astype(BF16)

    def values(m, p):
        r0 = pl.multiple_of(m * ATT_Q, ATT_Q)
        vb = jnp.concatenate([vtpad_ref[m + c] for c in range(ATT_BAND // ATT_Q)],
                             axis=1)
        ot = jnp.dot(vb, pb_refs[p][...], preferred_element_type=F32)
        ot = jnp.where(ot_mask, ot / den_refs[p][...], 0.0).astype(BF16)
        y = lax.dot_general(sel, ot, contract_last, preferred_element_type=F32)
        o_ref[0, pl.ds(r0, ATT_Q), :] = y.astype(BF16)

    n = s // ATT_Q
    n_masked = BAND_PAD // ATT_Q
    scores(0, 0)
    scores(1, 1)
    softmax(0, 0, True)

    def pair(masked, i, carry):
        m = 2 * i
        scores(m, 0)
        softmax(m - 1, 1, masked)
        values(m - 2, 0)
        scores(m + 1, 1)
        softmax(m, 0, masked)
        values(m - 1, 1)
        return carry

    split = min(n // 2, n_masked // 2 + 1)
    lax.fori_loop(1, split, functools.partial(pair, True), 0)
    lax.fori_loop(split, n // 2, functools.partial(pair, False), 0)
    softmax(n - 1, 1, n - 1 < n_masked)
    values(n - 2, 0)
    values(n - 1, 1)


def _attn_bias_t(rel_bias):
    rb = rel_bias.astype(F32)
    nu = ATT_BAND + ATT_Q - 1
    n_low = BAND_PAD - MAX_REL + ATT_Q
    t = jnp.concatenate([jnp.repeat(rb[:, :1], n_low, axis=1),
                         rb[:, 1:1 + nu - n_low]], axis=1)
    tp = jnp.pad(t, ((0, 0), (0, 1)))
    skew = jnp.tile(tp, (1, ATT_Q))[:, :ATT_Q * nu].reshape(N_HEADS, ATT_Q, nu)
    bias = skew[:, :, ATT_Q - 1:]
    bias = bias.reshape(ATT_GROUPS, ATT_HEADS, ATT_Q, ATT_BAND)
    bias = bias.transpose(0, 3, 1, 2).reshape(ATT_GROUPS, ATT_BAND, ATT_L)
    r = jnp.arange(ATT_BAND)[:, None]
    qq = jnp.arange(ATT_L)[None, :] % ATT_Q
    first = (qq // CHUNK) * CHUNK
    valid = (r >= first) & (r < first + BAND_PAD + CHUNK)
    return jnp.where(valid[None], bias, NEG_INF)


def _attention(qn, kn, vt, rel_bias):
    b, s, _ = qn.shape
    bias_t = _attn_bias_t(rel_bias)
    nck = s // ATT_Q
    assert nck >= 4 and nck % 2 == 0, "the attention pipeline runs query steps in pairs"
    return pl.pallas_call(
        _attn_kernel,
        out_shape=jax.ShapeDtypeStruct((b, s, D_ATT), BF16),
        grid=(b, ATT_GROUPS),
        in_specs=[
            pl.BlockSpec((1, s, ATT_W), lambda bi, gi: (bi, 0, gi)),
            pl.BlockSpec((1, s, ATT_W), lambda bi, gi: (bi, 0, gi)),
            pl.BlockSpec((1, nck, ATT_W, ATT_Q), lambda bi, gi: (bi, 0, gi, 0)),
            pl.BlockSpec((1, ATT_BAND, ATT_L), lambda bi, gi: (gi, 0, 0)),
        ],
        out_specs=pl.BlockSpec((1, s, ATT_W), lambda bi, gi: (bi, 0, gi)),
        scratch_shapes=[
            pltpu.VMEM((s + BAND_PAD, ATT_W), BF16),
            pltpu.VMEM((nck + BAND_PAD // ATT_Q, ATT_W, ATT_Q), BF16),
            pltpu.VMEM((ATT_BAND, ATT_L), F32), pltpu.VMEM((ATT_BAND, ATT_L), F32),
            pltpu.VMEM((ATT_BAND, ATT_L), BF16), pltpu.VMEM((ATT_BAND, ATT_L), BF16),
            pltpu.VMEM((1, ATT_L), F32), pltpu.VMEM((1, ATT_L), F32),
        ],
        compiler_params=_cparams(2),
        name="band_attention",
    )(qn, kn, vt, bias_t)


def _mix_out_kernel(x_ref, yc_ref, ya_ref, w_ref, g1_ref, gf_ref, sc_ref, sh_ref,
                    wr_ref, br_ref, x1_ref, h_ref, info_ref, info_t_ref, cnt_ref,
                    wbf_ref, carry_ref):
    first = jnp.logical_and(pl.program_id(0) == 0, pl.program_id(1) == 0)

    @pl.when(first)
    def _():
        wbf_ref[...] = w_ref[...].astype(BF16)
        carry_ref[...] = jnp.zeros_like(carry_ref)

    y = jnp.dot(yc_ref[0], wbf_ref[0:D_CONV, :], preferred_element_type=F32)
    y = y + jnp.dot(ya_ref[0], wbf_ref[D_CONV:, :], preferred_element_type=F32)
    x1 = x_ref[0] + g1_ref[0] * y
    x1_ref[0] = x1
    h = _rms_mod(x1, gf_ref[...], sc_ref[0], sh_ref[0])
    h_ref[0] = _pack_bf16_pairs(h)
    logits = jnp.dot(h.astype(BF16), wr_ref[...], preferred_element_type=F32) + br_ref[...]
    _route_tile(logits, info_ref, info_t_ref, cnt_ref, carry_ref)


def _mix_out_routed(x, yc, ya, w_out, layer, g1, gf, sc, sh, w_router, b_router, tm):
    b, s, d = x.shape
    nt = s // tm
    row = lambda bi, ti: (bi, ti, 0)
    per_b = lambda bi, ti: (bi, 0, 0)
    const2 = lambda bi, ti: (0, 0)
    w_pad = jnp.zeros((d, LANE_PAD_E), BF16).at[:, :N_EXPERTS].set(w_router.astype(BF16))
    b_pad = jnp.full((1, LANE_PAD_E), -jnp.inf, F32).at[0, :N_EXPERTS].set(
        b_router.astype(F32))
    return pl.pallas_call(
        _mix_out_kernel,
        out_shape=(jax.ShapeDtypeStruct((b, s, d), F32),
                   jax.ShapeDtypeStruct((b, s, d // 2), jnp.int32),
                   jax.ShapeDtypeStruct((b, s, LANE_PAD_E), F32),
                   jax.ShapeDtypeStruct((SUBLANES, b * s), F32),
                   jax.ShapeDtypeStruct((SUBLANES, LANE_PAD_E), F32)),
        grid=(b, nt),
        in_specs=[
            pl.BlockSpec((1, tm, d), row),
            pl.BlockSpec((1, tm, D_CONV), row),
            pl.BlockSpec((1, tm, D_ATT), row),
            pl.BlockSpec((None, d, d), lambda bi, ti: (layer, 0, 0)),
            pl.BlockSpec((1, 1, d), per_b),
            pl.BlockSpec((1, d), const2),
            pl.BlockSpec((1, 1, d), per_b),
            pl.BlockSpec((1, 1, d), per_b),
            pl.BlockSpec((d, LANE_PAD_E), const2),
            pl.BlockSpec((1, LANE_PAD_E), const2),
        ],
        out_specs=(pl.BlockSpec((1, tm, d), row), pl.BlockSpec((1, tm, d // 2), row),
                   pl.BlockSpec((1, tm, LANE_PAD_E), row),
                   pl.BlockSpec((SUBLANES, tm), lambda bi, ti: (0, bi * nt + ti)),
                   pl.BlockSpec((SUBLANES, LANE_PAD_E), const2)),
        scratch_shapes=[pltpu.VMEM((d, d), BF16), pltpu.VMEM((SUBLANES, LANE_PAD_E), F32)],
        compiler_params=_cparams(2),
        name="mix_out",
    )(x, yc, ya, w_out, g1, gf.reshape(1, d), sc, sh, w_pad, b_pad)


def _cast_ffn_kernel(wg_ref, wu_ref, wd_ref, wo_ref, wg3_ref, wu3_ref, wd3_ref, wob_ref):
    wg3_ref[0] = wg_ref[...].astype(BF16)
    wu3_ref[0] = wu_ref[...].astype(BF16)
    wd3_ref[0] = wd_ref[...].astype(BF16)

    @pl.when(pl.program_id(0) == 0)
    def _():
        wob_ref[...] = wo_ref[...].astype(BF16)


def _cast_ffn_weights(wg, wu, wd, w_out, layer, fc):
    d, ff = wg.shape
    nf = ff // fc
    return pl.pallas_call(
        _cast_ffn_kernel,
        out_shape=(jax.ShapeDtypeStruct((nf, d, fc), BF16),
                   jax.ShapeDtypeStruct((nf, d, fc), BF16),
                   jax.ShapeDtypeStruct((nf, fc, d), BF16),
                   jax.ShapeDtypeStruct((d, d), BF16)),
        grid=(nf,),
        in_specs=[
            pl.BlockSpec((d, fc), lambda f: (0, f)),
            pl.BlockSpec((d, fc), lambda f: (0, f)),
            pl.BlockSpec((fc, d), lambda f: (f, 0)),
            pl.BlockSpec((None, d, d), lambda f: (layer, 0, 0)),
        ],
        out_specs=(pl.BlockSpec((1, d, fc), lambda f: (f, 0, 0)),
                   pl.BlockSpec((1, d, fc), lambda f: (f, 0, 0)),
                   pl.BlockSpec((1, fc, d), lambda f: (f, 0, 0)),
                   pl.BlockSpec((d, d), lambda f: (0, 0))),
        compiler_params=_cparams(1),
        name="cast_ffn_weights",
    )(wg, wu, wd, w_out)


def _mix_ffn_kernel(x_ref, yc_ref, ya_ref, wo_ref, g1_ref, gf_ref, sc_ref, sh_ref, g2_ref,
                    wg_ref, wu_ref, wd_ref, o_ref, acc_ref, x1_ref, h_ref):
    f = pl.program_id(2)

    @pl.when(f == 0)
    def _():
        y = jnp.dot(yc_ref[0], wo_ref[0:D_CONV, :], preferred_element_type=F32)
        y = y + jnp.dot(ya_ref[0], wo_ref[D_CONV:, :], preferred_element_type=F32)
        x1 = x_ref[0] + g1_ref[0] * y
        x1_ref[...] = x1
        h_ref[...] = _rms_mod(x1, gf_ref[...], sc_ref[0], sh_ref[0]).astype(BF16)
        acc_ref[...] = jnp.zeros_like(acc_ref)

    h = h_ref[...]
    a = jnp.dot(h, wg_ref[0], preferred_element_type=F32)
    u = jnp.dot(h, wu_ref[0], preferred_element_type=F32)
    t = (_silu(a) * u).astype(BF16)
    acc_ref[...] += jnp.dot(t, wd_ref[0], preferred_element_type=F32)

    @pl.when(f == pl.num_programs(2) - 1)
    def _():
        o_ref[0] = x1_ref[...] + g2_ref[0] * acc_ref[...]


def _mix_out_dense_ffn(x, yc, ya, w_out, layer, g1, gf, sc, sh, g2, wg, wu, wd, tm, fc):
    b, s, d = x.shape
    wg3, wu3, wd3, wob = _cast_ffn_weights(wg, wu, wd, w_out, layer, fc)
    nf = wg3.shape[0]
    row = lambda bi, ti, fi: (bi, ti, 0)
    per_b = lambda bi, ti, fi: (bi, 0, 0)
    const2 = lambda bi, ti, fi: (0, 0)
    chunk = lambda bi, ti, fi: (fi, 0, 0)
    return pl.pallas_call(
        _mix_ffn_kernel,
        out_shape=jax.ShapeDtypeStruct((b, s, d), F32),
        grid=(b, s // tm, nf),
        in_specs=[
            pl.BlockSpec((1, tm, d), row),
            pl.BlockSpec((1, tm, D_CONV), row),
            pl.BlockSpec((1, tm, D_ATT), row),
            pl.BlockSpec((d, d), const2),
            pl.BlockSpec((1, 1, d), per_b),
            pl.BlockSpec((1, d), const2),
            pl.BlockSpec((1, 1, d), per_b),
            pl.BlockSpec((1, 1, d), per_b),
            pl.BlockSpec((1, 1, d), per_b),
            pl.BlockSpec((1, d, fc), chunk),
            pl.BlockSpec((1, d, fc), chunk),
            pl.BlockSpec((1, fc, d), chunk),
        ],
        out_specs=pl.BlockSpec((1, tm, d), row),
        scratch_shapes=[pltpu.VMEM((tm, d), F32), pltpu.VMEM((tm, d), F32),
                        pltpu.VMEM((tm, d), BF16)],
        compiler_params=_cparams(3),
        name="mix_out_dense_ffn",
    )(x, yc, ya, wob, g1, gf.reshape(1, d), sc, sh, g2, wg3, wu3, wd3)


def _route_tile(logits, info_ref, info_t_ref, cnt_ref, carry_ref):
    tr = logits.shape[0]
    lane = lax.broadcasted_iota(jnp.int32, (tr, LANE_PAD_E), 1).astype(F32)
    no_lane = float(LANE_PAD_E)
    v0 = jnp.max(logits, axis=-1, keepdims=True)
    i0 = jnp.min(jnp.where(logits == v0, lane, no_lane), axis=-1, keepdims=True)
    rest = jnp.where(lane == i0, -jnp.inf, logits)
    v1 = jnp.max(rest, axis=-1, keepdims=True)
    i1 = jnp.min(jnp.where(rest == v1, lane, no_lane), axis=-1, keepdims=True)
    e1 = jnp.exp(v1 - v0)
    w0 = 1.0 / (1.0 + e1)
    w1 = e1 / (1.0 + e1)
    oh0 = lane == i0
    oh1 = lane == i1
    cnt = jnp.where(jnp.logical_or(oh0, oh1), 1.0, 0.0)
    tri = (lax.broadcasted_iota(jnp.int32, (tr, tr), 1)
           < lax.broadcasted_iota(jnp.int32, (tr, tr), 0)).astype(BF16)
    before = jnp.dot(tri, cnt.astype(BF16), preferred_element_type=F32) + carry_ref[0:1, :]
    r0 = jnp.sum(jnp.where(oh0, before, 0.0), axis=-1, keepdims=True)
    r1 = jnp.sum(jnp.where(oh1, before, 0.0), axis=-1, keepdims=True)
    carry_ref[...] = carry_ref[...] + jnp.sum(cnt, axis=0, keepdims=True)
    cnt_ref[...] = carry_ref[...]
    info = jnp.where(lane == 0, i0, 0.0)
    info = jnp.where(lane == 1, i1, info)
    info = jnp.where(lane == 2, w0, info)
    info = jnp.where(lane == 3, w1, info)
    info = jnp.where(lane == 4, r0, info)
    info = jnp.where(lane == 5, r1, info)
    info_ref[0] = info
    for c in range(tr // LANES):
        blk = info[c * LANES:(c + 1) * LANES, :].T
        info_t_ref[:, c * LANES:(c + 1) * LANES] = blk[0:SUBLANES, :]


def _moe_kernel(te_ref, tn_ref, xs_ref, wg_ref, wu_ref, wd_ref, ys_ref,
                acc_ref, xb_ref, wgb_ref, wub_ref, wdb_ref):
    i = pl.program_id(0)
    f = pl.program_id(1)
    nhalf = tn_ref[i]
    nfull = nhalf // (MOE_SUB // MOE_HALF)

    @pl.when(f == 0)
    def _():
        acc_ref[...] = jnp.zeros_like(acc_ref)

        def unpack(sidx, carry):
            r = pl.multiple_of(sidx * MOE_HALF, MOE_HALF)
            xb_ref[pl.ds(r, MOE_HALF), :] = _unpack_bf16_pairs(
                xs_ref[pl.ds(r, MOE_HALF), :]).astype(BF16)
            return carry

        lax.fori_loop(0, nhalf, unpack, 0)

    @pl.when(nhalf > 0)
    def _():
        wgb_ref[...] = wg_ref[0].astype(BF16)
        wub_ref[...] = wu_ref[0].astype(BF16)
        wdb_ref[...] = wd_ref[0].astype(BF16)

    def block(r, rows):
        xb = xb_ref[pl.ds(r, rows), :]
        a = jnp.dot(xb, wgb_ref[...], preferred_element_type=F32)
        u = jnp.dot(xb, wub_ref[...], preferred_element_type=F32)
        t = (_silu(a) * u).astype(BF16)
        acc_ref[pl.ds(r, rows), :] += jnp.dot(t, wdb_ref[...], preferred_element_type=F32)

    def sub(sidx, carry):
        block(pl.multiple_of(sidx * MOE_SUB, MOE_SUB), MOE_SUB)
        return carry

    lax.fori_loop(0, nfull, sub, 0)
    done = nfull * MOE_SUB
    for rows in MOE_TAILS:
        units = rows // MOE_HALF

        @pl.when((nhalf & units) != 0)
        def _(rows=rows, units=units):
            higher = nhalf & (MOE_SUB // MOE_HALF - 1) & ~(2 * units - 1)
            block(pl.multiple_of(done + higher * MOE_HALF, MOE_HALF), rows)

    @pl.when(f == pl.num_programs(1) - 1)
    def _():
        ys_ref[...] = _pack_bf16_pairs(acc_ref[...])


def _moe_ffn(xs, tile_e, tile_nsub, wg, wu, wd):
    rpad, dw = xs.shape
    d = 2 * dw
    ntiles = rpad // MOE_TILE
    ff = wg.shape[2]
    nf = ff // MOE_FC

    def fcol(i, f, tn):
        return jnp.where(tn[i] > 0, f, nf - 1)

    return pl.pallas_call(
        _moe_kernel,
        out_shape=jax.ShapeDtypeStruct((rpad, dw), jnp.int32),
        grid_spec=pltpu.PrefetchScalarGridSpec(
            num_scalar_prefetch=2,
            grid=(ntiles, nf),
            in_specs=[
                pl.BlockSpec((MOE_TILE, dw), lambda i, f, te, tn: (i, 0)),
                pl.BlockSpec((1, d, MOE_FC), lambda i, f, te, tn: (te[i], 0, fcol(i, f, tn))),
                pl.BlockSpec((1, d, MOE_FC), lambda i, f, te, tn: (te[i], 0, fcol(i, f, tn))),
                pl.BlockSpec((1, MOE_FC, d), lambda i, f, te, tn: (te[i], fcol(i, f, tn), 0)),
            ],
            out_specs=pl.BlockSpec((MOE_TILE, dw), lambda i, f, te, tn: (i, 0)),
            scratch_shapes=[
                pltpu.VMEM((MOE_TILE, d), F32),
                pltpu.VMEM((MOE_TILE, d), BF16),
                pltpu.VMEM((d, MOE_FC), BF16),
                pltpu.VMEM((d, MOE_FC), BF16),
                pltpu.VMEM((MOE_FC, d), BF16),
            ],
        ),
        compiler_params=_cparams(2),
        name="moe_ffn",
    )(tile_e, tile_nsub, xs, wg, wu, wd)


def _combine_kernel(x_ref, y0_ref, y1_ref, info_ref, g2_ref, o_ref):
    info = info_ref[0]
    w0 = info[:, 2:3]
    w1 = info[:, 3:4]
    f = w0 * _unpack_bf16_pairs(y0_ref[0, 0]) + w1 * _unpack_bf16_pairs(y1_ref[0, 0])
    o_ref[0] = x_ref[0] + g2_ref[0] * f


def _combine(x1, y01, info, g2, tm):
    b, s, d = x1.shape
    row = lambda bi, ti: (bi, ti, 0)
    return pl.pallas_call(
        _combine_kernel,
        out_shape=jax.ShapeDtypeStruct((b, s, d), F32),
        grid=(b, s // tm),
        in_specs=[
            pl.BlockSpec((1, tm, d), row),
            pl.BlockSpec((1, 1, tm, d // 2), lambda bi, ti: (0, bi, ti, 0)),
            pl.BlockSpec((1, 1, tm, d // 2), lambda bi, ti: (1, bi, ti, 0)),
            pl.BlockSpec((1, tm, LANE_PAD_E), row),
            pl.BlockSpec((1, 1, d), lambda bi, ti: (bi, 0, 0)),
        ],
        out_specs=pl.BlockSpec((1, tm, d), row),
        compiler_params=_cparams(2),
        name="moe_combine",
    )(x1, y01, y01, info, g2)


def _moe_layer(hp, info, info_t, cnt, x1, g2, wg, wu, wd, tm):
    b, s, d = x1.shape
    n = b * s
    e0 = info_t[0].astype(jnp.int32)
    e1 = info_t[1].astype(jnp.int32)
    r0 = info_t[4].astype(jnp.int32)
    r1 = info_t[5].astype(jnp.int32)
    counts = cnt[0, :N_EXPERTS].astype(jnp.int32)

    ntiles = (2 * n) // MOE_TILE + N_EXPERTS
    tiles_per_e = (counts + MOE_TILE - 1) // MOE_TILE
    tile_end = jnp.cumsum(tiles_per_e)
    tile_start = tile_end - tiles_per_e
    total = tile_end[-1]
    tidx = jnp.arange(ntiles, dtype=jnp.int32)
    live = tidx < total
    tclip = jnp.minimum(tidx, total - 1)
    tile_e = jnp.minimum(jnp.sum(tclip[:, None] >= tile_end[None, :], axis=1),
                         N_EXPERTS - 1).astype(jnp.int32)
    rows_left = counts[tile_e] - (tclip - tile_start[tile_e]) * MOE_TILE
    rows_here = jnp.clip(rows_left, 0, MOE_TILE)
    tile_nsub = jnp.where(live, (rows_here + MOE_HALF - 1) // MOE_HALF, 0).astype(jnp.int32)

    row_start = tile_start * MOE_TILE
    eid = jnp.arange(N_EXPERTS, dtype=jnp.int32)[None, :]
    pos0 = jnp.sum(jnp.where(e0[:, None] == eid, row_start[None, :], 0), axis=1) + r0
    pos1 = jnp.sum(jnp.where(e1[:, None] == eid, row_start[None, :], 0), axis=1) + r1

    xs = _sc_scatter_rows2(hp.reshape(n, d // 2), pos0, pos1, ntiles * MOE_TILE)
    ys = _moe_ffn(xs, tile_e, tile_nsub, wg, wu, wd)
    y01 = _sc_gather_rows(ys, jnp.concatenate([pos0, pos1]))
    return _combine(x1, y01.reshape(2, b, s, d // 2), info.reshape(b, s, LANE_PAD_E), g2, tm)


def kernel(x, c, w_ada, b_ada, norm_mix_g, norm_ffn_g, w_in, w_out, conv_w, conv_b,
           conv_ln_g, conv_ln_b, q_norm_g, k_norm_g, rel_bias, ffn_w_gate, ffn_w_up,
           ffn_w_down, moe_w_router, moe_b_router, moe_w_gate, moe_w_up, moe_w_down):
    b, s, d = x.shape
    depth = w_ada.shape[0]
    tm = min(512, s)
    mod = _ada_mod(c, w_ada, b_ada)
    for l in range(depth):
        sh1, sc1, g1, sh2, sc2, g2 = [
            mod[l, :, j * d:(j + 1) * d].reshape(b, 1, d) for j in range(6)]
        z, qn, kn, vt = _mix_in(x, sc1, sh1, norm_mix_g[l], w_in, l,
                                q_norm_g[l], k_norm_g[l], tm)
        yc = _conv_branch(z, conv_w[l], conv_b[l], conv_ln_g[l], conv_ln_b[l], min(512, s))
        ya = _attention(qn, kn, vt, rel_bias[l])
        i = l // 2
        if l % 2 == 0:
            x = _mix_out_dense_ffn(x, yc, ya, w_out, l, g1, norm_ffn_g[l], sc2, sh2, g2,
                                   ffn_w_gate[i], ffn_w_up[i], ffn_w_down[i],
                                   min(1024, s), 256)
        else:
            x1, hp, info, info_t, cnt = _mix_out_routed(
                x, yc, ya, w_out, l, g1, norm_ffn_g[l], sc2, sh2,
                moe_w_router[i], moe_b_router[i], tm)
            x = _moe_layer(hp, info, info_t, cnt, x1, g2,
                           moe_w_gate[i], moe_w_up[i], moe_w_down[i], tm)
    return x
```

```python
import functools

import jax
import jax.numpy as jnp
from jax import lax
from jax.experimental import pallas as pl
from jax.experimental.pallas import tpu as pltpu
from jax.experimental.pallas import tpu_sc as plsc

F32 = jnp.float32
BF16 = jnp.bfloat16

D_MODEL = 1024
CHUNK = 64
N_PREV_CHUNKS = 8
BAND_PAD = N_PREV_CHUNKS * CHUNK
D_CONV = 512
D_ATT = 512
HEAD_DIM = 64
N_HEADS = 8
CONV_WIDTH = 31
MAX_REL = 128
D_IN_COLS = 2 * D_CONV + 3 * D_ATT
N_EXPERTS = 8
EPS = 1e-6
NEG_INF = -1e30

LANES = 128
SUBLANES = 8
VMEM_LIMIT_BYTES = 56 * 1024 * 1024

ATT_HEADS = 4
ATT_GROUPS = N_HEADS // ATT_HEADS
ATT_W = ATT_HEADS * HEAD_DIM
ATT_Q = 2 * CHUNK
ATT_BAND = BAND_PAD + ATT_Q
ATT_L = ATT_HEADS * ATT_Q

CONV_HALO = 32
CONV_ROWS = 32
LANE_PAD_E = LANES

MOE_HALF = 256
MOE_SUB = 4 * MOE_HALF
MOE_TAILS = (2 * MOE_HALF, MOE_HALF)
MOE_TILE = 9 * MOE_HALF
MOE_FC = 512


def _cparams(n_axes, vmem=VMEM_LIMIT_BYTES):
    return pltpu.CompilerParams(
        dimension_semantics=("arbitrary",) * n_axes, vmem_limit_bytes=vmem)


def _silu(v):
    return v * jax.nn.sigmoid(v)


def _pack_bf16_pairs(v):
    w = v.shape[1] // 2
    bits = lax.bitcast_convert_type(v.astype(BF16).astype(F32), jnp.uint32)
    packed = (bits[:, w:] & jnp.uint32(0xFFFF0000)) | (bits[:, :w] >> 16)
    return lax.bitcast_convert_type(packed, jnp.int32)


def _unpack_bf16_pairs(p):
    bits = lax.bitcast_convert_type(p, jnp.uint32)
    lo = lax.bitcast_convert_type(bits << 16, F32)
    hi = lax.bitcast_convert_type(bits & jnp.uint32(0xFFFF0000), F32)
    return jnp.concatenate([lo, hi], axis=1)


SC_CORES = 2
SC_SUBCORES = 16
SC_WORKERS = SC_CORES * SC_SUBCORES
SC_CHUNK = 64


def _sc_worker_id():
    return lax.axis_index("s") * SC_CORES + lax.axis_index("c")


def _sc_mesh():
    return plsc.VectorSubcoreMesh(core_axis_name="c", subcore_axis_name="s")


def _sc_gather_rows(table, idx):
    _, w = table.shape
    b = idx.shape[0]
    per_w = b // SC_WORKERS
    nch = per_w // SC_CHUNK

    def body(table_hbm, idx_hbm, out_hbm, idx_v, rows_v, gsem, wsem):
        wid = _sc_worker_id()
        base = wid * per_w
        pltpu.sync_copy(idx_hbm.at[wid], idx_v)
        gathers = [None] * nch
        writes = [None] * nch
        gathers[0] = pltpu.async_copy(table_hbm.at[idx_v.at[0]], rows_v.at[0], gsem.at[0])
        for c in range(nch):
            slot = c % 2
            gathers[c].wait()
            if c + 1 < nch:
                if c >= 1:
                    writes[c - 1].wait()
                gathers[c + 1] = pltpu.async_copy(
                    table_hbm.at[idx_v.at[c + 1]], rows_v.at[1 - slot], gsem.at[1 - slot])
            writes[c] = pltpu.async_copy(
                rows_v.at[slot], out_hbm.at[pl.ds(base + c * SC_CHUNK, SC_CHUNK)], wsem.at[slot])
        if nch >= 2:
            writes[nch - 2].wait()
        writes[nch - 1].wait()

    call = pl.kernel(
        body, mesh=_sc_mesh(),
        out_type=jax.ShapeDtypeStruct((b, w), jnp.int32),
        scratch_types=[pltpu.VMEM((nch, SC_CHUNK), jnp.int32),
                       pltpu.VMEM((2, SC_CHUNK, w), jnp.int32),
                       pltpu.SemaphoreType.DMA((2,)), pltpu.SemaphoreType.DMA((2,))],
        name="sc_gather_rows")
    return call(table, idx.reshape(SC_WORKERS, nch, SC_CHUNK))


def _sc_scatter_rows2(src, idx0, idx1, rows_out):
    n, w = src.shape
    per_w = n // SC_WORKERS
    nch = per_w // SC_CHUNK

    def body(src_hbm, i0_hbm, i1_hbm, out_hbm, i0_v, i1_v, rows_v, rsem, wsem):
        wid = _sc_worker_id()
        base = wid * per_w
        pltpu.sync_copy(i0_hbm.at[wid], i0_v)
        pltpu.sync_copy(i1_hbm.at[wid], i1_v)
        reads = [None] * nch
        writes = [None] * nch
        reads[0] = pltpu.async_copy(src_hbm.at[pl.ds(base, SC_CHUNK)], rows_v.at[0], rsem.at[0])
        for c in range(nch):
            slot = c % 2
            reads[c].wait()
            if c + 1 < nch:
                if c >= 1:
                    for wr in writes[c - 1]:
                        wr.wait()
                reads[c + 1] = pltpu.async_copy(
                    src_hbm.at[pl.ds(base + (c + 1) * SC_CHUNK, SC_CHUNK)],
                    rows_v.at[1 - slot], rsem.at[1 - slot])
            writes[c] = (
                pltpu.async_copy(rows_v.at[slot], out_hbm.at[i0_v.at[c]], wsem.at[slot, 0]),
                pltpu.async_copy(rows_v.at[slot], out_hbm.at[i1_v.at[c]], wsem.at[slot, 1]),
            )
        for c in range(max(nch - 2, 0), nch):
            for wr in writes[c]:
                wr.wait()

    call = pl.kernel(
        body, mesh=_sc_mesh(),
        out_type=jax.ShapeDtypeStruct((rows_out, w), jnp.int32),
        scratch_types=[pltpu.VMEM((nch, SC_CHUNK), jnp.int32),
                       pltpu.VMEM((nch, SC_CHUNK), jnp.int32),
                       pltpu.VMEM((2, SC_CHUNK, w), jnp.int32),
                       pltpu.SemaphoreType.DMA((2,)), pltpu.SemaphoreType.DMA((2, 2))],
        name="sc_scatter_rows")
    shape3 = (SC_WORKERS, nch, SC_CHUNK)
    return call(src, idx0.reshape(shape3), idx1.reshape(shape3))


def _ada_kernel(c_ref, w_ref, b_ref, o_ref):
    ca = _silu(c_ref[...]).astype(BF16)
    w = w_ref[0].astype(BF16)
    o_ref[0] = jnp.dot(ca, w, preferred_element_type=F32) + b_ref[0]


def _ada_mod(c, w_ada, b_ada):
    depth, d, n6 = w_ada.shape
    b = c.shape[0]
    rows = 16
    c_pad = jnp.zeros((rows, d), F32).at[:b].set(c)
    tn = 1536
    out = pl.pallas_call(
        _ada_kernel,
        out_shape=jax.ShapeDtypeStruct((depth, rows, n6), F32),
        grid=(depth, n6 // tn),
        in_specs=[
            pl.BlockSpec((rows, d), lambda l, j: (0, 0)),
            pl.BlockSpec((1, d, tn), lambda l, j: (l, 0, j)),
            pl.BlockSpec((1, 1, tn), lambda l, j: (l, 0, j)),
        ],
        out_specs=pl.BlockSpec((1, rows, tn), lambda l, j: (l, 0, j)),
        compiler_params=_cparams(2),
        name="ada_mod",
    )(c_pad, w_ada, b_ada.reshape(depth, 1, n6))
    return out[:, :b]


def _rms_mod(xf, g, sc, sh):
    ms = jnp.mean(xf * xf, axis=-1, keepdims=True)
    return xf * lax.rsqrt(ms + EPS) * g * (1.0 + sc) + sh


def _mix_in_kernel(x_ref, sc_ref, sh_ref, g_ref, w_ref, gq_ref, gk_ref, ones_ref,
                   z_ref, q_ref, k_ref, vt_ref, wbf_ref):
    first = jnp.logical_and(pl.program_id(0) == 0, pl.program_id(1) == 0)

    @pl.when(first)
    def _():
        wbf_ref[...] = w_ref[...].astype(BF16)

    h = _rms_mod(x_ref[0], g_ref[...], sc_ref[0], sh_ref[0]).astype(BF16)
    proj = jnp.dot(h, wbf_ref[...], preferred_element_type=F32)

    a = proj[:, :D_CONV]
    gate = proj[:, D_CONV:2 * D_CONV]
    z_ref[0] = (a * jax.nn.sigmoid(gate)).astype(BF16)

    def head_norm(t, g):
        sq = (t * t).astype(BF16)
        ss = jnp.concatenate(
            [jnp.dot(sq[:, c:c + ATT_W], ones_ref[...], preferred_element_type=F32)
             for c in range(0, D_ATT, ATT_W)], axis=1)
        return (t * lax.rsqrt(ss * (1.0 / HEAD_DIM) + EPS) * g).astype(BF16)

    o = 2 * D_CONV
    q_ref[0] = head_norm(proj[:, o:o + D_ATT], gq_ref[...])
    k_ref[0] = head_norm(proj[:, o + D_ATT:o + 2 * D_ATT], gk_ref[...])
    v = proj[:, o + 2 * D_ATT:]
    tm = v.shape[0]
    for cidx in range(tm // ATT_Q):
        vt_ref[0, cidx] = v[cidx * ATT_Q:(cidx + 1) * ATT_Q, :].T.astype(BF16)


def _mix_in(x, sc, sh, g, w_in, layer, gq, gk, tm):
    b, s, d = x.shape
    ones_bd = (jnp.arange(ATT_W)[:, None] // HEAD_DIM
               == jnp.arange(ATT_W)[None, :] // HEAD_DIM).astype(BF16)
    gq_t = (jnp.tile(gq, N_HEADS) * (HEAD_DIM ** -0.5)).reshape(1, D_ATT)
    gk_t = jnp.tile(gk, N_HEADS).reshape(1, D_ATT)
    row = lambda bi, ti: (bi, ti, 0)
    per_b = lambda bi, ti: (bi, 0, 0)
    const2 = lambda bi, ti: (0, 0)
    return pl.pallas_call(
        _mix_in_kernel,
        out_shape=(
            jax.ShapeDtypeStruct((b, s, D_CONV), BF16),
            jax.ShapeDtypeStruct((b, s, D_ATT), BF16),
            jax.ShapeDtypeStruct((b, s, D_ATT), BF16),
            jax.ShapeDtypeStruct((b, s // ATT_Q, D_ATT, ATT_Q), BF16),
        ),
        grid=(b, s // tm),
        in_specs=[
            pl.BlockSpec((1, tm, d), row),
            pl.BlockSpec((1, 1, d), per_b),
            pl.BlockSpec((1, 1, d), per_b),
            pl.BlockSpec((1, d), const2),
            pl.BlockSpec((None, d, D_IN_COLS), lambda bi, ti: (layer, 0, 0)),
            pl.BlockSpec((1, D_ATT), const2),
            pl.BlockSpec((1, D_ATT), const2),
            pl.BlockSpec((ATT_W, ATT_W), const2),
        ],
        out_specs=(
            pl.BlockSpec((1, tm, D_CONV), row),
            pl.BlockSpec((1, tm, D_ATT), row),
            pl.BlockSpec((1, tm, D_ATT), row),
            pl.BlockSpec((1, tm // ATT_Q, D_ATT, ATT_Q), lambda bi, ti: (bi, ti, 0, 0)),
        ),
        scratch_shapes=[pltpu.VMEM((d, D_IN_COLS), BF16)],
        compiler_params=_cparams(2),
        name="mix_in",
    )(x, sc, sh, g.reshape(1, d), w_in, gq_t, gk_t, ones_bd)


def _conv_kernel(zc_ref, zp_ref, w_ref, cb_ref, lg_ref, lb_ref, o_ref, win_ref, sh_ref,
                 acc_ref):
    tt = zc_ref.shape[1]
    t = pl.program_id(1)
    halo = zp_ref[0].astype(F32)
    win_ref[0:CONV_HALO, :] = jnp.where(t == 0, 0.0, halo)
    win_ref[CONV_HALO:, :] = zc_ref[0].astype(F32)
    span = tt + CONV_HALO - SUBLANES
    for sft in range(1, SUBLANES):
        sh_ref[sft - 1, 0:span, :] = win_ref[sft:sft + span, :]
    base = CONV_HALO - (CONV_WIDTH - 1)
    tiles = CONV_ROWS // SUBLANES

    def group(gidx, carry):
        r = pl.multiple_of(gidx * CONV_ROWS, CONV_ROWS)
        acc = jnp.zeros((tiles, SUBLANES, D_CONV), F32) + cb_ref[...]
        for j in range(CONV_WIDTH):
            whole, sft = divmod(base + j, SUBLANES)
            start = pl.multiple_of(r + whole * SUBLANES, SUBLANES)
            if sft == 0:
                tap = win_ref[pl.ds(start, CONV_ROWS), :]
            else:
                tap = sh_ref[sft - 1, pl.ds(start, CONV_ROWS), :]
            acc = acc + tap.reshape(tiles, SUBLANES, D_CONV) * w_ref[j]
        acc_ref[pl.ds(r, CONV_ROWS), :] = acc.reshape(CONV_ROWS, D_CONV)
        return carry

    lax.fori_loop(0, tt // CONV_ROWS, group, 0)
    acc = acc_ref[...]
    mu = jnp.mean(acc, axis=-1, keepdims=True)
    xc = acc - mu
    var = jnp.mean(xc * xc, axis=-1, keepdims=True)
    y = xc * lax.rsqrt(var + EPS) * lg_ref[...] + lb_ref[...]
    o_ref[0] = _silu(y).astype(BF16)


def _conv_branch(z, conv_w, conv_b, ln_g, ln_b, tt):
    b, s, c = z.shape
    hb = tt // CONV_HALO
    w_tiles = jnp.broadcast_to(conv_w.reshape(CONV_WIDTH, 1, c), (CONV_WIDTH, SUBLANES, c))
    const2 = lambda bi, ti: (0, 0)
    return pl.pallas_call(
        _conv_kernel,
        out_shape=jax.ShapeDtypeStruct((b, s, c), BF16),
        grid=(b, s // tt),
        in_specs=[
            pl.BlockSpec((1, tt, c), lambda bi, ti: (bi, ti, 0)),
            pl.BlockSpec((1, CONV_HALO, c),
                         lambda bi, ti: (bi, jnp.maximum(ti * hb - 1, 0), 0)),
            pl.BlockSpec((CONV_WIDTH, SUBLANES, c), lambda bi, ti: (0, 0, 0)),
            pl.BlockSpec((1, c), const2),
            pl.BlockSpec((1, c), const2),
            pl.BlockSpec((1, c), const2),
        ],
        out_specs=pl.BlockSpec((1, tt, c), lambda bi, ti: (bi, ti, 0)),
        scratch_shapes=[pltpu.VMEM((tt + CONV_HALO, c), F32),
                        pltpu.VMEM((SUBLANES - 1, tt + CONV_HALO, c), F32),
                        pltpu.VMEM((tt, c), F32)],
        compiler_params=_cparams(2),
        name="conv_branch",
    )(z, z, w_tiles, conv_b.reshape(1, c),
      ln_g.reshape(1, c), ln_b.reshape(1, c))


def _attn_kernel(q_ref, k_ref, vt_ref, bias_ref, o_ref, kpad_ref, vtpad_ref,
                 st0_ref, st1_ref, pb0_ref, pb1_ref, den0_ref, den1_ref):
    st_refs = (st0_ref, st1_ref)
    pb_refs = (pb0_ref, pb1_ref)
    den_refs = (den0_ref, den1_ref)
    s = q_ref.shape[1]
    npad = BAND_PAD // ATT_Q
    kpad_ref[0:BAND_PAD, :] = jnp.zeros((BAND_PAD, ATT_W), BF16)
    kpad_ref[BAND_PAD:, :] = k_ref[0]
    vtpad_ref[0:npad] = jnp.zeros((npad, ATT_W, ATT_Q), BF16)
    vtpad_ref[npad:] = vt_ref[0]

    iota = lambda shape, dim: lax.broadcasted_iota(jnp.int32, shape, dim)
    q_shift = ATT_Q.bit_length() - 1
    d_shift = HEAD_DIM.bit_length() - 1
    qb_mask = (iota((ATT_L, ATT_W), 0) >> q_shift) == (iota((ATT_L, ATT_W), 1) >> d_shift)
    ot_mask = (iota((ATT_W, ATT_L), 0) >> d_shift) == (iota((ATT_W, ATT_L), 1) >> q_shift)
    sel = jnp.where((iota((ATT_Q, ATT_L), 1) & (ATT_Q - 1)) == iota((ATT_Q, ATT_L), 0),
                    1.0, 0.0).astype(BF16)
    key_row = lax.broadcasted_iota(jnp.int32, (ATT_BAND, ATT_L), 0)
    contract_last = (((1,), (1,)), ((), ()))

    def scores(m, p):
        r0 = pl.multiple_of(m * ATT_Q, ATT_Q)
        qt = q_ref[0, pl.ds(r0, ATT_Q), :]
        qb = jnp.where(qb_mask, jnp.concatenate([qt] * ATT_HEADS, axis=0), 0)
        kb = kpad_ref[pl.ds(r0, ATT_BAND), :]
        st_refs[p][...] = lax.dot_general(kb, qb.astype(BF16), contract_last,
                                          preferred_element_type=F32)

    def softmax(m, p, masked):
        st = st_refs[p][...] + bias_ref[0]
        if masked:
            st = jnp.where(key_row >= BAND_PAD - m * ATT_Q, st, NEG_INF)
        mx = jnp.max(st, axis=0, keepdims=True)
        e = jnp.exp(st - mx)
        den_refs[p][...] = jnp.sum(e, axis=0, keepdims=True)
        pb_refs[p][...] = e.astype(BF16)

    def values(m, p):
        r0 = pl.multiple_of(m * ATT_Q, ATT_Q)
        vb = jnp.concatenate([vtpad_ref[m + c] for c in range(ATT_BAND // ATT_Q)],
                             axis=1)
        ot = jnp.dot(vb, pb_refs[p][...], preferred_element_type=F32)
        ot = jnp.where(ot_mask, ot / den_refs[p][...], 0.0).astype(BF16)
        y = lax.dot_general(sel, ot, contract_last, preferred_element_type=F32)
        o_ref[0, pl.ds(r0, ATT_Q), :] = y.astype(BF16)

    n = s // ATT_Q
    n_masked = BAND_PAD // ATT_Q
    scores(0, 0)
    scores(1, 1)
    softmax(0, 0, True)

    def pair(masked, i, carry):
        m = 2 * i
        scores(m, 0)
        softmax(m - 1, 1, masked)
        values(m - 2, 0)
        scores(m + 1, 1)
        softmax(m, 0, masked)
        values(m - 1, 1)
        return carry

    split = min(n // 2, n_masked // 2 + 1)
    lax.fori_loop(1, split, functools.partial(pair, True), 0)
    lax.fori_loop(split, n // 2, functools.partial(pair, False), 0)
    softmax(n - 1, 1, n - 1 < n_masked)
    values(n - 2, 0)
    values(n - 1, 1)


def _attn_bias_t(rel_bias):
    rb = rel_bias.astype(F32)
    nu = ATT_BAND + ATT_Q - 1
    n_low = BAND_PAD - MAX_REL + ATT_Q
    t = jnp.concatenate([jnp.repeat(rb[:, :1], n_low, axis=1),
                         rb[:, 1:1 + nu - n_low]], axis=1)
    tp = jnp.pad(t, ((0, 0), (0, 1)))
    skew = jnp.tile(tp, (1, ATT_Q))[:, :ATT_Q * nu].reshape(N_HEADS, ATT_Q, nu)
    bias = skew[:, :, ATT_Q - 1:]
    bias = bias.reshape(ATT_GROUPS, ATT_HEADS, ATT_Q, ATT_BAND)
    bias = bias.transpose(0, 3, 1, 2).reshape(ATT_GROUPS, ATT_BAND, ATT_L)
    r = jnp.arange(ATT_BAND)[:, None]
    qq = jnp.arange(ATT_L)[None, :] % ATT_Q
    first = (qq // CHUNK) * CHUNK
    valid = (r >= first) & (r < first + BAND_PAD + CHUNK)
    return jnp.where(valid[None], bias, NEG_INF)


def _attention(qn, kn, vt, rel_bias):
    b, s, _ = qn.shape
    bias_t = _attn_bias_t(rel_bias)
    nck = s // ATT_Q
    assert nck >= 4 and nck % 2 == 0, "the attention pipeline runs query steps in pairs"
    return pl.pallas_call(
        _attn_kernel,
        out_shape=jax.ShapeDtypeStruct((b, s, D_ATT), BF16),
        grid=(b, ATT_GROUPS),
        in_specs=[
            pl.BlockSpec((1, s, ATT_W), lambda bi, gi: (bi, 0, gi)),
            pl.BlockSpec((1, s, ATT_W), lambda bi, gi: (bi, 0, gi)),
            pl.BlockSpec((1, nck, ATT_W, ATT_Q), lambda bi, gi: (bi, 0, gi, 0)),
            pl.BlockSpec((1, ATT_BAND, ATT_L), lambda bi, gi: (gi, 0, 0)),
        ],
        out_specs=pl.BlockSpec((1, s, ATT_W), lambda bi, gi: (bi, 0, gi)),
        scratch_shapes=[
            pltpu.VMEM((s + BAND_PAD, ATT_W), BF16),
            pltpu.VMEM((nck + BAND_PAD // ATT_Q, ATT_W, ATT_Q), BF16),
            pltpu.VMEM((ATT_BAND, ATT_L), F32), pltpu.VMEM((ATT_BAND, ATT_L), F32),
            pltpu.VMEM((ATT_BAND, ATT_L), BF16), pltpu.VMEM((ATT_BAND, ATT_L), BF16),
            pltpu.VMEM((1, ATT_L), F32), pltpu.VMEM((1, ATT_L), F32),
        ],
        compiler_params=_cparams(2),
        name="band_attention",
    )(qn, kn, vt, bias_t)


def _mix_out_kernel(x_ref, yc_ref, ya_ref, w_ref, g1_ref, gf_ref, sc_ref, sh_ref,
                    wr_ref, br_ref, x1_ref, h_ref, info_ref, info_t_ref, cnt_ref,
                    wbf_ref, carry_ref):
    first = jnp.logical_and(pl.program_id(0) == 0, pl.program_id(1) == 0)

    @pl.when(first)
    def _():
        wbf_ref[...] = w_ref[...].astype(BF16)
        carry_ref[...] = jnp.zeros_like(carry_ref)

    y = jnp.dot(yc_ref[0], wbf_ref[0:D_CONV, :], preferred_element_type=F32)
    y = y + jnp.dot(ya_ref[0], wbf_ref[D_CONV:, :], preferred_element_type=F32)
    x1 = x_ref[0] + g1_ref[0] * y
    x1_ref[0] = x1
    h = _rms_mod(x1, gf_ref[...], sc_ref[0], sh_ref[0])
    h_ref[0] = _pack_bf16_pairs(h)
    logits = jnp.dot(h.astype(BF16), wr_ref[...], preferred_element_type=F32) + br_ref[...]
    _route_tile(logits, info_ref, info_t_ref, cnt_ref, carry_ref)


def _mix_out_routed(x, yc, ya, w_out, layer, g1, gf, sc, sh, w_router, b_router, tm):
    b, s, d = x.shape
    nt = s // tm
    row = lambda bi, ti: (bi, ti, 0)
    per_b = lambda bi, ti: (bi, 0, 0)
    const2 = lambda bi, ti: (0, 0)
    w_pad = jnp.zeros((d, LANE_PAD_E), BF16).at[:, :N_EXPERTS].set(w_router.astype(BF16))
    b_pad = jnp.full((1, LANE_PAD_E), -jnp.inf, F32).at[0, :N_EXPERTS].set(
        b_router.astype(F32))
    return pl.pallas_call(
        _mix_out_kernel,
        out_shape=(jax.ShapeDtypeStruct((b, s, d), F32),
                   jax.ShapeDtypeStruct((b, s, d // 2), jnp.int32),
                   jax.ShapeDtypeStruct((b, s, LANE_PAD_E), F32),
                   jax.ShapeDtypeStruct((SUBLANES, b * s), F32),
                   jax.ShapeDtypeStruct((SUBLANES, LANE_PAD_E), F32)),
        grid=(b, nt),
        in_specs=[
            pl.BlockSpec((1, tm, d), row),
            pl.BlockSpec((1, tm, D_CONV), row),
            pl.BlockSpec((1, tm, D_ATT), row),
            pl.BlockSpec((None, d, d), lambda bi, ti: (layer, 0, 0)),
            pl.BlockSpec((1, 1, d), per_b),
            pl.BlockSpec((1, d), const2),
            pl.BlockSpec((1, 1, d), per_b),
            pl.BlockSpec((1, 1, d), per_b),
            pl.BlockSpec((d, LANE_PAD_E), const2),
            pl.BlockSpec((1, LANE_PAD_E), const2),
        ],
        out_specs=(pl.BlockSpec((1, tm, d), row), pl.BlockSpec((1, tm, d // 2), row),
                   pl.BlockSpec((1, tm, LANE_PAD_E), row),
                   pl.BlockSpec((SUBLANES, tm), lambda bi, ti: (0, bi * nt + ti)),
                   pl.BlockSpec((SUBLANES, LANE_PAD_E), const2)),
        scratch_shapes=[pltpu.VMEM((d, d), BF16), pltpu.VMEM((SUBLANES, LANE_PAD_E), F32)],
        compiler_params=_cparams(2),
        name="mix_out",
    )(x, yc, ya, w_out, g1, gf.reshape(1, d), sc, sh, w_pad, b_pad)


def _cast_ffn_kernel(wg_ref, wu_ref, wd_ref, wo_ref, wg3_ref, wu3_ref, wd3_ref, wob_ref):
    wg3_ref[0] = wg_ref[...].astype(BF16)
    wu3_ref[0] = wu_ref[...].astype(BF16)
    wd3_ref[0] = wd_ref[...].astype(BF16)

    @pl.when(pl.program_id(0) == 0)
    def _():
        wob_ref[...] = wo_ref[...].astype(BF16)


def _cast_ffn_weights(wg, wu, wd, w_out, layer, fc):
    d, ff = wg.shape
    nf = ff // fc
    return pl.pallas_call(
        _cast_ffn_kernel,
        out_shape=(jax.ShapeDtypeStruct((nf, d, fc), BF16),
                   jax.ShapeDtypeStruct((nf, d, fc), BF16),
                   jax.ShapeDtypeStruct((nf, fc, d), BF16),
                   jax.ShapeDtypeStruct((d, d), BF16)),
        grid=(nf,),
        in_specs=[
            pl.BlockSpec((d, fc), lambda f: (0, f)),
            pl.BlockSpec((d, fc), lambda f: (0, f)),
            pl.BlockSpec((fc, d), lambda f: (f, 0)),
            pl.BlockSpec((None, d, d), lambda f: (layer, 0, 0)),
        ],
        out_specs=(pl.BlockSpec((1, d, fc), lambda f: (f, 0, 0)),
                   pl.BlockSpec((1, d, fc), lambda f: (f, 0, 0)),
                   pl.BlockSpec((1, fc, d), lambda f: (f, 0, 0)),
                   pl.BlockSpec((d, d), lambda f: (0, 0))),
        compiler_params=_cparams(1),
        name="cast_ffn_weights",
    )(wg, wu, wd, w_out)


def _mix_ffn_kernel(x_ref, yc_ref, ya_ref, wo_ref, g1_ref, gf_ref, sc_ref, sh_ref, g2_ref,
                    wg_ref, wu_ref, wd_ref, o_ref, acc_ref):
    y = jnp.dot(yc_ref[0], wo_ref[0:D_CONV, :], preferred_element_type=F32)
    y = y + jnp.dot(ya_ref[0], wo_ref[D_CONV:, :], preferred_element_type=F32)
    x1 = x_ref[0] + g1_ref[0] * y
    h = _rms_mod(x1, gf_ref[...], sc_ref[0], sh_ref[0]).astype(BF16)
    for c in range(wg_ref.shape[0]):
        a = jnp.dot(h, wg_ref[c], preferred_element_type=F32)
        u = jnp.dot(h, wu_ref[c], preferred_element_type=F32)
        t = (_silu(a) * u).astype(BF16)
        dn = jnp.dot(t, wd_ref[c], preferred_element_type=F32)
        if c == 0:
            acc_ref[...] = dn
        else:
            acc_ref[...] += dn
    o_ref[0] = x1 + g2_ref[0] * acc_ref[...]


def _mix_out_dense_ffn(x, yc, ya, w_out, layer, g1, gf, sc, sh, g2, wg, wu, wd, tm, fc):
    b, s, d = x.shape
    wg3, wu3, wd3, wob = _cast_ffn_weights(wg, wu, wd, w_out, layer, fc)
    nf = wg3.shape[0]
    row = lambda bi, ti: (bi, ti, 0)
    per_b = lambda bi, ti: (bi, 0, 0)
    const2 = lambda bi, ti: (0, 0)
    const3 = lambda bi, ti: (0, 0, 0)
    resident = pl.Buffered(1)
    return pl.pallas_call(
        _mix_ffn_kernel,
        out_shape=jax.ShapeDtypeStruct((b, s, d), F32),
        grid=(b, s // tm),
        in_specs=[
            pl.BlockSpec((1, tm, d), row),
            pl.BlockSpec((1, tm, D_CONV), row),
            pl.BlockSpec((1, tm, D_ATT), row),
            pl.BlockSpec((d, d), const2, pipeline_mode=resident),
            pl.BlockSpec((1, 1, d), per_b),
            pl.BlockSpec((1, d), const2),
            pl.BlockSpec((1, 1, d), per_b),
            pl.BlockSpec((1, 1, d), per_b),
            pl.BlockSpec((1, 1, d), per_b),
            pl.BlockSpec((nf, d, fc), const3, pipeline_mode=resident),
            pl.BlockSpec((nf, d, fc), const3, pipeline_mode=resident),
            pl.BlockSpec((nf, fc, d), const3, pipeline_mode=resident),
        ],
        out_specs=pl.BlockSpec((1, tm, d), row),
        scratch_shapes=[pltpu.VMEM((tm, d), F32)],
        compiler_params=_cparams(2),
        name="mix_out_dense_ffn",
    )(x, yc, ya, wob, g1, gf.reshape(1, d), sc, sh, g2, wg3, wu3, wd3)


def _route_tile(logits, info_ref, info_t_ref, cnt_ref, carry_ref):
    tr = logits.shape[0]
    lane = lax.broadcasted_iota(jnp.int32, (tr, LANE_PAD_E), 1).astype(F32)
    no_lane = float(LANE_PAD_E)
    v0 = jnp.max(logits, axis=-1, keepdims=True)
    i0 = jnp.min(jnp.where(logits == v0, lane, no_lane), axis=-1, keepdims=True)
    rest = jnp.where(lane == i0, -jnp.inf, logits)
    v1 = jnp.max(rest, axis=-1, keepdims=True)
    i1 = jnp.min(jnp.where(rest == v1, lane, no_lane), axis=-1, keepdims=True)
    e1 = jnp.exp(v1 - v0)
    w0 = 1.0 / (1.0 + e1)
    w1 = e1 / (1.0 + e1)
    oh0 = lane == i0
    oh1 = lane == i1
    cnt = jnp.where(jnp.logical_or(oh0, oh1), 1.0, 0.0)
    tri = (lax.broadcasted_iota(jnp.int32, (tr, tr), 1)
           < lax.broadcasted_iota(jnp.int32, (tr, tr), 0)).astype(BF16)
    before = jnp.dot(tri, cnt.astype(BF16), preferred_element_type=F32) + carry_ref[0:1, :]
    r0 = jnp.sum(jnp.where(oh0, before, 0.0), axis=-1, keepdims=True)
    r1 = jnp.sum(jnp.where(oh1, before, 0.0), axis=-1, keepdims=True)
    carry_ref[...] = carry_ref[...] + jnp.sum(cnt, axis=0, keepdims=True)
    cnt_ref[...] = carry_ref[...]
    info = jnp.where(lane == 0, i0, 0.0)
    info = jnp.where(lane == 1, i1, info)
    info = jnp.where(lane == 2, w0, info)
    info = jnp.where(lane == 3, w1, info)
    info = jnp.where(lane == 4, r0, info)
    info = jnp.where(lane == 5, r1, info)
    info_ref[0] = info
    for c in range(tr // LANES):
        blk = info[c * LANES:(c + 1) * LANES, :].T
        info_t_ref[:, c * LANES:(c + 1) * LANES] = blk[0:SUBLANES, :]


def _moe_kernel(te_ref, tn_ref, xs_ref, wg_ref, wu_ref, wd_ref, ys_ref,
                acc_ref, wgb_ref, wub_ref, wdb_ref):
    i = pl.program_id(0)
    f = pl.program_id(1)
    nhalf = tn_ref[i]
    nfull = nhalf // (MOE_SUB // MOE_HALF)

    @pl.when(f == 0)
    def _():
        acc_ref[...] = jnp.zeros_like(acc_ref)

    @pl.when(nhalf > 0)
    def _():
        wgb_ref[...] = wg_ref[0].astype(BF16)
        wub_ref[...] = wu_ref[0].astype(BF16)
        wdb_ref[...] = wd_ref[0].astype(BF16)

    def block(r, rows):
        xb = _unpack_bf16_pairs(xs_ref[pl.ds(r, rows), :]).astype(BF16)
        a = jnp.dot(xb, wgb_ref[...], preferred_element_type=F32)
        u = jnp.dot(xb, wub_ref[...], preferred_element_type=F32)
        t = (_silu(a) * u).astype(BF16)
        acc_ref[pl.ds(r, rows), :] += jnp.dot(t, wdb_ref[...], preferred_element_type=F32)

    def sub(sidx, carry):
        block(pl.multiple_of(sidx * MOE_SUB, MOE_SUB), MOE_SUB)
        return carry

    lax.fori_loop(0, nfull, sub, 0)
    done = nfull * MOE_SUB
    for rows in MOE_TAILS:
        units = rows // MOE_HALF

        @pl.when((nhalf & units) != 0)
        def _(rows=rows, units=units):
            higher = nhalf & (MOE_SUB // MOE_HALF - 1) & ~(2 * units - 1)
            block(pl.multiple_of(done + higher * MOE_HALF, MOE_HALF), rows)

    @pl.when(f == pl.num_programs(1) - 1)
    def _():
        ys_ref[...] = _pack_bf16_pairs(acc_ref[...])


def _moe_ffn(xs, tile_e, tile_nsub, wg, wu, wd):
    rpad, dw = xs.shape
    d = 2 * dw
    ntiles = rpad // MOE_TILE
    ff = wg.shape[2]
    nf = ff // MOE_FC

    def fcol(i, f, tn):
        return jnp.where(tn[i] > 0, f, nf - 1)

    return pl.pallas_call(
        _moe_kernel,
        out_shape=jax.ShapeDtypeStruct((rpad, dw), jnp.int32),
        grid_spec=pltpu.PrefetchScalarGridSpec(
            num_scalar_prefetch=2,
            grid=(ntiles, nf),
            in_specs=[
                pl.BlockSpec((MOE_TILE, dw), lambda i, f, te, tn: (i, 0)),
                pl.BlockSpec((1, d, MOE_FC), lambda i, f, te, tn: (te[i], 0, fcol(i, f, tn))),
                pl.BlockSpec((1, d, MOE_FC), lambda i, f, te, tn: (te[i], 0, fcol(i, f, tn))),
                pl.BlockSpec((1, MOE_FC, d), lambda i, f, te, tn: (te[i], fcol(i, f, tn), 0)),
            ],
            out_specs=pl.BlockSpec((MOE_TILE, dw), lambda i, f, te, tn: (i, 0)),
            scratch_shapes=[
                pltpu.VMEM((MOE_TILE, d), F32),
                pltpu.VMEM((d, MOE_FC), BF16),
                pltpu.VMEM((d, MOE_FC), BF16),
                pltpu.VMEM((MOE_FC, d), BF16),
            ],
        ),
        compiler_params=_cparams(2),
        name="moe_ffn",
    )(tile_e, tile_nsub, xs, wg, wu, wd)


def _combine_kernel(x_ref, y0_ref, y1_ref, info_ref, g2_ref, o_ref):
    info = info_ref[0]
    w0 = info[:, 2:3]
    w1 = info[:, 3:4]
    f = w0 * _unpack_bf16_pairs(y0_ref[0, 0]) + w1 * _unpack_bf16_pairs(y1_ref[0, 0])
    o_ref[0] = x_ref[0] + g2_ref[0] * f


def _combine(x1, y01, info, g2, tm):
    b, s, d = x1.shape
    row = lambda bi, ti: (bi, ti, 0)
    return pl.pallas_call(
        _combine_kernel,
        out_shape=jax.ShapeDtypeStruct((b, s, d), F32),
        grid=(b, s // tm),
        in_specs=[
            pl.BlockSpec((1, tm, d), row),
            pl.BlockSpec((1, 1, tm, d // 2), lambda bi, ti: (0, bi, ti, 0)),
            pl.BlockSpec((1, 1, tm, d // 2), lambda bi, ti: (1, bi, ti, 0)),
            pl.BlockSpec((1, tm, LANE_PAD_E), row),
            pl.BlockSpec((1, 1, d), lambda bi, ti: (bi, 0, 0)),
        ],
        out_specs=pl.BlockSpec((1, tm, d), row),
        compiler_params=_cparams(2),
        name="moe_combine",
    )(x1, y01, y01, info, g2)


def _moe_layer(hp, info, info_t, cnt, x1, g2, wg, wu, wd, tm):
    b, s, d = x1.shape
    n = b * s
    e0 = info_t[0].astype(jnp.int32)
    e1 = info_t[1].astype(jnp.int32)
    r0 = info_t[4].astype(jnp.int32)
    r1 = info_t[5].astype(jnp.int32)
    counts = cnt[0, :N_EXPERTS].astype(jnp.int32)

    ntiles = (2 * n) // MOE_TILE + N_EXPERTS
    tiles_per_e = (counts + MOE_TILE - 1) // MOE_TILE
    tile_end = jnp.cumsum(tiles_per_e)
    tile_start = tile_end - tiles_per_e
    total = tile_end[-1]
    tidx = jnp.arange(ntiles, dtype=jnp.int32)
    live = tidx < total
    tclip = jnp.minimum(tidx, total - 1)
    tile_e = jnp.minimum(jnp.sum(tclip[:, None] >= tile_end[None, :], axis=1),
                         N_EXPERTS - 1).astype(jnp.int32)
    rows_left = counts[tile_e] - (tclip - tile_start[tile_e]) * MOE_TILE
    rows_here = jnp.clip(rows_left, 0, MOE_TILE)
    tile_nsub = jnp.where(live, (rows_here + MOE_HALF - 1) // MOE_HALF, 0).astype(jnp.int32)

    row_start = tile_start * MOE_TILE
    eid = jnp.arange(N_EXPERTS, dtype=jnp.int32)[None, :]
    pos0 = jnp.sum(jnp.where(e0[:, None] == eid, row_start[None, :], 0), axis=1) + r0
    pos1 = jnp.sum(jnp.where(e1[:, None] == eid, row_start[None, :], 0), axis=1) + r1

    xs = _sc_scatter_rows2(hp.reshape(n, d // 2), pos0, pos1, ntiles * MOE_TILE)
    ys = _moe_ffn(xs, tile_e, tile_nsub, wg, wu, wd)
    y01 = _sc_gather_rows(ys, jnp.concatenate([pos0, pos1]))
    return _combine(x1, y01.reshape(2, b, s, d // 2), info.reshape(b, s, LANE_PAD_E), g2, tm)


def kernel(x, c, w_ada, b_ada, norm_mix_g, norm_ffn_g, w_in, w_out, conv_w, conv_b,
           conv_ln_g, conv_ln_b, q_norm_g, k_norm_g, rel_bias, ffn_w_gate, ffn_w_up,
           ffn_w_down, moe_w_router, moe_b_router, moe_w_gate, moe_w_up, moe_w_down):
    b, s, d = x.shape
    depth = w_ada.shape[0]
    tm = min(512, s)
    mod = _ada_mod(c, w_ada, b_ada)
    for l in range(depth):
        sh1, sc1, g1, sh2, sc2, g2 = [
            mod[l, :, j * d:(j + 1) * d].reshape(b, 1, d) for j in range(6)]
        z, qn, kn, vt = _mix_in(x, sc1, sh1, norm_mix_g[l], w_in, l,
                                q_norm_g[l], k_norm_g[l], tm)
        yc = _conv_branch(z, conv_w[l], conv_b[l], conv_ln_g[l], conv_ln_b[l], min(512, s))
        ya = _attention(qn, kn, vt, rel_bias[l])
        i = l // 2
        if l % 2 == 0:
            x = _mix_out_dense_ffn(x, yc, ya, w_out, l, g1, norm_ffn_g[l], sc2, sh2, g2,
                                   ffn_w_gate[i], ffn_w_up[i], ffn_w_down[i],
                                   min(1024, s), 256)
        else:
            x1, hp, info, info_t, cnt = _mix_out_routed(
                x, yc, ya, w_out, l, g1, norm_ffn_g[l], sc2, sh2,
                moe_w_router[i], moe_b_router[i], tm)
            x = _moe_layer(hp, info, info_t, cnt, x1, g2,
                           moe_w_gate[i], moe_w_up[i], moe_w_down[i], tm)
    return x
```

```python
import functools

import jax
import jax.numpy as jnp
from jax import lax
from jax.experimental import pallas as pl
from jax.experimental.pallas import tpu as pltpu
from jax.experimental.pallas import tpu_sc as plsc

F32 = jnp.float32
BF16 = jnp.bfloat16

D_MODEL = 1024
CHUNK = 64
N_PREV_CHUNKS = 8
BAND_PAD = N_PREV_CHUNKS * CHUNK
D_CONV = 512
D_ATT = 512
HEAD_DIM = 64
N_HEADS = 8
CONV_WIDTH = 31
MAX_REL = 128
D_IN_COLS = 2 * D_CONV + 3 * D_ATT
N_EXPERTS = 8
EPS = 1e-6
NEG_INF = -1e30

LANES = 128
SUBLANES = 8
VMEM_LIMIT_BYTES = 56 * 1024 * 1024

ATT_HEADS = 4
ATT_GROUPS = N_HEADS // ATT_HEADS
ATT_W = ATT_HEADS * HEAD_DIM
ATT_Q = 2 * CHUNK
ATT_BAND = BAND_PAD + ATT_Q
ATT_L = ATT_HEADS * ATT_Q

CONV_HALO = 32
CONV_ROWS = 32
LANE_PAD_E = LANES

MOE_HALF = 256
MOE_SUB = 4 * MOE_HALF
MOE_TAILS = (2 * MOE_HALF, MOE_HALF)
MOE_TILE = 9 * MOE_HALF
MOE_FC = 512


def _cparams(n_axes, vmem=VMEM_LIMIT_BYTES):
    return pltpu.CompilerParams(
        dimension_semantics=("arbitrary",) * n_axes, vmem_limit_bytes=vmem)


def _silu(v):
    return v * jax.nn.sigmoid(v)


def _pack_bf16_pairs(v):
    w = v.shape[1] // 2
    bits = lax.bitcast_convert_type(v.astype(BF16).astype(F32), jnp.uint32)
    packed = (bits[:, w:] & jnp.uint32(0xFFFF0000)) | (bits[:, :w] >> 16)
    return lax.bitcast_convert_type(packed, jnp.int32)


def _unpack_bf16_pairs(p):
    bits = lax.bitcast_convert_type(p, jnp.uint32)
    lo = lax.bitcast_convert_type(bits << 16, F32)
    hi = lax.bitcast_convert_type(bits & jnp.uint32(0xFFFF0000), F32)
    return jnp.concatenate([lo, hi], axis=1)


SC_CORES = 2
SC_SUBCORES = 16
SC_WORKERS = SC_CORES * SC_SUBCORES
SC_CHUNK = 64


def _sc_worker_id():
    return lax.axis_index("s") * SC_CORES + lax.axis_index("c")


def _sc_mesh():
    return plsc.VectorSubcoreMesh(core_axis_name="c", subcore_axis_name="s")


def _sc_gather_rows(table, idx):
    _, w = table.shape
    b = idx.shape[0]
    per_w = b // SC_WORKERS
    nch = per_w // SC_CHUNK

    def body(table_hbm, idx_hbm, out_hbm, idx_v, rows_v, gsem, wsem):
        wid = _sc_worker_id()
        base = wid * per_w
        pltpu.sync_copy(idx_hbm.at[wid], idx_v)
        gathers = [None] * nch
        writes = [None] * nch
        gathers[0] = pltpu.async_copy(table_hbm.at[idx_v.at[0]], rows_v.at[0], gsem.at[0])
        for c in range(nch):
            slot = c % 2
            gathers[c].wait()
            if c + 1 < nch:
                if c >= 1:
                    writes[c - 1].wait()
                gathers[c + 1] = pltpu.async_copy(
                    table_hbm.at[idx_v.at[c + 1]], rows_v.at[1 - slot], gsem.at[1 - slot])
            writes[c] = pltpu.async_copy(
                rows_v.at[slot], out_hbm.at[pl.ds(base + c * SC_CHUNK, SC_CHUNK)], wsem.at[slot])
        if nch >= 2:
            writes[nch - 2].wait()
        writes[nch - 1].wait()

    call = pl.kernel(
        body, mesh=_sc_mesh(),
        out_type=jax.ShapeDtypeStruct((b, w), jnp.int32),
        scratch_types=[pltpu.VMEM((nch, SC_CHUNK), jnp.int32),
                       pltpu.VMEM((2, SC_CHUNK, w), jnp.int32),
                       pltpu.SemaphoreType.DMA((2,)), pltpu.SemaphoreType.DMA((2,))],
        name="sc_gather_rows")
    return call(table, idx.reshape(SC_WORKERS, nch, SC_CHUNK))


def _sc_scatter_rows2(src, idx0, idx1, rows_out):
    n, w = src.shape
    per_w = n // SC_WORKERS
    nch = per_w // SC_CHUNK

    def body(src_hbm, i0_hbm, i1_hbm, out_hbm, i0_v, i1_v, rows_v, rsem, wsem):
        wid = _sc_worker_id()
        base = wid * per_w
        pltpu.sync_copy(i0_hbm.at[wid], i0_v)
        pltpu.sync_copy(i1_hbm.at[wid], i1_v)
        reads = [None] * nch
        writes = [None] * nch
        reads[0] = pltpu.async_copy(src_hbm.at[pl.ds(base, SC_CHUNK)], rows_v.at[0], rsem.at[0])
        for c in range(nch):
            slot = c % 2
            reads[c].wait()
            if c + 1 < nch:
                if c >= 1:
                    for wr in writes[c - 1]:
                        wr.wait()
                reads[c + 1] = pltpu.async_copy(
                    src_hbm.at[pl.ds(base + (c + 1) * SC_CHUNK, SC_CHUNK)],
                    rows_v.at[1 - slot], rsem.at[1 - slot])
            writes[c] = (
                pltpu.async_copy(rows_v.at[slot], out_hbm.at[i0_v.at[c]], wsem.at[slot, 0]),
                pltpu.async_copy(rows_v.at[slot], out_hbm.at[i1_v.at[c]], wsem.at[slot, 1]),
            )
        for c in range(max(nch - 2, 0), nch):
            for wr in writes[c]:
                wr.wait()

    call = pl.kernel(
        body, mesh=_sc_mesh(),
        out_type=jax.ShapeDtypeStruct((rows_out, w), jnp.int32),
        scratch_types=[pltpu.VMEM((nch, SC_CHUNK), jnp.int32),
                       pltpu.VMEM((nch, SC_CHUNK), jnp.int32),
                       pltpu.VMEM((2, SC_CHUNK, w), jnp.int32),
                       pltpu.SemaphoreType.DMA((2,)), pltpu.SemaphoreType.DMA((2, 2))],
        name="sc_scatter_rows")
    shape3 = (SC_WORKERS, nch, SC_CHUNK)
    return call(src, idx0.reshape(shape3), idx1.reshape(shape3))


def _ada_kernel(c_ref, w_ref, b_ref, o_ref):
    ca = _silu(c_ref[...]).astype(BF16)
    w = w_ref[0].astype(BF16)
    o_ref[0] = jnp.dot(ca, w, preferred_element_type=F32) + b_ref[0]


def _ada_mod(c, w_ada, b_ada):
    depth, d, n6 = w_ada.shape
    b = c.shape[0]
    rows = 16
    c_pad = jnp.zeros((rows, d), F32).at[:b].set(c)
    tn = 1536
    out = pl.pallas_call(
        _ada_kernel,
        out_shape=jax.ShapeDtypeStruct((depth, rows, n6), F32),
        grid=(depth, n6 // tn),
        in_specs=[
            pl.BlockSpec((rows, d), lambda l, j: (0, 0)),
            pl.BlockSpec((1, d, tn), lambda l, j: (l, 0, j)),
            pl.BlockSpec((1, 1, tn), lambda l, j: (l, 0, j)),
        ],
        out_specs=pl.BlockSpec((1, rows, tn), lambda l, j: (l, 0, j)),
        compiler_params=_cparams(2),
        name="ada_mod",
    )(c_pad, w_ada, b_ada.reshape(depth, 1, n6))
    return out[:, :b]


def _rms_mod(xf, g, sc, sh):
    ms = jnp.mean(xf * xf, axis=-1, keepdims=True)
    return xf * lax.rsqrt(ms + EPS) * g * (1.0 + sc) + sh


def _mix_in_kernel(x_ref, sc_ref, sh_ref, g_ref, w_ref, gq_ref, gk_ref, ones_ref,
                   z_ref, q_ref, k_ref, vt_ref, wbf_ref):
    first = jnp.logical_and(pl.program_id(0) == 0, pl.program_id(1) == 0)

    @pl.when(first)
    def _():
        wbf_ref[...] = w_ref[...].astype(BF16)

    h = _rms_mod(x_ref[0], g_ref[...], sc_ref[0], sh_ref[0]).astype(BF16)
    proj = jnp.dot(h, wbf_ref[...], preferred_element_type=F32)

    a = proj[:, :D_CONV]
    gate = proj[:, D_CONV:2 * D_CONV]
    z_ref[0] = (a * jax.nn.sigmoid(gate)).astype(BF16)

    def head_norm(t, g):
        sq = (t * t).astype(BF16)
        ss = jnp.concatenate(
            [jnp.dot(sq[:, c:c + ATT_W], ones_ref[...], preferred_element_type=F32)
             for c in range(0, D_ATT, ATT_W)], axis=1)
        return (t * lax.rsqrt(ss * (1.0 / HEAD_DIM) + EPS) * g).astype(BF16)

    o = 2 * D_CONV
    q_ref[0] = head_norm(proj[:, o:o + D_ATT], gq_ref[...])
    k_ref[0] = head_norm(proj[:, o + D_ATT:o + 2 * D_ATT], gk_ref[...])
    v = proj[:, o + 2 * D_ATT:]
    tm = v.shape[0]
    for cidx in range(tm // ATT_Q):
        vt_ref[0, cidx] = v[cidx * ATT_Q:(cidx + 1) * ATT_Q, :].T.astype(BF16)


def _mix_in(x, sc, sh, g, w_in, layer, gq, gk, tm):
    b, s, d = x.shape
    ones_bd = (jnp.arange(ATT_W)[:, None] // HEAD_DIM
               == jnp.arange(ATT_W)[None, :] // HEAD_DIM).astype(BF16)
    gq_t = (jnp.tile(gq, N_HEADS) * (HEAD_DIM ** -0.5)).reshape(1, D_ATT)
    gk_t = jnp.tile(gk, N_HEADS).reshape(1, D_ATT)
    row = lambda bi, ti: (bi, ti, 0)
    per_b = lambda bi, ti: (bi, 0, 0)
    const2 = lambda bi, ti: (0, 0)
    return pl.pallas_call(
        _mix_in_kernel,
        out_shape=(
            jax.ShapeDtypeStruct((b, s, D_CONV), BF16),
            jax.ShapeDtypeStruct((b, s, D_ATT), BF16),
            jax.ShapeDtypeStruct((b, s, D_ATT), BF16),
            jax.ShapeDtypeStruct((b, s // ATT_Q, D_ATT, ATT_Q), BF16),
        ),
        grid=(b, s // tm),
        in_specs=[
            pl.BlockSpec((1, tm, d), row),
            pl.BlockSpec((1, 1, d), per_b),
            pl.BlockSpec((1, 1, d), per_b),
            pl.BlockSpec((1, d), const2),
            pl.BlockSpec((None, d, D_IN_COLS), lambda bi, ti: (layer, 0, 0),
                         pipeline_mode=pl.Buffered(1)),
            pl.BlockSpec((1, D_ATT), const2),
            pl.BlockSpec((1, D_ATT), const2),
            pl.BlockSpec((ATT_W, ATT_W), const2),
        ],
        out_specs=(
            pl.BlockSpec((1, tm, D_CONV), row),
            pl.BlockSpec((1, tm, D_ATT), row),
            pl.BlockSpec((1, tm, D_ATT), row),
            pl.BlockSpec((1, tm // ATT_Q, D_ATT, ATT_Q), lambda bi, ti: (bi, ti, 0, 0)),
        ),
        scratch_shapes=[pltpu.VMEM((d, D_IN_COLS), BF16)],
        compiler_params=_cparams(2),
        name="mix_in",
    )(x, sc, sh, g.reshape(1, d), w_in, gq_t, gk_t, ones_bd)


def _conv_kernel(zc_ref, zp_ref, w_ref, cb_ref, lg_ref, lb_ref, o_ref, win_ref, sh_ref,
                 acc_ref):
    tt = zc_ref.shape[1]
    t = pl.program_id(1)
    halo = zp_ref[0].astype(F32)
    win_ref[0:CONV_HALO, :] = jnp.where(t == 0, 0.0, halo)
    win_ref[CONV_HALO:, :] = zc_ref[0].astype(F32)
    span = tt + CONV_HALO - SUBLANES
    for sft in range(1, SUBLANES):
        sh_ref[sft - 1, 0:span, :] = win_ref[sft:sft + span, :]
    base = CONV_HALO - (CONV_WIDTH - 1)
    tiles = CONV_ROWS // SUBLANES

    def group(gidx, carry):
        r = pl.multiple_of(gidx * CONV_ROWS, CONV_ROWS)
        acc = jnp.zeros((tiles, SUBLANES, D_CONV), F32) + cb_ref[...]
        for j in range(CONV_WIDTH):
            whole, sft = divmod(base + j, SUBLANES)
            start = pl.multiple_of(r + whole * SUBLANES, SUBLANES)
            if sft == 0:
                tap = win_ref[pl.ds(start, CONV_ROWS), :]
            else:
                tap = sh_ref[sft - 1, pl.ds(start, CONV_ROWS), :]
            acc = acc + tap.reshape(tiles, SUBLANES, D_CONV) * w_ref[j]
        acc_ref[pl.ds(r, CONV_ROWS), :] = acc.reshape(CONV_ROWS, D_CONV)
        return carry

    lax.fori_loop(0, tt // CONV_ROWS, group, 0)
    acc = acc_ref[...]
    mu = jnp.mean(acc, axis=-1, keepdims=True)
    xc = acc - mu
    var = jnp.mean(xc * xc, axis=-1, keepdims=True)
    y = xc * lax.rsqrt(var + EPS) * lg_ref[...] + lb_ref[...]
    o_ref[0] = _silu(y).astype(BF16)


def _conv_branch(z, conv_w, conv_b, ln_g, ln_b, tt):
    b, s, c = z.shape
    hb = tt // CONV_HALO
    w_tiles = jnp.broadcast_to(conv_w.reshape(CONV_WIDTH, 1, c), (CONV_WIDTH, SUBLANES, c))
    const2 = lambda bi, ti: (0, 0)
    return pl.pallas_call(
        _conv_kernel,
        out_shape=jax.ShapeDtypeStruct((b, s, c), BF16),
        grid=(b, s // tt),
        in_specs=[
            pl.BlockSpec((1, tt, c), lambda bi, ti: (bi, ti, 0)),
            pl.BlockSpec((1, CONV_HALO, c),
                         lambda bi, ti: (bi, jnp.maximum(ti * hb - 1, 0), 0)),
            pl.BlockSpec((CONV_WIDTH, SUBLANES, c), lambda bi, ti: (0, 0, 0)),
            pl.BlockSpec((1, c), const2),
            pl.BlockSpec((1, c), const2),
            pl.BlockSpec((1, c), const2),
        ],
        out_specs=pl.BlockSpec((1, tt, c), lambda bi, ti: (bi, ti, 0)),
        scratch_shapes=[pltpu.VMEM((tt + CONV_HALO, c), F32),
                        pltpu.VMEM((SUBLANES - 1, tt + CONV_HALO, c), F32),
                        pltpu.VMEM((tt, c), F32)],
        compiler_params=_cparams(2),
        name="conv_branch",
    )(z, z, w_tiles, conv_b.reshape(1, c),
      ln_g.reshape(1, c), ln_b.reshape(1, c))


def _attn_kernel(q_ref, k_ref, vt_ref, bias_ref, o_ref, kpad_ref, vtpad_ref,
                 st0_ref, st1_ref, pb0_ref, pb1_ref, den0_ref, den1_ref):
    st_refs = (st0_ref, st1_ref)
    pb_refs = (pb0_ref, pb1_ref)
    den_refs = (den0_ref, den1_ref)
    s = q_ref.shape[1]
    npad = BAND_PAD // ATT_Q
    kpad_ref[0:BAND_PAD, :] = jnp.zeros((BAND_PAD, ATT_W), BF16)
    kpad_ref[BAND_PAD:, :] = k_ref[0]
    vtpad_ref[0:npad] = jnp.zeros((npad, ATT_W, ATT_Q), BF16)
    vtpad_ref[npad:] = vt_ref[0]

    iota = lambda shape, dim: lax.broadcasted_iota(jnp.int32, shape, dim)
    q_shift = ATT_Q.bit_length() - 1
    d_shift = HEAD_DIM.bit_length() - 1
    qb_mask = (iota((ATT_L, ATT_W), 0) >> q_shift) == (iota((ATT_L, ATT_W), 1) >> d_shift)
    ot_mask = (iota((ATT_W, ATT_L), 0) >> d_shift) == (iota((ATT_W, ATT_L), 1) >> q_shift)
    sel = jnp.where((iota((ATT_Q, ATT_L), 1) & (ATT_Q - 1)) == iota((ATT_Q, ATT_L), 0),
                    1.0, 0.0).astype(BF16)
    key_row = lax.broadcasted_iota(jnp.int32, (ATT_BAND, ATT_L), 0)
    contract_last = (((1,), (1,)), ((), ()))

    def scores(m, p):
        r0 = pl.multiple_of(m * ATT_Q, ATT_Q)
        qt = q_ref[0, pl.ds(r0, ATT_Q), :]
        qb = jnp.where(qb_mask, jnp.concatenate([qt] * ATT_HEADS, axis=0), 0)
        kb = kpad_ref[pl.ds(r0, ATT_BAND), :]
        st_refs[p][...] = lax.dot_general(kb, qb.astype(BF16), contract_last,
                                          preferred_element_type=F32)

    def softmax(m, p, masked):
        st = st_refs[p][...] + bias_ref[0]
        if masked:
            st = jnp.where(key_row >= BAND_PAD - m * ATT_Q, st, NEG_INF)
        mx = jnp.max(st, axis=0, keepdims=True)
        e = jnp.exp(st - mx)
        den_refs[p][...] = jnp.sum(e, axis=0, keepdims=True)
        pb_refs[p][...] = e.astype(BF16)

    def values(m, p):
        r0 = pl.multiple_of(m * ATT_Q, ATT_Q)
        vb = jnp.concatenate([vtpad_ref[m + c] for c in range(ATT_BAND // ATT_Q)],
                             axis=1)
        ot = jnp.dot(vb, pb_refs[p][...], preferred_element_type=F32)
        ot = jnp.where(ot_mask, ot / den_refs[p][...], 0.0).astype(BF16)
        y = lax.dot_general(sel, ot, contract_last, preferred_element_type=F32)
        o_ref[0, pl.ds(r0, ATT_Q), :] = y.astype(BF16)

    n = s // ATT_Q
    n_masked = BAND_PAD // ATT_Q
    scores(0, 0)
    scores(1, 1)
    softmax(0, 0, True)

    def pair(masked, i, carry):
        m = 2 * i
        scores(m, 0)
        softmax(m - 1, 1, masked)
        values(m - 2, 0)
        scores(m + 1, 1)
        softmax(m, 0, masked)
        values(m - 1, 1)
        return carry

    split = min(n // 2, n_masked // 2 + 1)
    lax.fori_loop(1, split, functools.partial(pair, True), 0)
    lax.fori_loop(split, n // 2, functools.partial(pair, False), 0)
    softmax(n - 1, 1, n - 1 < n_masked)
    values(n - 2, 0)
    values(n - 1, 1)


def _attn_bias_t(rel_bias):
    rb = rel_bias.astype(F32)
    nu = ATT_BAND + ATT_Q - 1
    n_low = BAND_PAD - MAX_REL + ATT_Q
    t = jnp.concatenate([jnp.repeat(rb[:, :1], n_low, axis=1),
                         rb[:, 1:1 + nu - n_low]], axis=1)
    tp = jnp.pad(t, ((0, 0), (0, 1)))
    skew = jnp.tile(tp, (1, ATT_Q))[:, :ATT_Q * nu].reshape(N_HEADS, ATT_Q, nu)
    bias = skew[:, :, ATT_Q - 1:]
    bias = bias.reshape(ATT_GROUPS, ATT_HEADS, ATT_Q, ATT_BAND)
    bias = bias.transpose(0, 3, 1, 2).reshape(ATT_GROUPS, ATT_BAND, ATT_L)
    r = jnp.arange(ATT_BAND)[:, None]
    qq = jnp.arange(ATT_L)[None, :] % ATT_Q
    first = (qq // CHUNK) * CHUNK
    valid = (r >= first) & (r < first + BAND_PAD + CHUNK)
    return jnp.where(valid[None], bias, NEG_INF)


def _attention(qn, kn, vt, rel_bias):
    b, s, _ = qn.shape
    bias_t = _attn_bias_t(rel_bias)
    nck = s // ATT_Q
    assert nck >= 4 and nck % 2 == 0, "the attention pipeline runs query steps in pairs"
    return pl.pallas_call(
        _attn_kernel,
        out_shape=jax.ShapeDtypeStruct((b, s, D_ATT), BF16),
        grid=(b, ATT_GROUPS),
        in_specs=[
            pl.BlockSpec((1, s, ATT_W), lambda bi, gi: (bi, 0, gi)),
            pl.BlockSpec((1, s, ATT_W), lambda bi, gi: (bi, 0, gi)),
            pl.BlockSpec((1, nck, ATT_W, ATT_Q), lambda bi, gi: (bi, 0, gi, 0)),
            pl.BlockSpec((1, ATT_BAND, ATT_L), lambda bi, gi: (gi, 0, 0)),
        ],
        out_specs=pl.BlockSpec((1, s, ATT_W), lambda bi, gi: (bi, 0, gi)),
        scratch_shapes=[
            pltpu.VMEM((s + BAND_PAD, ATT_W), BF16),
            pltpu.VMEM((nck + BAND_PAD // ATT_Q, ATT_W, ATT_Q), BF16),
            pltpu.VMEM((ATT_BAND, ATT_L), F32), pltpu.VMEM((ATT_BAND, ATT_L), F32),
            pltpu.VMEM((ATT_BAND, ATT_L), BF16), pltpu.VMEM((ATT_BAND, ATT_L), BF16),
            pltpu.VMEM((1, ATT_L), F32), pltpu.VMEM((1, ATT_L), F32),
        ],
        compiler_params=_cparams(2),
        name="band_attention",
    )(qn, kn, vt, bias_t)


def _mix_out_kernel(x_ref, yc_ref, ya_ref, w_ref, g1_ref, gf_ref, sc_ref, sh_ref,
                    wr_ref, br_ref, x1_ref, h_ref, info_ref, info_t_ref, cnt_ref,
                    wbf_ref, carry_ref):
    first = jnp.logical_and(pl.program_id(0) == 0, pl.program_id(1) == 0)

    @pl.when(first)
    def _():
        wbf_ref[...] = w_ref[...].astype(BF16)
        carry_ref[...] = jnp.zeros_like(carry_ref)

    y = jnp.dot(yc_ref[0], wbf_ref[0:D_CONV, :], preferred_element_type=F32)
    y = y + jnp.dot(ya_ref[0], wbf_ref[D_CONV:, :], preferred_element_type=F32)
    x1 = x_ref[0] + g1_ref[0] * y
    x1_ref[0] = x1
    h = _rms_mod(x1, gf_ref[...], sc_ref[0], sh_ref[0])
    h_ref[0] = _pack_bf16_pairs(h)
    logits = jnp.dot(h.astype(BF16), wr_ref[...], preferred_element_type=F32) + br_ref[...]
    _route_tile(logits, info_ref, info_t_ref, cnt_ref, carry_ref)


def _mix_out_routed(x, yc, ya, w_out, layer, g1, gf, sc, sh, w_router, b_router, tm):
    b, s, d = x.shape
    nt = s // tm
    row = lambda bi, ti: (bi, ti, 0)
    per_b = lambda bi, ti: (bi, 0, 0)
    const2 = lambda bi, ti: (0, 0)
    w_pad = jnp.zeros((d, LANE_PAD_E), BF16).at[:, :N_EXPERTS].set(w_router.astype(BF16))
    b_pad = jnp.full((1, LANE_PAD_E), -jnp.inf, F32).at[0, :N_EXPERTS].set(
        b_router.astype(F32))
    return pl.pallas_call(
        _mix_out_kernel,
        out_shape=(jax.ShapeDtypeStruct((b, s, d), F32),
                   jax.ShapeDtypeStruct((b, s, d // 2), jnp.int32),
                   jax.ShapeDtypeStruct((b, s, LANE_PAD_E), F32),
                   jax.ShapeDtypeStruct((SUBLANES, b * s), F32),
                   jax.ShapeDtypeStruct((SUBLANES, LANE_PAD_E), F32)),
        grid=(b, nt),
        in_specs=[
            pl.BlockSpec((1, tm, d), row),
            pl.BlockSpec((1, tm, D_CONV), row),
            pl.BlockSpec((1, tm, D_ATT), row),
            pl.BlockSpec((None, d, d), lambda bi, ti: (layer, 0, 0),
                         pipeline_mode=pl.Buffered(1)),
            pl.BlockSpec((1, 1, d), per_b),
            pl.BlockSpec((1, d), const2),
            pl.BlockSpec((1, 1, d), per_b),
            pl.BlockSpec((1, 1, d), per_b),
            pl.BlockSpec((d, LANE_PAD_E), const2),
            pl.BlockSpec((1, LANE_PAD_E), const2),
        ],
        out_specs=(pl.BlockSpec((1, tm, d), row), pl.BlockSpec((1, tm, d // 2), row),
                   pl.BlockSpec((1, tm, LANE_PAD_E), row),
                   pl.BlockSpec((SUBLANES, tm), lambda bi, ti: (0, bi * nt + ti)),
                   pl.BlockSpec((SUBLANES, LANE_PAD_E), const2)),
        scratch_shapes=[pltpu.VMEM((d, d), BF16), pltpu.VMEM((SUBLANES, LANE_PAD_E), F32)],
        compiler_params=_cparams(2),
        name="mix_out",
    )(x, yc, ya, w_out, g1, gf.reshape(1, d), sc, sh, w_pad, b_pad)


def _cast_ffn_kernel(wg_ref, wu_ref, wd_ref, wo_ref, wg3_ref, wu3_ref, wd3_ref, wob_ref):
    wg3_ref[0] = wg_ref[...].astype(BF16)
    wu3_ref[0] = wu_ref[...].astype(BF16)
    wd3_ref[0] = wd_ref[...].astype(BF16)

    @pl.when(pl.program_id(0) == 0)
    def _():
        wob_ref[...] = wo_ref[...].astype(BF16)


def _cast_ffn_weights(wg, wu, wd, w_out, layer, fc):
    d, ff = wg.shape
    nf = ff // fc
    return pl.pallas_call(
        _cast_ffn_kernel,
        out_shape=(jax.ShapeDtypeStruct((nf, d, fc), BF16),
                   jax.ShapeDtypeStruct((nf, d, fc), BF16),
                   jax.ShapeDtypeStruct((nf, fc, d), BF16),
                   jax.ShapeDtypeStruct((d, d), BF16)),
        grid=(nf,),
        in_specs=[
            pl.BlockSpec((d, fc), lambda f: (0, f)),
            pl.BlockSpec((d, fc), lambda f: (0, f)),
            pl.BlockSpec((fc, d), lambda f: (f, 0)),
            pl.BlockSpec((None, d, d), lambda f: (layer, 0, 0)),
        ],
        out_specs=(pl.BlockSpec((1, d, fc), lambda f: (f, 0, 0)),
                   pl.BlockSpec((1, d, fc), lambda f: (f, 0, 0)),
                   pl.BlockSpec((1, fc, d), lambda f: (f, 0, 0)),
                   pl.BlockSpec((d, d), lambda f: (0, 0))),
        compiler_params=_cparams(1),
        name="cast_ffn_weights",
    )(wg, wu, wd, w_out)


def _mix_ffn_kernel(x_ref, yc_ref, ya_ref, wo_ref, g1_ref, gf_ref, sc_ref, sh_ref, g2_ref,
                    wg_ref, wu_ref, wd_ref, o_ref, acc_ref):
    y = jnp.dot(yc_ref[0], wo_ref[0:D_CONV, :], preferred_element_type=F32)
    y = y + jnp.dot(ya_ref[0], wo_ref[D_CONV:, :], preferred_element_type=F32)
    x1 = x_ref[0] + g1_ref[0] * y
    h = _rms_mod(x1, gf_ref[...], sc_ref[0], sh_ref[0]).astype(BF16)
    for c in range(wg_ref.shape[0]):
        a = jnp.dot(h, wg_ref[c], preferred_element_type=F32)
        u = jnp.dot(h, wu_ref[c], preferred_element_type=F32)
        t = (_silu(a) * u).astype(BF16)
        dn = jnp.dot(t, wd_ref[c], preferred_element_type=F32)
        if c == 0:
            acc_ref[...] = dn
        else:
            acc_ref[...] += dn
    o_ref[0] = x1 + g2_ref[0] * acc_ref[...]


def _mix_out_dense_ffn(x, yc, ya, w_out, layer, g1, gf, sc, sh, g2, wg, wu, wd, tm, fc):
    b, s, d = x.shape
    wg3, wu3, wd3, wob = _cast_ffn_weights(wg, wu, wd, w_out, layer, fc)
    nf = wg3.shape[0]
    row = lambda bi, ti: (bi, ti, 0)
    per_b = lambda bi, ti: (bi, 0, 0)
    const2 = lambda bi, ti: (0, 0)
    const3 = lambda bi, ti: (0, 0, 0)
    resident = pl.Buffered(1)
    return pl.pallas_call(
        _mix_ffn_kernel,
        out_shape=jax.ShapeDtypeStruct((b, s, d), F32),
        grid=(b, s // tm),
        in_specs=[
            pl.BlockSpec((1, tm, d), row),
            pl.BlockSpec((1, tm, D_CONV), row),
            pl.BlockSpec((1, tm, D_ATT), row),
            pl.BlockSpec((d, d), const2, pipeline_mode=resident),
            pl.BlockSpec((1, 1, d), per_b),
            pl.BlockSpec((1, d), const2),
            pl.BlockSpec((1, 1, d), per_b),
            pl.BlockSpec((1, 1, d), per_b),
            pl.BlockSpec((1, 1, d), per_b),
            pl.BlockSpec((nf, d, fc), const3, pipeline_mode=resident),
            pl.BlockSpec((nf, d, fc), const3, pipeline_mode=resident),
            pl.BlockSpec((nf, fc, d), const3, pipeline_mode=resident),
        ],
        out_specs=pl.BlockSpec((1, tm, d), row),
        scratch_shapes=[pltpu.VMEM((tm, d), F32)],
        compiler_params=_cparams(2),
        name="mix_out_dense_ffn",
    )(x, yc, ya, wob, g1, gf.reshape(1, d), sc, sh, g2, wg3, wu3, wd3)


def _route_tile(logits, info_ref, info_t_ref, cnt_ref, carry_ref):
    tr = logits.shape[0]
    lane = lax.broadcasted_iota(jnp.int32, (tr, LANE_PAD_E), 1).astype(F32)
    no_lane = float(LANE_PAD_E)
    v0 = jnp.max(logits, axis=-1, keepdims=True)
    i0 = jnp.min(jnp.where(logits == v0, lane, no_lane), axis=-1, keepdims=True)
    rest = jnp.where(lane == i0, -jnp.inf, logits)
    v1 = jnp.max(rest, axis=-1, keepdims=True)
    i1 = jnp.min(jnp.where(rest == v1, lane, no_lane), axis=-1, keepdims=True)
    e1 = jnp.exp(v1 - v0)
    w0 = 1.0 / (1.0 + e1)
    w1 = e1 / (1.0 + e1)
    oh0 = lane == i0
    oh1 = lane == i1
    cnt = jnp.where(jnp.logical_or(oh0, oh1), 1.0, 0.0)
    tri = (lax.broadcasted_iota(jnp.int32, (tr, tr), 1)
           < lax.broadcasted_iota(jnp.int32, (tr, tr), 0)).astype(BF16)
    before = jnp.dot(tri, cnt.astype(BF16), preferred_element_type=F32) + carry_ref[0:1, :]
    r0 = jnp.sum(jnp.where(oh0, before, 0.0), axis=-1, keepdims=True)
    r1 = jnp.sum(jnp.where(oh1, before, 0.0), axis=-1, keepdims=True)
    carry_ref[...] = carry_ref[...] + jnp.sum(cnt, axis=0, keepdims=True)
    cnt_ref[...] = carry_ref[...]
    info = jnp.where(lane == 0, i0, 0.0)
    info = jnp.where(lane == 1, i1, info)
    info = jnp.where(lane == 2, w0, info)
    info = jnp.where(lane == 3, w1, info)
    info = jnp.where(lane == 4, r0, info)
    info = jnp.where(lane == 5, r1, info)
    info_ref[0] = info
    for c in range(tr // LANES):
        blk = info[c * LANES:(c + 1) * LANES, :].T
        info_t_ref[:, c * LANES:(c + 1) * LANES] = blk[0:SUBLANES, :]


def _moe_kernel(te_ref, tn_ref, xs_ref, wg_ref, wu_ref, wd_ref, ys_ref,
                acc_ref, wgb_ref, wub_ref, wdb_ref):
    i = pl.program_id(0)
    f = pl.program_id(1)
    nhalf = tn_ref[i]
    nfull = nhalf // (MOE_SUB // MOE_HALF)

    @pl.when(f == 0)
    def _():
        acc_ref[...] = jnp.zeros_like(acc_ref)

    @pl.when(nhalf > 0)
    def _():
        wgb_ref[...] = wg_ref[0].astype(BF16)
        wub_ref[...] = wu_ref[0].astype(BF16)
        wdb_ref[...] = wd_ref[0].astype(BF16)

    def block(r, rows):
        xb = _unpack_bf16_pairs(xs_ref[pl.ds(r, rows), :]).astype(BF16)
        a = jnp.dot(xb, wgb_ref[...], preferred_element_type=F32)
        u = jnp.dot(xb, wub_ref[...], preferred_element_type=F32)
        t = (_silu(a) * u).astype(BF16)
        acc_ref[pl.ds(r, rows), :] += jnp.dot(t, wdb_ref[...], preferred_element_type=F32)

    def sub(sidx, carry):
        block(pl.multiple_of(sidx * MOE_SUB, MOE_SUB), MOE_SUB)
        return carry

    lax.fori_loop(0, nfull, sub, 0)
    done = nfull * MOE_SUB
    for rows in MOE_TAILS:
        units = rows // MOE_HALF

        @pl.when((nhalf & units) != 0)
        def _(rows=rows, units=units):
            higher = nhalf & (MOE_SUB // MOE_HALF - 1) & ~(2 * units - 1)
            block(pl.multiple_of(done + higher * MOE_HALF, MOE_HALF), rows)

    @pl.when(f == pl.num_programs(1) - 1)
    def _():
        ys_ref[...] = _pack_bf16_pairs(acc_ref[...])


def _moe_ffn(xs, tile_e, tile_nsub, wg, wu, wd):
    rpad, dw = xs.shape
    d = 2 * dw
    ntiles = rpad // MOE_TILE
    ff = wg.shape[2]
    nf = ff // MOE_FC

    def fcol(i, f, tn):
        return jnp.where(tn[i] > 0, f, nf - 1)

    return pl.pallas_call(
        _moe_kernel,
        out_shape=jax.ShapeDtypeStruct((rpad, dw), jnp.int32),
        grid_spec=pltpu.PrefetchScalarGridSpec(
            num_scalar_prefetch=2,
            grid=(ntiles, nf),
            in_specs=[
                pl.BlockSpec((MOE_TILE, dw), lambda i, f, te, tn: (i, 0)),
                pl.BlockSpec((1, d, MOE_FC), lambda i, f, te, tn: (te[i], 0, fcol(i, f, tn))),
                pl.BlockSpec((1, d, MOE_FC), lambda i, f, te, tn: (te[i], 0, fcol(i, f, tn))),
                pl.BlockSpec((1, MOE_FC, d), lambda i, f, te, tn: (te[i], fcol(i, f, tn), 0)),
            ],
            out_specs=pl.BlockSpec((MOE_TILE, dw), lambda i, f, te, tn: (i, 0)),
            scratch_shapes=[
                pltpu.VMEM((MOE_TILE, d), F32),
                pltpu.VMEM((d, MOE_FC), BF16),
                pltpu.VMEM((d, MOE_FC), BF16),
                pltpu.VMEM((MOE_FC, d), BF16),
            ],
        ),
        compiler_params=_cparams(2),
        name="moe_ffn",
    )(tile_e, tile_nsub, xs, wg, wu, wd)


def _combine_kernel(x_ref, y0_ref, y1_ref, info_ref, g2_ref, o_ref):
    info = info_ref[0]
    w0 = info[:, 2:3]
    w1 = info[:, 3:4]
    f = w0 * _unpack_bf16_pairs(y0_ref[0, 0]) + w1 * _unpack_bf16_pairs(y1_ref[0, 0])
    o_ref[0] = x_ref[0] + g2_ref[0] * f


def _combine(x1, y01, info, g2, tm):
    b, s, d = x1.shape
    row = lambda bi, ti: (bi, ti, 0)
    return pl.pallas_call(
        _combine_kernel,
        out_shape=jax.ShapeDtypeStruct((b, s, d), F32),
        grid=(b, s // tm),
        in_specs=[
            pl.BlockSpec((1, tm, d), row),
            pl.BlockSpec((1, 1, tm, d // 2), lambda bi, ti: (0, bi, ti, 0)),
            pl.BlockSpec((1, 1, tm, d // 2), lambda bi, ti: (1, bi, ti, 0)),
            pl.BlockSpec((1, tm, LANE_PAD_E), row),
            pl.BlockSpec((1, 1, d), lambda bi, ti: (bi, 0, 0)),
        ],
        out_specs=pl.BlockSpec((1, tm, d), row),
        compiler_params=_cparams(2),
        name="moe_combine",
    )(x1, y01, y01, info, g2)


def _moe_layer(hp, info, info_t, cnt, x1, g2, wg, wu, wd, tm):
    b, s, d = x1.shape
    n = b * s
    e0 = info_t[0].astype(jnp.int32)
    e1 = info_t[1].astype(jnp.int32)
    r0 = info_t[4].astype(jnp.int32)
    r1 = info_t[5].astype(jnp.int32)
    counts = cnt[0, :N_EXPERTS].astype(jnp.int32)

    ntiles = (2 * n) // MOE_TILE + N_EXPERTS
    tiles_per_e = (counts + MOE_TILE - 1) // MOE_TILE
    tile_end = jnp.cumsum(tiles_per_e)
    tile_start = tile_end - tiles_per_e
    total = tile_end[-1]
    tidx = jnp.arange(ntiles, dtype=jnp.int32)
    live = tidx < total
    tclip = jnp.minimum(tidx, total - 1)
    tile_e = jnp.minimum(jnp.sum(tclip[:, None] >= tile_end[None, :], axis=1),
                         N_EXPERTS - 1).astype(jnp.int32)
    rows_left = counts[tile_e] - (tclip - tile_start[tile_e]) * MOE_TILE
    rows_here = jnp.clip(rows_left, 0, MOE_TILE)
    tile_nsub = jnp.where(live, (rows_here + MOE_HALF - 1) // MOE_HALF, 0).astype(jnp.int32)

    row_start = tile_start * MOE_TILE
    eid = jnp.arange(N_EXPERTS, dtype=jnp.int32)[None, :]
    pos0 = jnp.sum(jnp.where(e0[:, None] == eid, row_start[None, :], 0), axis=1) + r0
    pos1 = jnp.sum(jnp.where(e1[:, None] == eid, row_start[None, :], 0), axis=1) + r1

    xs = _sc_scatter_rows2(hp.reshape(n, d // 2), pos0, pos1, ntiles * MOE_TILE)
    ys = _moe_ffn(xs, tile_e, tile_nsub, wg, wu, wd)
    y01 = _sc_gather_rows(ys, jnp.concatenate([pos0, pos1]))
    return _combine(x1, y01.reshape(2, b, s, d // 2), info.reshape(b, s, LANE_PAD_E), g2, tm)


def kernel(x, c, w_ada, b_ada, norm_mix_g, norm_ffn_g, w_in, w_out, conv_w, conv_b,
           conv_ln_g, conv_ln_b, q_norm_g, k_norm_g, rel_bias, ffn_w_gate, ffn_w_up,
           ffn_w_down, moe_w_router, moe_b_router, moe_w_gate, moe_w_up, moe_w_down):
    b, s, d = x.shape
    depth = w_ada.shape[0]
    tm = min(1024, s)
    mod = _ada_mod(c, w_ada, b_ada)
    for l in range(depth):
        sh1, sc1, g1, sh2, sc2, g2 = [
            mod[l, :, j * d:(j + 1) * d].reshape(b, 1, d) for j in range(6)]
        z, qn, kn, vt = _mix_in(x, sc1, sh1, norm_mix_g[l], w_in, l,
                                q_norm_g[l], k_norm_g[l], tm)
        yc = _conv_branch(z, conv_w[l], conv_b[l], conv_ln_g[l], conv_ln_b[l], tm)
        ya = _attention(qn, kn, vt, rel_bias[l])
        i = l // 2
        if l % 2 == 0:
            x = _mix_out_dense_ffn(x, yc, ya, w_out, l, g1, norm_ffn_g[l], sc2, sh2, g2,
                                   ffn_w_gate[i], ffn_w_up[i], ffn_w_down[i],
                                   min(1024, s), 256)
        else:
            x1, hp, info, info_t, cnt = _mix_out_routed(
                x, yc, ya, w_out, l, g1, norm_ffn_g[l], sc2, sh2,
                moe_w_router[i], moe_b_router[i], tm)
            x = _moe_layer(hp, info, info_t, cnt, x1, g2,
                           moe_w_gate[i], moe_w_up[i], moe_w_down[i], tm)
    return x
```

```python
import functools

import jax
import jax.numpy as jnp
from jax import lax
from jax.experimental import pallas as pl
from jax.experimental.pallas import tpu as pltpu
from jax.experimental.pallas import tpu_sc as plsc

F32 = jnp.float32
BF16 = jnp.bfloat16

D_MODEL = 1024
CHUNK = 64
N_PREV_CHUNKS = 8
BAND_PAD = N_PREV_CHUNKS * CHUNK
D_CONV = 512
D_ATT = 512
HEAD_DIM = 64
N_HEADS = 8
CONV_WIDTH = 31
MAX_REL = 128
D_IN_COLS = 2 * D_CONV + 3 * D_ATT
N_EXPERTS = 8
EPS = 1e-6
NEG_INF = -1e30

LANES = 128
SUBLANES = 8
VMEM_LIMIT_BYTES = 56 * 1024 * 1024

ATT_HEADS = 4
ATT_GROUPS = N_HEADS // ATT_HEADS
ATT_W = ATT_HEADS * HEAD_DIM
ATT_Q = 2 * CHUNK
ATT_BAND = BAND_PAD + ATT_Q
ATT_L = ATT_HEADS * ATT_Q

CONV_HALO = 32
CONV_ROWS = 32
LANE_PAD_E = LANES

MOE_HALF = 256
MOE_SUB = 4 * MOE_HALF
MOE_TAILS = (2 * MOE_HALF, MOE_HALF)
MOE_TILE = 9 * MOE_HALF
MOE_FC = 512


def _cparams(n_axes, vmem=VMEM_LIMIT_BYTES):
    return pltpu.CompilerParams(
        dimension_semantics=("arbitrary",) * n_axes, vmem_limit_bytes=vmem)


def _silu(v):
    return v * jax.nn.sigmoid(v)


def _pack_bf16_pairs(v):
    w = v.shape[1] // 2
    bits = lax.bitcast_convert_type(v.astype(BF16).astype(F32), jnp.uint32)
    packed = (bits[:, w:] & jnp.uint32(0xFFFF0000)) | (bits[:, :w] >> 16)
    return lax.bitcast_convert_type(packed, jnp.int32)


def _unpack_bf16_pairs(p):
    bits = lax.bitcast_convert_type(p, jnp.uint32)
    lo = lax.bitcast_convert_type(bits << 16, F32)
    hi = lax.bitcast_convert_type(bits & jnp.uint32(0xFFFF0000), F32)
    return jnp.concatenate([lo, hi], axis=1)


SC_CORES = 2
SC_SUBCORES = 16
SC_WORKERS = SC_CORES * SC_SUBCORES
SC_CHUNK = 64


def _sc_worker_id():
    return lax.axis_index("s") * SC_CORES + lax.axis_index("c")


def _sc_mesh():
    return plsc.VectorSubcoreMesh(core_axis_name="c", subcore_axis_name="s")


def _sc_gather_rows(table, idx):
    _, w = table.shape
    b = idx.shape[0]
    per_w = b // SC_WORKERS
    nch = per_w // SC_CHUNK

    def body(table_hbm, idx_hbm, out_hbm, idx_v, rows_v, gsem, wsem):
        wid = _sc_worker_id()
        base = wid * per_w
        pltpu.sync_copy(idx_hbm.at[wid], idx_v)
        gathers = [None] * nch
        writes = [None] * nch
        gathers[0] = pltpu.async_copy(table_hbm.at[idx_v.at[0]], rows_v.at[0], gsem.at[0])
        for c in range(nch):
            slot = c % 2
            gathers[c].wait()
            if c + 1 < nch:
                if c >= 1:
                    writes[c - 1].wait()
                gathers[c + 1] = pltpu.async_copy(
                    table_hbm.at[idx_v.at[c + 1]], rows_v.at[1 - slot], gsem.at[1 - slot])
            writes[c] = pltpu.async_copy(
                rows_v.at[slot], out_hbm.at[pl.ds(base + c * SC_CHUNK, SC_CHUNK)], wsem.at[slot])
        if nch >= 2:
            writes[nch - 2].wait()
        writes[nch - 1].wait()

    call = pl.kernel(
        body, mesh=_sc_mesh(),
        out_type=jax.ShapeDtypeStruct((b, w), jnp.int32),
        scratch_types=[pltpu.VMEM((nch, SC_CHUNK), jnp.int32),
                       pltpu.VMEM((2, SC_CHUNK, w), jnp.int32),
                       pltpu.SemaphoreType.DMA((2,)), pltpu.SemaphoreType.DMA((2,))],
        name="sc_gather_rows")
    return call(table, idx.reshape(SC_WORKERS, nch, SC_CHUNK))


def _sc_scatter_rows2(src, idx0, idx1, rows_out):
    n, w = src.shape
    per_w = n // SC_WORKERS
    nch = per_w // SC_CHUNK

    def body(src_hbm, i0_hbm, i1_hbm, out_hbm, i0_v, i1_v, rows_v, rsem, wsem):
        wid = _sc_worker_id()
        base = wid * per_w
        pltpu.sync_copy(i0_hbm.at[wid], i0_v)
        pltpu.sync_copy(i1_hbm.at[wid], i1_v)
        reads = [None] * nch
        writes = [None] * nch
        reads[0] = pltpu.async_copy(src_hbm.at[pl.ds(base, SC_CHUNK)], rows_v.at[0], rsem.at[0])
        for c in range(nch):
            slot = c % 2
            reads[c].wait()
            if c + 1 < nch:
                if c >= 1:
                    for wr in writes[c - 1]:
                        wr.wait()
                reads[c + 1] = pltpu.async_copy(
                    src_hbm.at[pl.ds(base + (c + 1) * SC_CHUNK, SC_CHUNK)],
                    rows_v.at[1 - slot], rsem.at[1 - slot])
            writes[c] = (
                pltpu.async_copy(rows_v.at[slot], out_hbm.at[i0_v.at[c]], wsem.at[slot, 0]),
                pltpu.async_copy(rows_v.at[slot], out_hbm.at[i1_v.at[c]], wsem.at[slot, 1]),
            )
        for c in range(max(nch - 2, 0), nch):
            for wr in writes[c]:
                wr.wait()

    call = pl.kernel(
        body, mesh=_sc_mesh(),
        out_type=jax.ShapeDtypeStruct((rows_out, w), jnp.int32),
        scratch_types=[pltpu.VMEM((nch, SC_CHUNK), jnp.int32),
                       pltpu.VMEM((nch, SC_CHUNK), jnp.int32),
                       pltpu.VMEM((2, SC_CHUNK, w), jnp.int32),
                       pltpu.SemaphoreType.DMA((2,)), pltpu.SemaphoreType.DMA((2, 2))],
        name="sc_scatter_rows")
    shape3 = (SC_WORKERS, nch, SC_CHUNK)
    return call(src, idx0.reshape(shape3), idx1.reshape(shape3))


def _ada_kernel(c_ref, w_ref, b_ref, o_ref):
    ca = _silu(c_ref[...]).astype(BF16)
    w = w_ref[0].astype(BF16)
    o_ref[0] = jnp.dot(ca, w, preferred_element_type=F32) + b_ref[0]


def _ada_mod(c, w_ada, b_ada):
    depth, d, n6 = w_ada.shape
    b = c.shape[0]
    rows = 16
    c_pad = jnp.zeros((rows, d), F32).at[:b].set(c)
    tn = 1536
    out = pl.pallas_call(
        _ada_kernel,
        out_shape=jax.ShapeDtypeStruct((depth, rows, n6), F32),
        grid=(depth, n6 // tn),
        in_specs=[
            pl.BlockSpec((rows, d), lambda l, j: (0, 0)),
            pl.BlockSpec((1, d, tn), lambda l, j: (l, 0, j)),
            pl.BlockSpec((1, 1, tn), lambda l, j: (l, 0, j)),
        ],
        out_specs=pl.BlockSpec((1, rows, tn), lambda l, j: (l, 0, j)),
        compiler_params=_cparams(2),
        name="ada_mod",
    )(c_pad, w_ada, b_ada.reshape(depth, 1, n6))
    return out[:, :b]


def _rms_mod(xf, g, sc, sh):
    ms = jnp.mean(xf * xf, axis=-1, keepdims=True)
    return xf * lax.rsqrt(ms + EPS) * g * (1.0 + sc) + sh


def _mix_in_kernel(x_ref, sc_ref, sh_ref, g_ref, w_ref, gq_ref, gk_ref, ones_ref,
                   z_ref, q_ref, k_ref, vt_ref, wbf_ref):
    first = jnp.logical_and(pl.program_id(0) == 0, pl.program_id(1) == 0)

    @pl.when(first)
    def _():
        wbf_ref[...] = w_ref[...].astype(BF16)

    h = _rms_mod(x_ref[0], g_ref[...], sc_ref[0], sh_ref[0]).astype(BF16)
    proj = jnp.dot(h, wbf_ref[...], preferred_element_type=F32)

    a = proj[:, :D_CONV]
    gate = proj[:, D_CONV:2 * D_CONV]
    z_ref[0] = (a * jax.nn.sigmoid(gate)).astype(BF16)

    def head_norm(t, g):
        sq = (t * t).astype(BF16)
        ss = jnp.concatenate(
            [jnp.dot(sq[:, c:c + ATT_W], ones_ref[...], preferred_element_type=F32)
             for c in range(0, D_ATT, ATT_W)], axis=1)
        return (t * lax.rsqrt(ss * (1.0 / HEAD_DIM) + EPS) * g).astype(BF16)

    o = 2 * D_CONV
    q_ref[0] = head_norm(proj[:, o:o + D_ATT], gq_ref[...])
    k_ref[0] = head_norm(proj[:, o + D_ATT:o + 2 * D_ATT], gk_ref[...])
    v = proj[:, o + 2 * D_ATT:]
    tm = v.shape[0]
    for cidx in range(tm // ATT_Q):
        vt_ref[0, cidx] = v[cidx * ATT_Q:(cidx + 1) * ATT_Q, :].T.astype(BF16)


def _mix_in(x, sc, sh, g, w_in, layer, gq, gk, tm):
    b, s, d = x.shape
    ones_bd = (jnp.arange(ATT_W)[:, None] // HEAD_DIM
               == jnp.arange(ATT_W)[None, :] // HEAD_DIM).astype(BF16)
    gq_t = (jnp.tile(gq, N_HEADS) * (HEAD_DIM ** -0.5)).reshape(1, D_ATT)
    gk_t = jnp.tile(gk, N_HEADS).reshape(1, D_ATT)
    row = lambda bi, ti: (bi, ti, 0)
    per_b = lambda bi, ti: (bi, 0, 0)
    const2 = lambda bi, ti: (0, 0)
    return pl.pallas_call(
        _mix_in_kernel,
        out_shape=(
            jax.ShapeDtypeStruct((b, s, D_CONV), BF16),
            jax.ShapeDtypeStruct((b, s, D_ATT), BF16),
            jax.ShapeDtypeStruct((b, s, D_ATT), BF16),
            jax.ShapeDtypeStruct((b, s // ATT_Q, D_ATT, ATT_Q), BF16),
        ),
        grid=(b, s // tm),
        in_specs=[
            pl.BlockSpec((1, tm, d), row),
            pl.BlockSpec((1, 1, d), per_b),
            pl.BlockSpec((1, 1, d), per_b),
            pl.BlockSpec((1, d), const2),
            pl.BlockSpec((None, d, D_IN_COLS), lambda bi, ti: (layer, 0, 0),
                         pipeline_mode=pl.Buffered(1)),
            pl.BlockSpec((1, D_ATT), const2),
            pl.BlockSpec((1, D_ATT), const2),
            pl.BlockSpec((ATT_W, ATT_W), const2),
        ],
        out_specs=(
            pl.BlockSpec((1, tm, D_CONV), row),
            pl.BlockSpec((1, tm, D_ATT), row),
            pl.BlockSpec((1, tm, D_ATT), row),
            pl.BlockSpec((1, tm // ATT_Q, D_ATT, ATT_Q), lambda bi, ti: (bi, ti, 0, 0)),
        ),
        scratch_shapes=[pltpu.VMEM((d, D_IN_COLS), BF16)],
        compiler_params=_cparams(2),
        name="mix_in",
    )(x, sc, sh, g.reshape(1, d), w_in, gq_t, gk_t, ones_bd)


def _conv_kernel(zc_ref, zp_ref, w_ref, cb_ref, lg_ref, lb_ref, o_ref, win_ref, sh_ref,
                 acc_ref):
    tt = zc_ref.shape[1]
    t = pl.program_id(1)
    halo = zp_ref[0].astype(F32)
    win_ref[0:CONV_HALO, :] = jnp.where(t == 0, 0.0, halo)
    win_ref[CONV_HALO:, :] = zc_ref[0].astype(F32)
    span = tt + CONV_HALO - SUBLANES
    for sft in range(1, SUBLANES):
        sh_ref[sft - 1, 0:span, :] = win_ref[sft:sft + span, :]
    base = CONV_HALO - (CONV_WIDTH - 1)
    tiles = CONV_ROWS // SUBLANES

    def group(gidx, carry):
        r = pl.multiple_of(gidx * CONV_ROWS, CONV_ROWS)
        acc = jnp.zeros((tiles, SUBLANES, D_CONV), F32) + cb_ref[...]
        for j in range(CONV_WIDTH):
            whole, sft = divmod(base + j, SUBLANES)
            start = pl.multiple_of(r + whole * SUBLANES, SUBLANES)
            if sft == 0:
                tap = win_ref[pl.ds(start, CONV_ROWS), :]
            else:
                tap = sh_ref[sft - 1, pl.ds(start, CONV_ROWS), :]
            acc = acc + tap.reshape(tiles, SUBLANES, D_CONV) * w_ref[j]
        acc_ref[pl.ds(r, CONV_ROWS), :] = acc.reshape(CONV_ROWS, D_CONV)
        return carry

    lax.fori_loop(0, tt // CONV_ROWS, group, 0)
    acc = acc_ref[...]
    mu = jnp.mean(acc, axis=-1, keepdims=True)
    xc = acc - mu
    var = jnp.mean(xc * xc, axis=-1, keepdims=True)
    y = xc * lax.rsqrt(var + EPS) * lg_ref[...] + lb_ref[...]
    o_ref[0] = _silu(y).astype(BF16)


def _conv_branch(z, conv_w, conv_b, ln_g, ln_b, tt):
    b, s, c = z.shape
    hb = tt // CONV_HALO
    w_tiles = jnp.broadcast_to(conv_w.reshape(CONV_WIDTH, 1, c), (CONV_WIDTH, SUBLANES, c))
    const2 = lambda bi, ti: (0, 0)
    return pl.pallas_call(
        _conv_kernel,
        out_shape=jax.ShapeDtypeStruct((b, s, c), BF16),
        grid=(b, s // tt),
        in_specs=[
            pl.BlockSpec((1, tt, c), lambda bi, ti: (bi, ti, 0)),
            pl.BlockSpec((1, CONV_HALO, c),
                         lambda bi, ti: (bi, jnp.maximum(ti * hb - 1, 0), 0)),
            pl.BlockSpec((CONV_WIDTH, SUBLANES, c), lambda bi, ti: (0, 0, 0)),
            pl.BlockSpec((1, c), const2),
            pl.BlockSpec((1, c), const2),
            pl.BlockSpec((1, c), const2),
        ],
        out_specs=pl.BlockSpec((1, tt, c), lambda bi, ti: (bi, ti, 0)),
        scratch_shapes=[pltpu.VMEM((tt + CONV_HALO, c), F32),
                        pltpu.VMEM((SUBLANES - 1, tt + CONV_HALO, c), F32),
                        pltpu.VMEM((tt, c), F32)],
        compiler_params=_cparams(2),
        name="conv_branch",
    )(z, z, w_tiles, conv_b.reshape(1, c),
      ln_g.reshape(1, c), ln_b.reshape(1, c))


def _attn_kernel(q_ref, k_ref, vt_ref, bias_ref, o_ref, kpad_ref, vtpad_ref,
                 st0_ref, st1_ref, pb0_ref, pb1_ref, den0_ref, den1_ref):
    st_refs = (st0_ref, st1_ref)
    pb_refs = (pb0_ref, pb1_ref)
    den_refs = (den0_ref, den1_ref)
    s = q_ref.shape[1]
    npad = BAND_PAD // ATT_Q
    kpad_ref[0:BAND_PAD, :] = jnp.zeros((BAND_PAD, ATT_W), BF16)
    kpad_ref[BAND_PAD:, :] = k_ref[0]
    vtpad_ref[0:npad] = jnp.zeros((npad, ATT_W, ATT_Q), BF16)
    vtpad_ref[npad:] = vt_ref[0]

    iota = lambda shape, dim: lax.broadcasted_iota(jnp.int32, shape, dim)
    q_shift = ATT_Q.bit_length() - 1
    d_shift = HEAD_DIM.bit_length() - 1
    qb_mask = (iota((ATT_L, ATT_W), 0) >> q_shift) == (iota((ATT_L, ATT_W), 1) >> d_shift)
    ot_mask = (iota((ATT_W, ATT_L), 0) >> d_shift) == (iota((ATT_W, ATT_L), 1) >> q_shift)
    sel = jnp.where((iota((ATT_Q, ATT_L), 1) & (ATT_Q - 1)) == iota((ATT_Q, ATT_L), 0),
                    1.0, 0.0).astype(BF16)
    key_row = lax.broadcasted_iota(jnp.int32, (ATT_BAND, ATT_L), 0)
    contract_last = (((1,), (1,)), ((), ()))

    def scores(m, p):
        r0 = m * ATT_Q
        qt = q_ref[0, pl.ds(r0, ATT_Q), :]
        qb = jnp.where(qb_mask, jnp.concatenate([qt] * ATT_HEADS, axis=0), 0)
        kb = kpad_ref[pl.ds(r0, ATT_BAND), :]
        st_refs[p][...] = lax.dot_general(kb, qb.astype(BF16), contract_last,
                                          preferred_element_type=F32)

    def softmax(m, p, masked):
        st = st_refs[p][...] + bias_ref[0]
        if masked:
            st = jnp.where(key_row >= BAND_PAD - m * ATT_Q, st, NEG_INF)
        mx = jnp.max(st, axis=0, keepdims=True)
        e = jnp.exp(st - mx)
        den_refs[p][...] = jnp.sum(e, axis=0, keepdims=True)
        pb_refs[p][...] = e.astype(BF16)

    def values(m, p):
        r0 = m * ATT_Q
        vb = jnp.concatenate([vtpad_ref[m + c] for c in range(ATT_BAND // ATT_Q)],
                             axis=1)
        ot = jnp.dot(vb, pb_refs[p][...], preferred_element_type=F32)
        ot = jnp.where(ot_mask, ot / den_refs[p][...], 0.0).astype(BF16)
        y = lax.dot_general(sel, ot, contract_last, preferred_element_type=F32)
        o_ref[0, pl.ds(r0, ATT_Q), :] = y.astype(BF16)

    n = s // ATT_Q
    n_masked = BAND_PAD // ATT_Q
    for m in range(n + 2):
        if m < n:
            scores(m, m % 2)
        if 1 <= m <= n:
            softmax(m - 1, (m - 1) % 2, m - 1 < n_masked)
        if m >= 2:
            values(m - 2, m % 2)


def _attn_bias_t(rel_bias):
    rb = rel_bias.astype(F32)
    nu = ATT_BAND + ATT_Q - 1
    n_low = BAND_PAD - MAX_REL + ATT_Q
    t = jnp.concatenate([jnp.repeat(rb[:, :1], n_low, axis=1),
                         rb[:, 1:1 + nu - n_low]], axis=1)
    tp = jnp.pad(t, ((0, 0), (0, 1)))
    skew = jnp.tile(tp, (1, ATT_Q))[:, :ATT_Q * nu].reshape(N_HEADS, ATT_Q, nu)
    bias = skew[:, :, ATT_Q - 1:]
    bias = bias.reshape(ATT_GROUPS, ATT_HEADS, ATT_Q, ATT_BAND)
    bias = bias.transpose(0, 3, 1, 2).reshape(ATT_GROUPS, ATT_BAND, ATT_L)
    r = jnp.arange(ATT_BAND)[:, None]
    qq = jnp.arange(ATT_L)[None, :] % ATT_Q
    first = (qq // CHUNK) * CHUNK
    valid = (r >= first) & (r < first + BAND_PAD + CHUNK)
    return jnp.where(valid[None], bias, NEG_INF)


def _attention(qn, kn, vt, rel_bias):
    b, s, _ = qn.shape
    bias_t = _attn_bias_t(rel_bias)
    nck = s // ATT_Q
    assert nck >= 4 and nck % 2 == 0, "the attention pipeline runs query steps in pairs"
    return pl.pallas_call(
        _attn_kernel,
        out_shape=jax.ShapeDtypeStruct((b, s, D_ATT), BF16),
        grid=(b, ATT_GROUPS),
        in_specs=[
            pl.BlockSpec((1, s, ATT_W), lambda bi, gi: (bi, 0, gi)),
            pl.BlockSpec((1, s, ATT_W), lambda bi, gi: (bi, 0, gi)),
            pl.BlockSpec((1, nck, ATT_W, ATT_Q), lambda bi, gi: (bi, 0, gi, 0)),
            pl.BlockSpec((1, ATT_BAND, ATT_L), lambda bi, gi: (gi, 0, 0)),
        ],
        out_specs=pl.BlockSpec((1, s, ATT_W), lambda bi, gi: (bi, 0, gi)),
        scratch_shapes=[
            pltpu.VMEM((s + BAND_PAD, ATT_W), BF16),
            pltpu.VMEM((nck + BAND_PAD // ATT_Q, ATT_W, ATT_Q), BF16),
            pltpu.VMEM((ATT_BAND, ATT_L), F32), pltpu.VMEM((ATT_BAND, ATT_L), F32),
            pltpu.VMEM((ATT_BAND, ATT_L), BF16), pltpu.VMEM((ATT_BAND, ATT_L), BF16),
            pltpu.VMEM((1, ATT_L), F32), pltpu.VMEM((1, ATT_L), F32),
        ],
        compiler_params=_cparams(2),
        name="band_attention",
    )(qn, kn, vt, bias_t)


def _mix_out_kernel(x_ref, yc_ref, ya_ref, w_ref, g1_ref, gf_ref, sc_ref, sh_ref,
                    wr_ref, br_ref, x1_ref, h_ref, info_ref, info_t_ref, cnt_ref,
                    wbf_ref, carry_ref):
    first = jnp.logical_and(pl.program_id(0) == 0, pl.program_id(1) == 0)

    @pl.when(first)
    def _():
        wbf_ref[...] = w_ref[...].astype(BF16)
        carry_ref[...] = jnp.zeros_like(carry_ref)

    y = jnp.dot(yc_ref[0], wbf_ref[0:D_CONV, :], preferred_element_type=F32)
    y = y + jnp.dot(ya_ref[0], wbf_ref[D_CONV:, :], preferred_element_type=F32)
    x1 = x_ref[0] + g1_ref[0] * y
    x1_ref[0] = x1
    h = _rms_mod(x1, gf_ref[...], sc_ref[0], sh_ref[0])
    h_ref[0] = _pack_bf16_pairs(h)
    logits = jnp.dot(h.astype(BF16), wr_ref[...], preferred_element_type=F32) + br_ref[...]
    _route_tile(logits, info_ref, info_t_ref, cnt_ref, carry_ref)


def _mix_out_routed(x, yc, ya, w_out, layer, g1, gf, sc, sh, w_router, b_router, tm):
    b, s, d = x.shape
    nt = s // tm
    row = lambda bi, ti: (bi, ti, 0)
    per_b = lambda bi, ti: (bi, 0, 0)
    const2 = lambda bi, ti: (0, 0)
    w_pad = jnp.zeros((d, LANE_PAD_E), BF16).at[:, :N_EXPERTS].set(w_router.astype(BF16))
    b_pad = jnp.full((1, LANE_PAD_E), -jnp.inf, F32).at[0, :N_EXPERTS].set(
        b_router.astype(F32))
    return pl.pallas_call(
        _mix_out_kernel,
        out_shape=(jax.ShapeDtypeStruct((b, s, d), F32),
                   jax.ShapeDtypeStruct((b, s, d // 2), jnp.int32),
                   jax.ShapeDtypeStruct((b, s, LANE_PAD_E), F32),
                   jax.ShapeDtypeStruct((SUBLANES, b * s), F32),
                   jax.ShapeDtypeStruct((SUBLANES, LANE_PAD_E), F32)),
        grid=(b, nt),
        in_specs=[
            pl.BlockSpec((1, tm, d), row),
            pl.BlockSpec((1, tm, D_CONV), row),
            pl.BlockSpec((1, tm, D_ATT), row),
            pl.BlockSpec((None, d, d), lambda bi, ti: (layer, 0, 0),
                         pipeline_mode=pl.Buffered(1)),
            pl.BlockSpec((1, 1, d), per_b),
            pl.BlockSpec((1, d), const2),
            pl.BlockSpec((1, 1, d), per_b),
            pl.BlockSpec((1, 1, d), per_b),
            pl.BlockSpec((d, LANE_PAD_E), const2),
            pl.BlockSpec((1, LANE_PAD_E), const2),
        ],
        out_specs=(pl.BlockSpec((1, tm, d), row), pl.BlockSpec((1, tm, d // 2), row),
                   pl.BlockSpec((1, tm, LANE_PAD_E), row),
                   pl.BlockSpec((SUBLANES, tm), lambda bi, ti: (0, bi * nt + ti)),
                   pl.BlockSpec((SUBLANES, LANE_PAD_E), const2)),
        scratch_shapes=[pltpu.VMEM((d, d), BF16), pltpu.VMEM((SUBLANES, LANE_PAD_E), F32)],
        compiler_params=_cparams(2),
        name="mix_out",
    )(x, yc, ya, w_out, g1, gf.reshape(1, d), sc, sh, w_pad, b_pad)


def _cast_ffn_kernel(wg_ref, wu_ref, wd_ref, wo_ref, wg3_ref, wu3_ref, wd3_ref, wob_ref):
    wg3_ref[0] = wg_ref[...].astype(BF16)
    wu3_ref[0] = wu_ref[...].astype(BF16)
    wd3_ref[0] = wd_ref[...].astype(BF16)

    @pl.when(pl.program_id(0) == 0)
    def _():
        wob_ref[...] = wo_ref[...].astype(BF16)


def _cast_ffn_weights(wg, wu, wd, w_out, layer, fc):
    d, ff = wg.shape
    nf = ff // fc
    return pl.pallas_call(
        _cast_ffn_kernel,
        out_shape=(jax.ShapeDtypeStruct((nf, d, fc), BF16),
                   jax.ShapeDtypeStruct((nf, d, fc), BF16),
                   jax.ShapeDtypeStruct((nf, fc, d), BF16),
                   jax.ShapeDtypeStruct((d, d), BF16)),
        grid=(nf,),
        in_specs=[
            pl.BlockSpec((d, fc), lambda f: (0, f)),
            pl.BlockSpec((d, fc), lambda f: (0, f)),
            pl.BlockSpec((fc, d), lambda f: (f, 0)),
            pl.BlockSpec((None, d, d), lambda f: (layer, 0, 0)),
        ],
        out_specs=(pl.BlockSpec((1, d, fc), lambda f: (f, 0, 0)),
                   pl.BlockSpec((1, d, fc), lambda f: (f, 0, 0)),
                   pl.BlockSpec((1, fc, d), lambda f: (f, 0, 0)),
                   pl.BlockSpec((d, d), lambda f: (0, 0))),
        compiler_params=_cparams(1),
        name="cast_ffn_weights",
    )(wg, wu, wd, w_out)


def _mix_ffn_kernel(x_ref, yc_ref, ya_ref, wo_ref, g1_ref, gf_ref, sc_ref, sh_ref, g2_ref,
                    wg_ref, wu_ref, wd_ref, o_ref, acc_ref):
    y = jnp.dot(yc_ref[0], wo_ref[0:D_CONV, :], preferred_element_type=F32)
    y = y + jnp.dot(ya_ref[0], wo_ref[D_CONV:, :], preferred_element_type=F32)
    x1 = x_ref[0] + g1_ref[0] * y
    h = _rms_mod(x1, gf_ref[...], sc_ref[0], sh_ref[0]).astype(BF16)
    for c in range(wg_ref.shape[0]):
        a = jnp.dot(h, wg_ref[c], preferred_element_type=F32)
        u = jnp.dot(h, wu_ref[c], preferred_element_type=F32)
        t = (_silu(a) * u).astype(BF16)
        dn = jnp.dot(t, wd_ref[c], preferred_element_type=F32)
        if c == 0:
            acc_ref[...] = dn
        else:
            acc_ref[...] += dn
    o_ref[0] = x1 + g2_ref[0] * acc_ref[...]


def _mix_out_dense_ffn(x, yc, ya, w_out, layer, g1, gf, sc, sh, g2, wg, wu, wd, tm, fc):
    b, s, d = x.shape
    wg3, wu3, wd3, wob = _cast_ffn_weights(wg, wu, wd, w_out, layer, fc)
    nf = wg3.shape[0]
    row = lambda bi, ti: (bi, ti, 0)
    per_b = lambda bi, ti: (bi, 0, 0)
    const2 = lambda bi, ti: (0, 0)
    const3 = lambda bi, ti: (0, 0, 0)
    resident = pl.Buffered(1)
    return pl.pallas_call(
        _mix_ffn_kernel,
        out_shape=jax.ShapeDtypeStruct((b, s, d), F32),
        grid=(b, s // tm),
        in_specs=[
            pl.BlockSpec((1, tm, d), row),
            pl.BlockSpec((1, tm, D_CONV), row),
            pl.BlockSpec((1, tm, D_ATT), row),
            pl.BlockSpec((d, d), const2, pipeline_mode=resident),
            pl.BlockSpec((1, 1, d), per_b),
            pl.BlockSpec((1, d), const2),
            pl.BlockSpec((1, 1, d), per_b),
            pl.BlockSpec((1, 1, d), per_b),
            pl.BlockSpec((1, 1, d), per_b),
            pl.BlockSpec((nf, d, fc), const3, pipeline_mode=resident),
            pl.BlockSpec((nf, d, fc), const3, pipeline_mode=resident),
            pl.BlockSpec((nf, fc, d), const3, pipeline_mode=resident),
        ],
        out_specs=pl.BlockSpec((1, tm, d), row),
        scratch_shapes=[pltpu.VMEM((tm, d), F32)],
        compiler_params=_cparams(2),
        name="mix_out_dense_ffn",
    )(x, yc, ya, wob, g1, gf.reshape(1, d), sc, sh, g2, wg3, wu3, wd3)


def _route_tile(logits, info_ref, info_t_ref, cnt_ref, carry_ref):
    tr = logits.shape[0]
    lane = lax.broadcasted_iota(jnp.int32, (tr, LANE_PAD_E), 1).astype(F32)
    no_lane = float(LANE_PAD_E)
    v0 = jnp.max(logits, axis=-1, keepdims=True)
    i0 = jnp.min(jnp.where(logits == v0, lane, no_lane), axis=-1, keepdims=True)
    rest = jnp.where(lane == i0, -jnp.inf, logits)
    v1 = jnp.max(rest, axis=-1, keepdims=True)
    i1 = jnp.min(jnp.where(rest == v1, lane, no_lane), axis=-1, keepdims=True)
    e1 = jnp.exp(v1 - v0)
    w0 = 1.0 / (1.0 + e1)
    w1 = e1 / (1.0 + e1)
    oh0 = lane == i0
    oh1 = lane == i1
    cnt = jnp.where(jnp.logical_or(oh0, oh1), 1.0, 0.0)
    tri = (lax.broadcasted_iota(jnp.int32, (tr, tr), 1)
           < lax.broadcasted_iota(jnp.int32, (tr, tr), 0)).astype(BF16)
    before = jnp.dot(tri, cnt.astype(BF16), preferred_element_type=F32) + carry_ref[0:1, :]
    r0 = jnp.sum(jnp.where(oh0, before, 0.0), axis=-1, keepdims=True)
    r1 = jnp.sum(jnp.where(oh1, before, 0.0), axis=-1, keepdims=True)
    carry_ref[...] = carry_ref[...] + jnp.sum(cnt, axis=0, keepdims=True)
    cnt_ref[...] = carry_ref[...]
    info = jnp.where(lane == 0, i0, 0.0)
    info = jnp.where(lane == 1, i1, info)
    info = jnp.where(lane == 2, w0, info)
    info = jnp.where(lane == 3, w1, info)
    info = jnp.where(lane == 4, r0, info)
    info = jnp.where(lane == 5, r1, info)
    info_ref[0] = info
    for c in range(tr // LANES):
        blk = info[c * LANES:(c + 1) * LANES, :].T
        info_t_ref[:, c * LANES:(c + 1) * LANES] = blk[0:SUBLANES, :]


def _moe_kernel(te_ref, tn_ref, xs_ref, wg_ref, wu_ref, wd_ref, ys_ref,
                acc_ref, wgb_ref, wub_ref, wdb_ref):
    i = pl.program_id(0)
    f = pl.program_id(1)
    nhalf = tn_ref[i]
    nfull = nhalf // (MOE_SUB // MOE_HALF)

    @pl.when(f == 0)
    def _():
        acc_ref[...] = jnp.zeros_like(acc_ref)

    @pl.when(nhalf > 0)
    def _():
        wgb_ref[...] = wg_ref[0].astype(BF16)
        wub_ref[...] = wu_ref[0].astype(BF16)
        wdb_ref[...] = wd_ref[0].astype(BF16)

    def block(r, rows):
        xb = _unpack_bf16_pairs(xs_ref[pl.ds(r, rows), :]).astype(BF16)
        a = jnp.dot(xb, wgb_ref[...], preferred_element_type=F32)
        u = jnp.dot(xb, wub_ref[...], preferred_element_type=F32)
        t = (_silu(a) * u).astype(BF16)
        acc_ref[pl.ds(r, rows), :] += jnp.dot(t, wdb_ref[...], preferred_element_type=F32)

    def sub(sidx, carry):
        block(pl.multiple_of(sidx * MOE_SUB, MOE_SUB), MOE_SUB)
        return carry

    lax.fori_loop(0, nfull, sub, 0)
    done = nfull * MOE_SUB
    for rows in MOE_TAILS:
        units = rows // MOE_HALF

        @pl.when((nhalf & units) != 0)
        def _(rows=rows, units=units):
            higher = nhalf & (MOE_SUB // MOE_HALF - 1) & ~(2 * units - 1)
            block(pl.multiple_of(done + higher * MOE_HALF, MOE_HALF), rows)

    @pl.when(f == pl.num_programs(1) - 1)
    def _():
        ys_ref[...] = _pack_bf16_pairs(acc_ref[...])


def _moe_ffn(xs, tile_e, tile_nsub, wg, wu, wd):
    rpad, dw = xs.shape
    d = 2 * dw
    ntiles = rpad // MOE_TILE
    ff = wg.shape[2]
    nf = ff // MOE_FC

    def fcol(i, f, tn):
        return jnp.where(tn[i] > 0, f, nf - 1)

    return pl.pallas_call(
        _moe_kernel,
        out_shape=jax.ShapeDtypeStruct((rpad, dw), jnp.int32),
        grid_spec=pltpu.PrefetchScalarGridSpec(
            num_scalar_prefetch=2,
            grid=(ntiles, nf),
            in_specs=[
                pl.BlockSpec((MOE_TILE, dw), lambda i, f, te, tn: (i, 0)),
                pl.BlockSpec((1, d, MOE_FC), lambda i, f, te, tn: (te[i], 0, fcol(i, f, tn))),
                pl.BlockSpec((1, d, MOE_FC), lambda i, f, te, tn: (te[i], 0, fcol(i, f, tn))),
                pl.BlockSpec((1, MOE_FC, d), lambda i, f, te, tn: (te[i], fcol(i, f, tn), 0)),
            ],
            out_specs=pl.BlockSpec((MOE_TILE, dw), lambda i, f, te, tn: (i, 0)),
            scratch_shapes=[
                pltpu.VMEM((MOE_TILE, d), F32),
                pltpu.VMEM((d, MOE_FC), BF16),
                pltpu.VMEM((d, MOE_FC), BF16),
                pltpu.VMEM((MOE_FC, d), BF16),
            ],
        ),
        compiler_params=_cparams(2),
        name="moe_ffn",
    )(tile_e, tile_nsub, xs, wg, wu, wd)


def _combine_kernel(x_ref, y0_ref, y1_ref, info_ref, g2_ref, o_ref):
    info = info_ref[0]
    w0 = info[:, 2:3]
    w1 = info[:, 3:4]
    f = w0 * _unpack_bf16_pairs(y0_ref[0, 0]) + w1 * _unpack_bf16_pairs(y1_ref[0, 0])
    o_ref[0] = x_ref[0] + g2_ref[0] * f


def _combine(x1, y01, info, g2, tm):
    b, s, d = x1.shape
    row = lambda bi, ti: (bi, ti, 0)
    return pl.pallas_call(
        _combine_kernel,
        out_shape=jax.ShapeDtypeStruct((b, s, d), F32),
        grid=(b, s // tm),
        in_specs=[
            pl.BlockSpec((1, tm, d), row),
            pl.BlockSpec((1, 1, tm, d // 2), lambda bi, ti: (0, bi, ti, 0)),
            pl.BlockSpec((1, 1, tm, d // 2), lambda bi, ti: (1, bi, ti, 0)),
            pl.BlockSpec((1, tm, LANE_PAD_E), row),
            pl.BlockSpec((1, 1, d), lambda bi, ti: (bi, 0, 0)),
        ],
        out_specs=pl.BlockSpec((1, tm, d), row),
        compiler_params=_cparams(2),
        name="moe_combine",
    )(x1, y01, y01, info, g2)


def _moe_layer(hp, info, info_t, cnt, x1, g2, wg, wu, wd, tm):
    b, s, d = x1.shape
    n = b * s
    e0 = info_t[0].astype(jnp.int32)
    e1 = info_t[1].astype(jnp.int32)
    r0 = info_t[4].astype(jnp.int32)
    r1 = info_t[5].astype(jnp.int32)
    counts = cnt[0, :N_EXPERTS].astype(jnp.int32)

    ntiles = (2 * n) // MOE_TILE + N_EXPERTS
    tiles_per_e = (counts + MOE_TILE - 1) // MOE_TILE
    tile_end = jnp.cumsum(tiles_per_e)
    tile_start = tile_end - tiles_per_e
    total = tile_end[-1]
    tidx = jnp.arange(ntiles, dtype=jnp.int32)
    live = tidx < total
    tclip = jnp.minimum(tidx, total - 1)
    tile_e = jnp.minimum(jnp.sum(tclip[:, None] >= tile_end[None, :], axis=1),
                         N_EXPERTS - 1).astype(jnp.int32)
    rows_left = counts[tile_e] - (tclip - tile_start[tile_e]) * MOE_TILE
    rows_here = jnp.clip(rows_left, 0, MOE_TILE)
    tile_nsub = jnp.where(live, (rows_here + MOE_HALF - 1) // MOE_HALF, 0).astype(jnp.int32)

    row_start = tile_start * MOE_TILE
    eid = jnp.arange(N_EXPERTS, dtype=jnp.int32)[None, :]
    pos0 = jnp.sum(jnp.where(e0[:, None] == eid, row_start[None, :], 0), axis=1) + r0
    pos1 = jnp.sum(jnp.where(e1[:, None] == eid, row_start[None, :], 0), axis=1) + r1

    xs = _sc_scatter_rows2(hp.reshape(n, d // 2), pos0, pos1, ntiles * MOE_TILE)
    ys = _moe_ffn(xs, tile_e, tile_nsub, wg, wu, wd)
    y01 = _sc_gather_rows(ys, jnp.concatenate([pos0, pos1]))
    return _combine(x1, y01.reshape(2, b, s, d // 2), info.reshape(b, s, LANE_PAD_E), g2, tm)


def kernel(x, c, w_ada, b_ada, norm_mix_g, norm_ffn_g, w_in, w_out, conv_w, conv_b,
           conv_ln_g, conv_ln_b, q_norm_g, k_norm_g, rel_bias, ffn_w_gate, ffn_w_up,
           ffn_w_down, moe_w_router, moe_b_router, moe_w_gate, moe_w_up, moe_w_down):
    b, s, d = x.shape
    depth = w_ada.shape[0]
    tm = min(1024, s)
    mod = _ada_mod(c, w_ada, b_ada)
    for l in range(depth):
        sh1, sc1, g1, sh2, sc2, g2 = [
            mod[l, :, j * d:(j + 1) * d].reshape(b, 1, d) for j in range(6)]
        z, qn, kn, vt = _mix_in(x, sc1, sh1, norm_mix_g[l], w_in, l,
                                q_norm_g[l], k_norm_g[l], tm)
        yc = _conv_branch(z, conv_w[l], conv_b[l], conv_ln_g[l], conv_ln_b[l], tm)
        ya = _attention(qn, kn, vt, rel_bias[l])
        i = l // 2
        if l % 2 == 0:
            x = _mix_out_dense_ffn(x, yc, ya, w_out, l, g1, norm_ffn_g[l], sc2, sh2, g2,
                                   ffn_w_gate[i], ffn_w_up[i], ffn_w_down[i],
                                   min(1024, s), 256)
        else:
            x1, hp, info, info_t, cnt = _mix_out_routed(
                x, yc, ya, w_out, l, g1, norm_ffn_g[l], sc2, sh2,
                moe_w_router[i], moe_b_router[i], tm)
            x = _moe_layer(hp, info, info_t, cnt, x1, g2,
                           moe_w_gate[i], moe_w_up[i], moe_w_down[i], tm)
    return x
```

```python
import functools

import jax
import jax.numpy as jnp
from jax import lax
from jax.experimental import pallas as pl
from jax.experimental.pallas import tpu as pltpu
from jax.experimental.pallas import tpu_sc as plsc

F32 = jnp.float32
BF16 = jnp.bfloat16

D_MODEL = 1024
CHUNK = 64
N_PREV_CHUNKS = 8
BAND_PAD = N_PREV_CHUNKS * CHUNK
D_CONV = 512
D_ATT = 512
HEAD_DIM = 64
N_HEADS = 8
CONV_WIDTH = 31
MAX_REL = 128
D_IN_COLS = 2 * D_CONV + 3 * D_ATT
N_EXPERTS = 8
EPS = 1e-6
NEG_INF = -1e30

LANES = 128
SUBLANES = 8
VMEM_LIMIT_BYTES = 56 * 1024 * 1024

ATT_HEADS = 4
ATT_GROUPS = N_HEADS // ATT_HEADS
ATT_W = ATT_HEADS * HEAD_DIM
ATT_Q = 2 * CHUNK
ATT_BAND = BAND_PAD + ATT_Q
ATT_L = ATT_HEADS * ATT_Q

CONV_HALO = 32
CONV_ROWS = 32
LANE_PAD_E = LANES

MOE_HALF = 256
MOE_SUB = 4 * MOE_HALF
MOE_TAILS = (2 * MOE_HALF, MOE_HALF)
MOE_TILE = 9 * MOE_HALF
MOE_STATIC_UNITS = (9, 8, 7)
MOE_FC = 512


def _cparams(n_axes, vmem=VMEM_LIMIT_BYTES):
    return pltpu.CompilerParams(
        dimension_semantics=("arbitrary",) * n_axes, vmem_limit_bytes=vmem)


def _silu(v):
    return v * jax.nn.sigmoid(v)


def _pack_bf16_pairs(v):
    w = v.shape[1] // 2
    bits = lax.bitcast_convert_type(v.astype(BF16).astype(F32), jnp.uint32)
    packed = (bits[:, w:] & jnp.uint32(0xFFFF0000)) | (bits[:, :w] >> 16)
    return lax.bitcast_convert_type(packed, jnp.int32)


def _unpack_bf16_pairs(p):
    bits = lax.bitcast_convert_type(p, jnp.uint32)
    lo = lax.bitcast_convert_type(bits << 16, F32)
    hi = lax.bitcast_convert_type(bits & jnp.uint32(0xFFFF0000), F32)
    return jnp.concatenate([lo, hi], axis=1)


SC_CORES = 2
SC_SUBCORES = 16
SC_WORKERS = SC_CORES * SC_SUBCORES
SC_CHUNK = 64


def _sc_worker_id():
    return lax.axis_index("s") * SC_CORES + lax.axis_index("c")


def _sc_mesh():
    return plsc.VectorSubcoreMesh(core_axis_name="c", subcore_axis_name="s")


def _sc_gather_rows(table, idx):
    _, w = table.shape
    b = idx.shape[0]
    per_w = b // SC_WORKERS
    nch = per_w // SC_CHUNK

    def body(table_hbm, idx_hbm, out_hbm, idx_v, rows_v, gsem, wsem):
        wid = _sc_worker_id()
        base = wid * per_w
        pltpu.sync_copy(idx_hbm.at[wid], idx_v)
        gathers = [None] * nch
        writes = [None] * nch
        gathers[0] = pltpu.async_copy(table_hbm.at[idx_v.at[0]], rows_v.at[0], gsem.at[0])
        for c in range(nch):
            slot = c % 2
            gathers[c].wait()
            if c + 1 < nch:
                if c >= 1:
                    writes[c - 1].wait()
                gathers[c + 1] = pltpu.async_copy(
                    table_hbm.at[idx_v.at[c + 1]], rows_v.at[1 - slot], gsem.at[1 - slot])
            writes[c] = pltpu.async_copy(
                rows_v.at[slot], out_hbm.at[pl.ds(base + c * SC_CHUNK, SC_CHUNK)], wsem.at[slot])
        if nch >= 2:
            writes[nch - 2].wait()
        writes[nch - 1].wait()

    call = pl.kernel(
        body, mesh=_sc_mesh(),
        out_type=jax.ShapeDtypeStruct((b, w), jnp.int32),
        scratch_types=[pltpu.VMEM((nch, SC_CHUNK), jnp.int32),
                       pltpu.VMEM((2, SC_CHUNK, w), jnp.int32),
                       pltpu.SemaphoreType.DMA((2,)), pltpu.SemaphoreType.DMA((2,))],
        name="sc_gather_rows")
    return call(table, idx.reshape(SC_WORKERS, nch, SC_CHUNK))


def _sc_scatter_rows2(src, idx0, idx1, rows_out):
    n, w = src.shape
    per_w = n // SC_WORKERS
    nch = per_w // SC_CHUNK

    def body(src_hbm, i0_hbm, i1_hbm, out_hbm, i0_v, i1_v, rows_v, rsem, wsem):
        wid = _sc_worker_id()
        base = wid * per_w
        pltpu.sync_copy(i0_hbm.at[wid], i0_v)
        pltpu.sync_copy(i1_hbm.at[wid], i1_v)
        reads = [None] * nch
        writes = [None] * nch
        reads[0] = pltpu.async_copy(src_hbm.at[pl.ds(base, SC_CHUNK)], rows_v.at[0], rsem.at[0])
        for c in range(nch):
            slot = c % 2
            reads[c].wait()
            if c + 1 < nch:
                if c >= 1:
                    for wr in writes[c - 1]:
                        wr.wait()
                reads[c + 1] = pltpu.async_copy(
                    src_hbm.at[pl.ds(base + (c + 1) * SC_CHUNK, SC_CHUNK)],
                    rows_v.at[1 - slot], rsem.at[1 - slot])
            writes[c] = (
                pltpu.async_copy(rows_v.at[slot], out_hbm.at[i0_v.at[c]], wsem.at[slot, 0]),
                pltpu.async_copy(rows_v.at[slot], out_hbm.at[i1_v.at[c]], wsem.at[slot, 1]),
            )
        for c in range(max(nch - 2, 0), nch):
            for wr in writes[c]:
                wr.wait()

    call = pl.kernel(
        body, mesh=_sc_mesh(),
        out_type=jax.ShapeDtypeStruct((rows_out, w), jnp.int32),
        scratch_types=[pltpu.VMEM((nch, SC_CHUNK), jnp.int32),
                       pltpu.VMEM((nch, SC_CHUNK), jnp.int32),
                       pltpu.VMEM((2, SC_CHUNK, w), jnp.int32),
                       pltpu.SemaphoreType.DMA((2,)), pltpu.SemaphoreType.DMA((2, 2))],
        name="sc_scatter_rows")
    shape3 = (SC_WORKERS, nch, SC_CHUNK)
    return call(src, idx0.reshape(shape3), idx1.reshape(shape3))


def _ada_kernel(c_ref, w_ref, b_ref, o_ref):
    ca = _silu(c_ref[...]).astype(BF16)
    w = w_ref[0].astype(BF16)
    o_ref[0] = jnp.dot(ca, w, preferred_element_type=F32) + b_ref[0]


def _ada_mod(c, w_ada, b_ada):
    depth, d, n6 = w_ada.shape
    b = c.shape[0]
    rows = 16
    c_pad = jnp.zeros((rows, d), F32).at[:b].set(c)
    tn = 1536
    out = pl.pallas_call(
        _ada_kernel,
        out_shape=jax.ShapeDtypeStruct((depth, rows, n6), F32),
        grid=(depth, n6 // tn),
        in_specs=[
            pl.BlockSpec((rows, d), lambda l, j: (0, 0)),
            pl.BlockSpec((1, d, tn), lambda l, j: (l, 0, j)),
            pl.BlockSpec((1, 1, tn), lambda l, j: (l, 0, j)),
        ],
        out_specs=pl.BlockSpec((1, rows, tn), lambda l, j: (l, 0, j)),
        compiler_params=_cparams(2),
        name="ada_mod",
    )(c_pad, w_ada, b_ada.reshape(depth, 1, n6))
    return out[:, :b]


def _rms_mod(xf, g, sc, sh):
    ms = jnp.mean(xf * xf, axis=-1, keepdims=True)
    return xf * lax.rsqrt(ms + EPS) * g * (1.0 + sc) + sh


def _mix_in_kernel(x_ref, sc_ref, sh_ref, g_ref, w_ref, gq_ref, gk_ref, ones_ref,
                   z_ref, q_ref, k_ref, vt_ref, wbf_ref):
    first = jnp.logical_and(pl.program_id(0) == 0, pl.program_id(1) == 0)

    @pl.when(first)
    def _():
        wbf_ref[...] = w_ref[...].astype(BF16)

    h = _rms_mod(x_ref[0], g_ref[...], sc_ref[0], sh_ref[0]).astype(BF16)
    proj = jnp.dot(h, wbf_ref[...], preferred_element_type=F32)

    a = proj[:, :D_CONV]
    gate = proj[:, D_CONV:2 * D_CONV]
    z_ref[0] = (a * jax.nn.sigmoid(gate)).astype(BF16)

    def head_norm(t, g):
        sq = (t * t).astype(BF16)
        ss = jnp.concatenate(
            [jnp.dot(sq[:, c:c + ATT_W], ones_ref[...], preferred_element_type=F32)
             for c in range(0, D_ATT, ATT_W)], axis=1)
        return (t * lax.rsqrt(ss * (1.0 / HEAD_DIM) + EPS) * g).astype(BF16)

    o = 2 * D_CONV
    q_ref[0] = head_norm(proj[:, o:o + D_ATT], gq_ref[...])
    k_ref[0] = head_norm(proj[:, o + D_ATT:o + 2 * D_ATT], gk_ref[...])
    v = proj[:, o + 2 * D_ATT:]
    tm = v.shape[0]
    for cidx in range(tm // ATT_Q):
        vt_ref[0, cidx] = v[cidx * ATT_Q:(cidx + 1) * ATT_Q, :].T.astype(BF16)


def _mix_in(x, sc, sh, g, w_in, layer, gq, gk, tm):
    b, s, d = x.shape
    ones_bd = (jnp.arange(ATT_W)[:, None] // HEAD_DIM
               == jnp.arange(ATT_W)[None, :] // HEAD_DIM).astype(BF16)
    gq_t = (jnp.tile(gq, N_HEADS) * (HEAD_DIM ** -0.5)).reshape(1, D_ATT)
    gk_t = jnp.tile(gk, N_HEADS).reshape(1, D_ATT)
    row = lambda bi, ti: (bi, ti, 0)
    per_b = lambda bi, ti: (bi, 0, 0)
    const2 = lambda bi, ti: (0, 0)
    return pl.pallas_call(
        _mix_in_kernel,
        out_shape=(
            jax.ShapeDtypeStruct((b, s, D_CONV), BF16),
            jax.ShapeDtypeStruct((b, s, D_ATT), BF16),
            jax.ShapeDtypeStruct((b, s, D_ATT), BF16),
            jax.ShapeDtypeStruct((b, s // ATT_Q, D_ATT, ATT_Q), BF16),
        ),
        grid=(b, s // tm),
        in_specs=[
            pl.BlockSpec((1, tm, d), row),
            pl.BlockSpec((1, 1, d), per_b),
            pl.BlockSpec((1, 1, d), per_b),
            pl.BlockSpec((1, d), const2),
            pl.BlockSpec((None, d, D_IN_COLS), lambda bi, ti: (layer, 0, 0),
                         pipeline_mode=pl.Buffered(1)),
            pl.BlockSpec((1, D_ATT), const2),
            pl.BlockSpec((1, D_ATT), const2),
            pl.BlockSpec((ATT_W, ATT_W), const2),
        ],
        out_specs=(
            pl.BlockSpec((1, tm, D_CONV), row),
            pl.BlockSpec((1, tm, D_ATT), row),
            pl.BlockSpec((1, tm, D_ATT), row),
            pl.BlockSpec((1, tm // ATT_Q, D_ATT, ATT_Q), lambda bi, ti: (bi, ti, 0, 0)),
        ),
        scratch_shapes=[pltpu.VMEM((d, D_IN_COLS), BF16)],
        compiler_params=_cparams(2),
        name="mix_in",
    )(x, sc, sh, g.reshape(1, d), w_in, gq_t, gk_t, ones_bd)


def _conv_kernel(zc_ref, zp_ref, w_ref, cb_ref, lg_ref, lb_ref, o_ref, win_ref, sh_ref,
                 acc_ref):
    tt = zc_ref.shape[1]
    t = pl.program_id(1)
    halo = zp_ref[0].astype(F32)
    win_ref[0:CONV_HALO, :] = jnp.where(t == 0, 0.0, halo)
    win_ref[CONV_HALO:, :] = zc_ref[0].astype(F32)
    span = tt + CONV_HALO - SUBLANES
    for sft in range(1, SUBLANES):
        sh_ref[sft - 1, 0:span, :] = win_ref[sft:sft + span, :]
    base = CONV_HALO - (CONV_WIDTH - 1)
    tiles = CONV_ROWS // SUBLANES

    def group(gidx, carry):
        r = pl.multiple_of(gidx * CONV_ROWS, CONV_ROWS)
        acc = jnp.zeros((tiles, SUBLANES, D_CONV), F32) + cb_ref[...]
        for j in range(CONV_WIDTH):
            whole, sft = divmod(base + j, SUBLANES)
            start = pl.multiple_of(r + whole * SUBLANES, SUBLANES)
            if sft == 0:
                tap = win_ref[pl.ds(start, CONV_ROWS), :]
            else:
                tap = sh_ref[sft - 1, pl.ds(start, CONV_ROWS), :]
            acc = acc + tap.reshape(tiles, SUBLANES, D_CONV) * w_ref[j]
        acc_ref[pl.ds(r, CONV_ROWS), :] = acc.reshape(CONV_ROWS, D_CONV)
        return carry

    lax.fori_loop(0, tt // CONV_ROWS, group, 0)
    acc = acc_ref[...]
    mu = jnp.mean(acc, axis=-1, keepdims=True)
    xc = acc - mu
    var = jnp.mean(xc * xc, axis=-1, keepdims=True)
    y = xc * lax.rsqrt(var + EPS) * lg_ref[...] + lb_ref[...]
    o_ref[0] = _silu(y).astype(BF16)


def _conv_branch(z, conv_w, conv_b, ln_g, ln_b, tt):
    b, s, c = z.shape
    hb = tt // CONV_HALO
    w_tiles = jnp.broadcast_to(conv_w.reshape(CONV_WIDTH, 1, c), (CONV_WIDTH, SUBLANES, c))
    const2 = lambda bi, ti: (0, 0)
    return pl.pallas_call(
        _conv_kernel,
        out_shape=jax.ShapeDtypeStruct((b, s, c), BF16),
        grid=(b, s // tt),
        in_specs=[
            pl.BlockSpec((1, tt, c), lambda bi, ti: (bi, ti, 0)),
            pl.BlockSpec((1, CONV_HALO, c),
                         lambda bi, ti: (bi, jnp.maximum(ti * hb - 1, 0), 0)),
            pl.BlockSpec((CONV_WIDTH, SUBLANES, c), lambda bi, ti: (0, 0, 0)),
            pl.BlockSpec((1, c), const2),
            pl.BlockSpec((1, c), const2),
            pl.BlockSpec((1, c), const2),
        ],
        out_specs=pl.BlockSpec((1, tt, c), lambda bi, ti: (bi, ti, 0)),
        scratch_shapes=[pltpu.VMEM((tt + CONV_HALO, c), F32),
                        pltpu.VMEM((SUBLANES - 1, tt + CONV_HALO, c), F32),
                        pltpu.VMEM((tt, c), F32)],
        compiler_params=_cparams(2),
        name="conv_branch",
    )(z, z, w_tiles, conv_b.reshape(1, c),
      ln_g.reshape(1, c), ln_b.reshape(1, c))


def _attn_kernel(q_ref, k_ref, vt_ref, bias_ref, o_ref, kpad_ref, vtpad_ref,
                 st0_ref, st1_ref, pb0_ref, pb1_ref, den0_ref, den1_ref):
    st_refs = (st0_ref, st1_ref)
    pb_refs = (pb0_ref, pb1_ref)
    den_refs = (den0_ref, den1_ref)
    s = q_ref.shape[1]
    npad = BAND_PAD // ATT_Q
    kpad_ref[0:BAND_PAD, :] = jnp.zeros((BAND_PAD, ATT_W), BF16)
    kpad_ref[BAND_PAD:, :] = k_ref[0]
    vtpad_ref[0:npad] = jnp.zeros((npad, ATT_W, ATT_Q), BF16)
    vtpad_ref[npad:] = vt_ref[0]

    iota = lambda shape, dim: lax.broadcasted_iota(jnp.int32, shape, dim)
    q_shift = ATT_Q.bit_length() - 1
    d_shift = HEAD_DIM.bit_length() - 1
    qb_mask = (iota((ATT_L, ATT_W), 0) >> q_shift) == (iota((ATT_L, ATT_W), 1) >> d_shift)
    ot_mask = (iota((ATT_W, ATT_L), 0) >> d_shift) == (iota((ATT_W, ATT_L), 1) >> q_shift)
    sel = jnp.where((iota((ATT_Q, ATT_L), 1) & (ATT_Q - 1)) == iota((ATT_Q, ATT_L), 0),
                    1.0, 0.0).astype(BF16)
    key_row = lax.broadcasted_iota(jnp.int32, (ATT_BAND, ATT_L), 0)
    contract_last = (((1,), (1,)), ((), ()))

    def scores(m, p):
        r0 = m * ATT_Q
        qt = q_ref[0, pl.ds(r0, ATT_Q), :]
        qb = jnp.where(qb_mask, jnp.concatenate([qt] * ATT_HEADS, axis=0), 0)
        kb = kpad_ref[pl.ds(r0, ATT_BAND), :]
        st_refs[p][...] = lax.dot_general(kb, qb.astype(BF16), contract_last,
                                          preferred_element_type=F32)

    def softmax(m, p, masked):
        st = st_refs[p][...] + bias_ref[0]
        if masked:
            st = jnp.where(key_row >= BAND_PAD - m * ATT_Q, st, NEG_INF)
        mx = jnp.max(st, axis=0, keepdims=True)
        e = jnp.exp(st - mx)
        den_refs[p][...] = jnp.sum(e, axis=0, keepdims=True)
        pb_refs[p][...] = e.astype(BF16)

    def values(m, p):
        r0 = m * ATT_Q
        vb = jnp.concatenate([vtpad_ref[m + c] for c in range(ATT_BAND // ATT_Q)],
                             axis=1)
        ot = jnp.dot(vb, pb_refs[p][...], preferred_element_type=F32)
        ot = jnp.where(ot_mask, ot / den_refs[p][...], 0.0).astype(BF16)
        y = lax.dot_general(sel, ot, contract_last, preferred_element_type=F32)
        o_ref[0, pl.ds(r0, ATT_Q), :] = y.astype(BF16)

    n = s // ATT_Q
    n_masked = BAND_PAD // ATT_Q
    for m in range(n + 2):
        if m < n:
            scores(m, m % 2)
        if 1 <= m <= n:
            softmax(m - 1, (m - 1) % 2, m - 1 < n_masked)
        if m >= 2:
            values(m - 2, m % 2)


def _attn_bias_t(rel_bias):
    rb = rel_bias.astype(F32)
    nu = ATT_BAND + ATT_Q - 1
    n_low = BAND_PAD - MAX_REL + ATT_Q
    t = jnp.concatenate([jnp.repeat(rb[:, :1], n_low, axis=1),
                         rb[:, 1:1 + nu - n_low]], axis=1)
    tp = jnp.pad(t, ((0, 0), (0, 1)))
    skew = jnp.tile(tp, (1, ATT_Q))[:, :ATT_Q * nu].reshape(N_HEADS, ATT_Q, nu)
    bias = skew[:, :, ATT_Q - 1:]
    bias = bias.reshape(ATT_GROUPS, ATT_HEADS, ATT_Q, ATT_BAND)
    bias = bias.transpose(0, 3, 1, 2).reshape(ATT_GROUPS, ATT_BAND, ATT_L)
    r = jnp.arange(ATT_BAND)[:, None]
    qq = jnp.arange(ATT_L)[None, :] % ATT_Q
    first = (qq // CHUNK) * CHUNK
    valid = (r >= first) & (r < first + BAND_PAD + CHUNK)
    return jnp.where(valid[None], bias, NEG_INF)


def _attention(qn, kn, vt, rel_bias):
    b, s, _ = qn.shape
    bias_t = _attn_bias_t(rel_bias)
    nck = s // ATT_Q
    assert nck >= 4 and nck % 2 == 0, "the attention pipeline runs query steps in pairs"
    return pl.pallas_call(
        _attn_kernel,
        out_shape=jax.ShapeDtypeStruct((b, s, D_ATT), BF16),
        grid=(b, ATT_GROUPS),
        in_specs=[
            pl.BlockSpec((1, s, ATT_W), lambda bi, gi: (bi, 0, gi)),
            pl.BlockSpec((1, s, ATT_W), lambda bi, gi: (bi, 0, gi)),
            pl.BlockSpec((1, nck, ATT_W, ATT_Q), lambda bi, gi: (bi, 0, gi, 0)),
            pl.BlockSpec((1, ATT_BAND, ATT_L), lambda bi, gi: (gi, 0, 0)),
        ],
        out_specs=pl.BlockSpec((1, s, ATT_W), lambda bi, gi: (bi, 0, gi)),
        scratch_shapes=[
            pltpu.VMEM((s + BAND_PAD, ATT_W), BF16),
            pltpu.VMEM((nck + BAND_PAD // ATT_Q, ATT_W, ATT_Q), BF16),
            pltpu.VMEM((ATT_BAND, ATT_L), F32), pltpu.VMEM((ATT_BAND, ATT_L), F32),
            pltpu.VMEM((ATT_BAND, ATT_L), BF16), pltpu.VMEM((ATT_BAND, ATT_L), BF16),
            pltpu.VMEM((1, ATT_L), F32), pltpu.VMEM((1, ATT_L), F32),
        ],
        compiler_params=_cparams(2),
        name="band_attention",
    )(qn, kn, vt, bias_t)


def _mix_out_kernel(x_ref, yc_ref, ya_ref, w_ref, g1_ref, gf_ref, sc_ref, sh_ref,
                    wr_ref, br_ref, x1_ref, h_ref, info_ref, info_t_ref, cnt_ref,
                    wbf_ref, carry_ref):
    first = jnp.logical_and(pl.program_id(0) == 0, pl.program_id(1) == 0)

    @pl.when(first)
    def _():
        wbf_ref[...] = w_ref[...].astype(BF16)
        carry_ref[...] = jnp.zeros_like(carry_ref)

    y = jnp.dot(yc_ref[0], wbf_ref[0:D_CONV, :], preferred_element_type=F32)
    y = y + jnp.dot(ya_ref[0], wbf_ref[D_CONV:, :], preferred_element_type=F32)
    x1 = x_ref[0] + g1_ref[0] * y
    x1_ref[0] = x1
    h = _rms_mod(x1, gf_ref[...], sc_ref[0], sh_ref[0])
    h_ref[0] = _pack_bf16_pairs(h)
    logits = jnp.dot(h.astype(BF16), wr_ref[...], preferred_element_type=F32) + br_ref[...]
    _route_tile(logits, info_ref, info_t_ref, cnt_ref, carry_ref)


def _mix_out_routed(x, yc, ya, w_out, layer, g1, gf, sc, sh, w_router, b_router, tm):
    b, s, d = x.shape
    nt = s // tm
    row = lambda bi, ti: (bi, ti, 0)
    per_b = lambda bi, ti: (bi, 0, 0)
    const2 = lambda bi, ti: (0, 0)
    w_pad = jnp.zeros((d, LANE_PAD_E), BF16).at[:, :N_EXPERTS].set(w_router.astype(BF16))
    b_pad = jnp.full((1, LANE_PAD_E), -jnp.inf, F32).at[0, :N_EXPERTS].set(
        b_router.astype(F32))
    return pl.pallas_call(
        _mix_out_kernel,
        out_shape=(jax.ShapeDtypeStruct((b, s, d), F32),
                   jax.ShapeDtypeStruct((b, s, d // 2), jnp.int32),
                   jax.ShapeDtypeStruct((b, s, LANE_PAD_E), F32),
                   jax.ShapeDtypeStruct((SUBLANES, b * s), F32),
                   jax.ShapeDtypeStruct((SUBLANES, LANE_PAD_E), F32)),
        grid=(b, nt),
        in_specs=[
            pl.BlockSpec((1, tm, d), row),
            pl.BlockSpec((1, tm, D_CONV), row),
            pl.BlockSpec((1, tm, D_ATT), row),
            pl.BlockSpec((None, d, d), lambda bi, ti: (layer, 0, 0),
                         pipeline_mode=pl.Buffered(1)),
            pl.BlockSpec((1, 1, d), per_b),
            pl.BlockSpec((1, d), const2),
            pl.BlockSpec((1, 1, d), per_b),
            pl.BlockSpec((1, 1, d), per_b),
            pl.BlockSpec((d, LANE_PAD_E), const2),
            pl.BlockSpec((1, LANE_PAD_E), const2),
        ],
        out_specs=(pl.BlockSpec((1, tm, d), row), pl.BlockSpec((1, tm, d // 2), row),
                   pl.BlockSpec((1, tm, LANE_PAD_E), row),
                   pl.BlockSpec((SUBLANES, tm), lambda bi, ti: (0, bi * nt + ti)),
                   pl.BlockSpec((SUBLANES, LANE_PAD_E), const2)),
        scratch_shapes=[pltpu.VMEM((d, d), BF16), pltpu.VMEM((SUBLANES, LANE_PAD_E), F32)],
        compiler_params=_cparams(2),
        name="mix_out",
    )(x, yc, ya, w_out, g1, gf.reshape(1, d), sc, sh, w_pad, b_pad)


def _cast_ffn_kernel(wg_ref, wu_ref, wd_ref, wo_ref, wg3_ref, wu3_ref, wd3_ref, wob_ref):
    wg3_ref[0] = wg_ref[...].astype(BF16)
    wu3_ref[0] = wu_ref[...].astype(BF16)
    wd3_ref[0] = wd_ref[...].astype(BF16)

    @pl.when(pl.program_id(0) == 0)
    def _():
        wob_ref[...] = wo_ref[...].astype(BF16)


def _cast_ffn_weights(wg, wu, wd, w_out, layer, fc):
    d, ff = wg.shape
    nf = ff // fc
    return pl.pallas_call(
        _cast_ffn_kernel,
        out_shape=(jax.ShapeDtypeStruct((nf, d, fc), BF16),
                   jax.ShapeDtypeStruct((nf, d, fc), BF16),
                   jax.ShapeDtypeStruct((nf, fc, d), BF16),
                   jax.ShapeDtypeStruct((d, d), BF16)),
        grid=(nf,),
        in_specs=[
            pl.BlockSpec((d, fc), lambda f: (0, f)),
            pl.BlockSpec((d, fc), lambda f: (0, f)),
            pl.BlockSpec((fc, d), lambda f: (f, 0)),
            pl.BlockSpec((None, d, d), lambda f: (layer, 0, 0)),
        ],
        out_specs=(pl.BlockSpec((1, d, fc), lambda f: (f, 0, 0)),
                   pl.BlockSpec((1, d, fc), lambda f: (f, 0, 0)),
                   pl.BlockSpec((1, fc, d), lambda f: (f, 0, 0)),
                   pl.BlockSpec((d, d), lambda f: (0, 0))),
        compiler_params=_cparams(1),
        name="cast_ffn_weights",
    )(wg, wu, wd, w_out)


def _mix_ffn_kernel(x_ref, yc_ref, ya_ref, wo_ref, g1_ref, gf_ref, sc_ref, sh_ref, g2_ref,
                    wg_ref, wu_ref, wd_ref, o_ref, acc_ref):
    y = jnp.dot(yc_ref[0], wo_ref[0:D_CONV, :], preferred_element_type=F32)
    y = y + jnp.dot(ya_ref[0], wo_ref[D_CONV:, :], preferred_element_type=F32)
    x1 = x_ref[0] + g1_ref[0] * y
    h = _rms_mod(x1, gf_ref[...], sc_ref[0], sh_ref[0]).astype(BF16)
    for c in range(wg_ref.shape[0]):
        a = jnp.dot(h, wg_ref[c], preferred_element_type=F32)
        u = jnp.dot(h, wu_ref[c], preferred_element_type=F32)
        t = (_silu(a) * u).astype(BF16)
        dn = jnp.dot(t, wd_ref[c], preferred_element_type=F32)
        if c == 0:
            acc_ref[...] = dn
        else:
            acc_ref[...] += dn
    o_ref[0] = x1 + g2_ref[0] * acc_ref[...]


def _mix_out_dense_ffn(x, yc, ya, w_out, layer, g1, gf, sc, sh, g2, wg, wu, wd, tm, fc):
    b, s, d = x.shape
    wg3, wu3, wd3, wob = _cast_ffn_weights(wg, wu, wd, w_out, layer, fc)
    nf = wg3.shape[0]
    row = lambda bi, ti: (bi, ti, 0)
    per_b = lambda bi, ti: (bi, 0, 0)
    const2 = lambda bi, ti: (0, 0)
    const3 = lambda bi, ti: (0, 0, 0)
    resident = pl.Buffered(1)
    return pl.pallas_call(
        _mix_ffn_kernel,
        out_shape=jax.ShapeDtypeStruct((b, s, d), F32),
        grid=(b, s // tm),
        in_specs=[
            pl.BlockSpec((1, tm, d), row),
            pl.BlockSpec((1, tm, D_CONV), row),
            pl.BlockSpec((1, tm, D_ATT), row),
            pl.BlockSpec((d, d), const2, pipeline_mode=resident),
            pl.BlockSpec((1, 1, d), per_b),
            pl.BlockSpec((1, d), const2),
            pl.BlockSpec((1, 1, d), per_b),
            pl.BlockSpec((1, 1, d), per_b),
            pl.BlockSpec((1, 1, d), per_b),
            pl.BlockSpec((nf, d, fc), const3, pipeline_mode=resident),
            pl.BlockSpec((nf, d, fc), const3, pipeline_mode=resident),
            pl.BlockSpec((nf, fc, d), const3, pipeline_mode=resident),
        ],
        out_specs=pl.BlockSpec((1, tm, d), row),
        scratch_shapes=[pltpu.VMEM((tm, d), F32)],
        compiler_params=_cparams(2),
        name="mix_out_dense_ffn",
    )(x, yc, ya, wob, g1, gf.reshape(1, d), sc, sh, g2, wg3, wu3, wd3)


def _route_tile(logits, info_ref, info_t_ref, cnt_ref, carry_ref):
    tr = logits.shape[0]
    lane = lax.broadcasted_iota(jnp.int32, (tr, LANE_PAD_E), 1).astype(F32)
    no_lane = float(LANE_PAD_E)
    v0 = jnp.max(logits, axis=-1, keepdims=True)
    i0 = jnp.min(jnp.where(logits == v0, lane, no_lane), axis=-1, keepdims=True)
    rest = jnp.where(lane == i0, -jnp.inf, logits)
    v1 = jnp.max(rest, axis=-1, keepdims=True)
    i1 = jnp.min(jnp.where(rest == v1, lane, no_lane), axis=-1, keepdims=True)
    e1 = jnp.exp(v1 - v0)
    w0 = 1.0 / (1.0 + e1)
    w1 = e1 / (1.0 + e1)
    oh0 = lane == i0
    oh1 = lane == i1
    cnt = jnp.where(jnp.logical_or(oh0, oh1), 1.0, 0.0)
    tri = (lax.broadcasted_iota(jnp.int32, (tr, tr), 1)
           < lax.broadcasted_iota(jnp.int32, (tr, tr), 0)).astype(BF16)
    before = jnp.dot(tri, cnt.astype(BF16), preferred_element_type=F32) + carry_ref[0:1, :]
    r0 = jnp.sum(jnp.where(oh0, before, 0.0), axis=-1, keepdims=True)
    r1 = jnp.sum(jnp.where(oh1, before, 0.0), axis=-1, keepdims=True)
    carry_ref[...] = carry_ref[...] + jnp.sum(cnt, axis=0, keepdims=True)
    cnt_ref[...] = carry_ref[...]
    info = jnp.where(lane == 0, i0, 0.0)
    info = jnp.where(lane == 1, i1, info)
    info = jnp.where(lane == 2, w0, info)
    info = jnp.where(lane == 3, w1, info)
    info = jnp.where(lane == 4, r0, info)
    info = jnp.where(lane == 5, r1, info)
    info_ref[0] = info
    for c in range(tr // LANES):
        blk = info[c * LANES:(c + 1) * LANES, :].T
        info_t_ref[:, c * LANES:(c + 1) * LANES] = blk[0:SUBLANES, :]


def _moe_kernel(te_ref, tn_ref, xs_ref, wg_ref, wu_ref, wd_ref, ys_ref,
                acc_ref, wgb_ref, wub_ref, wdb_ref):
    i = pl.program_id(0)
    f = pl.program_id(1)
    nhalf = tn_ref[i]
    nfull = nhalf // (MOE_SUB // MOE_HALF)

    @pl.when(f == 0)
    def _():
        acc_ref[...] = jnp.zeros_like(acc_ref)

    def block(r, rows, wgb, wub, wdb):
        xb = _unpack_bf16_pairs(xs_ref[pl.ds(r, rows), :]).astype(BF16)
        a = jnp.dot(xb, wgb, preferred_element_type=F32)
        u = jnp.dot(xb, wub, preferred_element_type=F32)
        t = (_silu(a) * u).astype(BF16)
        acc_ref[pl.ds(r, rows), :] += jnp.dot(t, wdb, preferred_element_type=F32)

    def split(units):
        out, r = [], 0
        for rows in (MOE_SUB,) * (units * MOE_HALF // MOE_SUB) + MOE_TAILS:
            if units * MOE_HALF - r >= rows:
                out.append((r, rows))
                r += rows
        return out

    for units in MOE_STATIC_UNITS:
        @pl.when(nhalf == units)
        def _(units=units):
            wgb, wub, wdb = (w[0].astype(BF16) for w in (wg_ref, wu_ref, wd_ref))
            for r, rows in split(units):
                block(r, rows, wgb, wub, wdb)

    other = nhalf > 0
    for units in MOE_STATIC_UNITS:
        other = jnp.logical_and(other, nhalf != units)

    @pl.when(other)
    def _():
        wgb_ref[...] = wg_ref[0].astype(BF16)
        wub_ref[...] = wu_ref[0].astype(BF16)
        wdb_ref[...] = wd_ref[0].astype(BF16)

        def sub(sidx, carry):
            block(pl.multiple_of(sidx * MOE_SUB, MOE_SUB), MOE_SUB,
                  wgb_ref[...], wub_ref[...], wdb_ref[...])
            return carry

        lax.fori_loop(0, nfull, sub, 0)
        done = nfull * MOE_SUB
        for rows in MOE_TAILS:
            units = rows // MOE_HALF

            @pl.when((nhalf & units) != 0)
            def _(rows=rows, units=units):
                higher = nhalf & (MOE_SUB // MOE_HALF - 1) & ~(2 * units - 1)
                block(pl.multiple_of(done + higher * MOE_HALF, MOE_HALF), rows,
                      wgb_ref[...], wub_ref[...], wdb_ref[...])

    @pl.when(f == pl.num_programs(1) - 1)
    def _():
        ys_ref[...] = _pack_bf16_pairs(acc_ref[...])


def _moe_ffn(xs, tile_e, tile_nsub, wg, wu, wd):
    rpad, dw = xs.shape
    d = 2 * dw
    ntiles = rpad // MOE_TILE
    ff = wg.shape[2]
    nf = ff // MOE_FC

    def fcol(i, f, tn):
        return jnp.where(tn[i] > 0, f, nf - 1)

    return pl.pallas_call(
        _moe_kernel,
        out_shape=jax.ShapeDtypeStruct((rpad, dw), jnp.int32),
        grid_spec=pltpu.PrefetchScalarGridSpec(
            num_scalar_prefetch=2,
            grid=(ntiles, nf),
            in_specs=[
                pl.BlockSpec((MOE_TILE, dw), lambda i, f, te, tn: (i, 0)),
                pl.BlockSpec((1, d, MOE_FC), lambda i, f, te, tn: (te[i], 0, fcol(i, f, tn))),
                pl.BlockSpec((1, d, MOE_FC), lambda i, f, te, tn: (te[i], 0, fcol(i, f, tn))),
                pl.BlockSpec((1, MOE_FC, d), lambda i, f, te, tn: (te[i], fcol(i, f, tn), 0)),
            ],
            out_specs=pl.BlockSpec((MOE_TILE, dw), lambda i, f, te, tn: (i, 0)),
            scratch_shapes=[
                pltpu.VMEM((MOE_TILE, d), F32),
                pltpu.VMEM((d, MOE_FC), BF16),
                pltpu.VMEM((d, MOE_FC), BF16),
                pltpu.VMEM((MOE_FC, d), BF16),
            ],
        ),
        compiler_params=_cparams(2),
        name="moe_ffn",
    )(tile_e, tile_nsub, xs, wg, wu, wd)


def _combine_kernel(x_ref, y0_ref, y1_ref, info_ref, g2_ref, o_ref):
    info = info_ref[0]
    w0 = info[:, 2:3]
    w1 = info[:, 3:4]
    f = w0 * _unpack_bf16_pairs(y0_ref[0, 0]) + w1 * _unpack_bf16_pairs(y1_ref[0, 0])
    o_ref[0] = x_ref[0] + g2_ref[0] * f


def _combine(x1, y01, info, g2, tm):
    b, s, d = x1.shape
    row = lambda bi, ti: (bi, ti, 0)
    return pl.pallas_call(
        _combine_kernel,
        out_shape=jax.ShapeDtypeStruct((b, s, d), F32),
        grid=(b, s // tm),
        in_specs=[
            pl.BlockSpec((1, tm, d), row),
            pl.BlockSpec((1, 1, tm, d // 2), lambda bi, ti: (0, bi, ti, 0)),
            pl.BlockSpec((1, 1, tm, d // 2), lambda bi, ti: (1, bi, ti, 0)),
            pl.BlockSpec((1, tm, LANE_PAD_E), row),
            pl.BlockSpec((1, 1, d), lambda bi, ti: (bi, 0, 0)),
        ],
        out_specs=pl.BlockSpec((1, tm, d), row),
        compiler_params=_cparams(2),
        name="moe_combine",
    )(x1, y01, y01, info, g2)


def _moe_layer(hp, info, info_t, cnt, x1, g2, wg, wu, wd, tm):
    b, s, d = x1.shape
    n = b * s
    e0 = info_t[0].astype(jnp.int32)
    e1 = info_t[1].astype(jnp.int32)
    r0 = info_t[4].astype(jnp.int32)
    r1 = info_t[5].astype(jnp.int32)
    counts = cnt[0, :N_EXPERTS].astype(jnp.int32)

    ntiles = (2 * n) // MOE_TILE + N_EXPERTS
    tiles_per_e = (counts + MOE_TILE - 1) // MOE_TILE
    tile_end = jnp.cumsum(tiles_per_e)
    tile_start = tile_end - tiles_per_e
    total = tile_end[-1]
    tidx = jnp.arange(ntiles, dtype=jnp.int32)
    live = tidx < total
    tclip = jnp.minimum(tidx, total - 1)
    tile_e = jnp.minimum(jnp.sum(tclip[:, None] >= tile_end[None, :], axis=1),
                         N_EXPERTS - 1).astype(jnp.int32)
    rows_left = counts[tile_e] - (tclip - tile_start[tile_e]) * MOE_TILE
    rows_here = jnp.clip(rows_left, 0, MOE_TILE)
    tile_nsub = jnp.where(live, (rows_here + MOE_HALF - 1) // MOE_HALF, 0).astype(jnp.int32)

    row_start = tile_start * MOE_TILE
    eid = jnp.arange(N_EXPERTS, dtype=jnp.int32)[None, :]
    pos0 = jnp.sum(jnp.where(e0[:, None] == eid, row_start[None, :], 0), axis=1) + r0
    pos1 = jnp.sum(jnp.where(e1[:, None] == eid, row_start[None, :], 0), axis=1) + r1

    xs = _sc_scatter_rows2(hp.reshape(n, d // 2), pos0, pos1, ntiles * MOE_TILE)
    ys = _moe_ffn(xs, tile_e, tile_nsub, wg, wu, wd)
    y01 = _sc_gather_rows(ys, jnp.concatenate([pos0, pos1]))
    return _combine(x1, y01.reshape(2, b, s, d // 2), info.reshape(b, s, LANE_PAD_E), g2, tm)


def kernel(x, c, w_ada, b_ada, norm_mix_g, norm_ffn_g, w_in, w_out, conv_w, conv_b,
           conv_ln_g, conv_ln_b, q_norm_g, k_norm_g, rel_bias, ffn_w_gate, ffn_w_up,
           ffn_w_down, moe_w_router, moe_b_router, moe_w_gate, moe_w_up, moe_w_down):
    b, s, d = x.shape
    depth = w_ada.shape[0]
    tm = min(1024, s)
    mod = _ada_mod(c, w_ada, b_ada)
    for l in range(depth):
        sh1, sc1, g1, sh2, sc2, g2 = [
            mod[l, :, j * d:(j + 1) * d].reshape(b, 1, d) for j in range(6)]
        z, qn, kn, vt = _mix_in(x, sc1, sh1, norm_mix_g[l], w_in, l,
                                q_norm_g[l], k_norm_g[l], tm)
        yc = _conv_branch(z, conv_w[l], conv_b[l], conv_ln_g[l], conv_ln_b[l], tm)
        ya = _attention(qn, kn, vt, rel_bias[l])
        i = l // 2
        if l % 2 == 0:
            x = _mix_out_dense_ffn(x, yc, ya, w_out, l, g1, norm_ffn_g[l], sc2, sh2, g2,
                                   ffn_w_gate[i], ffn_w_up[i], ffn_w_down[i],
                                   min(1024, s), 256)
        else:
            x1, hp, info, info_t, cnt = _mix_out_routed(
                x, yc, ya, w_out, l, g1, norm_ffn_g[l], sc2, sh2,
                moe_w_router[i], moe_b_router[i], tm)
            x = _moe_layer(hp, info, info_t, cnt, x1, g2,
                           moe_w_gate[i], moe_w_up[i], moe_w_down[i], tm)
    return x
```

```python
import functools

import jax
import jax.numpy as jnp
from jax import lax
from jax.experimental import pallas as pl
from jax.experimental.pallas import tpu as pltpu
from jax.experimental.pallas import tpu_sc as plsc

F32 = jnp.float32
BF16 = jnp.bfloat16

D_MODEL = 1024
CHUNK = 64
N_PREV_CHUNKS = 8
BAND_PAD = N_PREV_CHUNKS * CHUNK
D_CONV = 512
D_ATT = 512
HEAD_DIM = 64
N_HEADS = 8
CONV_WIDTH = 31
MAX_REL = 128
D_IN_COLS = 2 * D_CONV + 3 * D_ATT
N_EXPERTS = 8
EPS = 1e-6
NEG_INF = -1e30

LANES = 128
SUBLANES = 8
VMEM_LIMIT_BYTES = 56 * 1024 * 1024

ATT_HEADS = 4
ATT_GROUPS = N_HEADS // ATT_HEADS
ATT_W = ATT_HEADS * HEAD_DIM
ATT_Q = 2 * CHUNK
ATT_BAND = BAND_PAD + ATT_Q
ATT_L = ATT_HEADS * ATT_Q

CONV_HALO = 32
CONV_ROWS = 32
LANE_PAD_E = LANES

MOE_HALF = 256
MOE_SUB = 4 * MOE_HALF
MOE_TAILS = (2 * MOE_HALF, MOE_HALF)
MOE_TILE = 9 * MOE_HALF
MOE_STATIC_UNITS = (9, 8, 7)
MOE_FC = 512


def _cparams(n_axes, vmem=VMEM_LIMIT_BYTES):
    return pltpu.CompilerParams(
        dimension_semantics=("arbitrary",) * n_axes, vmem_limit_bytes=vmem)


def _silu(v):
    return v * jax.nn.sigmoid(v)


def _pack_bf16_pairs(v):
    w = v.shape[1] // 2
    bits = lax.bitcast_convert_type(v.astype(BF16).astype(F32), jnp.uint32)
    packed = (bits[:, w:] & jnp.uint32(0xFFFF0000)) | (bits[:, :w] >> 16)
    return lax.bitcast_convert_type(packed, jnp.int32)


def _unpack_bf16_pairs(p):
    bits = lax.bitcast_convert_type(p, jnp.uint32)
    lo = lax.bitcast_convert_type(bits << 16, F32)
    hi = lax.bitcast_convert_type(bits & jnp.uint32(0xFFFF0000), F32)
    return jnp.concatenate([lo, hi], axis=1)


SC_CORES = 2
SC_SUBCORES = 16
SC_WORKERS = SC_CORES * SC_SUBCORES
SC_CHUNK = 64


def _sc_worker_id():
    return lax.axis_index("s") * SC_CORES + lax.axis_index("c")


def _sc_mesh():
    return plsc.VectorSubcoreMesh(core_axis_name="c", subcore_axis_name="s")


def _sc_gather_rows(table, idx):
    _, w = table.shape
    b = idx.shape[0]
    per_w = b // SC_WORKERS
    nch = per_w // SC_CHUNK

    def body(table_hbm, idx_hbm, out_hbm, idx_v, rows_v, gsem, wsem):
        wid = _sc_worker_id()
        base = wid * per_w
        pltpu.sync_copy(idx_hbm.at[wid], idx_v)
        gathers = [None] * nch
        writes = [None] * nch
        gathers[0] = pltpu.async_copy(table_hbm.at[idx_v.at[0]], rows_v.at[0], gsem.at[0])
        for c in range(nch):
            slot = c % 2
            gathers[c].wait()
            if c + 1 < nch:
                if c >= 1:
                    writes[c - 1].wait()
                gathers[c + 1] = pltpu.async_copy(
                    table_hbm.at[idx_v.at[c + 1]], rows_v.at[1 - slot], gsem.at[1 - slot])
            writes[c] = pltpu.async_copy(
                rows_v.at[slot], out_hbm.at[pl.ds(base + c * SC_CHUNK, SC_CHUNK)], wsem.at[slot])
        if nch >= 2:
            writes[nch - 2].wait()
        writes[nch - 1].wait()

    call = pl.kernel(
        body, mesh=_sc_mesh(),
        out_type=jax.ShapeDtypeStruct((b, w), jnp.int32),
        scratch_types=[pltpu.VMEM((nch, SC_CHUNK), jnp.int32),
                       pltpu.VMEM((2, SC_CHUNK, w), jnp.int32),
                       pltpu.SemaphoreType.DMA((2,)), pltpu.SemaphoreType.DMA((2,))],
        name="sc_gather_rows")
    return call(table, idx.reshape(SC_WORKERS, nch, SC_CHUNK))


def _sc_scatter_rows2(src, idx0, idx1, rows_out):
    n, w = src.shape
    per_w = n // SC_WORKERS
    nch = per_w // SC_CHUNK

    def body(src_hbm, i0_hbm, i1_hbm, out_hbm, i0_v, i1_v, rows_v, rsem, wsem):
        wid = _sc_worker_id()
        base = wid * per_w
        pltpu.sync_copy(i0_hbm.at[wid], i0_v)
        pltpu.sync_copy(i1_hbm.at[wid], i1_v)
        reads = [None] * nch
        writes = [None] * nch
        reads[0] = pltpu.async_copy(src_hbm.at[pl.ds(base, SC_CHUNK)], rows_v.at[0], rsem.at[0])
        for c in range(nch):
            slot = c % 2
            reads[c].wait()
            if c + 1 < nch:
                if c >= 1:
                    for wr in writes[c - 1]:
                        wr.wait()
                reads[c + 1] = pltpu.async_copy(
                    src_hbm.at[pl.ds(base + (c + 1) * SC_CHUNK, SC_CHUNK)],
                    rows_v.at[1 - slot], rsem.at[1 - slot])
            writes[c] = (
                pltpu.async_copy(rows_v.at[slot], out_hbm.at[i0_v.at[c]], wsem.at[slot, 0]),
                pltpu.async_copy(rows_v.at[slot], out_hbm.at[i1_v.at[c]], wsem.at[slot, 1]),
            )
        for c in range(max(nch - 2, 0), nch):
            for wr in writes[c]:
                wr.wait()

    call = pl.kernel(
        body, mesh=_sc_mesh(),
        out_type=jax.ShapeDtypeStruct((rows_out, w), jnp.int32),
        scratch_types=[pltpu.VMEM((nch, SC_CHUNK), jnp.int32),
                       pltpu.VMEM((nch, SC_CHUNK), jnp.int32),
                       pltpu.VMEM((2, SC_CHUNK, w), jnp.int32),
                       pltpu.SemaphoreType.DMA((2,)), pltpu.SemaphoreType.DMA((2, 2))],
        name="sc_scatter_rows")
    shape3 = (SC_WORKERS, nch, SC_CHUNK)
    return call(src, idx0.reshape(shape3), idx1.reshape(shape3))


def _ada_kernel(c_ref, w_ref, b_ref, o_ref):
    ca = _silu(c_ref[...]).astype(BF16)
    w = w_ref[0].astype(BF16)
    o_ref[0] = jnp.dot(ca, w, preferred_element_type=F32) + b_ref[0]


def _ada_mod(c, w_ada, b_ada):
    depth, d, n6 = w_ada.shape
    b = c.shape[0]
    rows = 16
    c_pad = jnp.zeros((rows, d), F32).at[:b].set(c)
    tn = 1536
    out = pl.pallas_call(
        _ada_kernel,
        out_shape=jax.ShapeDtypeStruct((depth, rows, n6), F32),
        grid=(depth, n6 // tn),
        in_specs=[
            pl.BlockSpec((rows, d), lambda l, j: (0, 0)),
            pl.BlockSpec((1, d, tn), lambda l, j: (l, 0, j)),
            pl.BlockSpec((1, 1, tn), lambda l, j: (l, 0, j)),
        ],
        out_specs=pl.BlockSpec((1, rows, tn), lambda l, j: (l, 0, j)),
        compiler_params=_cparams(2),
        name="ada_mod",
    )(c_pad, w_ada, b_ada.reshape(depth, 1, n6))
    return out[:, :b]


MOD_SH1, MOD_SC1, MOD_G1, MOD_SH2, MOD_SC2, MOD_G2 = range(6)


def _mod_spec(layer, chunk):
    return pl.BlockSpec((None, 1, None, 1, D_MODEL),
                        lambda bi, *_: (layer, bi, chunk, 0, 0))


def _rms_mod(xf, g, sc, sh):
    ms = jnp.mean(xf * xf, axis=-1, keepdims=True)
    return xf * lax.rsqrt(ms + EPS) * g * (1.0 + sc) + sh


def _mix_in_kernel(x_ref, sc_ref, sh_ref, g_ref, w_ref, gq_ref, gk_ref, ones_ref,
                   z_ref, q_ref, k_ref, vt_ref, wbf_ref):
    first = jnp.logical_and(pl.program_id(0) == 0, pl.program_id(1) == 0)

    @pl.when(first)
    def _():
        wbf_ref[...] = w_ref[...].astype(BF16)

    h = _rms_mod(x_ref[0], g_ref[...], sc_ref[0], sh_ref[0]).astype(BF16)
    proj = jnp.dot(h, wbf_ref[...], preferred_element_type=F32)

    a = proj[:, :D_CONV]
    gate = proj[:, D_CONV:2 * D_CONV]
    z_ref[0] = (a * jax.nn.sigmoid(gate)).astype(BF16)

    def head_norm(t, g):
        sq = (t * t).astype(BF16)
        ss = jnp.concatenate(
            [jnp.dot(sq[:, c:c + ATT_W], ones_ref[...], preferred_element_type=F32)
             for c in range(0, D_ATT, ATT_W)], axis=1)
        return (t * lax.rsqrt(ss * (1.0 / HEAD_DIM) + EPS) * g).astype(BF16)

    o = 2 * D_CONV
    q_ref[0] = head_norm(proj[:, o:o + D_ATT], gq_ref[...])
    k_ref[0] = head_norm(proj[:, o + D_ATT:o + 2 * D_ATT], gk_ref[...])
    v = proj[:, o + 2 * D_ATT:]
    tm = v.shape[0]
    for cidx in range(tm // ATT_Q):
        vt_ref[0, cidx] = v[cidx * ATT_Q:(cidx + 1) * ATT_Q, :].T.astype(BF16)


def _mix_in(x, mod5, g, w_in, layer, gq, gk, tm):
    b, s, d = x.shape
    ones_bd = (jnp.arange(ATT_W)[:, None] // HEAD_DIM
               == jnp.arange(ATT_W)[None, :] // HEAD_DIM).astype(BF16)
    gq_t = (jnp.tile(gq, N_HEADS) * (HEAD_DIM ** -0.5)).reshape(1, D_ATT)
    gk_t = jnp.tile(gk, N_HEADS).reshape(1, D_ATT)
    row = lambda bi, ti: (bi, ti, 0)
    const2 = lambda bi, ti: (0, 0)
    return pl.pallas_call(
        _mix_in_kernel,
        out_shape=(
            jax.ShapeDtypeStruct((b, s, D_CONV), BF16),
            jax.ShapeDtypeStruct((b, s, D_ATT), BF16),
            jax.ShapeDtypeStruct((b, s, D_ATT), BF16),
            jax.ShapeDtypeStruct((b, s // ATT_Q, D_ATT, ATT_Q), BF16),
        ),
        grid=(b, s // tm),
        in_specs=[
            pl.BlockSpec((1, tm, d), row),
            _mod_spec(layer, MOD_SC1),
            _mod_spec(layer, MOD_SH1),
            pl.BlockSpec((1, d), const2),
            pl.BlockSpec((None, d, D_IN_COLS), lambda bi, ti: (layer, 0, 0),
                         pipeline_mode=pl.Buffered(1)),
            pl.BlockSpec((1, D_ATT), const2),
            pl.BlockSpec((1, D_ATT), const2),
            pl.BlockSpec((ATT_W, ATT_W), const2),
        ],
        out_specs=(
            pl.BlockSpec((1, tm, D_CONV), row),
            pl.BlockSpec((1, tm, D_ATT), row),
            pl.BlockSpec((1, tm, D_ATT), row),
            pl.BlockSpec((1, tm // ATT_Q, D_ATT, ATT_Q), lambda bi, ti: (bi, ti, 0, 0)),
        ),
        scratch_shapes=[pltpu.VMEM((d, D_IN_COLS), BF16)],
        compiler_params=_cparams(2),
        name="mix_in",
    )(x, mod5, mod5, g.reshape(1, d), w_in, gq_t, gk_t, ones_bd)


def _conv_kernel(zc_ref, zp_ref, w_ref, cb_ref, lg_ref, lb_ref, o_ref, win_ref, sh_ref,
                 acc_ref):
    tt = zc_ref.shape[1]
    t = pl.program_id(1)
    halo = zp_ref[0].astype(F32)
    win_ref[0:CONV_HALO, :] = jnp.where(t == 0, 0.0, halo)
    win_ref[CONV_HALO:, :] = zc_ref[0].astype(F32)
    span = tt + CONV_HALO - SUBLANES
    for sft in range(1, SUBLANES):
        sh_ref[sft - 1, 0:span, :] = win_ref[sft:sft + span, :]
    base = CONV_HALO - (CONV_WIDTH - 1)
    tiles = CONV_ROWS // SUBLANES

    for g in range(tt // CONV_ROWS):
        r = g * CONV_ROWS
        acc = jnp.zeros((tiles, SUBLANES, D_CONV), F32) + cb_ref[...]
        for j in range(CONV_WIDTH):
            whole, sft = divmod(base + j, SUBLANES)
            start = r + whole * SUBLANES
            if sft == 0:
                tap = win_ref[start:start + CONV_ROWS, :]
            else:
                tap = sh_ref[sft - 1, start:start + CONV_ROWS, :]
            acc = acc + tap.reshape(tiles, SUBLANES, D_CONV) * w_ref[j]
        acc_ref[r:r + CONV_ROWS, :] = acc.reshape(CONV_ROWS, D_CONV)
    acc = acc_ref[...]
    mu = jnp.mean(acc, axis=-1, keepdims=True)
    xc = acc - mu
    var = jnp.mean(xc * xc, axis=-1, keepdims=True)
    y = xc * lax.rsqrt(var + EPS) * lg_ref[...] + lb_ref[...]
    o_ref[0] = _silu(y).astype(BF16)


def _conv_branch(z, conv_w, conv_b, ln_g, ln_b, tt):
    b, s, c = z.shape
    hb = tt // CONV_HALO
    w_tiles = jnp.broadcast_to(conv_w.reshape(CONV_WIDTH, 1, c), (CONV_WIDTH, SUBLANES, c))
    const2 = lambda bi, ti: (0, 0)
    return pl.pallas_call(
        _conv_kernel,
        out_shape=jax.ShapeDtypeStruct((b, s, c), BF16),
        grid=(b, s // tt),
        in_specs=[
            pl.BlockSpec((1, tt, c), lambda bi, ti: (bi, ti, 0)),
            pl.BlockSpec((1, CONV_HALO, c),
                         lambda bi, ti: (bi, jnp.maximum(ti * hb - 1, 0), 0)),
            pl.BlockSpec((CONV_WIDTH, SUBLANES, c), lambda bi, ti: (0, 0, 0)),
            pl.BlockSpec((1, c), const2),
            pl.BlockSpec((1, c), const2),
            pl.BlockSpec((1, c), const2),
        ],
        out_specs=pl.BlockSpec((1, tt, c), lambda bi, ti: (bi, ti, 0)),
        scratch_shapes=[pltpu.VMEM((tt + CONV_HALO, c), F32),
                        pltpu.VMEM((SUBLANES - 1, tt + CONV_HALO, c), F32),
                        pltpu.VMEM((tt, c), F32)],
        compiler_params=_cparams(2),
        name="conv_branch",
    )(z, z, w_tiles, conv_b.reshape(1, c),
      ln_g.reshape(1, c), ln_b.reshape(1, c))


def _attn_kernel(q_ref, k_ref, vt_ref, bias_ref, o_ref, kpad_ref, vtpad_ref,
                 st0_ref, st1_ref, pb0_ref, pb1_ref, den0_ref, den1_ref):
    st_refs = (st0_ref, st1_ref)
    pb_refs = (pb0_ref, pb1_ref)
    den_refs = (den0_ref, den1_ref)
    s = q_ref.shape[1]
    npad = BAND_PAD // ATT_Q
    kpad_ref[0:BAND_PAD, :] = jnp.zeros((BAND_PAD, ATT_W), BF16)
    kpad_ref[BAND_PAD:, :] = k_ref[0]
    vtpad_ref[0:npad] = jnp.zeros((npad, ATT_W, ATT_Q), BF16)
    vtpad_ref[npad:] = vt_ref[0]

    iota = lambda shape, dim: lax.broadcasted_iota(jnp.int32, shape, dim)
    q_shift = ATT_Q.bit_length() - 1
    d_shift = HEAD_DIM.bit_length() - 1
    qb_mask = (iota((ATT_L, ATT_W), 0) >> q_shift) == (iota((ATT_L, ATT_W), 1) >> d_shift)
    ot_mask = (iota((ATT_W, ATT_L), 0) >> d_shift) == (iota((ATT_W, ATT_L), 1) >> q_shift)
    sel = jnp.where((iota((ATT_Q, ATT_L), 1) & (ATT_Q - 1)) == iota((ATT_Q, ATT_L), 0),
                    1.0, 0.0).astype(BF16)
    key_row = lax.broadcasted_iota(jnp.int32, (ATT_BAND, ATT_L), 0)
    contract_last = (((1,), (1,)), ((), ()))

    def scores(m, p):
        r0 = m * ATT_Q
        qt = q_ref[0, pl.ds(r0, ATT_Q), :]
        qb = jnp.where(qb_mask, jnp.concatenate([qt] * ATT_HEADS, axis=0), 0)
        kb = kpad_ref[pl.ds(r0, ATT_BAND), :]
        st_refs[p][...] = lax.dot_general(kb, qb.astype(BF16), contract_last,
                                          preferred_element_type=F32)

    def softmax(m, p, masked):
        st = st_refs[p][...] + bias_ref[0]
        if masked:
            st = jnp.where(key_row >= BAND_PAD - m * ATT_Q, st, NEG_INF)
        mx = jnp.max(st, axis=0, keepdims=True)
        e = jnp.exp(st - mx)
        den_refs[p][...] = jnp.sum(e, axis=0, keepdims=True)
        pb_refs[p][...] = e.astype(BF16)

    def values(m, p):
        r0 = m * ATT_Q
        vb = jnp.concatenate([vtpad_ref[m + c] for c in range(ATT_BAND // ATT_Q)],
                             axis=1)
        ot = jnp.dot(vb, pb_refs[p][...], preferred_element_type=F32)
        ot = jnp.where(ot_mask, ot / den_refs[p][...], 0.0).astype(BF16)
        y = lax.dot_general(sel, ot, contract_last, preferred_element_type=F32)
        o_ref[0, pl.ds(r0, ATT_Q), :] = y.astype(BF16)

    n = s // ATT_Q
    n_masked = BAND_PAD // ATT_Q
    for m in range(n + 2):
        if m < n:
            scores(m, m % 2)
        if 1 <= m <= n:
            softmax(m - 1, (m - 1) % 2, m - 1 < n_masked)
        if m >= 2:
            values(m - 2, m % 2)


def _attn_bias_t(rel_bias):
    rb = rel_bias.astype(F32)
    nu = ATT_BAND + ATT_Q - 1
    n_low = BAND_PAD - MAX_REL + ATT_Q
    t = jnp.concatenate([jnp.repeat(rb[:, :1], n_low, axis=1),
                         rb[:, 1:1 + nu - n_low]], axis=1)
    tp = jnp.pad(t, ((0, 0), (0, 1)))
    skew = jnp.tile(tp, (1, ATT_Q))[:, :ATT_Q * nu].reshape(N_HEADS, ATT_Q, nu)
    bias = skew[:, :, ATT_Q - 1:]
    bias = bias.reshape(ATT_GROUPS, ATT_HEADS, ATT_Q, ATT_BAND)
    bias = bias.transpose(0, 3, 1, 2).reshape(ATT_GROUPS, ATT_BAND, ATT_L)
    r = jnp.arange(ATT_BAND)[:, None]
    qq = jnp.arange(ATT_L)[None, :] % ATT_Q
    first = (qq // CHUNK) * CHUNK
    valid = (r >= first) & (r < first + BAND_PAD + CHUNK)
    return jnp.where(valid[None], bias, NEG_INF)


def _attention(qn, kn, vt, rel_bias):
    b, s, _ = qn.shape
    bias_t = _attn_bias_t(rel_bias)
    nck = s // ATT_Q
    assert nck >= 2, "the attention pipeline needs at least two query steps"
    return pl.pallas_call(
        _attn_kernel,
        out_shape=jax.ShapeDtypeStruct((b, s, D_ATT), BF16),
        grid=(b, ATT_GROUPS),
        in_specs=[
            pl.BlockSpec((1, s, ATT_W), lambda bi, gi: (bi, 0, gi)),
            pl.BlockSpec((1, s, ATT_W), lambda bi, gi: (bi, 0, gi)),
            pl.BlockSpec((1, nck, ATT_W, ATT_Q), lambda bi, gi: (bi, 0, gi, 0)),
            pl.BlockSpec((1, ATT_BAND, ATT_L), lambda bi, gi: (gi, 0, 0)),
        ],
        out_specs=pl.BlockSpec((1, s, ATT_W), lambda bi, gi: (bi, 0, gi)),
        scratch_shapes=[
            pltpu.VMEM((s + BAND_PAD, ATT_W), BF16),
            pltpu.VMEM((nck + BAND_PAD // ATT_Q, ATT_W, ATT_Q), BF16),
            pltpu.VMEM((ATT_BAND, ATT_L), F32), pltpu.VMEM((ATT_BAND, ATT_L), F32),
            pltpu.VMEM((ATT_BAND, ATT_L), BF16), pltpu.VMEM((ATT_BAND, ATT_L), BF16),
            pltpu.VMEM((1, ATT_L), F32), pltpu.VMEM((1, ATT_L), F32),
        ],
        compiler_params=_cparams(2),
        name="band_attention",
    )(qn, kn, vt, bias_t)


def _mix_out_kernel(x_ref, yc_ref, ya_ref, w_ref, g1_ref, gf_ref, sc_ref, sh_ref,
                    wr_ref, br_ref, x1_ref, h_ref, info_ref, info_t_ref, cnt_ref,
                    wbf_ref, carry_ref):
    first = jnp.logical_and(pl.program_id(0) == 0, pl.program_id(1) == 0)

    @pl.when(first)
    def _():
        wbf_ref[...] = w_ref[...].astype(BF16)
        carry_ref[...] = jnp.zeros_like(carry_ref)

    y = jnp.dot(yc_ref[0], wbf_ref[0:D_CONV, :], preferred_element_type=F32)
    y = y + jnp.dot(ya_ref[0], wbf_ref[D_CONV:, :], preferred_element_type=F32)
    x1 = x_ref[0] + g1_ref[0] * y
    x1_ref[0] = x1
    h = _rms_mod(x1, gf_ref[...], sc_ref[0], sh_ref[0])
    h_ref[0] = _pack_bf16_pairs(h)
    logits = jnp.dot(h.astype(BF16), wr_ref[...], preferred_element_type=F32) + br_ref[...]
    _route_tile(logits, info_ref, info_t_ref, cnt_ref, carry_ref)


def _mix_out_routed(x, yc, ya, w_out, layer, mod5, gf, w_router, b_router, tm):
    b, s, d = x.shape
    nt = s // tm
    row = lambda bi, ti: (bi, ti, 0)
    const2 = lambda bi, ti: (0, 0)
    w_pad = jnp.zeros((d, LANE_PAD_E), BF16).at[:, :N_EXPERTS].set(w_router.astype(BF16))
    b_pad = jnp.full((1, LANE_PAD_E), -jnp.inf, F32).at[0, :N_EXPERTS].set(
        b_router.astype(F32))
    return pl.pallas_call(
        _mix_out_kernel,
        out_shape=(jax.ShapeDtypeStruct((b, s, d), F32),
                   jax.ShapeDtypeStruct((b, s, d // 2), jnp.int32),
                   jax.ShapeDtypeStruct((b, s, LANE_PAD_E), F32),
                   jax.ShapeDtypeStruct((SUBLANES, b * s), F32),
                   jax.ShapeDtypeStruct((SUBLANES, LANE_PAD_E), F32)),
        grid=(b, nt),
        in_specs=[
            pl.BlockSpec((1, tm, d), row),
            pl.BlockSpec((1, tm, D_CONV), row),
            pl.BlockSpec((1, tm, D_ATT), row),
            pl.BlockSpec((None, d, d), lambda bi, ti: (layer, 0, 0),
                         pipeline_mode=pl.Buffered(1)),
            _mod_spec(layer, MOD_G1),
            pl.BlockSpec((1, d), const2),
            _mod_spec(layer, MOD_SC2),
            _mod_spec(layer, MOD_SH2),
            pl.BlockSpec((d, LANE_PAD_E), const2),
            pl.BlockSpec((1, LANE_PAD_E), const2),
        ],
        out_specs=(pl.BlockSpec((1, tm, d), row), pl.BlockSpec((1, tm, d // 2), row),
                   pl.BlockSpec((1, tm, LANE_PAD_E), row),
                   pl.BlockSpec((SUBLANES, tm), lambda bi, ti: (0, bi * nt + ti)),
                   pl.BlockSpec((SUBLANES, LANE_PAD_E), const2)),
        scratch_shapes=[pltpu.VMEM((d, d), BF16), pltpu.VMEM((SUBLANES, LANE_PAD_E), F32)],
        compiler_params=_cparams(2),
        name="mix_out",
    )(x, yc, ya, w_out, mod5, gf.reshape(1, d), mod5, mod5, w_pad, b_pad)


def _cast_ffn_kernel(wg_ref, wu_ref, wd_ref, wo_ref, wg3_ref, wu3_ref, wd3_ref, wob_ref):
    wg3_ref[0] = wg_ref[...].astype(BF16)
    wu3_ref[0] = wu_ref[...].astype(BF16)
    wd3_ref[0] = wd_ref[...].astype(BF16)

    @pl.when(pl.program_id(0) == 0)
    def _():
        wob_ref[...] = wo_ref[...].astype(BF16)


def _cast_ffn_weights(wg, wu, wd, w_out, layer, fc):
    d, ff = wg.shape
    nf = ff // fc
    return pl.pallas_call(
        _cast_ffn_kernel,
        out_shape=(jax.ShapeDtypeStruct((nf, d, fc), BF16),
                   jax.ShapeDtypeStruct((nf, d, fc), BF16),
                   jax.ShapeDtypeStruct((nf, fc, d), BF16),
                   jax.ShapeDtypeStruct((d, d), BF16)),
        grid=(nf,),
        in_specs=[
            pl.BlockSpec((d, fc), lambda f: (0, f)),
            pl.BlockSpec((d, fc), lambda f: (0, f)),
            pl.BlockSpec((fc, d), lambda f: (f, 0)),
            pl.BlockSpec((None, d, d), lambda f: (layer, 0, 0)),
        ],
        out_specs=(pl.BlockSpec((1, d, fc), lambda f: (f, 0, 0)),
                   pl.BlockSpec((1, d, fc), lambda f: (f, 0, 0)),
                   pl.BlockSpec((1, fc, d), lambda f: (f, 0, 0)),
                   pl.BlockSpec((d, d), lambda f: (0, 0))),
        compiler_params=_cparams(1),
        name="cast_ffn_weights",
    )(wg, wu, wd, w_out)


def _mix_ffn_kernel(x_ref, yc_ref, ya_ref, wo_ref, g1_ref, gf_ref, sc_ref, sh_ref, g2_ref,
                    wg_ref, wu_ref, wd_ref, o_ref, acc_ref):
    y = jnp.dot(yc_ref[0], wo_ref[0:D_CONV, :], preferred_element_type=F32)
    y = y + jnp.dot(ya_ref[0], wo_ref[D_CONV:, :], preferred_element_type=F32)
    x1 = x_ref[0] + g1_ref[0] * y
    h = _rms_mod(x1, gf_ref[...], sc_ref[0], sh_ref[0]).astype(BF16)
    for c in range(wg_ref.shape[0]):
        a = jnp.dot(h, wg_ref[c], preferred_element_type=F32)
        u = jnp.dot(h, wu_ref[c], preferred_element_type=F32)
        t = (_silu(a) * u).astype(BF16)
        dn = jnp.dot(t, wd_ref[c], preferred_element_type=F32)
        if c == 0:
            acc_ref[...] = dn
        else:
            acc_ref[...] += dn
    o_ref[0] = x1 + g2_ref[0] * acc_ref[...]


def _mix_out_dense_ffn(x, yc, ya, w_out, layer, mod5, gf, wg, wu, wd, tm, fc):
    b, s, d = x.shape
    wg3, wu3, wd3, wob = _cast_ffn_weights(wg, wu, wd, w_out, layer, fc)
    nf = wg3.shape[0]
    row = lambda bi, ti: (bi, ti, 0)
    const2 = lambda bi, ti: (0, 0)
    const3 = lambda bi, ti: (0, 0, 0)
    resident = pl.Buffered(1)
    return pl.pallas_call(
        _mix_ffn_kernel,
        out_shape=jax.ShapeDtypeStruct((b, s, d), F32),
        grid=(b, s // tm),
        in_specs=[
            pl.BlockSpec((1, tm, d), row),
            pl.BlockSpec((1, tm, D_CONV), row),
            pl.BlockSpec((1, tm, D_ATT), row),
            pl.BlockSpec((d, d), const2, pipeline_mode=resident),
            _mod_spec(layer, MOD_G1),
            pl.BlockSpec((1, d), const2),
            _mod_spec(layer, MOD_SC2),
            _mod_spec(layer, MOD_SH2),
            _mod_spec(layer, MOD_G2),
            pl.BlockSpec((nf, d, fc), const3, pipeline_mode=resident),
            pl.BlockSpec((nf, d, fc), const3, pipeline_mode=resident),
            pl.BlockSpec((nf, fc, d), const3, pipeline_mode=resident),
        ],
        out_specs=pl.BlockSpec((1, tm, d), row),
        scratch_shapes=[pltpu.VMEM((tm, d), F32)],
        compiler_params=_cparams(2),
        name="mix_out_dense_ffn",
    )(x, yc, ya, wob, mod5, gf.reshape(1, d), mod5, mod5, mod5, wg3, wu3, wd3)


def _route_tile(logits, info_ref, info_t_ref, cnt_ref, carry_ref):
    tr = logits.shape[0]
    lane = lax.broadcasted_iota(jnp.int32, (tr, LANE_PAD_E), 1).astype(F32)
    no_lane = float(LANE_PAD_E)
    v0 = jnp.max(logits, axis=-1, keepdims=True)
    i0 = jnp.min(jnp.where(logits == v0, lane, no_lane), axis=-1, keepdims=True)
    rest = jnp.where(lane == i0, -jnp.inf, logits)
    v1 = jnp.max(rest, axis=-1, keepdims=True)
    i1 = jnp.min(jnp.where(rest == v1, lane, no_lane), axis=-1, keepdims=True)
    e1 = jnp.exp(v1 - v0)
    w0 = 1.0 / (1.0 + e1)
    w1 = e1 / (1.0 + e1)
    oh0 = lane == i0
    oh1 = lane == i1
    cnt = jnp.where(jnp.logical_or(oh0, oh1), 1.0, 0.0)
    tri = (lax.broadcasted_iota(jnp.int32, (tr, tr), 1)
           < lax.broadcasted_iota(jnp.int32, (tr, tr), 0)).astype(BF16)
    before = jnp.dot(tri, cnt.astype(BF16), preferred_element_type=F32) + carry_ref[0:1, :]
    r0 = jnp.sum(jnp.where(oh0, before, 0.0), axis=-1, keepdims=True)
    r1 = jnp.sum(jnp.where(oh1, before, 0.0), axis=-1, keepdims=True)
    carry_ref[...] = carry_ref[...] + jnp.sum(cnt, axis=0, keepdims=True)
    cnt_ref[...] = carry_ref[...]
    info = jnp.where(lane == 0, i0, 0.0)
    info = jnp.where(lane == 1, i1, info)
    info = jnp.where(lane == 2, w0, info)
    info = jnp.where(lane == 3, w1, info)
    info = jnp.where(lane == 4, r0, info)
    info = jnp.where(lane == 5, r1, info)
    info_ref[0] = info
    for c in range(tr // LANES):
        blk = info[c * LANES:(c + 1) * LANES, :].T
        info_t_ref[:, c * LANES:(c + 1) * LANES] = blk[0:SUBLANES, :]


def _moe_kernel(te_ref, tn_ref, xs_ref, wg_ref, wu_ref, wd_ref, ys_ref,
                acc_ref, wgb_ref, wub_ref, wdb_ref):
    i = pl.program_id(0)
    f = pl.program_id(1)
    nhalf = tn_ref[i]
    nfull = nhalf // (MOE_SUB // MOE_HALF)

    @pl.when(f == 0)
    def _():
        acc_ref[...] = jnp.zeros_like(acc_ref)

    def block(r, rows, wgb, wub, wdb):
        xb = _unpack_bf16_pairs(xs_ref[pl.ds(r, rows), :]).astype(BF16)
        a = jnp.dot(xb, wgb, preferred_element_type=F32)
        u = jnp.dot(xb, wub, preferred_element_type=F32)
        t = (_silu(a) * u).astype(BF16)
        acc_ref[pl.ds(r, rows), :] += jnp.dot(t, wdb, preferred_element_type=F32)

    def split(units):
        out, r = [], 0
        for rows in (MOE_SUB,) * (units * MOE_HALF // MOE_SUB) + MOE_TAILS:
            if units * MOE_HALF - r >= rows:
                out.append((r, rows))
                r += rows
        return out

    for units in MOE_STATIC_UNITS:
        @pl.when(nhalf == units)
        def _(units=units):
            wgb, wub, wdb = (w[0].astype(BF16) for w in (wg_ref, wu_ref, wd_ref))
            for r, rows in split(units):
                block(r, rows, wgb, wub, wdb)

    other = nhalf > 0
    for units in MOE_STATIC_UNITS:
        other = jnp.logical_and(other, nhalf != units)

    @pl.when(other)
    def _():
        wgb_ref[...] = wg_ref[0].astype(BF16)
        wub_ref[...] = wu_ref[0].astype(BF16)
        wdb_ref[...] = wd_ref[0].astype(BF16)

        def sub(sidx, carry):
            block(pl.multiple_of(sidx * MOE_SUB, MOE_SUB), MOE_SUB,
                  wgb_ref[...], wub_ref[...], wdb_ref[...])
            return carry

        lax.fori_loop(0, nfull, sub, 0)
        done = nfull * MOE_SUB
        for rows in MOE_TAILS:
            units = rows // MOE_HALF

            @pl.when((nhalf & units) != 0)
            def _(rows=rows, units=units):
                higher = nhalf & (MOE_SUB // MOE_HALF - 1) & ~(2 * units - 1)
                block(pl.multiple_of(done + higher * MOE_HALF, MOE_HALF), rows,
                      wgb_ref[...], wub_ref[...], wdb_ref[...])

    @pl.when(f == pl.num_programs(1) - 1)
    def _():
        ys_ref[...] = _pack_bf16_pairs(acc_ref[...])


def _moe_ffn(xs, tile_e, tile_nsub, wg, wu, wd):
    rpad, dw = xs.shape
    d = 2 * dw
    ntiles = rpad // MOE_TILE
    ff = wg.shape[2]
    nf = ff // MOE_FC

    def fcol(i, f, tn):
        return jnp.where(tn[i] > 0, f, nf - 1)

    return pl.pallas_call(
        _moe_kernel,
        out_shape=jax.ShapeDtypeStruct((rpad, dw), jnp.int32),
        grid_spec=pltpu.PrefetchScalarGridSpec(
            num_scalar_prefetch=2,
            grid=(ntiles, nf),
            in_specs=[
                pl.BlockSpec((MOE_TILE, dw), lambda i, f, te, tn: (i, 0)),
                pl.BlockSpec((1, d, MOE_FC), lambda i, f, te, tn: (te[i], 0, fcol(i, f, tn))),
                pl.BlockSpec((1, d, MOE_FC), lambda i, f, te, tn: (te[i], 0, fcol(i, f, tn))),
                pl.BlockSpec((1, MOE_FC, d), lambda i, f, te, tn: (te[i], fcol(i, f, tn), 0)),
            ],
            out_specs=pl.BlockSpec((MOE_TILE, dw), lambda i, f, te, tn: (i, 0)),
            scratch_shapes=[
                pltpu.VMEM((MOE_TILE, d), F32),
                pltpu.VMEM((d, MOE_FC), BF16),
                pltpu.VMEM((d, MOE_FC), BF16),
                pltpu.VMEM((MOE_FC, d), BF16),
            ],
        ),
        compiler_params=_cparams(2),
        name="moe_ffn",
    )(tile_e, tile_nsub, xs, wg, wu, wd)


def _combine_kernel(x_ref, y0_ref, y1_ref, info_ref, g2_ref, o_ref):
    info = info_ref[0]
    w0 = info[:, 2:3]
    w1 = info[:, 3:4]
    f = w0 * _unpack_bf16_pairs(y0_ref[0, 0]) + w1 * _unpack_bf16_pairs(y1_ref[0, 0])
    o_ref[0] = x_ref[0] + g2_ref[0] * f


def _combine(x1, y01, info, mod5, layer, tm):
    b, s, d = x1.shape
    row = lambda bi, ti: (bi, ti, 0)
    return pl.pallas_call(
        _combine_kernel,
        out_shape=jax.ShapeDtypeStruct((b, s, d), F32),
        grid=(b, s // tm),
        in_specs=[
            pl.BlockSpec((1, tm, d), row),
            pl.BlockSpec((1, 1, tm, d // 2), lambda bi, ti: (0, bi, ti, 0)),
            pl.BlockSpec((1, 1, tm, d // 2), lambda bi, ti: (1, bi, ti, 0)),
            pl.BlockSpec((1, tm, LANE_PAD_E), row),
            _mod_spec(layer, MOD_G2),
        ],
        out_specs=pl.BlockSpec((1, tm, d), row),
        compiler_params=_cparams(2),
        name="moe_combine",
    )(x1, y01, y01, info, mod5)


def _moe_layer(hp, info, info_t, cnt, x1, mod5, layer, wg, wu, wd, tm):
    b, s, d = x1.shape
    n = b * s
    e0 = info_t[0].astype(jnp.int32)
    e1 = info_t[1].astype(jnp.int32)
    r0 = info_t[4].astype(jnp.int32)
    r1 = info_t[5].astype(jnp.int32)
    counts = cnt[0, :N_EXPERTS].astype(jnp.int32)

    ntiles = (2 * n) // MOE_TILE + N_EXPERTS
    tiles_per_e = (counts + MOE_TILE - 1) // MOE_TILE
    tile_end = jnp.cumsum(tiles_per_e)
    tile_start = tile_end - tiles_per_e
    total = tile_end[-1]
    tidx = jnp.arange(ntiles, dtype=jnp.int32)
    live = tidx < total
    tclip = jnp.minimum(tidx, total - 1)
    tile_e = jnp.minimum(jnp.sum(tclip[:, None] >= tile_end[None, :], axis=1),
                         N_EXPERTS - 1).astype(jnp.int32)
    rows_left = counts[tile_e] - (tclip - tile_start[tile_e]) * MOE_TILE
    rows_here = jnp.clip(rows_left, 0, MOE_TILE)
    tile_nsub = jnp.where(live, (rows_here + MOE_HALF - 1) // MOE_HALF, 0).astype(jnp.int32)

    row_start = tile_start * MOE_TILE
    eid = jnp.arange(N_EXPERTS, dtype=jnp.int32)[None, :]
    pos0 = jnp.sum(jnp.where(e0[:, None] == eid, row_start[None, :], 0), axis=1) + r0
    pos1 = jnp.sum(jnp.where(e1[:, None] == eid, row_start[None, :], 0), axis=1) + r1

    xs = _sc_scatter_rows2(hp.reshape(n, d // 2), pos0, pos1, ntiles * MOE_TILE)
    ys = _moe_ffn(xs, tile_e, tile_nsub, wg, wu, wd)
    y01 = _sc_gather_rows(ys, jnp.concatenate([pos0, pos1]))
    return _combine(x1, y01.reshape(2, b, s, d // 2), info, mod5, layer, tm)


def kernel(x, c, w_ada, b_ada, norm_mix_g, norm_ffn_g, w_in, w_out, conv_w, conv_b,
           conv_ln_g, conv_ln_b, q_norm_g, k_norm_g, rel_bias, ffn_w_gate, ffn_w_up,
           ffn_w_down, moe_w_router, moe_b_router, moe_w_gate, moe_w_up, moe_w_down):
    b, s, d = x.shape
    depth = w_ada.shape[0]
    tm = min(1024, s)
    mod5 = _ada_mod(c, w_ada, b_ada).reshape(depth, b, 6, 1, d)
    for l in range(depth):
        z, qn, kn, vt = _mix_in(x, mod5, norm_mix_g[l], w_in, l,
                                q_norm_g[l], k_norm_g[l], tm)
        yc = _conv_branch(z, conv_w[l], conv_b[l], conv_ln_g[l], conv_ln_b[l], tm)
        ya = _attention(qn, kn, vt, rel_bias[l])
        i = l // 2
        if l % 2 == 0:
            x = _mix_out_dense_ffn(x, yc, ya, w_out, l, mod5, norm_ffn_g[l],
                                   ffn_w_gate[i], ffn_w_up[i], ffn_w_down[i],
                                   min(1024, s), 256)
        else:
            x1, hp, info, info_t, cnt = _mix_out_routed(
                x, yc, ya, w_out, l, mod5, norm_ffn_g[l],
                moe_w_router[i], moe_b_router[i], tm)
            x = _moe_layer(hp, info, info_t, cnt, x1, mod5, l,
                           moe_w_gate[i], moe_w_up[i], moe_w_down[i], tm)
    return x
```

```python
import functools

import jax
import jax.numpy as jnp
from jax import lax
from jax.experimental import pallas as pl
from jax.experimental.pallas import tpu as pltpu
from jax.experimental.pallas import tpu_sc as plsc

F32 = jnp.float32
BF16 = jnp.bfloat16

D_MODEL = 1024
CHUNK = 64
N_PREV_CHUNKS = 8
BAND_PAD = N_PREV_CHUNKS * CHUNK
D_CONV = 512
D_ATT = 512
HEAD_DIM = 64
N_HEADS = 8
CONV_WIDTH = 31
MAX_REL = 128
D_IN_COLS = 2 * D_CONV + 3 * D_ATT
N_EXPERTS = 8
EPS = 1e-6
NEG_INF = -1e30

LANES = 128
SUBLANES = 8
VMEM_LIMIT_BYTES = 56 * 1024 * 1024

ATT_HEADS = 4
ATT_GROUPS = N_HEADS // ATT_HEADS
ATT_W = ATT_HEADS * HEAD_DIM
ATT_Q = 2 * CHUNK
ATT_BAND = BAND_PAD + ATT_Q
ATT_L = ATT_HEADS * ATT_Q

CONV_HALO = 32
CONV_ROWS = 32
LANE_PAD_E = LANES

MOE_HALF = 256
MOE_SUB = 4 * MOE_HALF
MOE_TAILS = (2 * MOE_HALF, MOE_HALF)
MOE_TILE = 9 * MOE_HALF
MOE_STATIC_UNITS = (9, 8, 7)
MOE_FC = 512


def _cparams(n_axes, vmem=VMEM_LIMIT_BYTES):
    return pltpu.CompilerParams(
        dimension_semantics=("arbitrary",) * n_axes, vmem_limit_bytes=vmem)


def _silu(v):
    return v * jax.nn.sigmoid(v)


def _pack_bf16_pairs(v):
    w = v.shape[1] // 2
    bits = lax.bitcast_convert_type(v.astype(BF16).astype(F32), jnp.uint32)
    packed = (bits[:, w:] & jnp.uint32(0xFFFF0000)) | (bits[:, :w] >> 16)
    return lax.bitcast_convert_type(packed, jnp.int32)


def _unpack_bf16_pairs(p):
    bits = lax.bitcast_convert_type(p, jnp.uint32)
    lo = lax.bitcast_convert_type(bits << 16, F32)
    hi = lax.bitcast_convert_type(bits & jnp.uint32(0xFFFF0000), F32)
    return jnp.concatenate([lo, hi], axis=1)


SC_CORES = 2
SC_SUBCORES = 16
SC_WORKERS = SC_CORES * SC_SUBCORES
SC_CHUNK = 64


def _sc_worker_id():
    return lax.axis_index("s") * SC_CORES + lax.axis_index("c")


def _sc_mesh():
    return plsc.VectorSubcoreMesh(core_axis_name="c", subcore_axis_name="s")


def _sc_gather_rows(table, idx):
    _, w = table.shape
    b = idx.shape[0]
    per_w = b // SC_WORKERS
    nch = per_w // SC_CHUNK

    def body(table_hbm, idx_hbm, out_hbm, idx_v, rows_v, gsem, wsem):
        wid = _sc_worker_id()
        base = wid * per_w
        pltpu.sync_copy(idx_hbm.at[wid], idx_v)
        gathers = [None] * nch
        writes = [None] * nch
        gathers[0] = pltpu.async_copy(table_hbm.at[idx_v.at[0]], rows_v.at[0], gsem.at[0])
        for c in range(nch):
            slot = c % 2
            gathers[c].wait()
            if c + 1 < nch:
                if c >= 1:
                    writes[c - 1].wait()
                gathers[c + 1] = pltpu.async_copy(
                    table_hbm.at[idx_v.at[c + 1]], rows_v.at[1 - slot], gsem.at[1 - slot])
            writes[c] = pltpu.async_copy(
                rows_v.at[slot], out_hbm.at[pl.ds(base + c * SC_CHUNK, SC_CHUNK)], wsem.at[slot])
        if nch >= 2:
            writes[nch - 2].wait()
        writes[nch - 1].wait()

    call = pl.kernel(
        body, mesh=_sc_mesh(),
        out_type=jax.ShapeDtypeStruct((b, w), jnp.int32),
        scratch_types=[pltpu.VMEM((nch, SC_CHUNK), jnp.int32),
                       pltpu.VMEM((2, SC_CHUNK, w), jnp.int32),
                       pltpu.SemaphoreType.DMA((2,)), pltpu.SemaphoreType.DMA((2,))],
        name="sc_gather_rows")
    return call(table, idx.reshape(SC_WORKERS, nch, SC_CHUNK))


def _sc_scatter_rows2(src, idx0, idx1, rows_out):
    n, w = src.shape
    per_w = n // SC_WORKERS
    nch = per_w // SC_CHUNK

    def body(src_hbm, i0_hbm, i1_hbm, out_hbm, i0_v, i1_v, rows_v, rsem, wsem):
        wid = _sc_worker_id()
        base = wid * per_w
        pltpu.sync_copy(i0_hbm.at[wid], i0_v)
        pltpu.sync_copy(i1_hbm.at[wid], i1_v)
        reads = [None] * nch
        writes = [None] * nch
        reads[0] = pltpu.async_copy(src_hbm.at[pl.ds(base, SC_CHUNK)], rows_v.at[0], rsem.at[0])
        for c in range(nch):
            slot = c % 2
            reads[c].wait()
            if c + 1 < nch:
                if c >= 1:
                    for wr in writes[c - 1]:
                        wr.wait()
                reads[c + 1] = pltpu.async_copy(
                    src_hbm.at[pl.ds(base + (c + 1) * SC_CHUNK, SC_CHUNK)],
                    rows_v.at[1 - slot], rsem.at[1 - slot])
            writes[c] = (
                pltpu.async_copy(rows_v.at[slot], out_hbm.at[i0_v.at[c]], wsem.at[slot, 0]),
                pltpu.async_copy(rows_v.at[slot], out_hbm.at[i1_v.at[c]], wsem.at[slot, 1]),
            )
        for c in range(max(nch - 2, 0), nch):
            for wr in writes[c]:
                wr.wait()

    call = pl.kernel(
        body, mesh=_sc_mesh(),
        out_type=jax.ShapeDtypeStruct((rows_out, w), jnp.int32),
        scratch_types=[pltpu.VMEM((nch, SC_CHUNK), jnp.int32),
                       pltpu.VMEM((nch, SC_CHUNK), jnp.int32),
                       pltpu.VMEM((2, SC_CHUNK, w), jnp.int32),
                       pltpu.SemaphoreType.DMA((2,)), pltpu.SemaphoreType.DMA((2, 2))],
        name="sc_scatter_rows")
    shape3 = (SC_WORKERS, nch, SC_CHUNK)
    return call(src, idx0.reshape(shape3), idx1.reshape(shape3))


def _ada_kernel(c_ref, w_ref, b_ref, o_ref):
    ca = _silu(c_ref[...]).astype(BF16)
    w = w_ref[0].astype(BF16)
    o_ref[0] = jnp.dot(ca, w, preferred_element_type=F32) + b_ref[0]


def _ada_mod(c, w_ada, b_ada):
    depth, d, n6 = w_ada.shape
    b = c.shape[0]
    rows = 16
    c_pad = jnp.zeros((rows, d), F32).at[:b].set(c)
    tn = 1536
    out = pl.pallas_call(
        _ada_kernel,
        out_shape=jax.ShapeDtypeStruct((depth, rows, n6), F32),
        grid=(depth, n6 // tn),
        in_specs=[
            pl.BlockSpec((rows, d), lambda l, j: (0, 0)),
            pl.BlockSpec((1, d, tn), lambda l, j: (l, 0, j)),
            pl.BlockSpec((1, 1, tn), lambda l, j: (l, 0, j)),
        ],
        out_specs=pl.BlockSpec((1, rows, tn), lambda l, j: (l, 0, j)),
        compiler_params=_cparams(2),
        name="ada_mod",
    )(c_pad, w_ada, b_ada.reshape(depth, 1, n6))
    return out[:, :b]


MOD_SH1, MOD_SC1, MOD_G1, MOD_SH2, MOD_SC2, MOD_G2 = range(6)


def _mod_spec(layer, chunk, batch0=0):
    return pl.BlockSpec((None, 1, None, 1, D_MODEL),
                        lambda bi, *_: (layer, bi + batch0, chunk, 0, 0))


def _rms_mod(xf, g, sc, sh):
    ms = jnp.mean(xf * xf, axis=-1, keepdims=True)
    return xf * lax.rsqrt(ms + EPS) * g * (1.0 + sc) + sh


def _mix_in_kernel(x_ref, sc_ref, sh_ref, g_ref, w_ref, gq_ref, gk_ref, ones_ref,
                   z_ref, q_ref, k_ref, vt_ref, wbf_ref):
    first = jnp.logical_and(pl.program_id(0) == 0, pl.program_id(1) == 0)

    @pl.when(first)
    def _():
        wbf_ref[...] = w_ref[...].astype(BF16)

    h = _rms_mod(x_ref[0], g_ref[...], sc_ref[0], sh_ref[0]).astype(BF16)
    proj = jnp.dot(h, wbf_ref[...], preferred_element_type=F32)

    a = proj[:, :D_CONV]
    gate = proj[:, D_CONV:2 * D_CONV]
    z_ref[0] = (a * jax.nn.sigmoid(gate)).astype(BF16)

    def head_norm(t, g):
        sq = (t * t).astype(BF16)
        ss = jnp.concatenate(
            [jnp.dot(sq[:, c:c + ATT_W], ones_ref[...], preferred_element_type=F32)
             for c in range(0, D_ATT, ATT_W)], axis=1)
        return (t * lax.rsqrt(ss * (1.0 / HEAD_DIM) + EPS) * g).astype(BF16)

    o = 2 * D_CONV
    q_ref[0] = head_norm(proj[:, o:o + D_ATT], gq_ref[...])
    k_ref[0] = head_norm(proj[:, o + D_ATT:o + 2 * D_ATT], gk_ref[...])
    v = proj[:, o + 2 * D_ATT:]
    tm = v.shape[0]
    for cidx in range(tm // ATT_Q):
        vt_ref[0, cidx] = v[cidx * ATT_Q:(cidx + 1) * ATT_Q, :].T.astype(BF16)


def _mix_in(x, mod5, g, w_in, layer, gq, gk, tm):
    b, s, d = x.shape
    ones_bd = (jnp.arange(ATT_W)[:, None] // HEAD_DIM
               == jnp.arange(ATT_W)[None, :] // HEAD_DIM).astype(BF16)
    gq_t = (jnp.tile(gq, N_HEADS) * (HEAD_DIM ** -0.5)).reshape(1, D_ATT)
    gk_t = jnp.tile(gk, N_HEADS).reshape(1, D_ATT)
    row = lambda bi, ti: (bi, ti, 0)
    const2 = lambda bi, ti: (0, 0)
    return pl.pallas_call(
        _mix_in_kernel,
        out_shape=(
            jax.ShapeDtypeStruct((b, s, D_CONV), BF16),
            jax.ShapeDtypeStruct((b, s, D_ATT), BF16),
            jax.ShapeDtypeStruct((b, s, D_ATT), BF16),
            jax.ShapeDtypeStruct((b, s // ATT_Q, D_ATT, ATT_Q), BF16),
        ),
        grid=(b, s // tm),
        in_specs=[
            pl.BlockSpec((1, tm, d), row),
            _mod_spec(layer, MOD_SC1),
            _mod_spec(layer, MOD_SH1),
            pl.BlockSpec((1, d), const2),
            pl.BlockSpec((None, d, D_IN_COLS), lambda bi, ti: (layer, 0, 0),
                         pipeline_mode=pl.Buffered(1)),
            pl.BlockSpec((1, D_ATT), const2),
            pl.BlockSpec((1, D_ATT), const2),
            pl.BlockSpec((ATT_W, ATT_W), const2),
        ],
        out_specs=(
            pl.BlockSpec((1, tm, D_CONV), row),
            pl.BlockSpec((1, tm, D_ATT), row),
            pl.BlockSpec((1, tm, D_ATT), row),
            pl.BlockSpec((1, tm // ATT_Q, D_ATT, ATT_Q), lambda bi, ti: (bi, ti, 0, 0)),
        ),
        scratch_shapes=[pltpu.VMEM((d, D_IN_COLS), BF16)],
        compiler_params=_cparams(2),
        name="mix_in",
    )(x, mod5, mod5, g.reshape(1, d), w_in, gq_t, gk_t, ones_bd)


def _conv_kernel(zc_ref, zp_ref, w_ref, cb_ref, lg_ref, lb_ref, o_ref, win_ref, sh_ref,
                 acc_ref):
    tt = zc_ref.shape[1]
    t = pl.program_id(1)
    halo = zp_ref[0].astype(F32)
    win_ref[0:CONV_HALO, :] = jnp.where(t == 0, 0.0, halo)
    win_ref[CONV_HALO:, :] = zc_ref[0].astype(F32)
    span = tt + CONV_HALO - SUBLANES
    for sft in range(1, SUBLANES):
        sh_ref[sft - 1, 0:span, :] = win_ref[sft:sft + span, :]
    base = CONV_HALO - (CONV_WIDTH - 1)
    tiles = CONV_ROWS // SUBLANES

    for g in range(tt // CONV_ROWS):
        r = g * CONV_ROWS
        acc = jnp.zeros((tiles, SUBLANES, D_CONV), F32) + cb_ref[...]
        for j in range(CONV_WIDTH):
            whole, sft = divmod(base + j, SUBLANES)
            start = r + whole * SUBLANES
            if sft == 0:
                tap = win_ref[start:start + CONV_ROWS, :]
            else:
                tap = sh_ref[sft - 1, start:start + CONV_ROWS, :]
            acc = acc + tap.reshape(tiles, SUBLANES, D_CONV) * w_ref[j]
        acc_ref[r:r + CONV_ROWS, :] = acc.reshape(CONV_ROWS, D_CONV)
    acc = acc_ref[...]
    mu = jnp.mean(acc, axis=-1, keepdims=True)
    xc = acc - mu
    var = jnp.mean(xc * xc, axis=-1, keepdims=True)
    y = xc * lax.rsqrt(var + EPS) * lg_ref[...] + lb_ref[...]
    o_ref[0] = _silu(y).astype(BF16)


def _conv_branch(z, conv_w, conv_b, ln_g, ln_b, tt):
    b, s, c = z.shape
    hb = tt // CONV_HALO
    w_tiles = jnp.broadcast_to(conv_w.reshape(CONV_WIDTH, 1, c), (CONV_WIDTH, SUBLANES, c))
    const2 = lambda bi, ti: (0, 0)
    return pl.pallas_call(
        _conv_kernel,
        out_shape=jax.ShapeDtypeStruct((b, s, c), BF16),
        grid=(b, s // tt),
        in_specs=[
            pl.BlockSpec((1, tt, c), lambda bi, ti: (bi, ti, 0)),
            pl.BlockSpec((1, CONV_HALO, c),
                         lambda bi, ti: (bi, jnp.maximum(ti * hb - 1, 0), 0)),
            pl.BlockSpec((CONV_WIDTH, SUBLANES, c), lambda bi, ti: (0, 0, 0)),
            pl.BlockSpec((1, c), const2),
            pl.BlockSpec((1, c), const2),
            pl.BlockSpec((1, c), const2),
        ],
        out_specs=pl.BlockSpec((1, tt, c), lambda bi, ti: (bi, ti, 0)),
        scratch_shapes=[pltpu.VMEM((tt + CONV_HALO, c), F32),
                        pltpu.VMEM((SUBLANES - 1, tt + CONV_HALO, c), F32),
                        pltpu.VMEM((tt, c), F32)],
        compiler_params=_cparams(2),
        name="conv_branch",
    )(z, z, w_tiles, conv_b.reshape(1, c),
      ln_g.reshape(1, c), ln_b.reshape(1, c))


def _attn_kernel(q_ref, k_ref, vt_ref, bias_ref, o_ref, kpad_ref, vtpad_ref,
                 st0_ref, st1_ref, pb0_ref, pb1_ref, den0_ref, den1_ref):
    st_refs = (st0_ref, st1_ref)
    pb_refs = (pb0_ref, pb1_ref)
    den_refs = (den0_ref, den1_ref)
    s = q_ref.shape[1]
    npad = BAND_PAD // ATT_Q
    kpad_ref[0:BAND_PAD, :] = jnp.zeros((BAND_PAD, ATT_W), BF16)
    kpad_ref[BAND_PAD:, :] = k_ref[0]
    vtpad_ref[0:npad] = jnp.zeros((npad, ATT_W, ATT_Q), BF16)
    vtpad_ref[npad:] = vt_ref[0]

    iota = lambda shape, dim: lax.broadcasted_iota(jnp.int32, shape, dim)
    q_shift = ATT_Q.bit_length() - 1
    d_shift = HEAD_DIM.bit_length() - 1
    qb_mask = (iota((ATT_L, ATT_W), 0) >> q_shift) == (iota((ATT_L, ATT_W), 1) >> d_shift)
    ot_mask = (iota((ATT_W, ATT_L), 0) >> d_shift) == (iota((ATT_W, ATT_L), 1) >> q_shift)
    sel = jnp.where((iota((ATT_Q, ATT_L), 1) & (ATT_Q - 1)) == iota((ATT_Q, ATT_L), 0),
                    1.0, 0.0).astype(BF16)
    key_row = lax.broadcasted_iota(jnp.int32, (ATT_BAND, ATT_L), 0)
    contract_last = (((1,), (1,)), ((), ()))

    def scores(m, p):
        r0 = m * ATT_Q
        qt = q_ref[0, pl.ds(r0, ATT_Q), :]
        qb = jnp.where(qb_mask, jnp.concatenate([qt] * ATT_HEADS, axis=0), 0)
        kb = kpad_ref[pl.ds(r0, ATT_BAND), :]
        st_refs[p][...] = lax.dot_general(kb, qb.astype(BF16), contract_last,
                                          preferred_element_type=F32)

    def softmax(m, p, masked):
        st = st_refs[p][...] + bias_ref[0]
        if masked:
            st = jnp.where(key_row >= BAND_PAD - m * ATT_Q, st, NEG_INF)
        mx = jnp.max(st, axis=0, keepdims=True)
        e = jnp.exp(st - mx)
        den_refs[p][...] = jnp.sum(e, axis=0, keepdims=True)
        pb_refs[p][...] = e.astype(BF16)

    def values(m, p):
        r0 = m * ATT_Q
        vb = jnp.concatenate([vtpad_ref[m + c] for c in range(ATT_BAND // ATT_Q)],
                             axis=1)
        ot = jnp.dot(vb, pb_refs[p][...], preferred_element_type=F32)
        ot = jnp.where(ot_mask, ot / den_refs[p][...], 0.0).astype(BF16)
        y = lax.dot_general(sel, ot, contract_last, preferred_element_type=F32)
        o_ref[0, pl.ds(r0, ATT_Q), :] = y.astype(BF16)

    n = s // ATT_Q
    n_masked = BAND_PAD // ATT_Q
    for m in range(n + 2):
        if m < n:
            scores(m, m % 2)
        if 1 <= m <= n:
            softmax(m - 1, (m - 1) % 2, m - 1 < n_masked)
        if m >= 2:
            values(m - 2, m % 2)


def _attn_bias_t(rel_bias):
    rb = rel_bias.astype(F32)
    nu = ATT_BAND + ATT_Q - 1
    n_low = BAND_PAD - MAX_REL + ATT_Q
    t = jnp.concatenate([jnp.repeat(rb[:, :1], n_low, axis=1),
                         rb[:, 1:1 + nu - n_low]], axis=1)
    tp = jnp.pad(t, ((0, 0), (0, 1)))
    skew = jnp.tile(tp, (1, ATT_Q))[:, :ATT_Q * nu].reshape(N_HEADS, ATT_Q, nu)
    bias = skew[:, :, ATT_Q - 1:]
    bias = bias.reshape(ATT_GROUPS, ATT_HEADS, ATT_Q, ATT_BAND)
    bias = bias.transpose(0, 3, 1, 2).reshape(ATT_GROUPS, ATT_BAND, ATT_L)
    r = jnp.arange(ATT_BAND)[:, None]
    qq = jnp.arange(ATT_L)[None, :] % ATT_Q
    first = (qq // CHUNK) * CHUNK
    valid = (r >= first) & (r < first + BAND_PAD + CHUNK)
    return jnp.where(valid[None], bias, NEG_INF)


def _attention(qn, kn, vt, rel_bias):
    b, s, _ = qn.shape
    bias_t = _attn_bias_t(rel_bias)
    nck = s // ATT_Q
    assert nck >= 2, "the attention pipeline needs at least two query steps"
    return pl.pallas_call(
        _attn_kernel,
        out_shape=jax.ShapeDtypeStruct((b, s, D_ATT), BF16),
        grid=(b, ATT_GROUPS),
        in_specs=[
            pl.BlockSpec((1, s, ATT_W), lambda bi, gi: (bi, 0, gi)),
            pl.BlockSpec((1, s, ATT_W), lambda bi, gi: (bi, 0, gi)),
            pl.BlockSpec((1, nck, ATT_W, ATT_Q), lambda bi, gi: (bi, 0, gi, 0)),
            pl.BlockSpec((1, ATT_BAND, ATT_L), lambda bi, gi: (gi, 0, 0)),
        ],
        out_specs=pl.BlockSpec((1, s, ATT_W), lambda bi, gi: (bi, 0, gi)),
        scratch_shapes=[
            pltpu.VMEM((s + BAND_PAD, ATT_W), BF16),
            pltpu.VMEM((nck + BAND_PAD // ATT_Q, ATT_W, ATT_Q), BF16),
            pltpu.VMEM((ATT_BAND, ATT_L), F32), pltpu.VMEM((ATT_BAND, ATT_L), F32),
            pltpu.VMEM((ATT_BAND, ATT_L), BF16), pltpu.VMEM((ATT_BAND, ATT_L), BF16),
            pltpu.VMEM((1, ATT_L), F32), pltpu.VMEM((1, ATT_L), F32),
        ],
        compiler_params=_cparams(2),
        name="band_attention",
    )(qn, kn, vt, bias_t)


def _mix_out_kernel(x_ref, yc_ref, ya_ref, w_ref, g1_ref, gf_ref, sc_ref, sh_ref,
                    wr_ref, br_ref, x1_ref, h_ref, info_ref, info_t_ref, cnt_ref,
                    wbf_ref, carry_ref):
    first = jnp.logical_and(pl.program_id(0) == 0, pl.program_id(1) == 0)

    @pl.when(first)
    def _():
        wbf_ref[...] = w_ref[...].astype(BF16)
        carry_ref[...] = jnp.zeros_like(carry_ref)

    y = jnp.dot(yc_ref[0], wbf_ref[0:D_CONV, :], preferred_element_type=F32)
    y = y + jnp.dot(ya_ref[0], wbf_ref[D_CONV:, :], preferred_element_type=F32)
    x1 = x_ref[0] + g1_ref[0] * y
    x1_ref[0] = x1
    h = _rms_mod(x1, gf_ref[...], sc_ref[0], sh_ref[0])
    h_ref[0] = _pack_bf16_pairs(h)
    logits = jnp.dot(h.astype(BF16), wr_ref[...], preferred_element_type=F32) + br_ref[...]
    _route_tile(logits, info_ref, info_t_ref, cnt_ref, carry_ref)


def _mix_out_routed(x, yc, ya, w_out, layer, mod5, gf, w_router, b_router, tm):
    b, s, d = x.shape
    nt = s // tm
    row = lambda bi, ti: (bi, ti, 0)
    const2 = lambda bi, ti: (0, 0)
    w_pad = jnp.zeros((d, LANE_PAD_E), BF16).at[:, :N_EXPERTS].set(w_router.astype(BF16))
    b_pad = jnp.full((1, LANE_PAD_E), -jnp.inf, F32).at[0, :N_EXPERTS].set(
        b_router.astype(F32))
    return pl.pallas_call(
        _mix_out_kernel,
        out_shape=(jax.ShapeDtypeStruct((b, s, d), F32),
                   jax.ShapeDtypeStruct((b, s, d // 2), jnp.int32),
                   jax.ShapeDtypeStruct((b, s, LANE_PAD_E), F32),
                   jax.ShapeDtypeStruct((SUBLANES, b * s), F32),
                   jax.ShapeDtypeStruct((SUBLANES, LANE_PAD_E), F32)),
        grid=(b, nt),
        in_specs=[
            pl.BlockSpec((1, tm, d), row),
            pl.BlockSpec((1, tm, D_CONV), row),
            pl.BlockSpec((1, tm, D_ATT), row),
            pl.BlockSpec((None, d, d), lambda bi, ti: (layer, 0, 0),
                         pipeline_mode=pl.Buffered(1)),
            _mod_spec(layer, MOD_G1),
            pl.BlockSpec((1, d), const2),
            _mod_spec(layer, MOD_SC2),
            _mod_spec(layer, MOD_SH2),
            pl.BlockSpec((d, LANE_PAD_E), const2),
            pl.BlockSpec((1, LANE_PAD_E), const2),
        ],
        out_specs=(pl.BlockSpec((1, tm, d), row), pl.BlockSpec((1, tm, d // 2), row),
                   pl.BlockSpec((1, tm, LANE_PAD_E), row),
                   pl.BlockSpec((SUBLANES, tm), lambda bi, ti: (0, bi * nt + ti)),
                   pl.BlockSpec((SUBLANES, LANE_PAD_E), const2)),
        scratch_shapes=[pltpu.VMEM((d, d), BF16), pltpu.VMEM((SUBLANES, LANE_PAD_E), F32)],
        compiler_params=_cparams(2),
        name="mix_out",
    )(x, yc, ya, w_out, mod5, gf.reshape(1, d), mod5, mod5, w_pad, b_pad)


def _cast_ffn_kernel(wg_ref, wu_ref, wd_ref, wo_ref, wg3_ref, wu3_ref, wd3_ref, wob_ref):
    wg3_ref[0] = wg_ref[...].astype(BF16)
    wu3_ref[0] = wu_ref[...].astype(BF16)
    wd3_ref[0] = wd_ref[...].astype(BF16)

    @pl.when(pl.program_id(0) == 0)
    def _():
        wob_ref[...] = wo_ref[...].astype(BF16)


def _cast_ffn_weights(wg, wu, wd, w_out, layer, fc):
    d, ff = wg.shape
    nf = ff // fc
    return pl.pallas_call(
        _cast_ffn_kernel,
        out_shape=(jax.ShapeDtypeStruct((nf, d, fc), BF16),
                   jax.ShapeDtypeStruct((nf, d, fc), BF16),
                   jax.ShapeDtypeStruct((nf, fc, d), BF16),
                   jax.ShapeDtypeStruct((d, d), BF16)),
        grid=(nf,),
        in_specs=[
            pl.BlockSpec((d, fc), lambda f: (0, f)),
            pl.BlockSpec((d, fc), lambda f: (0, f)),
            pl.BlockSpec((fc, d), lambda f: (f, 0)),
            pl.BlockSpec((None, d, d), lambda f: (layer, 0, 0)),
        ],
        out_specs=(pl.BlockSpec((1, d, fc), lambda f: (f, 0, 0)),
                   pl.BlockSpec((1, d, fc), lambda f: (f, 0, 0)),
                   pl.BlockSpec((1, fc, d), lambda f: (f, 0, 0)),
                   pl.BlockSpec((d, d), lambda f: (0, 0))),
        compiler_params=_cparams(1),
        name="cast_ffn_weights",
    )(wg, wu, wd, w_out)


def _mix_ffn_kernel(x_ref, yc_ref, ya_ref, wo_ref, g1_ref, gf_ref, sc_ref, sh_ref, g2_ref,
                    wg_ref, wu_ref, wd_ref, o_ref, acc_ref):
    y = jnp.dot(yc_ref[0], wo_ref[0:D_CONV, :], preferred_element_type=F32)
    y = y + jnp.dot(ya_ref[0], wo_ref[D_CONV:, :], preferred_element_type=F32)
    x1 = x_ref[0] + g1_ref[0] * y
    h = _rms_mod(x1, gf_ref[...], sc_ref[0], sh_ref[0]).astype(BF16)
    for c in range(wg_ref.shape[0]):
        a = jnp.dot(h, wg_ref[c], preferred_element_type=F32)
        u = jnp.dot(h, wu_ref[c], preferred_element_type=F32)
        t = (_silu(a) * u).astype(BF16)
        dn = jnp.dot(t, wd_ref[c], preferred_element_type=F32)
        if c == 0:
            acc_ref[...] = dn
        else:
            acc_ref[...] += dn
    o_ref[0] = x1 + g2_ref[0] * acc_ref[...]


def _mix_out_dense_ffn(x, yc, ya, w_out, layer, mod5, gf, wg, wu, wd, tm, fc):
    b, s, d = x.shape
    wg3, wu3, wd3, wob = _cast_ffn_weights(wg, wu, wd, w_out, layer, fc)
    nf = wg3.shape[0]
    row = lambda bi, ti: (bi, ti, 0)
    const2 = lambda bi, ti: (0, 0)
    const3 = lambda bi, ti: (0, 0, 0)
    resident = pl.Buffered(1)
    return pl.pallas_call(
        _mix_ffn_kernel,
        out_shape=jax.ShapeDtypeStruct((b, s, d), F32),
        grid=(b, s // tm),
        in_specs=[
            pl.BlockSpec((1, tm, d), row),
            pl.BlockSpec((1, tm, D_CONV), row),
            pl.BlockSpec((1, tm, D_ATT), row),
            pl.BlockSpec((d, d), const2, pipeline_mode=resident),
            _mod_spec(layer, MOD_G1),
            pl.BlockSpec((1, d), const2),
            _mod_spec(layer, MOD_SC2),
            _mod_spec(layer, MOD_SH2),
            _mod_spec(layer, MOD_G2),
            pl.BlockSpec((nf, d, fc), const3, pipeline_mode=resident),
            pl.BlockSpec((nf, d, fc), const3, pipeline_mode=resident),
            pl.BlockSpec((nf, fc, d), const3, pipeline_mode=resident),
        ],
        out_specs=pl.BlockSpec((1, tm, d), row),
        scratch_shapes=[pltpu.VMEM((tm, d), F32)],
        compiler_params=_cparams(2),
        name="mix_out_dense_ffn",
    )(x, yc, ya, wob, mod5, gf.reshape(1, d), mod5, mod5, mod5, wg3, wu3, wd3)


def _route_tile(logits, info_ref, info_t_ref, cnt_ref, carry_ref):
    tr = logits.shape[0]
    lane = lax.broadcasted_iota(jnp.int32, (tr, LANE_PAD_E), 1).astype(F32)
    no_lane = float(LANE_PAD_E)
    v0 = jnp.max(logits, axis=-1, keepdims=True)
    i0 = jnp.min(jnp.where(logits == v0, lane, no_lane), axis=-1, keepdims=True)
    rest = jnp.where(lane == i0, -jnp.inf, logits)
    v1 = jnp.max(rest, axis=-1, keepdims=True)
    i1 = jnp.min(jnp.where(rest == v1, lane, no_lane), axis=-1, keepdims=True)
    e1 = jnp.exp(v1 - v0)
    w0 = 1.0 / (1.0 + e1)
    w1 = e1 / (1.0 + e1)
    oh0 = lane == i0
    oh1 = lane == i1
    cnt = jnp.where(jnp.logical_or(oh0, oh1), 1.0, 0.0)
    tri = (lax.broadcasted_iota(jnp.int32, (tr, tr), 1)
           < lax.broadcasted_iota(jnp.int32, (tr, tr), 0)).astype(BF16)
    before = jnp.dot(tri, cnt.astype(BF16), preferred_element_type=F32) + carry_ref[0:1, :]
    r0 = jnp.sum(jnp.where(oh0, before, 0.0), axis=-1, keepdims=True)
    r1 = jnp.sum(jnp.where(oh1, before, 0.0), axis=-1, keepdims=True)
    carry_ref[...] = carry_ref[...] + jnp.sum(cnt, axis=0, keepdims=True)
    cnt_ref[...] = carry_ref[...]
    info = jnp.where(lane == 0, i0, 0.0)
    info = jnp.where(lane == 1, i1, info)
    info = jnp.where(lane == 2, w0, info)
    info = jnp.where(lane == 3, w1, info)
    info = jnp.where(lane == 4, r0, info)
    info = jnp.where(lane == 5, r1, info)
    info_ref[0] = info
    for c in range(tr // LANES):
        blk = info[c * LANES:(c + 1) * LANES, :].T
        info_t_ref[:, c * LANES:(c + 1) * LANES] = blk[0:SUBLANES, :]


def _moe_kernel(te_ref, tn_ref, xs_ref, wg_ref, wu_ref, wd_ref, ys_ref,
                acc_ref, wgb_ref, wub_ref, wdb_ref):
    i = pl.program_id(0)
    f = pl.program_id(1)
    nhalf = tn_ref[i]
    nfull = nhalf // (MOE_SUB // MOE_HALF)

    @pl.when(f == 0)
    def _():
        acc_ref[...] = jnp.zeros_like(acc_ref)

    def block(r, rows, wgb, wub, wdb):
        xb = _unpack_bf16_pairs(xs_ref[pl.ds(r, rows), :]).astype(BF16)
        a = jnp.dot(xb, wgb, preferred_element_type=F32)
        u = jnp.dot(xb, wub, preferred_element_type=F32)
        t = (_silu(a) * u).astype(BF16)
        acc_ref[pl.ds(r, rows), :] += jnp.dot(t, wdb, preferred_element_type=F32)

    def split(units):
        out, r = [], 0
        for rows in (MOE_SUB,) * (units * MOE_HALF // MOE_SUB) + MOE_TAILS:
            if units * MOE_HALF - r >= rows:
                out.append((r, rows))
                r += rows
        return out

    for units in MOE_STATIC_UNITS:
        @pl.when(nhalf == units)
        def _(units=units):
            wgb, wub, wdb = (w[0].astype(BF16) for w in (wg_ref, wu_ref, wd_ref))
            for r, rows in split(units):
                block(r, rows, wgb, wub, wdb)

    other = nhalf > 0
    for units in MOE_STATIC_UNITS:
        other = jnp.logical_and(other, nhalf != units)

    @pl.when(other)
    def _():
        wgb_ref[...] = wg_ref[0].astype(BF16)
        wub_ref[...] = wu_ref[0].astype(BF16)
        wdb_ref[...] = wd_ref[0].astype(BF16)

        def sub(sidx, carry):
            block(pl.multiple_of(sidx * MOE_SUB, MOE_SUB), MOE_SUB,
                  wgb_ref[...], wub_ref[...], wdb_ref[...])
            return carry

        lax.fori_loop(0, nfull, sub, 0)
        done = nfull * MOE_SUB
        for rows in MOE_TAILS:
            units = rows // MOE_HALF

            @pl.when((nhalf & units) != 0)
            def _(rows=rows, units=units):
                higher = nhalf & (MOE_SUB // MOE_HALF - 1) & ~(2 * units - 1)
                block(pl.multiple_of(done + higher * MOE_HALF, MOE_HALF), rows,
                      wgb_ref[...], wub_ref[...], wdb_ref[...])

    @pl.when(f == pl.num_programs(1) - 1)
    def _():
        ys_ref[...] = _pack_bf16_pairs(acc_ref[...])


def _moe_ffn(xs, tile_e, tile_nsub, wg, wu, wd):
    rpad, dw = xs.shape
    d = 2 * dw
    ntiles = rpad // MOE_TILE
    ff = wg.shape[2]
    nf = ff // MOE_FC

    def fcol(i, f, tn):
        return jnp.where(tn[i] > 0, f, nf - 1)

    return pl.pallas_call(
        _moe_kernel,
        out_shape=jax.ShapeDtypeStruct((rpad, dw), jnp.int32),
        grid_spec=pltpu.PrefetchScalarGridSpec(
            num_scalar_prefetch=2,
            grid=(ntiles, nf),
            in_specs=[
                pl.BlockSpec((MOE_TILE, dw), lambda i, f, te, tn: (i, 0)),
                pl.BlockSpec((1, d, MOE_FC), lambda i, f, te, tn: (te[i], 0, fcol(i, f, tn))),
                pl.BlockSpec((1, d, MOE_FC), lambda i, f, te, tn: (te[i], 0, fcol(i, f, tn))),
                pl.BlockSpec((1, MOE_FC, d), lambda i, f, te, tn: (te[i], fcol(i, f, tn), 0)),
            ],
            out_specs=pl.BlockSpec((MOE_TILE, dw), lambda i, f, te, tn: (i, 0)),
            scratch_shapes=[
                pltpu.VMEM((MOE_TILE, d), F32),
                pltpu.VMEM((d, MOE_FC), BF16),
                pltpu.VMEM((d, MOE_FC), BF16),
                pltpu.VMEM((MOE_FC, d), BF16),
            ],
        ),
        compiler_params=_cparams(2),
        name="moe_ffn",
    )(tile_e, tile_nsub, xs, wg, wu, wd)


def _combine_kernel(x_ref, y0_ref, y1_ref, info_ref, g2_ref, *rest):
    o_ref = rest[-1]
    info = info_ref[0]
    w0 = info[:, 2:3]
    w1 = info[:, 3:4]
    f = w0 * _unpack_bf16_pairs(y0_ref[0, 0]) + w1 * _unpack_bf16_pairs(y1_ref[0, 0])
    o_ref[0] = x_ref[0] + g2_ref[0] * f


def _combine(x1, y01, info, mod5, layer, tm, batch0, prev=None):
    b, s, d = x1.shape
    nb = y01.shape[1]
    row = lambda bi, ti: (bi + batch0, ti, 0)
    in_specs = [
        pl.BlockSpec((1, tm, d), row),
        pl.BlockSpec((1, 1, tm, d // 2), lambda bi, ti: (0, bi, ti, 0)),
        pl.BlockSpec((1, 1, tm, d // 2), lambda bi, ti: (1, bi, ti, 0)),
        pl.BlockSpec((1, tm, LANE_PAD_E), row),
        _mod_spec(layer, MOD_G2, batch0),
    ]
    args = [x1, y01, y01, info, mod5]
    aliases = {}
    if prev is not None:
        in_specs.append(pl.BlockSpec(memory_space=pl.ANY))
        args.append(prev)
        aliases = {len(args) - 1: 0}
    return pl.pallas_call(
        _combine_kernel,
        out_shape=jax.ShapeDtypeStruct((b, s, d), F32),
        grid=(nb, s // tm),
        in_specs=in_specs,
        out_specs=pl.BlockSpec((1, tm, d), row),
        input_output_aliases=aliases,
        compiler_params=_cparams(2),
        name="moe_combine",
    )(*args)


def _moe_layer(hp, info, info_t, cnt, x1, mod5, layer, wg, wu, wd, tm):
    b, s, d = x1.shape
    n = b * s
    e0 = info_t[0].astype(jnp.int32)
    e1 = info_t[1].astype(jnp.int32)
    r0 = info_t[4].astype(jnp.int32)
    r1 = info_t[5].astype(jnp.int32)
    counts = cnt[0, :N_EXPERTS].astype(jnp.int32)

    ntiles = (2 * n) // MOE_TILE + N_EXPERTS
    tiles_per_e = (counts + MOE_TILE - 1) // MOE_TILE
    tile_end = jnp.cumsum(tiles_per_e)
    tile_start = tile_end - tiles_per_e
    total = tile_end[-1]
    tidx = jnp.arange(ntiles, dtype=jnp.int32)
    live = tidx < total
    tclip = jnp.minimum(tidx, total - 1)
    tile_e = jnp.minimum(jnp.sum(tclip[:, None] >= tile_end[None, :], axis=1),
                         N_EXPERTS - 1).astype(jnp.int32)
    rows_left = counts[tile_e] - (tclip - tile_start[tile_e]) * MOE_TILE
    rows_here = jnp.clip(rows_left, 0, MOE_TILE)
    tile_nsub = jnp.where(live, (rows_here + MOE_HALF - 1) // MOE_HALF, 0).astype(jnp.int32)

    row_start = tile_start * MOE_TILE
    eid = jnp.arange(N_EXPERTS, dtype=jnp.int32)[None, :]
    pos0 = jnp.sum(jnp.where(e0[:, None] == eid, row_start[None, :], 0), axis=1) + r0
    pos1 = jnp.sum(jnp.where(e1[:, None] == eid, row_start[None, :], 0), axis=1) + r1

    xs = _sc_scatter_rows2(hp.reshape(n, d // 2), pos0, pos1, ntiles * MOE_TILE)
    ys = _moe_ffn(xs, tile_e, tile_nsub, wg, wu, wd)
    nb = b // 2
    out = None
    for batch0 in (0, nb):
        tok = slice(batch0 * s, (batch0 + nb) * s)
        y01 = _sc_gather_rows(ys, jnp.concatenate([pos0[tok], pos1[tok]]))
        out = _combine(x1, y01.reshape(2, nb, s, d // 2), info, mod5, layer, tm, batch0, out)
    return out


def kernel(x, c, w_ada, b_ada, norm_mix_g, norm_ffn_g, w_in, w_out, conv_w, conv_b,
           conv_ln_g, conv_ln_b, q_norm_g, k_norm_g, rel_bias, ffn_w_gate, ffn_w_up,
           ffn_w_down, moe_w_router, moe_b_router, moe_w_gate, moe_w_up, moe_w_down):
    b, s, d = x.shape
    depth = w_ada.shape[0]
    tm = min(1024, s)
    mod5 = _ada_mod(c, w_ada, b_ada).reshape(depth, b, 6, 1, d)
    for l in range(depth):
        z, qn, kn, vt = _mix_in(x, mod5, norm_mix_g[l], w_in, l,
                                q_norm_g[l], k_norm_g[l], tm)
        yc = _conv_branch(z, conv_w[l], conv_b[l], conv_ln_g[l], conv_ln_b[l], tm)
        ya = _attention(qn, kn, vt, rel_bias[l])
        i = l // 2
        if l % 2 == 0:
            x = _mix_out_dense_ffn(x, yc, ya, w_out, l, mod5, norm_ffn_g[l],
                                   ffn_w_gate[i], ffn_w_up[i], ffn_w_down[i],
                                   min(1024, s), 256)
        else:
            x1, hp, info, info_t, cnt = _mix_out_routed(
                x, yc, ya, w_out, l, mod5, norm_ffn_g[l],
                moe_w_router[i], moe_b_router[i], tm)
            x = _moe_layer(hp, info, info_t, cnt, x1, mod5, l,
                           moe_w_gate[i], moe_w_up[i], moe_w_down[i], tm)
    return x
```

```python
import functools

import jax
import jax.numpy as jnp
from jax import lax
from jax.experimental import pallas as pl
from jax.experimental.pallas import tpu as pltpu
from jax.experimental.pallas import tpu_sc as plsc

F32 = jnp.float32
BF16 = jnp.bfloat16

D_MODEL = 1024
CHUNK = 64
N_PREV_CHUNKS = 8
BAND_PAD = N_PREV_CHUNKS * CHUNK
D_CONV = 512
D_ATT = 512
HEAD_DIM = 64
N_HEADS = 8
CONV_WIDTH = 31
MAX_REL = 128
D_IN_COLS = 2 * D_CONV + 3 * D_ATT
N_EXPERTS = 8
EPS = 1e-6
NEG_INF = -1e30

LANES = 128
SUBLANES = 8
VMEM_LIMIT_BYTES = 56 * 1024 * 1024

ATT_HEADS = 4
ATT_GROUPS = N_HEADS // ATT_HEADS
ATT_W = ATT_HEADS * HEAD_DIM
ATT_Q = 2 * CHUNK
ATT_BAND = BAND_PAD + ATT_Q
ATT_L = ATT_HEADS * ATT_Q

CONV_HALO = 32
CONV_ROWS = 32
LANE_PAD_E = LANES

MOE_HALF = 256
MOE_SUB = 4 * MOE_HALF
MOE_TAILS = (2 * MOE_HALF, MOE_HALF)
MOE_TILE = 9 * MOE_HALF
MOE_STATIC_UNITS = (9, 8, 7)
MOE_FC = 512


def _cparams(n_axes, vmem=VMEM_LIMIT_BYTES):
    return pltpu.CompilerParams(
        dimension_semantics=("arbitrary",) * n_axes, vmem_limit_bytes=vmem)


def _silu(v):
    return v * jax.nn.sigmoid(v)


def _pack_bf16_pairs(v):
    w = v.shape[1] // 2
    bits = lax.bitcast_convert_type(v.astype(BF16).astype(F32), jnp.uint32)
    packed = (bits[:, w:] & jnp.uint32(0xFFFF0000)) | (bits[:, :w] >> 16)
    return lax.bitcast_convert_type(packed, jnp.int32)


def _unpack_bf16_pairs(p):
    bits = lax.bitcast_convert_type(p, jnp.uint32)
    lo = lax.bitcast_convert_type(bits << 16, F32)
    hi = lax.bitcast_convert_type(bits & jnp.uint32(0xFFFF0000), F32)
    return jnp.concatenate([lo, hi], axis=1)


SC_CORES = 2
SC_SUBCORES = 16
SC_WORKERS = SC_CORES * SC_SUBCORES
SC_CHUNK = 64


def _sc_worker_id():
    return lax.axis_index("s") * SC_CORES + lax.axis_index("c")


def _sc_mesh():
    return plsc.VectorSubcoreMesh(core_axis_name="c", subcore_axis_name="s")


def _sc_gather_rows(table, idx):
    _, w = table.shape
    b = idx.shape[0]
    per_w = b // SC_WORKERS
    nch = per_w // SC_CHUNK

    def body(table_hbm, idx_hbm, out_hbm, idx_v, rows_v, gsem, wsem):
        wid = _sc_worker_id()
        base = wid * per_w
        pltpu.sync_copy(idx_hbm.at[wid], idx_v)
        gathers = [None] * nch
        writes = [None] * nch
        gathers[0] = pltpu.async_copy(table_hbm.at[idx_v.at[0]], rows_v.at[0], gsem.at[0])
        for c in range(nch):
            slot = c % 2
            gathers[c].wait()
            if c + 1 < nch:
                if c >= 1:
                    writes[c - 1].wait()
                gathers[c + 1] = pltpu.async_copy(
                    table_hbm.at[idx_v.at[c + 1]], rows_v.at[1 - slot], gsem.at[1 - slot])
            writes[c] = pltpu.async_copy(
                rows_v.at[slot], out_hbm.at[pl.ds(base + c * SC_CHUNK, SC_CHUNK)], wsem.at[slot])
        if nch >= 2:
            writes[nch - 2].wait()
        writes[nch - 1].wait()

    call = pl.kernel(
        body, mesh=_sc_mesh(),
        out_type=jax.ShapeDtypeStruct((b, w), jnp.int32),
        scratch_types=[pltpu.VMEM((nch, SC_CHUNK), jnp.int32),
                       pltpu.VMEM((2, SC_CHUNK, w), jnp.int32),
                       pltpu.SemaphoreType.DMA((2,)), pltpu.SemaphoreType.DMA((2,))],
        name="sc_gather_rows")
    return call(table, idx.reshape(SC_WORKERS, nch, SC_CHUNK))


def _sc_scatter_rows2(src, idx0, idx1, rows_out):
    n, w = src.shape
    per_w = n // SC_WORKERS
    nch = per_w // SC_CHUNK

    def body(src_hbm, i0_hbm, i1_hbm, out_hbm, i0_v, i1_v, rows_v, rsem, wsem):
        wid = _sc_worker_id()
        base = wid * per_w
        pltpu.sync_copy(i0_hbm.at[wid], i0_v)
        pltpu.sync_copy(i1_hbm.at[wid], i1_v)
        reads = [None] * nch
        writes = [None] * nch
        reads[0] = pltpu.async_copy(src_hbm.at[pl.ds(base, SC_CHUNK)], rows_v.at[0], rsem.at[0])
        for c in range(nch):
            slot = c % 2
            reads[c].wait()
            if c + 1 < nch:
                if c >= 1:
                    for wr in writes[c - 1]:
                        wr.wait()
                reads[c + 1] = pltpu.async_copy(
                    src_hbm.at[pl.ds(base + (c + 1) * SC_CHUNK, SC_CHUNK)],
                    rows_v.at[1 - slot], rsem.at[1 - slot])
            writes[c] = (
                pltpu.async_copy(rows_v.at[slot], out_hbm.at[i0_v.at[c]], wsem.at[slot, 0]),
                pltpu.async_copy(rows_v.at[slot], out_hbm.at[i1_v.at[c]], wsem.at[slot, 1]),
            )
        for c in range(max(nch - 2, 0), nch):
            for wr in writes[c]:
                wr.wait()

    call = pl.kernel(
        body, mesh=_sc_mesh(),
        out_type=jax.ShapeDtypeStruct((rows_out, w), jnp.int32),
        scratch_types=[pltpu.VMEM((nch, SC_CHUNK), jnp.int32),
                       pltpu.VMEM((nch, SC_CHUNK), jnp.int32),
                       pltpu.VMEM((2, SC_CHUNK, w), jnp.int32),
                       pltpu.SemaphoreType.DMA((2,)), pltpu.SemaphoreType.DMA((2, 2))],
        name="sc_scatter_rows")
    shape3 = (SC_WORKERS, nch, SC_CHUNK)
    return call(src, idx0.reshape(shape3), idx1.reshape(shape3))


def _ada_kernel(c_ref, w_ref, b_ref, o_ref):
    ca = _silu(c_ref[...]).astype(BF16)
    w = w_ref[0].astype(BF16)
    o_ref[0] = jnp.dot(ca, w, preferred_element_type=F32) + b_ref[0]


def _ada_mod(c, w_ada, b_ada):
    depth, d, n6 = w_ada.shape
    b = c.shape[0]
    rows = 16
    c_pad = jnp.zeros((rows, d), F32).at[:b].set(c)
    tn = 1536
    out = pl.pallas_call(
        _ada_kernel,
        out_shape=jax.ShapeDtypeStruct((depth, rows, n6), F32),
        grid=(depth, n6 // tn),
        in_specs=[
            pl.BlockSpec((rows, d), lambda l, j: (0, 0)),
            pl.BlockSpec((1, d, tn), lambda l, j: (l, 0, j)),
            pl.BlockSpec((1, 1, tn), lambda l, j: (l, 0, j)),
        ],
        out_specs=pl.BlockSpec((1, rows, tn), lambda l, j: (l, 0, j)),
        compiler_params=_cparams(2),
        name="ada_mod",
    )(c_pad, w_ada, b_ada.reshape(depth, 1, n6))
    return out[:, :b]


MOD_SH1, MOD_SC1, MOD_G1, MOD_SH2, MOD_SC2, MOD_G2 = range(6)


def _mod_spec(layer, chunk):
    return pl.BlockSpec((None, 1, None, 1, D_MODEL),
                        lambda bi, *_: (layer, bi, chunk, 0, 0))


def _rms_mod(xf, g, sc, sh):
    ms = jnp.mean(xf * xf, axis=-1, keepdims=True)
    return xf * lax.rsqrt(ms + EPS) * g * (1.0 + sc) + sh


def _mix_in_kernel(x_ref, sc_ref, sh_ref, g_ref, w_ref, gq_ref, gk_ref, ones_ref,
                   z_ref, q_ref, k_ref, vt_ref, wbf_ref):
    first = jnp.logical_and(pl.program_id(0) == 0, pl.program_id(1) == 0)

    @pl.when(first)
    def _():
        wbf_ref[...] = w_ref[...].astype(BF16)

    h = _rms_mod(x_ref[0], g_ref[...], sc_ref[0], sh_ref[0]).astype(BF16)
    proj = jnp.dot(h, wbf_ref[...], preferred_element_type=F32)

    a = proj[:, :D_CONV]
    gate = proj[:, D_CONV:2 * D_CONV]
    z_ref[0] = (a * jax.nn.sigmoid(gate)).astype(BF16)

    def head_norm(t, g):
        sq = (t * t).astype(BF16)
        ss = jnp.concatenate(
            [jnp.dot(sq[:, c:c + ATT_W], ones_ref[...], preferred_element_type=F32)
             for c in range(0, D_ATT, ATT_W)], axis=1)
        return (t * lax.rsqrt(ss * (1.0 / HEAD_DIM) + EPS) * g).astype(BF16)

    o = 2 * D_CONV
    q_ref[0] = head_norm(proj[:, o:o + D_ATT], gq_ref[...])
    k_ref[0] = head_norm(proj[:, o + D_ATT:o + 2 * D_ATT], gk_ref[...])
    v = proj[:, o + 2 * D_ATT:]
    tm = v.shape[0]
    for cidx in range(tm // ATT_Q):
        vt_ref[0, cidx] = v[cidx * ATT_Q:(cidx + 1) * ATT_Q, :].T.astype(BF16)


def _mix_in(x, mod5, g, w_in, layer, gq, gk, tm):
    b, s, d = x.shape
    ones_bd = (jnp.arange(ATT_W)[:, None] // HEAD_DIM
               == jnp.arange(ATT_W)[None, :] // HEAD_DIM).astype(BF16)
    gq_t = (jnp.tile(gq, N_HEADS) * (HEAD_DIM ** -0.5)).reshape(1, D_ATT)
    gk_t = jnp.tile(gk, N_HEADS).reshape(1, D_ATT)
    row = lambda bi, ti: (bi, ti, 0)
    const2 = lambda bi, ti: (0, 0)
    return pl.pallas_call(
        _mix_in_kernel,
        out_shape=(
            jax.ShapeDtypeStruct((b, s, D_CONV), BF16),
            jax.ShapeDtypeStruct((b, s, D_ATT), BF16),
            jax.ShapeDtypeStruct((b, s, D_ATT), BF16),
            jax.ShapeDtypeStruct((b, s // ATT_Q, D_ATT, ATT_Q), BF16),
        ),
        grid=(b, s // tm),
        in_specs=[
            pl.BlockSpec((1, tm, d), row),
            _mod_spec(layer, MOD_SC1),
            _mod_spec(layer, MOD_SH1),
            pl.BlockSpec((1, d), const2),
            pl.BlockSpec((None, d, D_IN_COLS), lambda bi, ti: (layer, 0, 0),
                         pipeline_mode=pl.Buffered(1)),
            pl.BlockSpec((1, D_ATT), const2),
            pl.BlockSpec((1, D_ATT), const2),
            pl.BlockSpec((ATT_W, ATT_W), const2),
        ],
        out_specs=(
            pl.BlockSpec((1, tm, D_CONV), row),
            pl.BlockSpec((1, tm, D_ATT), row),
            pl.BlockSpec((1, tm, D_ATT), row),
            pl.BlockSpec((1, tm // ATT_Q, D_ATT, ATT_Q), lambda bi, ti: (bi, ti, 0, 0)),
        ),
        scratch_shapes=[pltpu.VMEM((d, D_IN_COLS), BF16)],
        compiler_params=_cparams(2),
        name="mix_in",
    )(x, mod5, mod5, g.reshape(1, d), w_in, gq_t, gk_t, ones_bd)


def _conv_kernel(zc_ref, zp_ref, w_ref, cb_ref, lg_ref, lb_ref, o_ref, win_ref, sh_ref,
                 acc_ref):
    tt = zc_ref.shape[1]
    t = pl.program_id(1)
    halo = zp_ref[0].astype(F32)
    win_ref[0:CONV_HALO, :] = jnp.where(t == 0, 0.0, halo)
    win_ref[CONV_HALO:, :] = zc_ref[0].astype(F32)
    span = tt + CONV_HALO - SUBLANES
    for sft in range(1, SUBLANES):
        sh_ref[sft - 1, 0:span, :] = win_ref[sft:sft + span, :]
    base = CONV_HALO - (CONV_WIDTH - 1)
    tiles = CONV_ROWS // SUBLANES

    for g in range(tt // CONV_ROWS):
        r = g * CONV_ROWS
        acc = jnp.zeros((tiles, SUBLANES, D_CONV), F32) + cb_ref[...]
        for j in range(CONV_WIDTH):
            whole, sft = divmod(base + j, SUBLANES)
            start = r + whole * SUBLANES
            if sft == 0:
                tap = win_ref[start:start + CONV_ROWS, :]
            else:
                tap = sh_ref[sft - 1, start:start + CONV_ROWS, :]
            acc = acc + tap.reshape(tiles, SUBLANES, D_CONV) * w_ref[j]
        acc_ref[r:r + CONV_ROWS, :] = acc.reshape(CONV_ROWS, D_CONV)
    acc = acc_ref[...]
    mu = jnp.mean(acc, axis=-1, keepdims=True)
    xc = acc - mu
    var = jnp.mean(xc * xc, axis=-1, keepdims=True)
    y = xc * lax.rsqrt(var + EPS) * lg_ref[...] + lb_ref[...]
    o_ref[0] = _silu(y).astype(BF16)


def _conv_branch(z, conv_w, conv_b, ln_g, ln_b, tt):
    b, s, c = z.shape
    hb = tt // CONV_HALO
    w_tiles = jnp.broadcast_to(conv_w.reshape(CONV_WIDTH, 1, c), (CONV_WIDTH, SUBLANES, c))
    const2 = lambda bi, ti: (0, 0)
    return pl.pallas_call(
        _conv_kernel,
        out_shape=jax.ShapeDtypeStruct((b, s, c), BF16),
        grid=(b, s // tt),
        in_specs=[
            pl.BlockSpec((1, tt, c), lambda bi, ti: (bi, ti, 0)),
            pl.BlockSpec((1, CONV_HALO, c),
                         lambda bi, ti: (bi, jnp.maximum(ti * hb - 1, 0), 0)),
            pl.BlockSpec((CONV_WIDTH, SUBLANES, c), lambda bi, ti: (0, 0, 0)),
            pl.BlockSpec((1, c), const2),
            pl.BlockSpec((1, c), const2),
            pl.BlockSpec((1, c), const2),
        ],
        out_specs=pl.BlockSpec((1, tt, c), lambda bi, ti: (bi, ti, 0)),
        scratch_shapes=[pltpu.VMEM((tt + CONV_HALO, c), F32),
                        pltpu.VMEM((SUBLANES - 1, tt + CONV_HALO, c), F32),
                        pltpu.VMEM((tt, c), F32)],
        compiler_params=_cparams(2),
        name="conv_branch",
    )(z, z, w_tiles, conv_b.reshape(1, c),
      ln_g.reshape(1, c), ln_b.reshape(1, c))


def _attn_kernel(q_ref, k_ref, vt_ref, bias_ref, o_ref, kpad_ref, vtpad_ref,
                 st0_ref, st1_ref, pb0_ref, pb1_ref, den0_ref, den1_ref):
    st_refs = (st0_ref, st1_ref)
    pb_refs = (pb0_ref, pb1_ref)
    den_refs = (den0_ref, den1_ref)
    s = q_ref.shape[1]
    npad = BAND_PAD // ATT_Q
    kpad_ref[0:BAND_PAD, :] = jnp.zeros((BAND_PAD, ATT_W), BF16)
    kpad_ref[BAND_PAD:, :] = k_ref[0]
    vtpad_ref[0:npad] = jnp.zeros((npad, ATT_W, ATT_Q), BF16)
    vtpad_ref[npad:] = vt_ref[0]

    iota = lambda shape, dim: lax.broadcasted_iota(jnp.int32, shape, dim)
    q_shift = ATT_Q.bit_length() - 1
    d_shift = HEAD_DIM.bit_length() - 1
    qb_mask = (iota((ATT_L, ATT_W), 0) >> q_shift) == (iota((ATT_L, ATT_W), 1) >> d_shift)
    ot_mask = (iota((ATT_W, ATT_L), 0) >> d_shift) == (iota((ATT_W, ATT_L), 1) >> q_shift)
    sel = jnp.where((iota((ATT_Q, ATT_L), 1) & (ATT_Q - 1)) == iota((ATT_Q, ATT_L), 0),
                    1.0, 0.0).astype(BF16)
    key_row = lax.broadcasted_iota(jnp.int32, (ATT_BAND, ATT_L), 0)
    contract_last = (((1,), (1,)), ((), ()))

    def scores(m, p):
        r0 = m * ATT_Q
        qt = q_ref[0, pl.ds(r0, ATT_Q), :]
        qb = jnp.where(qb_mask, jnp.concatenate([qt] * ATT_HEADS, axis=0), 0)
        kb = kpad_ref[pl.ds(r0, ATT_BAND), :]
        st_refs[p][...] = lax.dot_general(kb, qb.astype(BF16), contract_last,
                                          preferred_element_type=F32)

    def softmax(m, p, masked):
        st = st_refs[p][...] + bias_ref[0]
        if masked:
            st = jnp.where(key_row >= BAND_PAD - m * ATT_Q, st, NEG_INF)
        mx = jnp.max(st, axis=0, keepdims=True)
        e = jnp.exp(st - mx)
        den_refs[p][...] = jnp.sum(e, axis=0, keepdims=True)
        pb_refs[p][...] = e.astype(BF16)

    def values(m, p):
        r0 = m * ATT_Q
        vb = jnp.concatenate([vtpad_ref[m + c] for c in range(ATT_BAND // ATT_Q)],
                             axis=1)
        ot = jnp.dot(vb, pb_refs[p][...], preferred_element_type=F32)
        ot = jnp.where(ot_mask, ot / den_refs[p][...], 0.0).astype(BF16)
        y = lax.dot_general(sel, ot, contract_last, preferred_element_type=F32)
        o_ref[0, pl.ds(r0, ATT_Q), :] = y.astype(BF16)

    n = s // ATT_Q
    n_masked = BAND_PAD // ATT_Q
    for m in range(n + 2):
        if m < n:
            scores(m, m % 2)
        if 1 <= m <= n:
            softmax(m - 1, (m - 1) % 2, m - 1 < n_masked)
        if m >= 2:
            values(m - 2, m % 2)


def _attn_bias_t(rel_bias):
    rb = rel_bias.astype(F32).reshape(-1, rel_bias.shape[-1])
    n_rows = rb.shape[0]
    nu = ATT_BAND + ATT_Q - 1
    n_low = BAND_PAD - MAX_REL + ATT_Q
    t = jnp.concatenate([jnp.repeat(rb[:, :1], n_low, axis=1),
                         rb[:, 1:1 + nu - n_low]], axis=1)
    tp = jnp.pad(t, ((0, 0), (0, 1)))
    flat = jnp.tile(tp, (1, ATT_Q + 1))[:, ATT_Q - 1:ATT_Q - 1 + ATT_Q * nu]
    bias = flat.reshape(n_rows, ATT_Q, nu)[:, :, :ATT_BAND]
    bias = bias.reshape(n_rows // ATT_HEADS, ATT_HEADS, ATT_Q, ATT_BAND)
    bias = bias.transpose(0, 3, 1, 2).reshape(n_rows // ATT_HEADS, ATT_BAND, ATT_L)
    r = jnp.arange(ATT_BAND)[:, None]
    qq = jnp.arange(ATT_L)[None, :] % ATT_Q
    first = (qq // CHUNK) * CHUNK
    valid = (r >= first) & (r < first + BAND_PAD + CHUNK)
    return jnp.where(valid[None], bias, NEG_INF)


def _attention(qn, kn, vt, bias_t, layer):
    b, s, _ = qn.shape
    nck = s // ATT_Q
    assert nck >= 2, "the attention pipeline needs at least two query steps"
    return pl.pallas_call(
        _attn_kernel,
        out_shape=jax.ShapeDtypeStruct((b, s, D_ATT), BF16),
        grid=(b, ATT_GROUPS),
        in_specs=[
            pl.BlockSpec((1, s, ATT_W), lambda bi, gi: (bi, 0, gi)),
            pl.BlockSpec((1, s, ATT_W), lambda bi, gi: (bi, 0, gi)),
            pl.BlockSpec((1, nck, ATT_W, ATT_Q), lambda bi, gi: (bi, 0, gi, 0)),
            pl.BlockSpec((1, ATT_BAND, ATT_L), lambda bi, gi: (layer * ATT_GROUPS + gi, 0, 0)),
        ],
        out_specs=pl.BlockSpec((1, s, ATT_W), lambda bi, gi: (bi, 0, gi)),
        scratch_shapes=[
            pltpu.VMEM((s + BAND_PAD, ATT_W), BF16),
            pltpu.VMEM((nck + BAND_PAD // ATT_Q, ATT_W, ATT_Q), BF16),
            pltpu.VMEM((ATT_BAND, ATT_L), F32), pltpu.VMEM((ATT_BAND, ATT_L), F32),
            pltpu.VMEM((ATT_BAND, ATT_L), BF16), pltpu.VMEM((ATT_BAND, ATT_L), BF16),
            pltpu.VMEM((1, ATT_L), F32), pltpu.VMEM((1, ATT_L), F32),
        ],
        compiler_params=_cparams(2),
        name="band_attention",
    )(qn, kn, vt, bias_t)


def _mix_out_kernel(x_ref, yc_ref, ya_ref, w_ref, g1_ref, gf_ref, sc_ref, sh_ref,
                    wr_ref, br_ref, x1_ref, h_ref, info_ref, info_t_ref, cnt_ref,
                    wbf_ref, carry_ref):
    first = jnp.logical_and(pl.program_id(0) == 0, pl.program_id(1) == 0)

    @pl.when(first)
    def _():
        wbf_ref[...] = w_ref[...].astype(BF16)
        carry_ref[...] = jnp.zeros_like(carry_ref)

    y = jnp.dot(yc_ref[0], wbf_ref[0:D_CONV, :], preferred_element_type=F32)
    y = y + jnp.dot(ya_ref[0], wbf_ref[D_CONV:, :], preferred_element_type=F32)
    x1 = x_ref[0] + g1_ref[0] * y
    x1_ref[0] = x1
    h = _rms_mod(x1, gf_ref[...], sc_ref[0], sh_ref[0])
    h_ref[0] = _pack_bf16_pairs(h)
    logits = jnp.dot(h.astype(BF16), wr_ref[...], preferred_element_type=F32) + br_ref[...]
    _route_tile(logits, info_ref, info_t_ref, cnt_ref, carry_ref)


def _mix_out_routed(x, yc, ya, w_out, layer, mod5, gf, w_router, b_router, tm):
    b, s, d = x.shape
    nt = s // tm
    row = lambda bi, ti: (bi, ti, 0)
    const2 = lambda bi, ti: (0, 0)
    w_pad = jnp.zeros((d, LANE_PAD_E), BF16).at[:, :N_EXPERTS].set(w_router.astype(BF16))
    b_pad = jnp.full((1, LANE_PAD_E), -jnp.inf, F32).at[0, :N_EXPERTS].set(
        b_router.astype(F32))
    return pl.pallas_call(
        _mix_out_kernel,
        out_shape=(jax.ShapeDtypeStruct((b, s, d), F32),
                   jax.ShapeDtypeStruct((b, s, d // 2), jnp.int32),
                   jax.ShapeDtypeStruct((b, s, LANE_PAD_E), F32),
                   jax.ShapeDtypeStruct((SUBLANES, b * s), F32),
                   jax.ShapeDtypeStruct((SUBLANES, LANE_PAD_E), F32)),
        grid=(b, nt),
        in_specs=[
            pl.BlockSpec((1, tm, d), row),
            pl.BlockSpec((1, tm, D_CONV), row),
            pl.BlockSpec((1, tm, D_ATT), row),
            pl.BlockSpec((None, d, d), lambda bi, ti: (layer, 0, 0),
                         pipeline_mode=pl.Buffered(1)),
            _mod_spec(layer, MOD_G1),
            pl.BlockSpec((1, d), const2),
            _mod_spec(layer, MOD_SC2),
            _mod_spec(layer, MOD_SH2),
            pl.BlockSpec((d, LANE_PAD_E), const2),
            pl.BlockSpec((1, LANE_PAD_E), const2),
        ],
        out_specs=(pl.BlockSpec((1, tm, d), row), pl.BlockSpec((1, tm, d // 2), row),
                   pl.BlockSpec((1, tm, LANE_PAD_E), row),
                   pl.BlockSpec((SUBLANES, tm), lambda bi, ti: (0, bi * nt + ti)),
                   pl.BlockSpec((SUBLANES, LANE_PAD_E), const2)),
        scratch_shapes=[pltpu.VMEM((d, d), BF16), pltpu.VMEM((SUBLANES, LANE_PAD_E), F32)],
        compiler_params=_cparams(2),
        name="mix_out",
    )(x, yc, ya, w_out, mod5, gf.reshape(1, d), mod5, mod5, w_pad, b_pad)


def _cast_ffn_kernel(wg_ref, wu_ref, wd_ref, wo_ref, wg3_ref, wu3_ref, wd3_ref, wob_ref):
    wg3_ref[0] = wg_ref[...].astype(BF16)
    wu3_ref[0] = wu_ref[...].astype(BF16)
    wd3_ref[0] = wd_ref[...].astype(BF16)

    @pl.when(pl.program_id(0) == 0)
    def _():
        wob_ref[...] = wo_ref[...].astype(BF16)


def _cast_ffn_weights(wg, wu, wd, w_out, layer, fc):
    d, ff = wg.shape
    nf = ff // fc
    return pl.pallas_call(
        _cast_ffn_kernel,
        out_shape=(jax.ShapeDtypeStruct((nf, d, fc), BF16),
                   jax.ShapeDtypeStruct((nf, d, fc), BF16),
                   jax.ShapeDtypeStruct((nf, fc, d), BF16),
                   jax.ShapeDtypeStruct((d, d), BF16)),
        grid=(nf,),
        in_specs=[
            pl.BlockSpec((d, fc), lambda f: (0, f)),
            pl.BlockSpec((d, fc), lambda f: (0, f)),
            pl.BlockSpec((fc, d), lambda f: (f, 0)),
            pl.BlockSpec((None, d, d), lambda f: (layer, 0, 0)),
        ],
        out_specs=(pl.BlockSpec((1, d, fc), lambda f: (f, 0, 0)),
                   pl.BlockSpec((1, d, fc), lambda f: (f, 0, 0)),
                   pl.BlockSpec((1, fc, d), lambda f: (f, 0, 0)),
                   pl.BlockSpec((d, d), lambda f: (0, 0))),
        compiler_params=_cparams(1),
        name="cast_ffn_weights",
    )(wg, wu, wd, w_out)


def _mix_ffn_kernel(x_ref, yc_ref, ya_ref, wo_ref, g1_ref, gf_ref, sc_ref, sh_ref, g2_ref,
                    wg_ref, wu_ref, wd_ref, o_ref, acc_ref):
    y = jnp.dot(yc_ref[0], wo_ref[0:D_CONV, :], preferred_element_type=F32)
    y = y + jnp.dot(ya_ref[0], wo_ref[D_CONV:, :], preferred_element_type=F32)
    x1 = x_ref[0] + g1_ref[0] * y
    h = _rms_mod(x1, gf_ref[...], sc_ref[0], sh_ref[0]).astype(BF16)
    for c in range(wg_ref.shape[0]):
        a = jnp.dot(h, wg_ref[c], preferred_element_type=F32)
        u = jnp.dot(h, wu_ref[c], preferred_element_type=F32)
        t = (_silu(a) * u).astype(BF16)
        dn = jnp.dot(t, wd_ref[c], preferred_element_type=F32)
        if c == 0:
            acc_ref[...] = dn
        else:
            acc_ref[...] += dn
    o_ref[0] = x1 + g2_ref[0] * acc_ref[...]


def _mix_out_dense_ffn(x, yc, ya, w_out, layer, mod5, gf, wg, wu, wd, tm, fc):
    b, s, d = x.shape
    wg3, wu3, wd3, wob = _cast_ffn_weights(wg, wu, wd, w_out, layer, fc)
    nf = wg3.shape[0]
    row = lambda bi, ti: (bi, ti, 0)
    const2 = lambda bi, ti: (0, 0)
    const3 = lambda bi, ti: (0, 0, 0)
    resident = pl.Buffered(1)
    return pl.pallas_call(
        _mix_ffn_kernel,
        out_shape=jax.ShapeDtypeStruct((b, s, d), F32),
        grid=(b, s // tm),
        in_specs=[
            pl.BlockSpec((1, tm, d), row),
            pl.BlockSpec((1, tm, D_CONV), row),
            pl.BlockSpec((1, tm, D_ATT), row),
            pl.BlockSpec((d, d), const2, pipeline_mode=resident),
            _mod_spec(layer, MOD_G1),
            pl.BlockSpec((1, d), const2),
            _mod_spec(layer, MOD_SC2),
            _mod_spec(layer, MOD_SH2),
            _mod_spec(layer, MOD_G2),
            pl.BlockSpec((nf, d, fc), const3, pipeline_mode=resident),
            pl.BlockSpec((nf, d, fc), const3, pipeline_mode=resident),
            pl.BlockSpec((nf, fc, d), const3, pipeline_mode=resident),
        ],
        out_specs=pl.BlockSpec((1, tm, d), row),
        scratch_shapes=[pltpu.VMEM((tm, d), F32)],
        compiler_params=_cparams(2),
        name="mix_out_dense_ffn",
    )(x, yc, ya, wob, mod5, gf.reshape(1, d), mod5, mod5, mod5, wg3, wu3, wd3)


def _route_tile(logits, info_ref, info_t_ref, cnt_ref, carry_ref):
    tr = logits.shape[0]
    lane = lax.broadcasted_iota(jnp.int32, (tr, LANE_PAD_E), 1).astype(F32)
    no_lane = float(LANE_PAD_E)
    v0 = jnp.max(logits, axis=-1, keepdims=True)
    i0 = jnp.min(jnp.where(logits == v0, lane, no_lane), axis=-1, keepdims=True)
    rest = jnp.where(lane == i0, -jnp.inf, logits)
    v1 = jnp.max(rest, axis=-1, keepdims=True)
    i1 = jnp.min(jnp.where(rest == v1, lane, no_lane), axis=-1, keepdims=True)
    e1 = jnp.exp(v1 - v0)
    w0 = 1.0 / (1.0 + e1)
    w1 = e1 / (1.0 + e1)
    oh0 = lane == i0
    oh1 = lane == i1
    cnt = jnp.where(jnp.logical_or(oh0, oh1), 1.0, 0.0)
    tri = (lax.broadcasted_iota(jnp.int32, (tr, tr), 1)
           < lax.broadcasted_iota(jnp.int32, (tr, tr), 0)).astype(BF16)
    before = jnp.dot(tri, cnt.astype(BF16), preferred_element_type=F32) + carry_ref[0:1, :]
    r0 = jnp.sum(jnp.where(oh0, before, 0.0), axis=-1, keepdims=True)
    r1 = jnp.sum(jnp.where(oh1, before, 0.0), axis=-1, keepdims=True)
    carry_ref[...] = carry_ref[...] + jnp.sum(cnt, axis=0, keepdims=True)
    cnt_ref[...] = carry_ref[...]
    info = jnp.where(lane == 0, i0, 0.0)
    info = jnp.where(lane == 1, i1, info)
    info = jnp.where(lane == 2, w0, info)
    info = jnp.where(lane == 3, w1, info)
    info = jnp.where(lane == 4, r0, info)
    info = jnp.where(lane == 5, r1, info)
    info_ref[0] = info
    for c in range(tr // LANES):
        blk = info[c * LANES:(c + 1) * LANES, :].T
        info_t_ref[:, c * LANES:(c + 1) * LANES] = blk[0:SUBLANES, :]


def _moe_kernel(te_ref, tn_ref, xs_ref, wg_ref, wu_ref, wd_ref, ys_ref,
                acc_ref, wgb_ref, wub_ref, wdb_ref):
    i = pl.program_id(0)
    f = pl.program_id(1)
    nhalf = tn_ref[i]
    nfull = nhalf // (MOE_SUB // MOE_HALF)

    @pl.when(f == 0)
    def _():
        acc_ref[...] = jnp.zeros_like(acc_ref)

    def block(r, rows, wgb, wub, wdb):
        xb = _unpack_bf16_pairs(xs_ref[pl.ds(r, rows), :]).astype(BF16)
        a = jnp.dot(xb, wgb, preferred_element_type=F32)
        u = jnp.dot(xb, wub, preferred_element_type=F32)
        t = (_silu(a) * u).astype(BF16)
        acc_ref[pl.ds(r, rows), :] += jnp.dot(t, wdb, preferred_element_type=F32)

    def split(units):
        out, r = [], 0
        for rows in (MOE_SUB,) * (units * MOE_HALF // MOE_SUB) + MOE_TAILS:
            if units * MOE_HALF - r >= rows:
                out.append((r, rows))
                r += rows
        return out

    for units in MOE_STATIC_UNITS:
        @pl.when(nhalf == units)
        def _(units=units):
            wgb, wub, wdb = (w[0].astype(BF16) for w in (wg_ref, wu_ref, wd_ref))
            for r, rows in split(units):
                block(r, rows, wgb, wub, wdb)

    other = nhalf > 0
    for units in MOE_STATIC_UNITS:
        other = jnp.logical_and(other, nhalf != units)

    @pl.when(other)
    def _():
        wgb_ref[...] = wg_ref[0].astype(BF16)
        wub_ref[...] = wu_ref[0].astype(BF16)
        wdb_ref[...] = wd_ref[0].astype(BF16)

        def sub(sidx, carry):
            block(pl.multiple_of(sidx * MOE_SUB, MOE_SUB), MOE_SUB,
                  wgb_ref[...], wub_ref[...], wdb_ref[...])
            return carry

        lax.fori_loop(0, nfull, sub, 0)
        done = nfull * MOE_SUB
        for rows in MOE_TAILS:
            units = rows // MOE_HALF

            @pl.when((nhalf & units) != 0)
            def _(rows=rows, units=units):
                higher = nhalf & (MOE_SUB // MOE_HALF - 1) & ~(2 * units - 1)
                block(pl.multiple_of(done + higher * MOE_HALF, MOE_HALF), rows,
                      wgb_ref[...], wub_ref[...], wdb_ref[...])

    @pl.when(f == pl.num_programs(1) - 1)
    def _():
        ys_ref[...] = _pack_bf16_pairs(acc_ref[...])


def _moe_ffn(xs, tile_e, tile_nsub, wg, wu, wd):
    rpad, dw = xs.shape
    d = 2 * dw
    ntiles = rpad // MOE_TILE
    ff = wg.shape[2]
    nf = ff // MOE_FC

    def fcol(i, f, tn):
        return jnp.where(tn[i] > 0, f, nf - 1)

    return pl.pallas_call(
        _moe_kernel,
        out_shape=jax.ShapeDtypeStruct((rpad, dw), jnp.int32),
        grid_spec=pltpu.PrefetchScalarGridSpec(
            num_scalar_prefetch=2,
            grid=(ntiles, nf),
            in_specs=[
                pl.BlockSpec((MOE_TILE, dw), lambda i, f, te, tn: (i, 0)),
                pl.BlockSpec((1, d, MOE_FC), lambda i, f, te, tn: (te[i], 0, fcol(i, f, tn))),
                pl.BlockSpec((1, d, MOE_FC), lambda i, f, te, tn: (te[i], 0, fcol(i, f, tn))),
                pl.BlockSpec((1, MOE_FC, d), lambda i, f, te, tn: (te[i], fcol(i, f, tn), 0)),
            ],
            out_specs=pl.BlockSpec((MOE_TILE, dw), lambda i, f, te, tn: (i, 0)),
            scratch_shapes=[
                pltpu.VMEM((MOE_TILE, d), F32),
                pltpu.VMEM((d, MOE_FC), BF16),
                pltpu.VMEM((d, MOE_FC), BF16),
                pltpu.VMEM((MOE_FC, d), BF16),
            ],
        ),
        compiler_params=_cparams(2),
        name="moe_ffn",
    )(tile_e, tile_nsub, xs, wg, wu, wd)


def _combine_kernel(x_ref, y0_ref, y1_ref, info_ref, g2_ref, o_ref):
    info = info_ref[0]
    w0 = info[:, 2:3]
    w1 = info[:, 3:4]
    f = w0 * _unpack_bf16_pairs(y0_ref[0, 0]) + w1 * _unpack_bf16_pairs(y1_ref[0, 0])
    o_ref[0] = x_ref[0] + g2_ref[0] * f


def _combine(x1, y01, info, mod5, layer, tm):
    b, s, d = x1.shape
    row = lambda bi, ti: (bi, ti, 0)
    return pl.pallas_call(
        _combine_kernel,
        out_shape=jax.ShapeDtypeStruct((b, s, d), F32),
        grid=(b, s // tm),
        in_specs=[
            pl.BlockSpec((1, tm, d), row),
            pl.BlockSpec((1, 1, tm, d // 2), lambda bi, ti: (0, bi, ti, 0)),
            pl.BlockSpec((1, 1, tm, d // 2), lambda bi, ti: (1, bi, ti, 0)),
            pl.BlockSpec((1, tm, LANE_PAD_E), row),
            _mod_spec(layer, MOD_G2),
        ],
        out_specs=pl.BlockSpec((1, tm, d), row),
        compiler_params=_cparams(2),
        name="moe_combine",
    )(x1, y01, y01, info, mod5)


def _moe_layer(hp, info, info_t, cnt, x1, mod5, layer, wg, wu, wd, tm):
    b, s, d = x1.shape
    n = b * s
    e0 = info_t[0].astype(jnp.int32)
    e1 = info_t[1].astype(jnp.int32)
    r0 = info_t[4].astype(jnp.int32)
    r1 = info_t[5].astype(jnp.int32)
    counts = cnt[0, :N_EXPERTS].astype(jnp.int32)

    ntiles = (2 * n) // MOE_TILE + N_EXPERTS
    tiles_per_e = (counts + MOE_TILE - 1) // MOE_TILE
    tile_end = jnp.cumsum(tiles_per_e)
    tile_start = tile_end - tiles_per_e
    total = tile_end[-1]
    tidx = jnp.arange(ntiles, dtype=jnp.int32)
    live = tidx < total
    tclip = jnp.minimum(tidx, total - 1)
    tile_e = jnp.minimum(jnp.sum(tclip[:, None] >= tile_end[None, :], axis=1),
                         N_EXPERTS - 1).astype(jnp.int32)
    rows_left = counts[tile_e] - (tclip - tile_start[tile_e]) * MOE_TILE
    rows_here = jnp.clip(rows_left, 0, MOE_TILE)
    tile_nsub = jnp.where(live, (rows_here + MOE_HALF - 1) // MOE_HALF, 0).astype(jnp.int32)

    row_start = tile_start * MOE_TILE
    eid = jnp.arange(N_EXPERTS, dtype=jnp.int32)[None, :]
    pos0 = jnp.sum(jnp.where(e0[:, None] == eid, row_start[None, :], 0), axis=1) + r0
    pos1 = jnp.sum(jnp.where(e1[:, None] == eid, row_start[None, :], 0), axis=1) + r1

    xs = _sc_scatter_rows2(hp.reshape(n, d // 2), pos0, pos1, ntiles * MOE_TILE)
    ys = _moe_ffn(xs, tile_e, tile_nsub, wg, wu, wd)
    y01 = _sc_gather_rows(ys, jnp.concatenate([pos0, pos1]))
    return _combine(x1, y01.reshape(2, b, s, d // 2), info, mod5, layer, tm)


def kernel(x, c, w_ada, b_ada, norm_mix_g, norm_ffn_g, w_in, w_out, conv_w, conv_b,
           conv_ln_g, conv_ln_b, q_norm_g, k_norm_g, rel_bias, ffn_w_gate, ffn_w_up,
           ffn_w_down, moe_w_router, moe_b_router, moe_w_gate, moe_w_up, moe_w_down):
    b, s, d = x.shape
    depth = w_ada.shape[0]
    tm = min(1024, s)
    mod5 = _ada_mod(c, w_ada, b_ada).reshape(depth, b, 6, 1, d)
    bias_t = _attn_bias_t(rel_bias)
    for l in range(depth):
        z, qn, kn, vt = _mix_in(x, mod5, norm_mix_g[l], w_in, l,
                                q_norm_g[l], k_norm_g[l], tm)
        yc = _conv_branch(z, conv_w[l], conv_b[l], conv_ln_g[l], conv_ln_b[l], tm)
        ya = _attention(qn, kn, vt, bias_t, l)
        i = l // 2
        if l % 2 == 0:
            x = _mix_out_dense_ffn(x, yc, ya, w_out, l, mod5, norm_ffn_g[l],
                                   ffn_w_gate[i], ffn_w_up[i], ffn_w_down[i],
                                   min(1024, s), 256)
        else:
            x1, hp, info, info_t, cnt = _mix_out_routed(
                x, yc, ya, w_out, l, mod5, norm_ffn_g[l],
                moe_w_router[i], moe_b_router[i], tm)
            x = _moe_layer(hp, info, info_t, cnt, x1, mod5, l,
                           moe_w_gate[i], moe_w_up[i], moe_w_down[i], tm)
    return x
```

```python
import functools

import jax
import jax.numpy as jnp
from jax import lax
from jax.experimental import pallas as pl
from jax.experimental.pallas import tpu as pltpu
from jax.experimental.pallas import tpu_sc as plsc

F32 = jnp.float32
BF16 = jnp.bfloat16

D_MODEL = 1024
CHUNK = 64
N_PREV_CHUNKS = 8
BAND_PAD = N_PREV_CHUNKS * CHUNK
D_CONV = 512
D_ATT = 512
HEAD_DIM = 64
N_HEADS = 8
CONV_WIDTH = 31
MAX_REL = 128
D_IN_COLS = 2 * D_CONV + 3 * D_ATT
N_EXPERTS = 8
EPS = 1e-6
NEG_INF = -1e30

LANES = 128
SUBLANES = 8
VMEM_LIMIT_BYTES = 56 * 1024 * 1024

ATT_HEADS = 4
ATT_GROUPS = N_HEADS // ATT_HEADS
ATT_W = ATT_HEADS * HEAD_DIM
ATT_Q = 2 * CHUNK
ATT_BAND = BAND_PAD + ATT_Q
ATT_L = ATT_HEADS * ATT_Q

CONV_HALO = 32
CONV_ROWS = 32
LANE_PAD_E = LANES

MOE_HALF = 256
MOE_SUB = 4 * MOE_HALF
MOE_TAILS = (2 * MOE_HALF, MOE_HALF)
MOE_TILE = 9 * MOE_HALF
MOE_STATIC_UNITS = (9, 8, 7)
MOE_FC = 512


def _cparams(n_axes, vmem=VMEM_LIMIT_BYTES):
    return pltpu.CompilerParams(
        dimension_semantics=("arbitrary",) * n_axes, vmem_limit_bytes=vmem)


def _silu(v):
    return v * jax.nn.sigmoid(v)


def _pack_bf16_pairs(v):
    w = v.shape[1] // 2
    bits = lax.bitcast_convert_type(v.astype(BF16).astype(F32), jnp.uint32)
    packed = (bits[:, w:] & jnp.uint32(0xFFFF0000)) | (bits[:, :w] >> 16)
    return lax.bitcast_convert_type(packed, jnp.int32)


def _unpack_bf16_pairs(p):
    bits = lax.bitcast_convert_type(p, jnp.uint32)
    lo = lax.bitcast_convert_type(bits << 16, F32)
    hi = lax.bitcast_convert_type(bits & jnp.uint32(0xFFFF0000), F32)
    return jnp.concatenate([lo, hi], axis=1)


SC_CORES = 2
SC_SUBCORES = 16
SC_WORKERS = SC_CORES * SC_SUBCORES
SC_CHUNK = 64


def _sc_worker_id():
    return lax.axis_index("s") * SC_CORES + lax.axis_index("c")


def _sc_mesh():
    return plsc.VectorSubcoreMesh(core_axis_name="c", subcore_axis_name="s")


def _sc_gather_rows(table, idx):
    _, w = table.shape
    b = idx.shape[0]
    per_w = b // SC_WORKERS
    nch = per_w // SC_CHUNK

    def body(table_hbm, idx_hbm, out_hbm, idx_v, rows_v, gsem, wsem):
        wid = _sc_worker_id()
        base = wid * per_w
        pltpu.sync_copy(idx_hbm.at[wid], idx_v)
        gathers = [None] * nch
        writes = [None] * nch
        gathers[0] = pltpu.async_copy(table_hbm.at[idx_v.at[0]], rows_v.at[0], gsem.at[0])
        for c in range(nch):
            slot = c % 2
            gathers[c].wait()
            if c + 1 < nch:
                if c >= 1:
                    writes[c - 1].wait()
                gathers[c + 1] = pltpu.async_copy(
                    table_hbm.at[idx_v.at[c + 1]], rows_v.at[1 - slot], gsem.at[1 - slot])
            writes[c] = pltpu.async_copy(
                rows_v.at[slot], out_hbm.at[pl.ds(base + c * SC_CHUNK, SC_CHUNK)], wsem.at[slot])
        if nch >= 2:
            writes[nch - 2].wait()
        writes[nch - 1].wait()

    call = pl.kernel(
        body, mesh=_sc_mesh(),
        out_type=jax.ShapeDtypeStruct((b, w), jnp.int32),
        scratch_types=[pltpu.VMEM((nch, SC_CHUNK), jnp.int32),
                       pltpu.VMEM((2, SC_CHUNK, w), jnp.int32),
                       pltpu.SemaphoreType.DMA((2,)), pltpu.SemaphoreType.DMA((2,))],
        name="sc_gather_rows")
    return call(table, idx.reshape(SC_WORKERS, nch, SC_CHUNK))


def _sc_scatter_rows2(src, idx0, idx1, rows_out):
    n, w = src.shape
    per_w = n // SC_WORKERS
    nch = per_w // SC_CHUNK

    def body(src_hbm, i0_hbm, i1_hbm, out_hbm, i0_v, i1_v, rows_v, rsem, wsem):
        wid = _sc_worker_id()
        base = wid * per_w
        pltpu.sync_copy(i0_hbm.at[wid], i0_v)
        pltpu.sync_copy(i1_hbm.at[wid], i1_v)
        reads = [None] * nch
        writes = [None] * nch
        reads[0] = pltpu.async_copy(src_hbm.at[pl.ds(base, SC_CHUNK)], rows_v.at[0], rsem.at[0])
        for c in range(nch):
            slot = c % 2
            reads[c].wait()
            if c + 1 < nch:
                if c >= 1:
                    for wr in writes[c - 1]:
                        wr.wait()
                reads[c + 1] = pltpu.async_copy(
                    src_hbm.at[pl.ds(base + (c + 1) * SC_CHUNK, SC_CHUNK)],
                    rows_v.at[1 - slot], rsem.at[1 - slot])
            writes[c] = (
                pltpu.async_copy(rows_v.at[slot], out_hbm.at[i0_v.at[c]], wsem.at[slot, 0]),
                pltpu.async_copy(rows_v.at[slot], out_hbm.at[i1_v.at[c]], wsem.at[slot, 1]),
            )
        for c in range(max(nch - 2, 0), nch):
            for wr in writes[c]:
                wr.wait()

    call = pl.kernel(
        body, mesh=_sc_mesh(),
        out_type=jax.ShapeDtypeStruct((rows_out, w), jnp.int32),
        scratch_types=[pltpu.VMEM((nch, SC_CHUNK), jnp.int32),
                       pltpu.VMEM((nch, SC_CHUNK), jnp.int32),
                       pltpu.VMEM((2, SC_CHUNK, w), jnp.int32),
                       pltpu.SemaphoreType.DMA((2,)), pltpu.SemaphoreType.DMA((2, 2))],
        name="sc_scatter_rows")
    shape3 = (SC_WORKERS, nch, SC_CHUNK)
    return call(src, idx0.reshape(shape3), idx1.reshape(shape3))


def _ada_kernel(c_ref, w_ref, b_ref, o_ref):
    ca = _silu(c_ref[...]).astype(BF16)
    w = w_ref[0].astype(BF16)
    o_ref[0] = jnp.dot(ca, w, preferred_element_type=F32) + b_ref[0]


def _ada_mod(c, w_ada, b_ada):
    depth, d, n6 = w_ada.shape
    b = c.shape[0]
    rows = 16
    c_pad = jnp.zeros((rows, d), F32).at[:b].set(c)
    tn = 1536
    out = pl.pallas_call(
        _ada_kernel,
        out_shape=jax.ShapeDtypeStruct((depth, rows, n6), F32),
        grid=(depth, n6 // tn),
        in_specs=[
            pl.BlockSpec((rows, d), lambda l, j: (0, 0)),
            pl.BlockSpec((1, d, tn), lambda l, j: (l, 0, j)),
            pl.BlockSpec((1, 1, tn), lambda l, j: (l, 0, j)),
        ],
        out_specs=pl.BlockSpec((1, rows, tn), lambda l, j: (l, 0, j)),
        compiler_params=_cparams(2),
        name="ada_mod",
    )(c_pad, w_ada, b_ada.reshape(depth, 1, n6))
    return out[:, :b]


MOD_SH1, MOD_SC1, MOD_G1, MOD_SH2, MOD_SC2, MOD_G2 = range(6)


def _mod_spec(layer, chunk):
    return pl.BlockSpec((None, 1, None, 1, D_MODEL),
                        lambda bi, *_: (layer, bi, chunk, 0, 0))


def _rms_mod(xf, g, sc, sh):
    ms = jnp.mean(xf * xf, axis=-1, keepdims=True)
    return xf * lax.rsqrt(ms + EPS) * g * (1.0 + sc) + sh


def _mix_in_kernel(x_ref, sc_ref, sh_ref, g_ref, w_ref, gq_ref, gk_ref, ones_ref,
                   z_ref, q_ref, k_ref, vt_ref, wbf_ref):
    first = jnp.logical_and(pl.program_id(0) == 0, pl.program_id(1) == 0)

    @pl.when(first)
    def _():
        wbf_ref[...] = w_ref[...].astype(BF16)

    h = _rms_mod(x_ref[0], g_ref[...], sc_ref[0], sh_ref[0]).astype(BF16)
    proj = jnp.dot(h, wbf_ref[...], preferred_element_type=F32)

    a = proj[:, :D_CONV]
    gate = proj[:, D_CONV:2 * D_CONV]
    z_ref[0] = (a * jax.nn.sigmoid(gate)).astype(BF16)

    def head_norm(t, g):
        sq = (t * t).astype(BF16)
        ss = jnp.concatenate(
            [jnp.dot(sq[:, c:c + ATT_W], ones_ref[...], preferred_element_type=F32)
             for c in range(0, D_ATT, ATT_W)], axis=1)
        return (t * lax.rsqrt(ss * (1.0 / HEAD_DIM) + EPS) * g).astype(BF16)

    o = 2 * D_CONV
    q_ref[0] = head_norm(proj[:, o:o + D_ATT], gq_ref[...])
    k_ref[0] = head_norm(proj[:, o + D_ATT:o + 2 * D_ATT], gk_ref[...])
    v = proj[:, o + 2 * D_ATT:]
    tm = v.shape[0]
    for cidx in range(tm // ATT_Q):
        vt_ref[0, cidx] = v[cidx * ATT_Q:(cidx + 1) * ATT_Q, :].T.astype(BF16)


def _mix_in(x, mod5, g, w_in, layer, gq, gk, tm):
    b, s, d = x.shape
    ones_bd = (jnp.arange(ATT_W)[:, None] // HEAD_DIM
               == jnp.arange(ATT_W)[None, :] // HEAD_DIM).astype(BF16)
    gq_t = (jnp.tile(gq, N_HEADS) * (HEAD_DIM ** -0.5)).reshape(1, D_ATT)
    gk_t = jnp.tile(gk, N_HEADS).reshape(1, D_ATT)
    row = lambda bi, ti: (bi, ti, 0)
    const2 = lambda bi, ti: (0, 0)
    return pl.pallas_call(
        _mix_in_kernel,
        out_shape=(
            jax.ShapeDtypeStruct((b, s, D_CONV), BF16),
            jax.ShapeDtypeStruct((b, s, D_ATT), BF16),
            jax.ShapeDtypeStruct((b, s, D_ATT), BF16),
            jax.ShapeDtypeStruct((b, s // ATT_Q, D_ATT, ATT_Q), BF16),
        ),
        grid=(b, s // tm),
        in_specs=[
            pl.BlockSpec((1, tm, d), row),
            _mod_spec(layer, MOD_SC1),
            _mod_spec(layer, MOD_SH1),
            pl.BlockSpec((1, d), const2),
            pl.BlockSpec((None, d, D_IN_COLS), lambda bi, ti: (layer, 0, 0),
                         pipeline_mode=pl.Buffered(1)),
            pl.BlockSpec((1, D_ATT), const2),
            pl.BlockSpec((1, D_ATT), const2),
            pl.BlockSpec((ATT_W, ATT_W), const2),
        ],
        out_specs=(
            pl.BlockSpec((1, tm, D_CONV), row),
            pl.BlockSpec((1, tm, D_ATT), row),
            pl.BlockSpec((1, tm, D_ATT), row),
            pl.BlockSpec((1, tm // ATT_Q, D_ATT, ATT_Q), lambda bi, ti: (bi, ti, 0, 0)),
        ),
        scratch_shapes=[pltpu.VMEM((d, D_IN_COLS), BF16)],
        compiler_params=_cparams(2),
        name="mix_in",
    )(x, mod5, mod5, g.reshape(1, d), w_in, gq_t, gk_t, ones_bd)


def _conv_kernel(zc_ref, zp_ref, w_ref, cb_ref, lg_ref, lb_ref, o_ref, win_ref, sh_ref,
                 acc_ref):
    tt = zc_ref.shape[1]
    t = pl.program_id(1)
    halo = zp_ref[0].astype(F32)
    win_ref[0:CONV_HALO, :] = jnp.where(t == 0, 0.0, halo)
    win_ref[CONV_HALO:, :] = zc_ref[0].astype(F32)
    span = tt + CONV_HALO - SUBLANES
    for sft in range(1, SUBLANES):
        sh_ref[sft - 1, 0:span, :] = win_ref[sft:sft + span, :]
    base = CONV_HALO - (CONV_WIDTH - 1)
    tiles = CONV_ROWS // SUBLANES

    for g in range(tt // CONV_ROWS):
        r = g * CONV_ROWS
        acc = jnp.zeros((tiles, SUBLANES, D_CONV), F32) + cb_ref[...]
        for j in range(CONV_WIDTH):
            whole, sft = divmod(base + j, SUBLANES)
            start = r + whole * SUBLANES
            if sft == 0:
                tap = win_ref[start:start + CONV_ROWS, :]
            else:
                tap = sh_ref[sft - 1, start:start + CONV_ROWS, :]
            acc = acc + tap.reshape(tiles, SUBLANES, D_CONV) * w_ref[j]
        acc_ref[r:r + CONV_ROWS, :] = acc.reshape(CONV_ROWS, D_CONV)
    acc = acc_ref[...]
    mu = jnp.mean(acc, axis=-1, keepdims=True)
    xc = acc - mu
    var = jnp.mean(xc * xc, axis=-1, keepdims=True)
    y = xc * lax.rsqrt(var + EPS) * lg_ref[...] + lb_ref[...]
    o_ref[0] = _silu(y).astype(BF16)


def _conv_branch(z, conv_w, conv_b, ln_g, ln_b, tt):
    b, s, c = z.shape
    hb = tt // CONV_HALO
    w_tiles = jnp.broadcast_to(conv_w.reshape(CONV_WIDTH, 1, c), (CONV_WIDTH, SUBLANES, c))
    const2 = lambda bi, ti: (0, 0)
    return pl.pallas_call(
        _conv_kernel,
        out_shape=jax.ShapeDtypeStruct((b, s, c), BF16),
        grid=(b, s // tt),
        in_specs=[
            pl.BlockSpec((1, tt, c), lambda bi, ti: (bi, ti, 0)),
            pl.BlockSpec((1, CONV_HALO, c),
                         lambda bi, ti: (bi, jnp.maximum(ti * hb - 1, 0), 0)),
            pl.BlockSpec((CONV_WIDTH, SUBLANES, c), lambda bi, ti: (0, 0, 0)),
            pl.BlockSpec((1, c), const2),
            pl.BlockSpec((1, c), const2),
            pl.BlockSpec((1, c), const2),
        ],
        out_specs=pl.BlockSpec((1, tt, c), lambda bi, ti: (bi, ti, 0)),
        scratch_shapes=[pltpu.VMEM((tt + CONV_HALO, c), F32),
                        pltpu.VMEM((SUBLANES - 1, tt + CONV_HALO, c), F32),
                        pltpu.VMEM((tt, c), F32)],
        compiler_params=_cparams(2),
        name="conv_branch",
    )(z, z, w_tiles, conv_b.reshape(1, c),
      ln_g.reshape(1, c), ln_b.reshape(1, c))


def _attn_kernel(q_ref, k_ref, vt_ref, bias_ref, o_ref, kpad_ref, vtpad_ref,
                 st0_ref, st1_ref, pb0_ref, pb1_ref, den0_ref, den1_ref):
    st_refs = (st0_ref, st1_ref)
    pb_refs = (pb0_ref, pb1_ref)
    den_refs = (den0_ref, den1_ref)
    s = q_ref.shape[1]
    npad = BAND_PAD // ATT_Q
    kpad_ref[0:BAND_PAD, :] = jnp.zeros((BAND_PAD, ATT_W), BF16)
    kpad_ref[BAND_PAD:, :] = k_ref[0]
    vtpad_ref[0:npad] = jnp.zeros((npad, ATT_W, ATT_Q), BF16)
    vtpad_ref[npad:] = vt_ref[0]

    iota = lambda shape, dim: lax.broadcasted_iota(jnp.int32, shape, dim)
    q_shift = ATT_Q.bit_length() - 1
    d_shift = HEAD_DIM.bit_length() - 1
    qb_mask = (iota((ATT_L, ATT_W), 0) >> q_shift) == (iota((ATT_L, ATT_W), 1) >> d_shift)
    key_row = lax.broadcasted_iota(jnp.int32, (ATT_BAND, ATT_L), 0)
    contract_last = (((1,), (1,)), ((), ()))

    def scores(m, p):
        r0 = m * ATT_Q
        qt = q_ref[0, pl.ds(r0, ATT_Q), :]
        qb = jnp.where(qb_mask, jnp.concatenate([qt] * ATT_HEADS, axis=0), 0)
        kb = kpad_ref[pl.ds(r0, ATT_BAND), :]
        st_refs[p][...] = lax.dot_general(kb, qb.astype(BF16), contract_last,
                                          preferred_element_type=F32)

    def softmax(m, p, masked):
        st = st_refs[p][...] + bias_ref[0]
        if masked:
            st = jnp.where(key_row >= BAND_PAD - m * ATT_Q, st, NEG_INF)
        mx = jnp.max(st, axis=0, keepdims=True)
        e = jnp.exp(st - mx)
        den_refs[p][...] = jnp.sum(e, axis=0, keepdims=True)
        pb_refs[p][...] = e.astype(BF16)

    def values(m, p):
        r0 = m * ATT_Q
        vb = jnp.concatenate([vtpad_ref[m + c] for c in range(ATT_BAND // ATT_Q)],
                             axis=1)
        ot = jnp.dot(vb, pb_refs[p][...], preferred_element_type=F32)
        ot = ot / den_refs[p][...]
        y = jnp.concatenate(
            [ot[h * HEAD_DIM:(h + 1) * HEAD_DIM, h * ATT_Q:(h + 1) * ATT_Q].T
             for h in range(ATT_HEADS)], axis=1)
        o_ref[0, pl.ds(r0, ATT_Q), :] = y.astype(BF16)

    n = s // ATT_Q
    n_masked = BAND_PAD // ATT_Q
    for m in range(n + 2):
        if m < n:
            scores(m, m % 2)
        if 1 <= m <= n:
            softmax(m - 1, (m - 1) % 2, m - 1 < n_masked)
        if m >= 2:
            values(m - 2, m % 2)


def _attn_bias_t(rel_bias):
    rb = rel_bias.astype(F32).reshape(-1, rel_bias.shape[-1])
    n_rows = rb.shape[0]
    nu = ATT_BAND + ATT_Q - 1
    n_low = BAND_PAD - MAX_REL + ATT_Q
    t = jnp.concatenate([jnp.repeat(rb[:, :1], n_low, axis=1),
                         rb[:, 1:1 + nu - n_low]], axis=1)
    tp = jnp.pad(t, ((0, 0), (0, 1)))
    flat = jnp.tile(tp, (1, ATT_Q + 1))[:, ATT_Q - 1:ATT_Q - 1 + ATT_Q * nu]
    bias = flat.reshape(n_rows, ATT_Q, nu)[:, :, :ATT_BAND]
    bias = bias.reshape(n_rows // ATT_HEADS, ATT_HEADS, ATT_Q, ATT_BAND)
    bias = bias.transpose(0, 3, 1, 2).reshape(n_rows // ATT_HEADS, ATT_BAND, ATT_L)
    r = jnp.arange(ATT_BAND)[:, None]
    qq = jnp.arange(ATT_L)[None, :] % ATT_Q
    first = (qq // CHUNK) * CHUNK
    valid = (r >= first) & (r < first + BAND_PAD + CHUNK)
    return jnp.where(valid[None], bias, NEG_INF)


def _attention(qn, kn, vt, bias_t, layer):
    b, s, _ = qn.shape
    nck = s // ATT_Q
    assert nck >= 2, "the attention pipeline needs at least two query steps"
    return pl.pallas_call(
        _attn_kernel,
        out_shape=jax.ShapeDtypeStruct((b, s, D_ATT), BF16),
        grid=(b, ATT_GROUPS),
        in_specs=[
            pl.BlockSpec((1, s, ATT_W), lambda bi, gi: (bi, 0, gi)),
            pl.BlockSpec((1, s, ATT_W), lambda bi, gi: (bi, 0, gi)),
            pl.BlockSpec((1, nck, ATT_W, ATT_Q), lambda bi, gi: (bi, 0, gi, 0)),
            pl.BlockSpec((1, ATT_BAND, ATT_L), lambda bi, gi: (layer * ATT_GROUPS + gi, 0, 0)),
        ],
        out_specs=pl.BlockSpec((1, s, ATT_W), lambda bi, gi: (bi, 0, gi)),
        scratch_shapes=[
            pltpu.VMEM((s + BAND_PAD, ATT_W), BF16),
            pltpu.VMEM((nck + BAND_PAD // ATT_Q, ATT_W, ATT_Q), BF16),
            pltpu.VMEM((ATT_BAND, ATT_L), F32), pltpu.VMEM((ATT_BAND, ATT_L), F32),
            pltpu.VMEM((ATT_BAND, ATT_L), BF16), pltpu.VMEM((ATT_BAND, ATT_L), BF16),
            pltpu.VMEM((1, ATT_L), F32), pltpu.VMEM((1, ATT_L), F32),
        ],
        compiler_params=_cparams(2),
        name="band_attention",
    )(qn, kn, vt, bias_t)


def _mix_out_kernel(x_ref, yc_ref, ya_ref, w_ref, g1_ref, gf_ref, sc_ref, sh_ref,
                    wr_ref, br_ref, x1_ref, h_ref, info_ref, info_t_ref, cnt_ref,
                    wbf_ref, carry_ref):
    first = jnp.logical_and(pl.program_id(0) == 0, pl.program_id(1) == 0)

    @pl.when(first)
    def _():
        wbf_ref[...] = w_ref[...].astype(BF16)
        carry_ref[...] = jnp.zeros_like(carry_ref)

    y = jnp.dot(yc_ref[0], wbf_ref[0:D_CONV, :], preferred_element_type=F32)
    y = y + jnp.dot(ya_ref[0], wbf_ref[D_CONV:, :], preferred_element_type=F32)
    x1 = x_ref[0] + g1_ref[0] * y
    x1_ref[0] = x1
    h = _rms_mod(x1, gf_ref[...], sc_ref[0], sh_ref[0])
    h_ref[0] = _pack_bf16_pairs(h)
    logits = jnp.dot(h.astype(BF16), wr_ref[...], preferred_element_type=F32) + br_ref[...]
    _route_tile(logits, info_ref, info_t_ref, cnt_ref, carry_ref)


def _mix_out_routed(x, yc, ya, w_out, layer, mod5, gf, w_router, b_router, tm):
    b, s, d = x.shape
    nt = s // tm
    row = lambda bi, ti: (bi, ti, 0)
    const2 = lambda bi, ti: (0, 0)
    w_pad = jnp.zeros((d, LANE_PAD_E), BF16).at[:, :N_EXPERTS].set(w_router.astype(BF16))
    b_pad = jnp.full((1, LANE_PAD_E), -jnp.inf, F32).at[0, :N_EXPERTS].set(
        b_router.astype(F32))
    return pl.pallas_call(
        _mix_out_kernel,
        out_shape=(jax.ShapeDtypeStruct((b, s, d), F32),
                   jax.ShapeDtypeStruct((b, s, d // 2), jnp.int32),
                   jax.ShapeDtypeStruct((b, s, LANE_PAD_E), F32),
                   jax.ShapeDtypeStruct((SUBLANES, b * s), F32),
                   jax.ShapeDtypeStruct((SUBLANES, LANE_PAD_E), F32)),
        grid=(b, nt),
        in_specs=[
            pl.BlockSpec((1, tm, d), row),
            pl.BlockSpec((1, tm, D_CONV), row),
            pl.BlockSpec((1, tm, D_ATT), row),
            pl.BlockSpec((None, d, d), lambda bi, ti: (layer, 0, 0),
                         pipeline_mode=pl.Buffered(1)),
            _mod_spec(layer, MOD_G1),
            pl.BlockSpec((1, d), const2),
            _mod_spec(layer, MOD_SC2),
            _mod_spec(layer, MOD_SH2),
            pl.BlockSpec((d, LANE_PAD_E), const2),
            pl.BlockSpec((1, LANE_PAD_E), const2),
        ],
        out_specs=(pl.BlockSpec((1, tm, d), row), pl.BlockSpec((1, tm, d // 2), row),
                   pl.BlockSpec((1, tm, LANE_PAD_E), row),
                   pl.BlockSpec((SUBLANES, tm), lambda bi, ti: (0, bi * nt + ti)),
                   pl.BlockSpec((SUBLANES, LANE_PAD_E), const2)),
        scratch_shapes=[pltpu.VMEM((d, d), BF16), pltpu.VMEM((SUBLANES, LANE_PAD_E), F32)],
        compiler_params=_cparams(2),
        name="mix_out",
    )(x, yc, ya, w_out, mod5, gf.reshape(1, d), mod5, mod5, w_pad, b_pad)


def _cast_ffn_kernel(wg_ref, wu_ref, wd_ref, wo_ref, wg3_ref, wu3_ref, wd3_ref, wob_ref):
    wg3_ref[0] = wg_ref[...].astype(BF16)
    wu3_ref[0] = wu_ref[...].astype(BF16)
    wd3_ref[0] = wd_ref[...].astype(BF16)

    @pl.when(pl.program_id(0) == 0)
    def _():
        wob_ref[...] = wo_ref[...].astype(BF16)


def _cast_ffn_weights(wg, wu, wd, w_out, layer, fc):
    d, ff = wg.shape
    nf = ff // fc
    return pl.pallas_call(
        _cast_ffn_kernel,
        out_shape=(jax.ShapeDtypeStruct((nf, d, fc), BF16),
                   jax.ShapeDtypeStruct((nf, d, fc), BF16),
                   jax.ShapeDtypeStruct((nf, fc, d), BF16),
                   jax.ShapeDtypeStruct((d, d), BF16)),
        grid=(nf,),
        in_specs=[
            pl.BlockSpec((d, fc), lambda f: (0, f)),
            pl.BlockSpec((d, fc), lambda f: (0, f)),
            pl.BlockSpec((fc, d), lambda f: (f, 0)),
            pl.BlockSpec((None, d, d), lambda f: (layer, 0, 0)),
        ],
        out_specs=(pl.BlockSpec((1, d, fc), lambda f: (f, 0, 0)),
                   pl.BlockSpec((1, d, fc), lambda f: (f, 0, 0)),
                   pl.BlockSpec((1, fc, d), lambda f: (f, 0, 0)),
                   pl.BlockSpec((d, d), lambda f: (0, 0))),
        compiler_params=_cparams(1),
        name="cast_ffn_weights",
    )(wg, wu, wd, w_out)


def _mix_ffn_kernel(x_ref, yc_ref, ya_ref, wo_ref, g1_ref, gf_ref, sc_ref, sh_ref, g2_ref,
                    wg_ref, wu_ref, wd_ref, o_ref, acc_ref):
    y = jnp.dot(yc_ref[0], wo_ref[0:D_CONV, :], preferred_element_type=F32)
    y = y + jnp.dot(ya_ref[0], wo_ref[D_CONV:, :], preferred_element_type=F32)
    x1 = x_ref[0] + g1_ref[0] * y
    h = _rms_mod(x1, gf_ref[...], sc_ref[0], sh_ref[0]).astype(BF16)
    for c in range(wg_ref.shape[0]):
        a = jnp.dot(h, wg_ref[c], preferred_element_type=F32)
        u = jnp.dot(h, wu_ref[c], preferred_element_type=F32)
        t = (_silu(a) * u).astype(BF16)
        dn = jnp.dot(t, wd_ref[c], preferred_element_type=F32)
        if c == 0:
            acc_ref[...] = dn
        else:
            acc_ref[...] += dn
    o_ref[0] = x1 + g2_ref[0] * acc_ref[...]


def _mix_out_dense_ffn(x, yc, ya, w_out, layer, mod5, gf, wg, wu, wd, tm, fc):
    b, s, d = x.shape
    wg3, wu3, wd3, wob = _cast_ffn_weights(wg, wu, wd, w_out, layer, fc)
    nf = wg3.shape[0]
    row = lambda bi, ti: (bi, ti, 0)
    const2 = lambda bi, ti: (0, 0)
    const3 = lambda bi, ti: (0, 0, 0)
    resident = pl.Buffered(1)
    return pl.pallas_call(
        _mix_ffn_kernel,
        out_shape=jax.ShapeDtypeStruct((b, s, d), F32),
        grid=(b, s // tm),
        in_specs=[
            pl.BlockSpec((1, tm, d), row),
            pl.BlockSpec((1, tm, D_CONV), row),
            pl.BlockSpec((1, tm, D_ATT), row),
            pl.BlockSpec((d, d), const2, pipeline_mode=resident),
            _mod_spec(layer, MOD_G1),
            pl.BlockSpec((1, d), const2),
            _mod_spec(layer, MOD_SC2),
            _mod_spec(layer, MOD_SH2),
            _mod_spec(layer, MOD_G2),
            pl.BlockSpec((nf, d, fc), const3, pipeline_mode=resident),
            pl.BlockSpec((nf, d, fc), const3, pipeline_mode=resident),
            pl.BlockSpec((nf, fc, d), const3, pipeline_mode=resident),
        ],
        out_specs=pl.BlockSpec((1, tm, d), row),
        scratch_shapes=[pltpu.VMEM((tm, d), F32)],
        compiler_params=_cparams(2),
        name="mix_out_dense_ffn",
    )(x, yc, ya, wob, mod5, gf.reshape(1, d), mod5, mod5, mod5, wg3, wu3, wd3)


def _route_tile(logits, info_ref, info_t_ref, cnt_ref, carry_ref):
    tr = logits.shape[0]
    lane = lax.broadcasted_iota(jnp.int32, (tr, LANE_PAD_E), 1).astype(F32)
    no_lane = float(LANE_PAD_E)
    v0 = jnp.max(logits, axis=-1, keepdims=True)
    i0 = jnp.min(jnp.where(logits == v0, lane, no_lane), axis=-1, keepdims=True)
    rest = jnp.where(lane == i0, -jnp.inf, logits)
    v1 = jnp.max(rest, axis=-1, keepdims=True)
    i1 = jnp.min(jnp.where(rest == v1, lane, no_lane), axis=-1, keepdims=True)
    e1 = jnp.exp(v1 - v0)
    w0 = 1.0 / (1.0 + e1)
    w1 = e1 / (1.0 + e1)
    oh0 = lane == i0
    oh1 = lane == i1
    cnt = jnp.where(jnp.logical_or(oh0, oh1), 1.0, 0.0)
    tri = (lax.broadcasted_iota(jnp.int32, (tr, tr), 1)
           < lax.broadcasted_iota(jnp.int32, (tr, tr), 0)).astype(BF16)
    before = jnp.dot(tri, cnt.astype(BF16), preferred_element_type=F32) + carry_ref[0:1, :]
    r0 = jnp.sum(jnp.where(oh0, before, 0.0), axis=-1, keepdims=True)
    r1 = jnp.sum(jnp.where(oh1, before, 0.0), axis=-1, keepdims=True)
    carry_ref[...] = carry_ref[...] + jnp.sum(cnt, axis=0, keepdims=True)
    cnt_ref[...] = carry_ref[...]
    info = jnp.where(lane == 0, i0, 0.0)
    info = jnp.where(lane == 1, i1, info)
    info = jnp.where(lane == 2, w0, info)
    info = jnp.where(lane == 3, w1, info)
    info = jnp.where(lane == 4, r0, info)
    info = jnp.where(lane == 5, r1, info)
    info_ref[0] = info
    for c in range(tr // LANES):
        blk = info[c * LANES:(c + 1) * LANES, :].T
        info_t_ref[:, c * LANES:(c + 1) * LANES] = blk[0:SUBLANES, :]


def _moe_kernel(te_ref, tn_ref, xs_ref, wg_ref, wu_ref, wd_ref, ys_ref,
                acc_ref, wgb_ref, wub_ref, wdb_ref):
    i = pl.program_id(0)
    f = pl.program_id(1)
    nhalf = tn_ref[i]
    nfull = nhalf // (MOE_SUB // MOE_HALF)

    @pl.when(f == 0)
    def _():
        acc_ref[...] = jnp.zeros_like(acc_ref)

    def block(r, rows, wgb, wub, wdb):
        xb = _unpack_bf16_pairs(xs_ref[pl.ds(r, rows), :]).astype(BF16)
        a = jnp.dot(xb, wgb, preferred_element_type=F32)
        u = jnp.dot(xb, wub, preferred_element_type=F32)
        t = (_silu(a) * u).astype(BF16)
        acc_ref[pl.ds(r, rows), :] += jnp.dot(t, wdb, preferred_element_type=F32)

    def split(units):
        out, r = [], 0
        for rows in (MOE_SUB,) * (units * MOE_HALF // MOE_SUB) + MOE_TAILS:
            if units * MOE_HALF - r >= rows:
                out.append((r, rows))
                r += rows
        return out

    for units in MOE_STATIC_UNITS:
        @pl.when(nhalf == units)
        def _(units=units):
            wgb, wub, wdb = (w[0].astype(BF16) for w in (wg_ref, wu_ref, wd_ref))
            for r, rows in split(units):
                block(r, rows, wgb, wub, wdb)

    other = nhalf > 0
    for units in MOE_STATIC_UNITS:
        other = jnp.logical_and(other, nhalf != units)

    @pl.when(other)
    def _():
        wgb_ref[...] = wg_ref[0].astype(BF16)
        wub_ref[...] = wu_ref[0].astype(BF16)
        wdb_ref[...] = wd_ref[0].astype(BF16)

        def sub(sidx, carry):
            block(pl.multiple_of(sidx * MOE_SUB, MOE_SUB), MOE_SUB,
                  wgb_ref[...], wub_ref[...], wdb_ref[...])
            return carry

        lax.fori_loop(0, nfull, sub, 0)
        done = nfull * MOE_SUB
        for rows in MOE_TAILS:
            units = rows // MOE_HALF

            @pl.when((nhalf & units) != 0)
            def _(rows=rows, units=units):
                higher = nhalf & (MOE_SUB // MOE_HALF - 1) & ~(2 * units - 1)
                block(pl.multiple_of(done + higher * MOE_HALF, MOE_HALF), rows,
                      wgb_ref[...], wub_ref[...], wdb_ref[...])

    @pl.when(f == pl.num_programs(1) - 1)
    def _():
        ys_ref[...] = _pack_bf16_pairs(acc_ref[...])


def _moe_ffn(xs, tile_e, tile_nsub, wg, wu, wd):
    rpad, dw = xs.shape
    d = 2 * dw
    ntiles = rpad // MOE_TILE
    ff = wg.shape[2]
    nf = ff // MOE_FC

    def fcol(i, f, tn):
        return jnp.where(tn[i] > 0, f, nf - 1)

    return pl.pallas_call(
        _moe_kernel,
        out_shape=jax.ShapeDtypeStruct((rpad, dw), jnp.int32),
        grid_spec=pltpu.PrefetchScalarGridSpec(
            num_scalar_prefetch=2,
            grid=(ntiles, nf),
            in_specs=[
                pl.BlockSpec((MOE_TILE, dw), lambda i, f, te, tn: (i, 0)),
                pl.BlockSpec((1, d, MOE_FC), lambda i, f, te, tn: (te[i], 0, fcol(i, f, tn))),
                pl.BlockSpec((1, d, MOE_FC), lambda i, f, te, tn: (te[i], 0, fcol(i, f, tn))),
                pl.BlockSpec((1, MOE_FC, d), lambda i, f, te, tn: (te[i], fcol(i, f, tn), 0)),
            ],
            out_specs=pl.BlockSpec((MOE_TILE, dw), lambda i, f, te, tn: (i, 0)),
            scratch_shapes=[
                pltpu.VMEM((MOE_TILE, d), F32),
                pltpu.VMEM((d, MOE_FC), BF16),
                pltpu.VMEM((d, MOE_FC), BF16),
                pltpu.VMEM((MOE_FC, d), BF16),
            ],
        ),
        compiler_params=_cparams(2),
        name="moe_ffn",
    )(tile_e, tile_nsub, xs, wg, wu, wd)


def _combine_kernel(x_ref, y0_ref, y1_ref, info_ref, g2_ref, o_ref):
    info = info_ref[0]
    w0 = info[:, 2:3]
    w1 = info[:, 3:4]
    f = w0 * _unpack_bf16_pairs(y0_ref[0, 0]) + w1 * _unpack_bf16_pairs(y1_ref[0, 0])
    o_ref[0] = x_ref[0] + g2_ref[0] * f


def _combine(x1, y01, info, mod5, layer, tm):
    b, s, d = x1.shape
    row = lambda bi, ti: (bi, ti, 0)
    return pl.pallas_call(
        _combine_kernel,
        out_shape=jax.ShapeDtypeStruct((b, s, d), F32),
        grid=(b, s // tm),
        in_specs=[
            pl.BlockSpec((1, tm, d), row),
            pl.BlockSpec((1, 1, tm, d // 2), lambda bi, ti: (0, bi, ti, 0)),
            pl.BlockSpec((1, 1, tm, d // 2), lambda bi, ti: (1, bi, ti, 0)),
            pl.BlockSpec((1, tm, LANE_PAD_E), row),
            _mod_spec(layer, MOD_G2),
        ],
        out_specs=pl.BlockSpec((1, tm, d), row),
        compiler_params=_cparams(2),
        name="moe_combine",
    )(x1, y01, y01, info, mod5)


def _moe_layer(hp, info, info_t, cnt, x1, mod5, layer, wg, wu, wd, tm):
    b, s, d = x1.shape
    n = b * s
    e0 = info_t[0].astype(jnp.int32)
    e1 = info_t[1].astype(jnp.int32)
    r0 = info_t[4].astype(jnp.int32)
    r1 = info_t[5].astype(jnp.int32)
    counts = cnt[0, :N_EXPERTS].astype(jnp.int32)

    ntiles = (2 * n) // MOE_TILE + N_EXPERTS
    tiles_per_e = (counts + MOE_TILE - 1) // MOE_TILE
    tile_end = jnp.cumsum(tiles_per_e)
    tile_start = tile_end - tiles_per_e
    total = tile_end[-1]
    tidx = jnp.arange(ntiles, dtype=jnp.int32)
    live = tidx < total
    tclip = jnp.minimum(tidx, total - 1)
    tile_e = jnp.minimum(jnp.sum(tclip[:, None] >= tile_end[None, :], axis=1),
                         N_EXPERTS - 1).astype(jnp.int32)
    rows_left = counts[tile_e] - (tclip - tile_start[tile_e]) * MOE_TILE
    rows_here = jnp.clip(rows_left, 0, MOE_TILE)
    tile_nsub = jnp.where(live, (rows_here + MOE_HALF - 1) // MOE_HALF, 0).astype(jnp.int32)

    row_start = tile_start * MOE_TILE
    eid = jnp.arange(N_EXPERTS, dtype=jnp.int32)[None, :]
    pos0 = jnp.sum(jnp.where(e0[:, None] == eid, row_start[None, :], 0), axis=1) + r0
    pos1 = jnp.sum(jnp.where(e1[:, None] == eid, row_start[None, :], 0), axis=1) + r1

    xs = _sc_scatter_rows2(hp.reshape(n, d // 2), pos0, pos1, ntiles * MOE_TILE)
    ys = _moe_ffn(xs, tile_e, tile_nsub, wg, wu, wd)
    y01 = _sc_gather_rows(ys, jnp.concatenate([pos0, pos1]))
    return _combine(x1, y01.reshape(2, b, s, d // 2), info, mod5, layer, tm)


def kernel(x, c, w_ada, b_ada, norm_mix_g, norm_ffn_g, w_in, w_out, conv_w, conv_b,
           conv_ln_g, conv_ln_b, q_norm_g, k_norm_g, rel_bias, ffn_w_gate, ffn_w_up,
           ffn_w_down, moe_w_router, moe_b_router, moe_w_gate, moe_w_up, moe_w_down):
    b, s, d = x.shape
    depth = w_ada.shape[0]
    tm = min(1024, s)
    mod5 = _ada_mod(c, w_ada, b_ada).reshape(depth, b, 6, 1, d)
    bias_t = _attn_bias_t(rel_bias)
    for l in range(depth):
        z, qn, kn, vt = _mix_in(x, mod5, norm_mix_g[l], w_in, l,
                                q_norm_g[l], k_norm_g[l], tm)
        yc = _conv_branch(z, conv_w[l], conv_b[l], conv_ln_g[l], conv_ln_b[l], tm)
        ya = _attention(qn, kn, vt, bias_t, l)
        i = l // 2
        if l % 2 == 0:
            x = _mix_out_dense_ffn(x, yc, ya, w_out, l, mod5, norm_ffn_g[l],
                                   ffn_w_gate[i], ffn_w_up[i], ffn_w_down[i],
                                   min(1024, s), 256)
        else:
            x1, hp, info, info_t, cnt = _mix_out_routed(
                x, yc, ya, w_out, l, mod5, norm_ffn_g[l],
                moe_w_router[i], moe_b_router[i], tm)
            x = _moe_layer(hp, info, info_t, cnt, x1, mod5, l,
                           moe_w_gate[i], moe_w_up[i], moe_w_down[i], tm)
    return x
```

```python
import functools

import jax
import jax.numpy as jnp
from jax import lax
from jax.experimental import pallas as pl
from jax.experimental.pallas import tpu as pltpu
from jax.experimental.pallas import tpu_sc as plsc

F32 = jnp.float32
BF16 = jnp.bfloat16

D_MODEL = 1024
CHUNK = 64
N_PREV_CHUNKS = 8
BAND_PAD = N_PREV_CHUNKS * CHUNK
D_CONV = 512
D_ATT = 512
HEAD_DIM = 64
N_HEADS = 8
CONV_WIDTH = 31
MAX_REL = 128
D_IN_COLS = 2 * D_CONV + 3 * D_ATT
N_EXPERTS = 8
EPS = 1e-6
NEG_INF = -1e30

LANES = 128
SUBLANES = 8
VMEM_LIMIT_BYTES = 56 * 1024 * 1024

ATT_HEADS = 4
ATT_GROUPS = N_HEADS // ATT_HEADS
ATT_W = ATT_HEADS * HEAD_DIM
ATT_Q = 2 * CHUNK
ATT_BAND = BAND_PAD + ATT_Q
ATT_L = ATT_HEADS * ATT_Q

CONV_HALO = 32
CONV_ROWS = 32
LANE_PAD_E = LANES

MOE_HALF = 256
MOE_SUB = 4 * MOE_HALF
MOE_TAILS = (2 * MOE_HALF, MOE_HALF)
MOE_TILE = 9 * MOE_HALF
MOE_STATIC_UNITS = (9, 8, 7)
MOE_FC = 512


def _cparams(n_axes, vmem=VMEM_LIMIT_BYTES):
    return pltpu.CompilerParams(
        dimension_semantics=("arbitrary",) * n_axes, vmem_limit_bytes=vmem)


def _silu(v):
    return v * jax.nn.sigmoid(v)


def _pack_bf16_pairs(v):
    w = v.shape[1] // 2
    bits = lax.bitcast_convert_type(v.astype(BF16).astype(F32), jnp.uint32)
    packed = (bits[:, w:] & jnp.uint32(0xFFFF0000)) | (bits[:, :w] >> 16)
    return lax.bitcast_convert_type(packed, jnp.int32)


def _unpack_bf16_pairs(p):
    bits = lax.bitcast_convert_type(p, jnp.uint32)
    lo = lax.bitcast_convert_type(bits << 16, F32)
    hi = lax.bitcast_convert_type(bits & jnp.uint32(0xFFFF0000), F32)
    return jnp.concatenate([lo, hi], axis=1)


SC_CORES = 2
SC_SUBCORES = 16
SC_WORKERS = SC_CORES * SC_SUBCORES
SC_CHUNK = 64


def _sc_worker_id():
    return lax.axis_index("s") * SC_CORES + lax.axis_index("c")


def _sc_mesh():
    return plsc.VectorSubcoreMesh(core_axis_name="c", subcore_axis_name="s")


def _sc_gather_rows(table, idx):
    _, w = table.shape
    b = idx.shape[0]
    per_w = b // SC_WORKERS
    nch = per_w // SC_CHUNK

    def body(table_hbm, idx_hbm, out_hbm, idx_v, rows_v, gsem, wsem):
        wid = _sc_worker_id()
        base = wid * per_w
        pltpu.sync_copy(idx_hbm.at[wid], idx_v)
        gathers = [None] * nch
        writes = [None] * nch
        gathers[0] = pltpu.async_copy(table_hbm.at[idx_v.at[0]], rows_v.at[0], gsem.at[0])
        for c in range(nch):
            slot = c % 2
            gathers[c].wait()
            if c + 1 < nch:
                if c >= 1:
                    writes[c - 1].wait()
                gathers[c + 1] = pltpu.async_copy(
                    table_hbm.at[idx_v.at[c + 1]], rows_v.at[1 - slot], gsem.at[1 - slot])
            writes[c] = pltpu.async_copy(
                rows_v.at[slot], out_hbm.at[pl.ds(base + c * SC_CHUNK, SC_CHUNK)], wsem.at[slot])
        if nch >= 2:
            writes[nch - 2].wait()
        writes[nch - 1].wait()

    call = pl.kernel(
        body, mesh=_sc_mesh(),
        out_type=jax.ShapeDtypeStruct((b, w), jnp.int32),
        scratch_types=[pltpu.VMEM((nch, SC_CHUNK), jnp.int32),
                       pltpu.VMEM((2, SC_CHUNK, w), jnp.int32),
                       pltpu.SemaphoreType.DMA((2,)), pltpu.SemaphoreType.DMA((2,))],
        name="sc_gather_rows")
    return call(table, idx.reshape(SC_WORKERS, nch, SC_CHUNK))


def _sc_scatter_rows2(src, idx0, idx1, rows_out):
    n, w = src.shape
    per_w = n // SC_WORKERS
    nch = per_w // SC_CHUNK

    def body(src_hbm, i0_hbm, i1_hbm, out_hbm, i0_v, i1_v, rows_v, rsem, wsem):
        wid = _sc_worker_id()
        base = wid * per_w
        pltpu.sync_copy(i0_hbm.at[wid], i0_v)
        pltpu.sync_copy(i1_hbm.at[wid], i1_v)
        reads = [None] * nch
        writes = [None] * nch
        reads[0] = pltpu.async_copy(src_hbm.at[pl.ds(base, SC_CHUNK)], rows_v.at[0], rsem.at[0])
        for c in range(nch):
            slot = c % 2
            reads[c].wait()
            if c + 1 < nch:
                if c >= 1:
                    for wr in writes[c - 1]:
                        wr.wait()
                reads[c + 1] = pltpu.async_copy(
                    src_hbm.at[pl.ds(base + (c + 1) * SC_CHUNK, SC_CHUNK)],
                    rows_v.at[1 - slot], rsem.at[1 - slot])
            writes[c] = (
                pltpu.async_copy(rows_v.at[slot], out_hbm.at[i0_v.at[c]], wsem.at[slot, 0]),
                pltpu.async_copy(rows_v.at[slot], out_hbm.at[i1_v.at[c]], wsem.at[slot, 1]),
            )
        for c in range(max(nch - 2, 0), nch):
            for wr in writes[c]:
                wr.wait()

    call = pl.kernel(
        body, mesh=_sc_mesh(),
        out_type=jax.ShapeDtypeStruct((rows_out, w), jnp.int32),
        scratch_types=[pltpu.VMEM((nch, SC_CHUNK), jnp.int32),
                       pltpu.VMEM((nch, SC_CHUNK), jnp.int32),
                       pltpu.VMEM((2, SC_CHUNK, w), jnp.int32),
                       pltpu.SemaphoreType.DMA((2,)), pltpu.SemaphoreType.DMA((2, 2))],
        name="sc_scatter_rows")
    shape3 = (SC_WORKERS, nch, SC_CHUNK)
    return call(src, idx0.reshape(shape3), idx1.reshape(shape3))


def _ada_kernel(c_ref, w_ref, b_ref, o_ref):
    ca = _silu(c_ref[...]).astype(BF16)
    w = w_ref[0].astype(BF16)
    o_ref[0] = jnp.dot(ca, w, preferred_element_type=F32) + b_ref[0]


def _ada_mod(c, w_ada, b_ada):
    depth, d, n6 = w_ada.shape
    b = c.shape[0]
    rows = 16
    c_pad = jnp.zeros((rows, d), F32).at[:b].set(c)
    tn = 1536
    out = pl.pallas_call(
        _ada_kernel,
        out_shape=jax.ShapeDtypeStruct((depth, rows, n6), F32),
        grid=(depth, n6 // tn),
        in_specs=[
            pl.BlockSpec((rows, d), lambda l, j: (0, 0)),
            pl.BlockSpec((1, d, tn), lambda l, j: (l, 0, j)),
            pl.BlockSpec((1, 1, tn), lambda l, j: (l, 0, j)),
        ],
        out_specs=pl.BlockSpec((1, rows, tn), lambda l, j: (l, 0, j)),
        compiler_params=_cparams(2),
        name="ada_mod",
    )(c_pad, w_ada, b_ada.reshape(depth, 1, n6))
    return out[:, :b]


MOD_SH1, MOD_SC1, MOD_G1, MOD_SH2, MOD_SC2, MOD_G2 = range(6)


def _mod_spec(layer, chunk):
    return pl.BlockSpec((None, 1, None, 1, D_MODEL),
                        lambda bi, *_: (layer, bi, chunk, 0, 0))


def _rms_mod(xf, g, sc, sh):
    ms = jnp.mean(xf * xf, axis=-1, keepdims=True)
    return xf * lax.rsqrt(ms + EPS) * g * (1.0 + sc) + sh


def _mix_in_kernel(x_ref, sc_ref, sh_ref, g_ref, w_ref, gq_ref, gk_ref, ones_ref,
                   z_ref, qt_ref, k_ref, vt_ref, wbf_ref):
    first = jnp.logical_and(pl.program_id(0) == 0, pl.program_id(1) == 0)

    @pl.when(first)
    def _():
        wbf_ref[...] = w_ref[...].astype(BF16)

    h = _rms_mod(x_ref[0], g_ref[...], sc_ref[0], sh_ref[0]).astype(BF16)
    proj = jnp.dot(h, wbf_ref[...], preferred_element_type=F32)

    a = proj[:, :D_CONV]
    gate = proj[:, D_CONV:2 * D_CONV]
    z_ref[0] = (a * jax.nn.sigmoid(gate)).astype(BF16)

    def head_norm(t, g):
        sq = (t * t).astype(BF16)
        ss = jnp.concatenate(
            [jnp.dot(sq[:, c:c + ATT_W], ones_ref[...], preferred_element_type=F32)
             for c in range(0, D_ATT, ATT_W)], axis=1)
        return t * lax.rsqrt(ss * (1.0 / HEAD_DIM) + EPS) * g

    o = 2 * D_CONV
    q = head_norm(proj[:, o:o + D_ATT], gq_ref[...])
    k_ref[0] = head_norm(proj[:, o + D_ATT:o + 2 * D_ATT], gk_ref[...]).astype(BF16)
    v = proj[:, o + 2 * D_ATT:]
    tm = v.shape[0]
    for cidx in range(tm // ATT_Q):
        rows = slice(cidx * ATT_Q, (cidx + 1) * ATT_Q)
        qt_ref[0, cidx] = q[rows, :].T.astype(BF16)
        vt_ref[0, cidx] = v[rows, :].T.astype(BF16)


def _mix_in(x, mod5, g, w_in, layer, gq, gk, tm):
    b, s, d = x.shape
    ones_bd = (jnp.arange(ATT_W)[:, None] // HEAD_DIM
               == jnp.arange(ATT_W)[None, :] // HEAD_DIM).astype(BF16)
    gq_t = (jnp.tile(gq, N_HEADS) * (HEAD_DIM ** -0.5)).reshape(1, D_ATT)
    gk_t = jnp.tile(gk, N_HEADS).reshape(1, D_ATT)
    row = lambda bi, ti: (bi, ti, 0)
    const2 = lambda bi, ti: (0, 0)
    return pl.pallas_call(
        _mix_in_kernel,
        out_shape=(
            jax.ShapeDtypeStruct((b, s, D_CONV), BF16),
            jax.ShapeDtypeStruct((b, s // ATT_Q, D_ATT, ATT_Q), BF16),
            jax.ShapeDtypeStruct((b, s, D_ATT), BF16),
            jax.ShapeDtypeStruct((b, s // ATT_Q, D_ATT, ATT_Q), BF16),
        ),
        grid=(b, s // tm),
        in_specs=[
            pl.BlockSpec((1, tm, d), row),
            _mod_spec(layer, MOD_SC1),
            _mod_spec(layer, MOD_SH1),
            pl.BlockSpec((1, d), const2),
            pl.BlockSpec((None, d, D_IN_COLS), lambda bi, ti: (layer, 0, 0),
                         pipeline_mode=pl.Buffered(1)),
            pl.BlockSpec((1, D_ATT), const2),
            pl.BlockSpec((1, D_ATT), const2),
            pl.BlockSpec((ATT_W, ATT_W), const2),
        ],
        out_specs=(
            pl.BlockSpec((1, tm, D_CONV), row),
            pl.BlockSpec((1, tm // ATT_Q, D_ATT, ATT_Q), lambda bi, ti: (bi, ti, 0, 0)),
            pl.BlockSpec((1, tm, D_ATT), row),
            pl.BlockSpec((1, tm // ATT_Q, D_ATT, ATT_Q), lambda bi, ti: (bi, ti, 0, 0)),
        ),
        scratch_shapes=[pltpu.VMEM((d, D_IN_COLS), BF16)],
        compiler_params=_cparams(2),
        name="mix_in",
    )(x, mod5, mod5, g.reshape(1, d), w_in, gq_t, gk_t, ones_bd)


def _conv_kernel(zc_ref, zp_ref, w_ref, cb_ref, lg_ref, lb_ref, o_ref, win_ref, sh_ref,
                 acc_ref):
    tt = zc_ref.shape[1]
    t = pl.program_id(1)
    halo = zp_ref[0].astype(F32)
    win_ref[0:CONV_HALO, :] = jnp.where(t == 0, 0.0, halo)
    win_ref[CONV_HALO:, :] = zc_ref[0].astype(F32)
    span = tt + CONV_HALO - SUBLANES
    for sft in range(1, SUBLANES):
        sh_ref[sft - 1, 0:span, :] = win_ref[sft:sft + span, :]
    base = CONV_HALO - (CONV_WIDTH - 1)
    tiles = CONV_ROWS // SUBLANES

    for g in range(tt // CONV_ROWS):
        r = g * CONV_ROWS
        acc = jnp.zeros((tiles, SUBLANES, D_CONV), F32) + cb_ref[...]
        for j in range(CONV_WIDTH):
            whole, sft = divmod(base + j, SUBLANES)
            start = r + whole * SUBLANES
            if sft == 0:
                tap = win_ref[start:start + CONV_ROWS, :]
            else:
                tap = sh_ref[sft - 1, start:start + CONV_ROWS, :]
            acc = acc + tap.reshape(tiles, SUBLANES, D_CONV) * w_ref[j]
        acc_ref[r:r + CONV_ROWS, :] = acc.reshape(CONV_ROWS, D_CONV)
    acc = acc_ref[...]
    mu = jnp.mean(acc, axis=-1, keepdims=True)
    xc = acc - mu
    var = jnp.mean(xc * xc, axis=-1, keepdims=True)
    y = xc * lax.rsqrt(var + EPS) * lg_ref[...] + lb_ref[...]
    o_ref[0] = _silu(y).astype(BF16)


def _conv_branch(z, conv_w, conv_b, ln_g, ln_b, tt):
    b, s, c = z.shape
    hb = tt // CONV_HALO
    w_tiles = jnp.broadcast_to(conv_w.reshape(CONV_WIDTH, 1, c), (CONV_WIDTH, SUBLANES, c))
    const2 = lambda bi, ti: (0, 0)
    return pl.pallas_call(
        _conv_kernel,
        out_shape=jax.ShapeDtypeStruct((b, s, c), BF16),
        grid=(b, s // tt),
        in_specs=[
            pl.BlockSpec((1, tt, c), lambda bi, ti: (bi, ti, 0)),
            pl.BlockSpec((1, CONV_HALO, c),
                         lambda bi, ti: (bi, jnp.maximum(ti * hb - 1, 0), 0)),
            pl.BlockSpec((CONV_WIDTH, SUBLANES, c), lambda bi, ti: (0, 0, 0)),
            pl.BlockSpec((1, c), const2),
            pl.BlockSpec((1, c), const2),
            pl.BlockSpec((1, c), const2),
        ],
        out_specs=pl.BlockSpec((1, tt, c), lambda bi, ti: (bi, ti, 0)),
        scratch_shapes=[pltpu.VMEM((tt + CONV_HALO, c), F32),
                        pltpu.VMEM((SUBLANES - 1, tt + CONV_HALO, c), F32),
                        pltpu.VMEM((tt, c), F32)],
        compiler_params=_cparams(2),
        name="conv_branch",
    )(z, z, w_tiles, conv_b.reshape(1, c),
      ln_g.reshape(1, c), ln_b.reshape(1, c))


def _attn_kernel(qt_ref, k_ref, vt_ref, bias_ref, o_ref, kpad_ref, vtpad_ref,
                 st0_ref, st1_ref, pb0_ref, pb1_ref, den0_ref, den1_ref):
    st_refs = (st0_ref, st1_ref)
    pb_refs = (pb0_ref, pb1_ref)
    den_refs = (den0_ref, den1_ref)
    s = k_ref.shape[1]
    npad = BAND_PAD // ATT_Q
    kpad_ref[0:BAND_PAD, :] = jnp.zeros((BAND_PAD, ATT_W), BF16)
    kpad_ref[BAND_PAD:, :] = k_ref[0]
    vtpad_ref[0:npad] = jnp.zeros((npad, ATT_W, ATT_Q), BF16)
    vtpad_ref[npad:] = vt_ref[0]

    iota = lambda shape, dim: lax.broadcasted_iota(jnp.int32, shape, dim)
    q_shift = ATT_Q.bit_length() - 1
    d_shift = HEAD_DIM.bit_length() - 1
    qb_mask = (iota((ATT_W, ATT_L), 0) >> d_shift) == (iota((ATT_W, ATT_L), 1) >> q_shift)
    key_row = lax.broadcasted_iota(jnp.int32, (ATT_BAND, ATT_L), 0)

    def scores(m, p):
        r0 = m * ATT_Q
        qt = qt_ref[0, m]
        qb = jnp.where(qb_mask, jnp.concatenate([qt] * ATT_HEADS, axis=1), 0)
        kb = kpad_ref[pl.ds(r0, ATT_BAND), :]
        st_refs[p][...] = jnp.dot(kb, qb.astype(BF16),
                                  preferred_element_type=F32)

    def softmax(m, p, masked):
        st = st_refs[p][...] + bias_ref[0]
        if masked:
            st = jnp.where(key_row >= BAND_PAD - m * ATT_Q, st, NEG_INF)
        mx = jnp.max(st, axis=0, keepdims=True)
        e = jnp.exp(st - mx)
        den_refs[p][...] = jnp.sum(e, axis=0, keepdims=True)
        pb_refs[p][...] = e.astype(BF16)

    def values(m, p):
        r0 = m * ATT_Q
        vb = jnp.concatenate([vtpad_ref[m + c] for c in range(ATT_BAND // ATT_Q)],
                             axis=1)
        ot = jnp.dot(vb, pb_refs[p][...], preferred_element_type=F32)
        ot = ot / den_refs[p][...]
        y = jnp.concatenate(
            [ot[h * HEAD_DIM:(h + 1) * HEAD_DIM, h * ATT_Q:(h + 1) * ATT_Q].T
             for h in range(ATT_HEADS)], axis=1)
        o_ref[0, pl.ds(r0, ATT_Q), :] = y.astype(BF16)

    n = s // ATT_Q
    n_masked = BAND_PAD // ATT_Q
    for m in range(n + 2):
        if m < n:
            scores(m, m % 2)
        if 1 <= m <= n:
            softmax(m - 1, (m - 1) % 2, m - 1 < n_masked)
        if m >= 2:
            values(m - 2, m % 2)


def _attn_bias_t(rel_bias):
    rb = rel_bias.astype(F32).reshape(-1, rel_bias.shape[-1])
    n_rows = rb.shape[0]
    nu = ATT_BAND + ATT_Q - 1
    n_low = BAND_PAD - MAX_REL + ATT_Q
    t = jnp.concatenate([jnp.repeat(rb[:, :1], n_low, axis=1),
                         rb[:, 1:1 + nu - n_low]], axis=1)
    tp = jnp.pad(t, ((0, 0), (0, 1)))
    flat = jnp.tile(tp, (1, ATT_Q + 1))[:, ATT_Q - 1:ATT_Q - 1 + ATT_Q * nu]
    bias = flat.reshape(n_rows, ATT_Q, nu)[:, :, :ATT_BAND]
    bias = bias.reshape(n_rows // ATT_HEADS, ATT_HEADS, ATT_Q, ATT_BAND)
    bias = bias.transpose(0, 3, 1, 2).reshape(n_rows // ATT_HEADS, ATT_BAND, ATT_L)
    r = jnp.arange(ATT_BAND)[:, None]
    qq = jnp.arange(ATT_L)[None, :] % ATT_Q
    first = (qq // CHUNK) * CHUNK
    valid = (r >= first) & (r < first + BAND_PAD + CHUNK)
    return jnp.where(valid[None], bias, NEG_INF)


def _attention(qt, kn, vt, bias_t, layer):
    b, s, _ = kn.shape
    nck = s // ATT_Q
    assert nck >= 2, "the attention pipeline needs at least two query steps"
    return pl.pallas_call(
        _attn_kernel,
        out_shape=jax.ShapeDtypeStruct((b, s, D_ATT), BF16),
        grid=(b, ATT_GROUPS),
        in_specs=[
            pl.BlockSpec((1, nck, ATT_W, ATT_Q), lambda bi, gi: (bi, 0, gi, 0)),
            pl.BlockSpec((1, s, ATT_W), lambda bi, gi: (bi, 0, gi)),
            pl.BlockSpec((1, nck, ATT_W, ATT_Q), lambda bi, gi: (bi, 0, gi, 0)),
            pl.BlockSpec((1, ATT_BAND, ATT_L), lambda bi, gi: (layer * ATT_GROUPS + gi, 0, 0)),
        ],
        out_specs=pl.BlockSpec((1, s, ATT_W), lambda bi, gi: (bi, 0, gi)),
        scratch_shapes=[
            pltpu.VMEM((s + BAND_PAD, ATT_W), BF16),
            pltpu.VMEM((nck + BAND_PAD // ATT_Q, ATT_W, ATT_Q), BF16),
            pltpu.VMEM((ATT_BAND, ATT_L), F32), pltpu.VMEM((ATT_BAND, ATT_L), F32),
            pltpu.VMEM((ATT_BAND, ATT_L), BF16), pltpu.VMEM((ATT_BAND, ATT_L), BF16),
            pltpu.VMEM((1, ATT_L), F32), pltpu.VMEM((1, ATT_L), F32),
        ],
        compiler_params=_cparams(2),
        name="band_attention",
    )(qt, kn, vt, bias_t)


def _mix_out_kernel(x_ref, yc_ref, ya_ref, w_ref, g1_ref, gf_ref, sc_ref, sh_ref,
                    wr_ref, br_ref, x1_ref, h_ref, info_ref, info_t_ref, cnt_ref,
                    wbf_ref, carry_ref):
    first = jnp.logical_and(pl.program_id(0) == 0, pl.program_id(1) == 0)

    @pl.when(first)
    def _():
        wbf_ref[...] = w_ref[...].astype(BF16)
        carry_ref[...] = jnp.zeros_like(carry_ref)

    y = jnp.dot(yc_ref[0], wbf_ref[0:D_CONV, :], preferred_element_type=F32)
    y = y + jnp.dot(ya_ref[0], wbf_ref[D_CONV:, :], preferred_element_type=F32)
    x1 = x_ref[0] + g1_ref[0] * y
    x1_ref[0] = x1
    h = _rms_mod(x1, gf_ref[...], sc_ref[0], sh_ref[0])
    h_ref[0] = _pack_bf16_pairs(h)
    logits = jnp.dot(h.astype(BF16), wr_ref[...], preferred_element_type=F32) + br_ref[...]
    _route_tile(logits, info_ref, info_t_ref, cnt_ref, carry_ref)


def _mix_out_routed(x, yc, ya, w_out, layer, mod5, gf, w_router, b_router, tm):
    b, s, d = x.shape
    nt = s // tm
    row = lambda bi, ti: (bi, ti, 0)
    const2 = lambda bi, ti: (0, 0)
    w_pad = jnp.zeros((d, LANE_PAD_E), BF16).at[:, :N_EXPERTS].set(w_router.astype(BF16))
    b_pad = jnp.full((1, LANE_PAD_E), -jnp.inf, F32).at[0, :N_EXPERTS].set(
        b_router.astype(F32))
    return pl.pallas_call(
        _mix_out_kernel,
        out_shape=(jax.ShapeDtypeStruct((b, s, d), F32),
                   jax.ShapeDtypeStruct((b, s, d // 2), jnp.int32),
                   jax.ShapeDtypeStruct((b, s, LANE_PAD_E), F32),
                   jax.ShapeDtypeStruct((SUBLANES, b * s), F32),
                   jax.ShapeDtypeStruct((SUBLANES, LANE_PAD_E), F32)),
        grid=(b, nt),
        in_specs=[
            pl.BlockSpec((1, tm, d), row),
            pl.BlockSpec((1, tm, D_CONV), row),
            pl.BlockSpec((1, tm, D_ATT), row),
            pl.BlockSpec((None, d, d), lambda bi, ti: (layer, 0, 0),
                         pipeline_mode=pl.Buffered(1)),
            _mod_spec(layer, MOD_G1),
            pl.BlockSpec((1, d), const2),
            _mod_spec(layer, MOD_SC2),
            _mod_spec(layer, MOD_SH2),
            pl.BlockSpec((d, LANE_PAD_E), const2),
            pl.BlockSpec((1, LANE_PAD_E), const2),
        ],
        out_specs=(pl.BlockSpec((1, tm, d), row), pl.BlockSpec((1, tm, d // 2), row),
                   pl.BlockSpec((1, tm, LANE_PAD_E), row),
                   pl.BlockSpec((SUBLANES, tm), lambda bi, ti: (0, bi * nt + ti)),
                   pl.BlockSpec((SUBLANES, LANE_PAD_E), const2)),
        scratch_shapes=[pltpu.VMEM((d, d), BF16), pltpu.VMEM((SUBLANES, LANE_PAD_E), F32)],
        compiler_params=_cparams(2),
        name="mix_out",
    )(x, yc, ya, w_out, mod5, gf.reshape(1, d), mod5, mod5, w_pad, b_pad)


def _cast_ffn_kernel(wg_ref, wu_ref, wd_ref, wo_ref, wg3_ref, wu3_ref, wd3_ref, wob_ref):
    wg3_ref[0] = wg_ref[...].astype(BF16)
    wu3_ref[0] = wu_ref[...].astype(BF16)
    wd3_ref[0] = wd_ref[...].astype(BF16)

    @pl.when(pl.program_id(0) == 0)
    def _():
        wob_ref[...] = wo_ref[...].astype(BF16)


def _cast_ffn_weights(wg, wu, wd, w_out, layer, fc):
    d, ff = wg.shape
    nf = ff // fc
    return pl.pallas_call(
        _cast_ffn_kernel,
        out_shape=(jax.ShapeDtypeStruct((nf, d, fc), BF16),
                   jax.ShapeDtypeStruct((nf, d, fc), BF16),
                   jax.ShapeDtypeStruct((nf, fc, d), BF16),
                   jax.ShapeDtypeStruct((d, d), BF16)),
        grid=(nf,),
        in_specs=[
            pl.BlockSpec((d, fc), lambda f: (0, f)),
            pl.BlockSpec((d, fc), lambda f: (0, f)),
            pl.BlockSpec((fc, d), lambda f: (f, 0)),
            pl.BlockSpec((None, d, d), lambda f: (layer, 0, 0)),
        ],
        out_specs=(pl.BlockSpec((1, d, fc), lambda f: (f, 0, 0)),
                   pl.BlockSpec((1, d, fc), lambda f: (f, 0, 0)),
                   pl.BlockSpec((1, fc, d), lambda f: (f, 0, 0)),
                   pl.BlockSpec((d, d), lambda f: (0, 0))),
        compiler_params=_cparams(1),
        name="cast_ffn_weights",
    )(wg, wu, wd, w_out)


def _mix_ffn_kernel(x_ref, yc_ref, ya_ref, wo_ref, g1_ref, gf_ref, sc_ref, sh_ref, g2_ref,
                    wg_ref, wu_ref, wd_ref, o_ref, acc_ref):
    y = jnp.dot(yc_ref[0], wo_ref[0:D_CONV, :], preferred_element_type=F32)
    y = y + jnp.dot(ya_ref[0], wo_ref[D_CONV:, :], preferred_element_type=F32)
    x1 = x_ref[0] + g1_ref[0] * y
    h = _rms_mod(x1, gf_ref[...], sc_ref[0], sh_ref[0]).astype(BF16)
    for c in range(wg_ref.shape[0]):
        a = jnp.dot(h, wg_ref[c], preferred_element_type=F32)
        u = jnp.dot(h, wu_ref[c], preferred_element_type=F32)
        t = (_silu(a) * u).astype(BF16)
        dn = jnp.dot(t, wd_ref[c], preferred_element_type=F32)
        if c == 0:
            acc_ref[...] = dn
        else:
            acc_ref[...] += dn
    o_ref[0] = x1 + g2_ref[0] * acc_ref[...]


def _mix_out_dense_ffn(x, yc, ya, w_out, layer, mod5, gf, wg, wu, wd, tm, fc):
    b, s, d = x.shape
    wg3, wu3, wd3, wob = _cast_ffn_weights(wg, wu, wd, w_out, layer, fc)
    nf = wg3.shape[0]
    row = lambda bi, ti: (bi, ti, 0)
    const2 = lambda bi, ti: (0, 0)
    const3 = lambda bi, ti: (0, 0, 0)
    resident = pl.Buffered(1)
    return pl.pallas_call(
        _mix_ffn_kernel,
        out_shape=jax.ShapeDtypeStruct((b, s, d), F32),
        grid=(b, s // tm),
        in_specs=[
            pl.BlockSpec((1, tm, d), row),
            pl.BlockSpec((1, tm, D_CONV), row),
            pl.BlockSpec((1, tm, D_ATT), row),
            pl.BlockSpec((d, d), const2, pipeline_mode=resident),
            _mod_spec(layer, MOD_G1),
            pl.BlockSpec((1, d), const2),
            _mod_spec(layer, MOD_SC2),
            _mod_spec(layer, MOD_SH2),
            _mod_spec(layer, MOD_G2),
            pl.BlockSpec((nf, d, fc), const3, pipeline_mode=resident),
            pl.BlockSpec((nf, d, fc), const3, pipeline_mode=resident),
            pl.BlockSpec((nf, fc, d), const3, pipeline_mode=resident),
        ],
        out_specs=pl.BlockSpec((1, tm, d), row),
        scratch_shapes=[pltpu.VMEM((tm, d), F32)],
        compiler_params=_cparams(2),
        name="mix_out_dense_ffn",
    )(x, yc, ya, wob, mod5, gf.reshape(1, d), mod5, mod5, mod5, wg3, wu3, wd3)


def _route_tile(logits, info_ref, info_t_ref, cnt_ref, carry_ref):
    tr = logits.shape[0]
    lane = lax.broadcasted_iota(jnp.int32, (tr, LANE_PAD_E), 1).astype(F32)
    no_lane = float(LANE_PAD_E)
    v0 = jnp.max(logits, axis=-1, keepdims=True)
    i0 = jnp.min(jnp.where(logits == v0, lane, no_lane), axis=-1, keepdims=True)
    rest = jnp.where(lane == i0, -jnp.inf, logits)
    v1 = jnp.max(rest, axis=-1, keepdims=True)
    i1 = jnp.min(jnp.where(rest == v1, lane, no_lane), axis=-1, keepdims=True)
    e1 = jnp.exp(v1 - v0)
    w0 = 1.0 / (1.0 + e1)
    w1 = e1 / (1.0 + e1)
    oh0 = lane == i0
    oh1 = lane == i1
    cnt = jnp.where(jnp.logical_or(oh0, oh1), 1.0, 0.0)
    tri = (lax.broadcasted_iota(jnp.int32, (tr, tr), 1)
           < lax.broadcasted_iota(jnp.int32, (tr, tr), 0)).astype(BF16)
    before = jnp.dot(tri, cnt.astype(BF16), preferred_element_type=F32) + carry_ref[0:1, :]
    r0 = jnp.sum(jnp.where(oh0, before, 0.0), axis=-1, keepdims=True)
    r1 = jnp.sum(jnp.where(oh1, before, 0.0), axis=-1, keepdims=True)
    carry_ref[...] = carry_ref[...] + jnp.sum(cnt, axis=0, keepdims=True)
    cnt_ref[...] = carry_ref[...]
    info = jnp.where(lane == 0, i0, 0.0)
    info = jnp.where(lane == 1, i1, info)
    info = jnp.where(lane == 2, w0, info)
    info = jnp.where(lane == 3, w1, info)
    info = jnp.where(lane == 4, r0, info)
    info = jnp.where(lane == 5, r1, info)
    info_ref[0] = info
    for c in range(tr // LANES):
        blk = info[c * LANES:(c + 1) * LANES, :].T
        info_t_ref[:, c * LANES:(c + 1) * LANES] = blk[0:SUBLANES, :]


def _moe_kernel(te_ref, tn_ref, xs_ref, wg_ref, wu_ref, wd_ref, ys_ref,
                acc_ref, wgb_ref, wub_ref, wdb_ref):
    i = pl.program_id(0)
    f = pl.program_id(1)
    nhalf = tn_ref[i]
    nfull = nhalf // (MOE_SUB // MOE_HALF)

    @pl.when(f == 0)
    def _():
        acc_ref[...] = jnp.zeros_like(acc_ref)

    def block(r, rows, wgb, wub, wdb):
        xb = _unpack_bf16_pairs(xs_ref[pl.ds(r, rows), :]).astype(BF16)
        a = jnp.dot(xb, wgb, preferred_element_type=F32)
        u = jnp.dot(xb, wub, preferred_element_type=F32)
        t = (_silu(a) * u).astype(BF16)
        acc_ref[pl.ds(r, rows), :] += jnp.dot(t, wdb, preferred_element_type=F32)

    def split(units):
        out, r = [], 0
        for rows in (MOE_SUB,) * (units * MOE_HALF // MOE_SUB) + MOE_TAILS:
            if units * MOE_HALF - r >= rows:
                out.append((r, rows))
                r += rows
        return out

    for units in MOE_STATIC_UNITS:
        @pl.when(nhalf == units)
        def _(units=units):
            wgb, wub, wdb = (w[0].astype(BF16) for w in (wg_ref, wu_ref, wd_ref))
            for r, rows in split(units):
                block(r, rows, wgb, wub, wdb)

    other = nhalf > 0
    for units in MOE_STATIC_UNITS:
        other = jnp.logical_and(other, nhalf != units)

    @pl.when(other)
    def _():
        wgb_ref[...] = wg_ref[0].astype(BF16)
        wub_ref[...] = wu_ref[0].astype(BF16)
        wdb_ref[...] = wd_ref[0].astype(BF16)

        def sub(sidx, carry):
            block(pl.multiple_of(sidx * MOE_SUB, MOE_SUB), MOE_SUB,
                  wgb_ref[...], wub_ref[...], wdb_ref[...])
            return carry

        lax.fori_loop(0, nfull, sub, 0)
        done = nfull * MOE_SUB
        for rows in MOE_TAILS:
            units = rows // MOE_HALF

            @pl.when((nhalf & units) != 0)
            def _(rows=rows, units=units):
                higher = nhalf & (MOE_SUB // MOE_HALF - 1) & ~(2 * units - 1)
                block(pl.multiple_of(done + higher * MOE_HALF, MOE_HALF), rows,
                      wgb_ref[...], wub_ref[...], wdb_ref[...])

    @pl.when(f == pl.num_programs(1) - 1)
    def _():
        ys_ref[...] = _pack_bf16_pairs(acc_ref[...])


def _moe_ffn(xs, tile_e, tile_nsub, wg, wu, wd):
    rpad, dw = xs.shape
    d = 2 * dw
    ntiles = rpad // MOE_TILE
    ff = wg.shape[2]
    nf = ff // MOE_FC

    def fcol(i, f, tn):
        return jnp.where(tn[i] > 0, f, nf - 1)

    return pl.pallas_call(
        _moe_kernel,
        out_shape=jax.ShapeDtypeStruct((rpad, dw), jnp.int32),
        grid_spec=pltpu.PrefetchScalarGridSpec(
            num_scalar_prefetch=2,
            grid=(ntiles, nf),
            in_specs=[
                pl.BlockSpec((MOE_TILE, dw), lambda i, f, te, tn: (i, 0)),
                pl.BlockSpec((1, d, MOE_FC), lambda i, f, te, tn: (te[i], 0, fcol(i, f, tn))),
                pl.BlockSpec((1, d, MOE_FC), lambda i, f, te, tn: (te[i], 0, fcol(i, f, tn))),
                pl.BlockSpec((1, MOE_FC, d), lambda i, f, te, tn: (te[i], fcol(i, f, tn), 0)),
            ],
            out_specs=pl.BlockSpec((MOE_TILE, dw), lambda i, f, te, tn: (i, 0)),
            scratch_shapes=[
                pltpu.VMEM((MOE_TILE, d), F32),
                pltpu.VMEM((d, MOE_FC), BF16),
                pltpu.VMEM((d, MOE_FC), BF16),
                pltpu.VMEM((MOE_FC, d), BF16),
            ],
        ),
        compiler_params=_cparams(2),
        name="moe_ffn",
    )(tile_e, tile_nsub, xs, wg, wu, wd)


def _combine_kernel(x_ref, y0_ref, y1_ref, info_ref, g2_ref, o_ref):
    info = info_ref[0]
    w0 = info[:, 2:3]
    w1 = info[:, 3:4]
    f = w0 * _unpack_bf16_pairs(y0_ref[0, 0]) + w1 * _unpack_bf16_pairs(y1_ref[0, 0])
    o_ref[0] = x_ref[0] + g2_ref[0] * f


def _combine(x1, y01, info, mod5, layer, tm):
    b, s, d = x1.shape
    row = lambda bi, ti: (bi, ti, 0)
    return pl.pallas_call(
        _combine_kernel,
        out_shape=jax.ShapeDtypeStruct((b, s, d), F32),
        grid=(b, s // tm),
        in_specs=[
            pl.BlockSpec((1, tm, d), row),
            pl.BlockSpec((1, 1, tm, d // 2), lambda bi, ti: (0, bi, ti, 0)),
            pl.BlockSpec((1, 1, tm, d // 2), lambda bi, ti: (1, bi, ti, 0)),
            pl.BlockSpec((1, tm, LANE_PAD_E), row),
            _mod_spec(layer, MOD_G2),
        ],
        out_specs=pl.BlockSpec((1, tm, d), row),
        compiler_params=_cparams(2),
        name="moe_combine",
    )(x1, y01, y01, info, mod5)


def _moe_layer(hp, info, info_t, cnt, x1, mod5, layer, wg, wu, wd, tm):
    b, s, d = x1.shape
    n = b * s
    e0 = info_t[0].astype(jnp.int32)
    e1 = info_t[1].astype(jnp.int32)
    r0 = info_t[4].astype(jnp.int32)
    r1 = info_t[5].astype(jnp.int32)
    counts = cnt[0, :N_EXPERTS].astype(jnp.int32)

    ntiles = (2 * n) // MOE_TILE + N_EXPERTS
    tiles_per_e = (counts + MOE_TILE - 1) // MOE_TILE
    tile_end = jnp.cumsum(tiles_per_e)
    tile_start = tile_end - tiles_per_e
    total = tile_end[-1]
    tidx = jnp.arange(ntiles, dtype=jnp.int32)
    live = tidx < total
    tclip = jnp.minimum(tidx, total - 1)
    tile_e = jnp.minimum(jnp.sum(tclip[:, None] >= tile_end[None, :], axis=1),
                         N_EXPERTS - 1).astype(jnp.int32)
    rows_left = counts[tile_e] - (tclip - tile_start[tile_e]) * MOE_TILE
    rows_here = jnp.clip(rows_left, 0, MOE_TILE)
    tile_nsub = jnp.where(live, (rows_here + MOE_HALF - 1) // MOE_HALF, 0).astype(jnp.int32)

    row_start = tile_start * MOE_TILE
    eid = jnp.arange(N_EXPERTS, dtype=jnp.int32)[None, :]
    pos0 = jnp.sum(jnp.where(e0[:, None] == eid, row_start[None, :], 0), axis=1) + r0
    pos1 = jnp.sum(jnp.where(e1[:, None] == eid, row_start[None, :], 0), axis=1) + r1

    xs = _sc_scatter_rows2(hp.reshape(n, d // 2), pos0, pos1, ntiles * MOE_TILE)
    ys = _moe_ffn(xs, tile_e, tile_nsub, wg, wu, wd)
    y01 = _sc_gather_rows(ys, jnp.concatenate([pos0, pos1]))
    return _combine(x1, y01.reshape(2, b, s, d // 2), info, mod5, layer, tm)


def kernel(x, c, w_ada, b_ada, norm_mix_g, norm_ffn_g, w_in, w_out, conv_w, conv_b,
           conv_ln_g, conv_ln_b, q_norm_g, k_norm_g, rel_bias, ffn_w_gate, ffn_w_up,
           ffn_w_down, moe_w_router, moe_b_router, moe_w_gate, moe_w_up, moe_w_down):
    b, s, d = x.shape
    depth = w_ada.shape[0]
    tm = min(1024, s)
    mod5 = _ada_mod(c, w_ada, b_ada).reshape(depth, b, 6, 1, d)
    bias_t = _attn_bias_t(rel_bias)
    for l in range(depth):
        z, qt, kn, vt = _mix_in(x, mod5, norm_mix_g[l], w_in, l,
                                q_norm_g[l], k_norm_g[l], tm)
        yc = _conv_branch(z, conv_w[l], conv_b[l], conv_ln_g[l], conv_ln_b[l], tm)
        ya = _attention(qt, kn, vt, bias_t, l)
        i = l // 2
        if l % 2 == 0:
            x = _mix_out_dense_ffn(x, yc, ya, w_out, l, mod5, norm_ffn_g[l],
                                   ffn_w_gate[i], ffn_w_up[i], ffn_w_down[i],
                                   min(1024, s), 256)
        else:
            x1, hp, info, info_t, cnt = _mix_out_routed(
                x, yc, ya, w_out, l, mod5, norm_ffn_g[l],
                moe_w_router[i], moe_b_router[i], tm)
            x = _moe_layer(hp, info, info_t, cnt, x1, mod5, l,
                           moe_w_gate[i], moe_w_up[i], moe_w_down[i], tm)
    return x
```

```python
import functools

import jax
import jax.numpy as jnp
from jax import lax
from jax.experimental import pallas as pl
from jax.experimental.pallas import tpu as pltpu
from jax.experimental.pallas import tpu_sc as plsc

F32 = jnp.float32
BF16 = jnp.bfloat16

D_MODEL = 1024
CHUNK = 64
N_PREV_CHUNKS = 8
BAND_PAD = N_PREV_CHUNKS * CHUNK
D_CONV = 512
D_ATT = 512
HEAD_DIM = 64
N_HEADS = 8
CONV_WIDTH = 31
MAX_REL = 128
D_IN_COLS = 2 * D_CONV + 3 * D_ATT
N_EXPERTS = 8
EPS = 1e-6
NEG_INF = -1e30

LANES = 128
SUBLANES = 8
VMEM_LIMIT_BYTES = 56 * 1024 * 1024

ATT_HEADS = 4
ATT_GROUPS = N_HEADS // ATT_HEADS
ATT_W = ATT_HEADS * HEAD_DIM
ATT_Q = 2 * CHUNK
ATT_BAND = BAND_PAD + ATT_Q
ATT_L = ATT_HEADS * ATT_Q

CONV_HALO = 32
CONV_ROWS = 32
LANE_PAD_E = LANES

MOE_HALF = 256
MOE_SUB = 4 * MOE_HALF
MOE_TAILS = (2 * MOE_HALF, MOE_HALF)
MOE_TILE = 9 * MOE_HALF
MOE_STATIC_UNITS = (9, 8, 7)
MOE_FC = 512


def _cparams(n_axes, vmem=VMEM_LIMIT_BYTES):
    return pltpu.CompilerParams(
        dimension_semantics=("arbitrary",) * n_axes, vmem_limit_bytes=vmem)


def _silu(v):
    return v * jax.nn.sigmoid(v)


def _pack_bf16_pairs(v):
    w = v.shape[1] // 2
    bits = lax.bitcast_convert_type(v.astype(BF16).astype(F32), jnp.uint32)
    packed = (bits[:, w:] & jnp.uint32(0xFFFF0000)) | (bits[:, :w] >> 16)
    return lax.bitcast_convert_type(packed, jnp.int32)


def _unpack_bf16_pairs(p):
    bits = lax.bitcast_convert_type(p, jnp.uint32)
    lo = lax.bitcast_convert_type(bits << 16, F32)
    hi = lax.bitcast_convert_type(bits & jnp.uint32(0xFFFF0000), F32)
    return jnp.concatenate([lo, hi], axis=1)


SC_CORES = 2
SC_SUBCORES = 16
SC_WORKERS = SC_CORES * SC_SUBCORES
SC_CHUNK = 64


def _sc_worker_id():
    return lax.axis_index("s") * SC_CORES + lax.axis_index("c")


def _sc_mesh():
    return plsc.VectorSubcoreMesh(core_axis_name="c", subcore_axis_name="s")


def _sc_gather_rows(table, idx):
    _, w = table.shape
    b = idx.shape[0]
    per_w = b // SC_WORKERS
    nch = per_w // SC_CHUNK

    def body(table_hbm, idx_hbm, out_hbm, idx_v, rows_v, gsem, wsem):
        wid = _sc_worker_id()
        base = wid * per_w
        pltpu.sync_copy(idx_hbm.at[wid], idx_v)
        gathers = [None] * nch
        writes = [None] * nch
        gathers[0] = pltpu.async_copy(table_hbm.at[idx_v.at[0]], rows_v.at[0], gsem.at[0])
        for c in range(nch):
            slot = c % 2
            gathers[c].wait()
            if c + 1 < nch:
                if c >= 1:
                    writes[c - 1].wait()
                gathers[c + 1] = pltpu.async_copy(
                    table_hbm.at[idx_v.at[c + 1]], rows_v.at[1 - slot], gsem.at[1 - slot])
            writes[c] = pltpu.async_copy(
                rows_v.at[slot], out_hbm.at[pl.ds(base + c * SC_CHUNK, SC_CHUNK)], wsem.at[slot])
        if nch >= 2:
            writes[nch - 2].wait()
        writes[nch - 1].wait()

    call = pl.kernel(
        body, mesh=_sc_mesh(),
        out_type=jax.ShapeDtypeStruct((b, w), jnp.int32),
        scratch_types=[pltpu.VMEM((nch, SC_CHUNK), jnp.int32),
                       pltpu.VMEM((2, SC_CHUNK, w), jnp.int32),
                       pltpu.SemaphoreType.DMA((2,)), pltpu.SemaphoreType.DMA((2,))],
        name="sc_gather_rows")
    return call(table, idx.reshape(SC_WORKERS, nch, SC_CHUNK))


def _sc_scatter_rows2(src, idx0, idx1, rows_out):
    n, w = src.shape
    per_w = n // SC_WORKERS
    nch = per_w // SC_CHUNK

    def body(src_hbm, i0_hbm, i1_hbm, out_hbm, i0_v, i1_v, rows_v, rsem, wsem):
        wid = _sc_worker_id()
        base = wid * per_w
        pltpu.sync_copy(i0_hbm.at[wid], i0_v)
        pltpu.sync_copy(i1_hbm.at[wid], i1_v)
        reads = [None] * nch
        writes = [None] * nch
        reads[0] = pltpu.async_copy(src_hbm.at[pl.ds(base, SC_CHUNK)], rows_v.at[0], rsem.at[0])
        for c in range(nch):
            slot = c % 2
            reads[c].wait()
            if c + 1 < nch:
                if c >= 1:
                    for wr in writes[c - 1]:
                        wr.wait()
                reads[c + 1] = pltpu.async_copy(
                    src_hbm.at[pl.ds(base + (c + 1) * SC_CHUNK, SC_CHUNK)],
                    rows_v.at[1 - slot], rsem.at[1 - slot])
            writes[c] = (
                pltpu.async_copy(rows_v.at[slot], out_hbm.at[i0_v.at[c]], wsem.at[slot, 0]),
                pltpu.async_copy(rows_v.at[slot], out_hbm.at[i1_v.at[c]], wsem.at[slot, 1]),
            )
        for c in range(max(nch - 2, 0), nch):
            for wr in writes[c]:
                wr.wait()

    call = pl.kernel(
        body, mesh=_sc_mesh(),
        out_type=jax.ShapeDtypeStruct((rows_out, w), jnp.int32),
        scratch_types=[pltpu.VMEM((nch, SC_CHUNK), jnp.int32),
                       pltpu.VMEM((nch, SC_CHUNK), jnp.int32),
                       pltpu.VMEM((2, SC_CHUNK, w), jnp.int32),
                       pltpu.SemaphoreType.DMA((2,)), pltpu.SemaphoreType.DMA((2, 2))],
        name="sc_scatter_rows")
    shape3 = (SC_WORKERS, nch, SC_CHUNK)
    return call(src, idx0.reshape(shape3), idx1.reshape(shape3))


def _ada_kernel(c_ref, w_ref, b_ref, o_ref):
    ca = _silu(c_ref[...]).astype(BF16)
    w = w_ref[0].astype(BF16)
    o_ref[0] = jnp.dot(ca, w, preferred_element_type=F32) + b_ref[0]


def _ada_mod(c, w_ada, b_ada):
    depth, d, n6 = w_ada.shape
    b = c.shape[0]
    rows = 16
    c_pad = jnp.zeros((rows, d), F32).at[:b].set(c)
    tn = 1536
    out = pl.pallas_call(
        _ada_kernel,
        out_shape=jax.ShapeDtypeStruct((depth, rows, n6), F32),
        grid=(depth, n6 // tn),
        in_specs=[
            pl.BlockSpec((rows, d), lambda l, j: (0, 0)),
            pl.BlockSpec((1, d, tn), lambda l, j: (l, 0, j)),
            pl.BlockSpec((1, 1, tn), lambda l, j: (l, 0, j)),
        ],
        out_specs=pl.BlockSpec((1, rows, tn), lambda l, j: (l, 0, j)),
        compiler_params=_cparams(2),
        name="ada_mod",
    )(c_pad, w_ada, b_ada.reshape(depth, 1, n6))
    return out[:, :b]


MOD_SH1, MOD_SC1, MOD_G1, MOD_SH2, MOD_SC2, MOD_G2 = range(6)


def _mod_spec(layer, chunk):
    return pl.BlockSpec((None, 1, None, 1, D_MODEL),
                        lambda bi, *_: (layer, bi, chunk, 0, 0))


def _rms_mod(xf, g, sc, sh):
    ms = jnp.mean(xf * xf, axis=-1, keepdims=True)
    return xf * lax.rsqrt(ms + EPS) * g * (1.0 + sc) + sh


def _mix_in_kernel(x_ref, sc_ref, sh_ref, g_ref, w_ref, gq_ref, gk_ref, ones_ref,
                   z_ref, qt_ref, k_ref, vt_ref, wbf_ref):
    first = jnp.logical_and(pl.program_id(0) == 0, pl.program_id(1) == 0)

    @pl.when(first)
    def _():
        wbf_ref[...] = w_ref[...].astype(BF16)

    h = _rms_mod(x_ref[0], g_ref[...], sc_ref[0], sh_ref[0]).astype(BF16)
    proj = jnp.dot(h, wbf_ref[...], preferred_element_type=F32)

    a = proj[:, :D_CONV]
    gate = proj[:, D_CONV:2 * D_CONV]
    z_ref[0] = (a * jax.nn.sigmoid(gate)).astype(BF16)

    def head_norm(t, g):
        sq = (t * t).astype(BF16)
        ss = jnp.concatenate(
            [jnp.dot(sq[:, c:c + ATT_W], ones_ref[...], preferred_element_type=F32)
             for c in range(0, D_ATT, ATT_W)], axis=1)
        return t * lax.rsqrt(ss * (1.0 / HEAD_DIM) + EPS) * g

    o = 2 * D_CONV
    q = head_norm(proj[:, o:o + D_ATT], gq_ref[...])
    k_ref[0] = head_norm(proj[:, o + D_ATT:o + 2 * D_ATT], gk_ref[...]).astype(BF16)
    v = proj[:, o + 2 * D_ATT:]
    tm = v.shape[0]
    for cidx in range(tm // ATT_Q):
        rows = slice(cidx * ATT_Q, (cidx + 1) * ATT_Q)
        qt_ref[0, cidx] = q[rows, :].T.astype(BF16)
    vt_ref[0] = v.astype(BF16)


def _mix_in(x, mod5, g, w_in, layer, gq, gk, tm):
    b, s, d = x.shape
    ones_bd = (jnp.arange(ATT_W)[:, None] // HEAD_DIM
               == jnp.arange(ATT_W)[None, :] // HEAD_DIM).astype(BF16)
    gq_t = (jnp.tile(gq, N_HEADS) * (HEAD_DIM ** -0.5)).reshape(1, D_ATT)
    gk_t = jnp.tile(gk, N_HEADS).reshape(1, D_ATT)
    row = lambda bi, ti: (bi, ti, 0)
    const2 = lambda bi, ti: (0, 0)
    return pl.pallas_call(
        _mix_in_kernel,
        out_shape=(
            jax.ShapeDtypeStruct((b, s, D_CONV), BF16),
            jax.ShapeDtypeStruct((b, s // ATT_Q, D_ATT, ATT_Q), BF16),
            jax.ShapeDtypeStruct((b, s, D_ATT), BF16),
            jax.ShapeDtypeStruct((b, s, D_ATT), BF16),
        ),
        grid=(b, s // tm),
        in_specs=[
            pl.BlockSpec((1, tm, d), row),
            _mod_spec(layer, MOD_SC1),
            _mod_spec(layer, MOD_SH1),
            pl.BlockSpec((1, d), const2),
            pl.BlockSpec((None, d, D_IN_COLS), lambda bi, ti: (layer, 0, 0),
                         pipeline_mode=pl.Buffered(1)),
            pl.BlockSpec((1, D_ATT), const2),
            pl.BlockSpec((1, D_ATT), const2),
            pl.BlockSpec((ATT_W, ATT_W), const2),
        ],
        out_specs=(
            pl.BlockSpec((1, tm, D_CONV), row),
            pl.BlockSpec((1, tm // ATT_Q, D_ATT, ATT_Q), lambda bi, ti: (bi, ti, 0, 0)),
            pl.BlockSpec((1, tm, D_ATT), row),
            pl.BlockSpec((1, tm, D_ATT), row),
        ),
        scratch_shapes=[pltpu.VMEM((d, D_IN_COLS), BF16)],
        compiler_params=_cparams(2),
        name="mix_in",
    )(x, mod5, mod5, g.reshape(1, d), w_in, gq_t, gk_t, ones_bd)


def _conv_kernel(zc_ref, zp_ref, w_ref, cb_ref, lg_ref, lb_ref, o_ref, win_ref, sh_ref,
                 acc_ref):
    tt = zc_ref.shape[1]
    t = pl.program_id(1)
    halo = zp_ref[0].astype(F32)
    win_ref[0:CONV_HALO, :] = jnp.where(t == 0, 0.0, halo)
    win_ref[CONV_HALO:, :] = zc_ref[0].astype(F32)
    span = tt + CONV_HALO - SUBLANES
    for sft in range(1, SUBLANES):
        sh_ref[sft - 1, 0:span, :] = win_ref[sft:sft + span, :]
    base = CONV_HALO - (CONV_WIDTH - 1)
    tiles = CONV_ROWS // SUBLANES

    for g in range(tt // CONV_ROWS):
        r = g * CONV_ROWS
        acc = jnp.zeros((tiles, SUBLANES, D_CONV), F32) + cb_ref[...]
        for j in range(CONV_WIDTH):
            whole, sft = divmod(base + j, SUBLANES)
            start = r + whole * SUBLANES
            if sft == 0:
                tap = win_ref[start:start + CONV_ROWS, :]
            else:
                tap = sh_ref[sft - 1, start:start + CONV_ROWS, :]
            acc = acc + tap.reshape(tiles, SUBLANES, D_CONV) * w_ref[j]
        acc_ref[r:r + CONV_ROWS, :] = acc.reshape(CONV_ROWS, D_CONV)
    acc = acc_ref[...]
    mu = jnp.mean(acc, axis=-1, keepdims=True)
    xc = acc - mu
    var = jnp.mean(xc * xc, axis=-1, keepdims=True)
    y = xc * lax.rsqrt(var + EPS) * lg_ref[...] + lb_ref[...]
    o_ref[0] = _silu(y).astype(BF16)


def _conv_branch(z, conv_w, conv_b, ln_g, ln_b, tt):
    b, s, c = z.shape
    hb = tt // CONV_HALO
    w_tiles = jnp.broadcast_to(conv_w.reshape(CONV_WIDTH, 1, c), (CONV_WIDTH, SUBLANES, c))
    const2 = lambda bi, ti: (0, 0)
    return pl.pallas_call(
        _conv_kernel,
        out_shape=jax.ShapeDtypeStruct((b, s, c), BF16),
        grid=(b, s // tt),
        in_specs=[
            pl.BlockSpec((1, tt, c), lambda bi, ti: (bi, ti, 0)),
            pl.BlockSpec((1, CONV_HALO, c),
                         lambda bi, ti: (bi, jnp.maximum(ti * hb - 1, 0), 0)),
            pl.BlockSpec((CONV_WIDTH, SUBLANES, c), lambda bi, ti: (0, 0, 0)),
            pl.BlockSpec((1, c), const2),
            pl.BlockSpec((1, c), const2),
            pl.BlockSpec((1, c), const2),
        ],
        out_specs=pl.BlockSpec((1, tt, c), lambda bi, ti: (bi, ti, 0)),
        scratch_shapes=[pltpu.VMEM((tt + CONV_HALO, c), F32),
                        pltpu.VMEM((SUBLANES - 1, tt + CONV_HALO, c), F32),
                        pltpu.VMEM((tt, c), F32)],
        compiler_params=_cparams(2),
        name="conv_branch",
    )(z, z, w_tiles, conv_b.reshape(1, c),
      ln_g.reshape(1, c), ln_b.reshape(1, c))


def _attn_kernel(qt_ref, k_ref, vt_ref, bias_ref, o_ref, kpad_ref, vtpad_ref,
                 st0_ref, st1_ref, pb0_ref, pb1_ref, den0_ref, den1_ref):
    st_refs = (st0_ref, st1_ref)
    pb_refs = (pb0_ref, pb1_ref)
    den_refs = (den0_ref, den1_ref)
    s = k_ref.shape[1]
    npad = BAND_PAD // ATT_Q
    kpad_ref[0:BAND_PAD, :] = jnp.zeros((BAND_PAD, ATT_W), BF16)
    kpad_ref[BAND_PAD:, :] = k_ref[0]
    vtpad_ref[0:BAND_PAD, :] = jnp.zeros((BAND_PAD, ATT_W), BF16)
    vtpad_ref[BAND_PAD:, :] = vt_ref[0]

    iota = lambda shape, dim: lax.broadcasted_iota(jnp.int32, shape, dim)
    q_shift = ATT_Q.bit_length() - 1
    d_shift = HEAD_DIM.bit_length() - 1
    qb_mask = (iota((ATT_W, ATT_L), 0) >> d_shift) == (iota((ATT_W, ATT_L), 1) >> q_shift)
    key_row = lax.broadcasted_iota(jnp.int32, (ATT_BAND, ATT_L), 0)

    def scores(m, p):
        r0 = m * ATT_Q
        qt = qt_ref[0, m]
        qb = jnp.where(qb_mask, jnp.concatenate([qt] * ATT_HEADS, axis=1), 0)
        kb = kpad_ref[pl.ds(r0, ATT_BAND), :]
        st_refs[p][...] = jnp.dot(kb, qb.astype(BF16),
                                  preferred_element_type=F32)

    def softmax(m, p, masked):
        st = st_refs[p][...] + bias_ref[0]
        if masked:
            st = jnp.where(key_row >= BAND_PAD - m * ATT_Q, st, NEG_INF)
        mx = jnp.max(st, axis=0, keepdims=True)
        e = jnp.exp(st - mx)
        den_refs[p][...] = jnp.sum(e, axis=0, keepdims=True)
        pb_refs[p][...] = e.astype(BF16)

    def values(m, p):
        r0 = m * ATT_Q
        vb = vtpad_ref[pl.ds(r0, ATT_BAND), :]
        o = lax.dot_general(pb_refs[p][...], vb, (((0,), (0,)), ((), ())),
                            preferred_element_type=F32)
        inv = jnp.transpose(jnp.broadcast_to(1.0 / den_refs[p][...], (SUBLANES, ATT_L)))[:, :1]
        y = jnp.concatenate(
            [o[h * ATT_Q:(h + 1) * ATT_Q, h * HEAD_DIM:(h + 1) * HEAD_DIM]
             * inv[h * ATT_Q:(h + 1) * ATT_Q] for h in range(ATT_HEADS)], axis=1)
        o_ref[0, pl.ds(r0, ATT_Q), :] = y.astype(BF16)

    n = s // ATT_Q
    n_masked = BAND_PAD // ATT_Q
    for m in range(n + 2):
        if m < n:
            scores(m, m % 2)
        if 1 <= m <= n:
            softmax(m - 1, (m - 1) % 2, m - 1 < n_masked)
        if m >= 2:
            values(m - 2, m % 2)


def _attn_bias_t(rel_bias):
    rb = rel_bias.astype(F32).reshape(-1, rel_bias.shape[-1])
    n_rows = rb.shape[0]
    nu = ATT_BAND + ATT_Q - 1
    n_low = BAND_PAD - MAX_REL + ATT_Q
    t = jnp.concatenate([jnp.repeat(rb[:, :1], n_low, axis=1),
                         rb[:, 1:1 + nu - n_low]], axis=1)
    tp = jnp.pad(t, ((0, 0), (0, 1)))
    flat = jnp.tile(tp, (1, ATT_Q + 1))[:, ATT_Q - 1:ATT_Q - 1 + ATT_Q * nu]
    bias = flat.reshape(n_rows, ATT_Q, nu)[:, :, :ATT_BAND]
    bias = bias.reshape(n_rows // ATT_HEADS, ATT_HEADS, ATT_Q, ATT_BAND)
    bias = bias.transpose(0, 3, 1, 2).reshape(n_rows // ATT_HEADS, ATT_BAND, ATT_L)
    r = jnp.arange(ATT_BAND)[:, None]
    qq = jnp.arange(ATT_L)[None, :] % ATT_Q
    first = (qq // CHUNK) * CHUNK
    valid = (r >= first) & (r < first + BAND_PAD + CHUNK)
    return jnp.where(valid[None], bias, NEG_INF)


def _attention(qt, kn, vt, bias_t, layer):
    b, s, _ = kn.shape
    nck = s // ATT_Q
    assert nck >= 2, "the attention pipeline needs at least two query steps"
    return pl.pallas_call(
        _attn_kernel,
        out_shape=jax.ShapeDtypeStruct((b, s, D_ATT), BF16),
        grid=(b, ATT_GROUPS),
        in_specs=[
            pl.BlockSpec((1, nck, ATT_W, ATT_Q), lambda bi, gi: (bi, 0, gi, 0)),
            pl.BlockSpec((1, s, ATT_W), lambda bi, gi: (bi, 0, gi)),
            pl.BlockSpec((1, s, ATT_W), lambda bi, gi: (bi, 0, gi)),
            pl.BlockSpec((1, ATT_BAND, ATT_L), lambda bi, gi: (layer * ATT_GROUPS + gi, 0, 0)),
        ],
        out_specs=pl.BlockSpec((1, s, ATT_W), lambda bi, gi: (bi, 0, gi)),
        scratch_shapes=[
            pltpu.VMEM((s + BAND_PAD, ATT_W), BF16),
            pltpu.VMEM((s + BAND_PAD, ATT_W), BF16),
            pltpu.VMEM((ATT_BAND, ATT_L), F32), pltpu.VMEM((ATT_BAND, ATT_L), F32),
            pltpu.VMEM((ATT_BAND, ATT_L), BF16), pltpu.VMEM((ATT_BAND, ATT_L), BF16),
            pltpu.VMEM((1, ATT_L), F32), pltpu.VMEM((1, ATT_L), F32),
        ],
        compiler_params=_cparams(2),
        name="band_attention",
    )(qt, kn, vt, bias_t)


def _mix_out_kernel(x_ref, yc_ref, ya_ref, w_ref, g1_ref, gf_ref, sc_ref, sh_ref,
                    wr_ref, br_ref, x1_ref, h_ref, info_ref, info_t_ref, cnt_ref,
                    wbf_ref, carry_ref):
    first = jnp.logical_and(pl.program_id(0) == 0, pl.program_id(1) == 0)

    @pl.when(first)
    def _():
        wbf_ref[...] = w_ref[...].astype(BF16)
        carry_ref[...] = jnp.zeros_like(carry_ref)

    y = jnp.dot(yc_ref[0], wbf_ref[0:D_CONV, :], preferred_element_type=F32)
    y = y + jnp.dot(ya_ref[0], wbf_ref[D_CONV:, :], preferred_element_type=F32)
    x1 = x_ref[0] + g1_ref[0] * y
    x1_ref[0] = x1
    h = _rms_mod(x1, gf_ref[...], sc_ref[0], sh_ref[0])
    h_ref[0] = _pack_bf16_pairs(h)
    logits = jnp.dot(h.astype(BF16), wr_ref[...], preferred_element_type=F32) + br_ref[...]
    _route_tile(logits, info_ref, info_t_ref, cnt_ref, carry_ref)


def _mix_out_routed(x, yc, ya, w_out, layer, mod5, gf, w_router, b_router, tm):
    b, s, d = x.shape
    nt = s // tm
    row = lambda bi, ti: (bi, ti, 0)
    const2 = lambda bi, ti: (0, 0)
    w_pad = jnp.zeros((d, LANE_PAD_E), BF16).at[:, :N_EXPERTS].set(w_router.astype(BF16))
    b_pad = jnp.full((1, LANE_PAD_E), -jnp.inf, F32).at[0, :N_EXPERTS].set(
        b_router.astype(F32))
    return pl.pallas_call(
        _mix_out_kernel,
        out_shape=(jax.ShapeDtypeStruct((b, s, d), F32),
                   jax.ShapeDtypeStruct((b, s, d // 2), jnp.int32),
                   jax.ShapeDtypeStruct((b, s, LANE_PAD_E), F32),
                   jax.ShapeDtypeStruct((SUBLANES, b * s), F32),
                   jax.ShapeDtypeStruct((SUBLANES, LANE_PAD_E), F32)),
        grid=(b, nt),
        in_specs=[
            pl.BlockSpec((1, tm, d), row),
            pl.BlockSpec((1, tm, D_CONV), row),
            pl.BlockSpec((1, tm, D_ATT), row),
            pl.BlockSpec((None, d, d), lambda bi, ti: (layer, 0, 0),
                         pipeline_mode=pl.Buffered(1)),
            _mod_spec(layer, MOD_G1),
            pl.BlockSpec((1, d), const2),
            _mod_spec(layer, MOD_SC2),
            _mod_spec(layer, MOD_SH2),
            pl.BlockSpec((d, LANE_PAD_E), const2),
            pl.BlockSpec((1, LANE_PAD_E), const2),
        ],
        out_specs=(pl.BlockSpec((1, tm, d), row), pl.BlockSpec((1, tm, d // 2), row),
                   pl.BlockSpec((1, tm, LANE_PAD_E), row),
                   pl.BlockSpec((SUBLANES, tm), lambda bi, ti: (0, bi * nt + ti)),
                   pl.BlockSpec((SUBLANES, LANE_PAD_E), const2)),
        scratch_shapes=[pltpu.VMEM((d, d), BF16), pltpu.VMEM((SUBLANES, LANE_PAD_E), F32)],
        compiler_params=_cparams(2),
        name="mix_out",
    )(x, yc, ya, w_out, mod5, gf.reshape(1, d), mod5, mod5, w_pad, b_pad)


def _cast_ffn_kernel(wg_ref, wu_ref, wd_ref, wo_ref, wg3_ref, wu3_ref, wd3_ref, wob_ref):
    wg3_ref[0] = wg_ref[...].astype(BF16)
    wu3_ref[0] = wu_ref[...].astype(BF16)
    wd3_ref[0] = wd_ref[...].astype(BF16)

    @pl.when(pl.program_id(0) == 0)
    def _():
        wob_ref[...] = wo_ref[...].astype(BF16)


def _cast_ffn_weights(wg, wu, wd, w_out, layer, fc):
    d, ff = wg.shape
    nf = ff // fc
    return pl.pallas_call(
        _cast_ffn_kernel,
        out_shape=(jax.ShapeDtypeStruct((nf, d, fc), BF16),
                   jax.ShapeDtypeStruct((nf, d, fc), BF16),
                   jax.ShapeDtypeStruct((nf, fc, d), BF16),
                   jax.ShapeDtypeStruct((d, d), BF16)),
        grid=(nf,),
        in_specs=[
            pl.BlockSpec((d, fc), lambda f: (0, f)),
            pl.BlockSpec((d, fc), lambda f: (0, f)),
            pl.BlockSpec((fc, d), lambda f: (f, 0)),
            pl.BlockSpec((None, d, d), lambda f: (layer, 0, 0)),
        ],
        out_specs=(pl.BlockSpec((1, d, fc), lambda f: (f, 0, 0)),
                   pl.BlockSpec((1, d, fc), lambda f: (f, 0, 0)),
                   pl.BlockSpec((1, fc, d), lambda f: (f, 0, 0)),
                   pl.BlockSpec((d, d), lambda f: (0, 0))),
        compiler_params=_cparams(1),
        name="cast_ffn_weights",
    )(wg, wu, wd, w_out)


def _mix_ffn_kernel(x_ref, yc_ref, ya_ref, wo_ref, g1_ref, gf_ref, sc_ref, sh_ref, g2_ref,
                    wg_ref, wu_ref, wd_ref, o_ref, acc_ref):
    y = jnp.dot(yc_ref[0], wo_ref[0:D_CONV, :], preferred_element_type=F32)
    y = y + jnp.dot(ya_ref[0], wo_ref[D_CONV:, :], preferred_element_type=F32)
    x1 = x_ref[0] + g1_ref[0] * y
    h = _rms_mod(x1, gf_ref[...], sc_ref[0], sh_ref[0]).astype(BF16)
    for c in range(wg_ref.shape[0]):
        a = jnp.dot(h, wg_ref[c], preferred_element_type=F32)
        u = jnp.dot(h, wu_ref[c], preferred_element_type=F32)
        t = (_silu(a) * u).astype(BF16)
        dn = jnp.dot(t, wd_ref[c], preferred_element_type=F32)
        if c == 0:
            acc_ref[...] = dn
        else:
            acc_ref[...] += dn
    o_ref[0] = x1 + g2_ref[0] * acc_ref[...]


def _mix_out_dense_ffn(x, yc, ya, w_out, layer, mod5, gf, wg, wu, wd, tm, fc):
    b, s, d = x.shape
    wg3, wu3, wd3, wob = _cast_ffn_weights(wg, wu, wd, w_out, layer, fc)
    nf = wg3.shape[0]
    row = lambda bi, ti: (bi, ti, 0)
    const2 = lambda bi, ti: (0, 0)
    const3 = lambda bi, ti: (0, 0, 0)
    resident = pl.Buffered(1)
    return pl.pallas_call(
        _mix_ffn_kernel,
        out_shape=jax.ShapeDtypeStruct((b, s, d), F32),
        grid=(b, s // tm),
        in_specs=[
            pl.BlockSpec((1, tm, d), row),
            pl.BlockSpec((1, tm, D_CONV), row),
            pl.BlockSpec((1, tm, D_ATT), row),
            pl.BlockSpec((d, d), const2, pipeline_mode=resident),
            _mod_spec(layer, MOD_G1),
            pl.BlockSpec((1, d), const2),
            _mod_spec(layer, MOD_SC2),
            _mod_spec(layer, MOD_SH2),
            _mod_spec(layer, MOD_G2),
            pl.BlockSpec((nf, d, fc), const3, pipeline_mode=resident),
            pl.BlockSpec((nf, d, fc), const3, pipeline_mode=resident),
            pl.BlockSpec((nf, fc, d), const3, pipeline_mode=resident),
        ],
        out_specs=pl.BlockSpec((1, tm, d), row),
        scratch_shapes=[pltpu.VMEM((tm, d), F32)],
        compiler_params=_cparams(2),
        name="mix_out_dense_ffn",
    )(x, yc, ya, wob, mod5, gf.reshape(1, d), mod5, mod5, mod5, wg3, wu3, wd3)


def _route_tile(logits, info_ref, info_t_ref, cnt_ref, carry_ref):
    tr = logits.shape[0]
    lane = lax.broadcasted_iota(jnp.int32, (tr, LANE_PAD_E), 1).astype(F32)
    no_lane = float(LANE_PAD_E)
    v0 = jnp.max(logits, axis=-1, keepdims=True)
    i0 = jnp.min(jnp.where(logits == v0, lane, no_lane), axis=-1, keepdims=True)
    rest = jnp.where(lane == i0, -jnp.inf, logits)
    v1 = jnp.max(rest, axis=-1, keepdims=True)
    i1 = jnp.min(jnp.where(rest == v1, lane, no_lane), axis=-1, keepdims=True)
    e1 = jnp.exp(v1 - v0)
    w0 = 1.0 / (1.0 + e1)
    w1 = e1 / (1.0 + e1)
    oh0 = lane == i0
    oh1 = lane == i1
    cnt = jnp.where(jnp.logical_or(oh0, oh1), 1.0, 0.0)
    tri = (lax.broadcasted_iota(jnp.int32, (tr, tr), 1)
           < lax.broadcasted_iota(jnp.int32, (tr, tr), 0)).astype(BF16)
    before = jnp.dot(tri, cnt.astype(BF16), preferred_element_type=F32) + carry_ref[0:1, :]
    r0 = jnp.sum(jnp.where(oh0, before, 0.0), axis=-1, keepdims=True)
    r1 = jnp.sum(jnp.where(oh1, before, 0.0), axis=-1, keepdims=True)
    carry_ref[...] = carry_ref[...] + jnp.sum(cnt, axis=0, keepdims=True)
    cnt_ref[...] = carry_ref[...]
    info = jnp.where(lane == 0, i0, 0.0)
    info = jnp.where(lane == 1, i1, info)
    info = jnp.where(lane == 2, w0, info)
    info = jnp.where(lane == 3, w1, info)
    info = jnp.where(lane == 4, r0, info)
    info = jnp.where(lane == 5, r1, info)
    info_ref[0] = info
    for c in range(tr // LANES):
        blk = info[c * LANES:(c + 1) * LANES, :].T
        info_t_ref[:, c * LANES:(c + 1) * LANES] = blk[0:SUBLANES, :]


def _moe_kernel(te_ref, tn_ref, xs_ref, wg_ref, wu_ref, wd_ref, ys_ref,
                acc_ref, wgb_ref, wub_ref, wdb_ref):
    i = pl.program_id(0)
    f = pl.program_id(1)
    nhalf = tn_ref[i]
    nfull = nhalf // (MOE_SUB // MOE_HALF)

    @pl.when(f == 0)
    def _():
        acc_ref[...] = jnp.zeros_like(acc_ref)

    def block(r, rows, wgb, wub, wdb):
        xb = _unpack_bf16_pairs(xs_ref[pl.ds(r, rows), :]).astype(BF16)
        a = jnp.dot(xb, wgb, preferred_element_type=F32)
        u = jnp.dot(xb, wub, preferred_element_type=F32)
        t = (_silu(a) * u).astype(BF16)
        acc_ref[pl.ds(r, rows), :] += jnp.dot(t, wdb, preferred_element_type=F32)

    def split(units):
        out, r = [], 0
        for rows in (MOE_SUB,) * (units * MOE_HALF // MOE_SUB) + MOE_TAILS:
            if units * MOE_HALF - r >= rows:
                out.append((r, rows))
                r += rows
        return out

    for units in MOE_STATIC_UNITS:
        @pl.when(nhalf == units)
        def _(units=units):
            wgb, wub, wdb = (w[0].astype(BF16) for w in (wg_ref, wu_ref, wd_ref))
            for r, rows in split(units):
                block(r, rows, wgb, wub, wdb)

    other = nhalf > 0
    for units in MOE_STATIC_UNITS:
        other = jnp.logical_and(other, nhalf != units)

    @pl.when(other)
    def _():
        wgb_ref[...] = wg_ref[0].astype(BF16)
        wub_ref[...] = wu_ref[0].astype(BF16)
        wdb_ref[...] = wd_ref[0].astype(BF16)

        def sub(sidx, carry):
            block(pl.multiple_of(sidx * MOE_SUB, MOE_SUB), MOE_SUB,
                  wgb_ref[...], wub_ref[...], wdb_ref[...])
            return carry

        lax.fori_loop(0, nfull, sub, 0)
        done = nfull * MOE_SUB
        for rows in MOE_TAILS:
            units = rows // MOE_HALF

            @pl.when((nhalf & units) != 0)
            def _(rows=rows, units=units):
                higher = nhalf & (MOE_SUB // MOE_HALF - 1) & ~(2 * units - 1)
                block(pl.multiple_of(done + higher * MOE_HALF, MOE_HALF), rows,
                      wgb_ref[...], wub_ref[...], wdb_ref[...])

    @pl.when(f == pl.num_programs(1) - 1)
    def _():
        ys_ref[...] = _pack_bf16_pairs(acc_ref[...])


def _moe_ffn(xs, tile_e, tile_nsub, wg, wu, wd):
    rpad, dw = xs.shape
    d = 2 * dw
    ntiles = rpad // MOE_TILE
    ff = wg.shape[2]
    nf = ff // MOE_FC

    def fcol(i, f, tn):
        return jnp.where(tn[i] > 0, f, nf - 1)

    return pl.pallas_call(
        _moe_kernel,
        out_shape=jax.ShapeDtypeStruct((rpad, dw), jnp.int32),
        grid_spec=pltpu.PrefetchScalarGridSpec(
            num_scalar_prefetch=2,
            grid=(ntiles, nf),
            in_specs=[
                pl.BlockSpec((MOE_TILE, dw), lambda i, f, te, tn: (i, 0)),
                pl.BlockSpec((1, d, MOE_FC), lambda i, f, te, tn: (te[i], 0, fcol(i, f, tn))),
                pl.BlockSpec((1, d, MOE_FC), lambda i, f, te, tn: (te[i], 0, fcol(i, f, tn))),
                pl.BlockSpec((1, MOE_FC, d), lambda i, f, te, tn: (te[i], fcol(i, f, tn), 0)),
            ],
            out_specs=pl.BlockSpec((MOE_TILE, dw), lambda i, f, te, tn: (i, 0)),
            scratch_shapes=[
                pltpu.VMEM((MOE_TILE, d), F32),
                pltpu.VMEM((d, MOE_FC), BF16),
                pltpu.VMEM((d, MOE_FC), BF16),
                pltpu.VMEM((MOE_FC, d), BF16),
            ],
        ),
        compiler_params=_cparams(2),
        name="moe_ffn",
    )(tile_e, tile_nsub, xs, wg, wu, wd)


def _combine_kernel(x_ref, y0_ref, y1_ref, info_ref, g2_ref, o_ref):
    info = info_ref[0]
    w0 = info[:, 2:3]
    w1 = info[:, 3:4]
    f = w0 * _unpack_bf16_pairs(y0_ref[0, 0]) + w1 * _unpack_bf16_pairs(y1_ref[0, 0])
    o_ref[0] = x_ref[0] + g2_ref[0] * f


def _combine(x1, y01, info, mod5, layer, tm):
    b, s, d = x1.shape
    row = lambda bi, ti: (bi, ti, 0)
    return pl.pallas_call(
        _combine_kernel,
        out_shape=jax.ShapeDtypeStruct((b, s, d), F32),
        grid=(b, s // tm),
        in_specs=[
            pl.BlockSpec((1, tm, d), row),
            pl.BlockSpec((1, 1, tm, d // 2), lambda bi, ti: (0, bi, ti, 0)),
            pl.BlockSpec((1, 1, tm, d // 2), lambda bi, ti: (1, bi, ti, 0)),
            pl.BlockSpec((1, tm, LANE_PAD_E), row),
            _mod_spec(layer, MOD_G2),
        ],
        out_specs=pl.BlockSpec((1, tm, d), row),
        compiler_params=_cparams(2),
        name="moe_combine",
    )(x1, y01, y01, info, mod5)


def _moe_layer(hp, info, info_t, cnt, x1, mod5, layer, wg, wu, wd, tm):
    b, s, d = x1.shape
    n = b * s
    e0 = info_t[0].astype(jnp.int32)
    e1 = info_t[1].astype(jnp.int32)
    r0 = info_t[4].astype(jnp.int32)
    r1 = info_t[5].astype(jnp.int32)
    counts = cnt[0, :N_EXPERTS].astype(jnp.int32)

    ntiles = (2 * n) // MOE_TILE + N_EXPERTS
    tiles_per_e = (counts + MOE_TILE - 1) // MOE_TILE
    tile_end = jnp.cumsum(tiles_per_e)
    tile_start = tile_end - tiles_per_e
    total = tile_end[-1]
    tidx = jnp.arange(ntiles, dtype=jnp.int32)
    live = tidx < total
    tclip = jnp.minimum(tidx, total - 1)
    tile_e = jnp.minimum(jnp.sum(tclip[:, None] >= tile_end[None, :], axis=1),
                         N_EXPERTS - 1).astype(jnp.int32)
    rows_left = counts[tile_e] - (tclip - tile_start[tile_e]) * MOE_TILE
    rows_here = jnp.clip(rows_left, 0, MOE_TILE)
    tile_nsub = jnp.where(live, (rows_here + MOE_HALF - 1) // MOE_HALF, 0).astype(jnp.int32)

    row_start = tile_start * MOE_TILE
    eid = jnp.arange(N_EXPERTS, dtype=jnp.int32)[None, :]
    pos0 = jnp.sum(jnp.where(e0[:, None] == eid, row_start[None, :], 0), axis=1) + r0
    pos1 = jnp.sum(jnp.where(e1[:, None] == eid, row_start[None, :], 0), axis=1) + r1

    xs = _sc_scatter_rows2(hp.reshape(n, d // 2), pos0, pos1, ntiles * MOE_TILE)
    ys = _moe_ffn(xs, tile_e, tile_nsub, wg, wu, wd)
    y01 = _sc_gather_rows(ys, jnp.concatenate([pos0, pos1]))
    return _combine(x1, y01.reshape(2, b, s, d // 2), info, mod5, layer, tm)


def kernel(x, c, w_ada, b_ada, norm_mix_g, norm_ffn_g, w_in, w_out, conv_w, conv_b,
           conv_ln_g, conv_ln_b, q_norm_g, k_norm_g, rel_bias, ffn_w_gate, ffn_w_up,
           ffn_w_down, moe_w_router, moe_b_router, moe_w_gate, moe_w_up, moe_w_down):
    b, s, d = x.shape
    depth = w_ada.shape[0]
    tm = min(1024, s)
    mod5 = _ada_mod(c, w_ada, b_ada).reshape(depth, b, 6, 1, d)
    bias_t = _attn_bias_t(rel_bias)
    for l in range(depth):
        z, qt, kn, vt = _mix_in(x, mod5, norm_mix_g[l], w_in, l,
                                q_norm_g[l], k_norm_g[l], tm)
        yc = _conv_branch(z, conv_w[l], conv_b[l], conv_ln_g[l], conv_ln_b[l], tm)
        ya = _attention(qt, kn, vt, bias_t, l)
        i = l // 2
        if l % 2 == 0:
            x = _mix_out_dense_ffn(x, yc, ya, w_out, l, mod5, norm_ffn_g[l],
                                   ffn_w_gate[i], ffn_w_up[i], ffn_w_down[i],
                                   min(1024, s), 256)
        else:
            x1, hp, info, info_t, cnt = _mix_out_routed(
                x, yc, ya, w_out, l, mod5, norm_ffn_g[l],
                moe_w_router[i], moe_b_router[i], tm)
            x = _moe_layer(hp, info, info_t, cnt, x1, mod5, l,
                           moe_w_gate[i], moe_w_up[i], moe_w_down[i], tm)
    return x
```

```python
import functools

import jax
import jax.numpy as jnp
from jax import lax
from jax.experimental import pallas as pl
from jax.experimental.pallas import tpu as pltpu
from jax.experimental.pallas import tpu_sc as plsc

F32 = jnp.float32
BF16 = jnp.bfloat16

D_MODEL = 1024
CHUNK = 64
N_PREV_CHUNKS = 8
BAND_PAD = N_PREV_CHUNKS * CHUNK
D_CONV = 512
D_ATT = 512
HEAD_DIM = 64
N_HEADS = 8
CONV_WIDTH = 31
MAX_REL = 128
D_IN_COLS = 2 * D_CONV + 3 * D_ATT
N_EXPERTS = 8
EPS = 1e-6
NEG_INF = -1e30

LANES = 128
SUBLANES = 8
VMEM_LIMIT_BYTES = 56 * 1024 * 1024

ATT_HEADS = 4
ATT_GROUPS = N_HEADS // ATT_HEADS
ATT_W = ATT_HEADS * HEAD_DIM
ATT_Q = 2 * CHUNK
ATT_BAND = BAND_PAD + ATT_Q
ATT_L = ATT_HEADS * ATT_Q

CONV_HALO = 32
CONV_ROWS = 32
LANE_PAD_E = LANES

MOE_HALF = 256
MOE_SUB = 4 * MOE_HALF
MOE_TAILS = (2 * MOE_HALF, MOE_HALF)
MOE_TILE = 9 * MOE_HALF
MOE_STATIC_UNITS = (9, 8, 7)
MOE_FC = 512


def _cparams(n_axes, vmem=VMEM_LIMIT_BYTES):
    return pltpu.CompilerParams(
        dimension_semantics=("arbitrary",) * n_axes, vmem_limit_bytes=vmem)


def _silu(v):
    return v * jax.nn.sigmoid(v)


def _pack_bf16_pairs(v):
    w = v.shape[1] // 2
    bits = lax.bitcast_convert_type(v.astype(BF16).astype(F32), jnp.uint32)
    packed = (bits[:, w:] & jnp.uint32(0xFFFF0000)) | (bits[:, :w] >> 16)
    return lax.bitcast_convert_type(packed, jnp.int32)


def _unpack_bf16_pairs(p):
    bits = lax.bitcast_convert_type(p, jnp.uint32)
    lo = lax.bitcast_convert_type(bits << 16, F32)
    hi = lax.bitcast_convert_type(bits & jnp.uint32(0xFFFF0000), F32)
    return jnp.concatenate([lo, hi], axis=1)


SC_CORES = 2
SC_SUBCORES = 16
SC_WORKERS = SC_CORES * SC_SUBCORES
SC_CHUNK = 64


def _sc_worker_id():
    return lax.axis_index("s") * SC_CORES + lax.axis_index("c")


def _sc_mesh():
    return plsc.VectorSubcoreMesh(core_axis_name="c", subcore_axis_name="s")


def _sc_gather_rows(table, idx):
    _, w = table.shape
    b = idx.shape[0]
    per_w = b // SC_WORKERS
    nch = per_w // SC_CHUNK

    def body(table_hbm, idx_hbm, out_hbm, idx_v, rows_v, gsem, wsem):
        wid = _sc_worker_id()
        base = wid * per_w
        pltpu.sync_copy(idx_hbm.at[wid], idx_v)
        gathers = [None] * nch
        writes = [None] * nch
        gathers[0] = pltpu.async_copy(table_hbm.at[idx_v.at[0]], rows_v.at[0], gsem.at[0])
        for c in range(nch):
            slot = c % 2
            gathers[c].wait()
            if c + 1 < nch:
                if c >= 1:
                    writes[c - 1].wait()
                gathers[c + 1] = pltpu.async_copy(
                    table_hbm.at[idx_v.at[c + 1]], rows_v.at[1 - slot], gsem.at[1 - slot])
            writes[c] = pltpu.async_copy(
                rows_v.at[slot], out_hbm.at[pl.ds(base + c * SC_CHUNK, SC_CHUNK)], wsem.at[slot])
        if nch >= 2:
            writes[nch - 2].wait()
        writes[nch - 1].wait()

    call = pl.kernel(
        body, mesh=_sc_mesh(),
        out_type=jax.ShapeDtypeStruct((b, w), jnp.int32),
        scratch_types=[pltpu.VMEM((nch, SC_CHUNK), jnp.int32),
                       pltpu.VMEM((2, SC_CHUNK, w), jnp.int32),
                       pltpu.SemaphoreType.DMA((2,)), pltpu.SemaphoreType.DMA((2,))],
        name="sc_gather_rows")
    return call(table, idx.reshape(SC_WORKERS, nch, SC_CHUNK))


def _sc_scatter_rows2(src, idx0, idx1, rows_out):
    n, w = src.shape
    per_w = n // SC_WORKERS
    nch = per_w // SC_CHUNK

    def body(src_hbm, i0_hbm, i1_hbm, out_hbm, i0_v, i1_v, rows_v, rsem, wsem):
        wid = _sc_worker_id()
        base = wid * per_w
        pltpu.sync_copy(i0_hbm.at[wid], i0_v)
        pltpu.sync_copy(i1_hbm.at[wid], i1_v)
        reads = [None] * nch
        writes = [None] * nch
        reads[0] = pltpu.async_copy(src_hbm.at[pl.ds(base, SC_CHUNK)], rows_v.at[0], rsem.at[0])
        for c in range(nch):
            slot = c % 2
            reads[c].wait()
            if c + 1 < nch:
                if c >= 1:
                    for wr in writes[c - 1]:
                        wr.wait()
                reads[c + 1] = pltpu.async_copy(
                    src_hbm.at[pl.ds(base + (c + 1) * SC_CHUNK, SC_CHUNK)],
                    rows_v.at[1 - slot], rsem.at[1 - slot])
            writes[c] = (
                pltpu.async_copy(rows_v.at[slot], out_hbm.at[i0_v.at[c]], wsem.at[slot, 0]),
                pltpu.async_copy(rows_v.at[slot], out_hbm.at[i1_v.at[c]], wsem.at[slot, 1]),
            )
        for c in range(max(nch - 2, 0), nch):
            for wr in writes[c]:
                wr.wait()

    call = pl.kernel(
        body, mesh=_sc_mesh(),
        out_type=jax.ShapeDtypeStruct((rows_out, w), jnp.int32),
        scratch_types=[pltpu.VMEM((nch, SC_CHUNK), jnp.int32),
                       pltpu.VMEM((nch, SC_CHUNK), jnp.int32),
                       pltpu.VMEM((2, SC_CHUNK, w), jnp.int32),
                       pltpu.SemaphoreType.DMA((2,)), pltpu.SemaphoreType.DMA((2, 2))],
        name="sc_scatter_rows")
    shape3 = (SC_WORKERS, nch, SC_CHUNK)
    return call(src, idx0.reshape(shape3), idx1.reshape(shape3))


def _ada_kernel(c_ref, w_ref, b_ref, o_ref):
    ca = _silu(c_ref[...]).astype(BF16)
    w = w_ref[0].astype(BF16)
    o_ref[0] = jnp.dot(ca, w, preferred_element_type=F32) + b_ref[0]


def _ada_mod(c, w_ada, b_ada):
    depth, d, n6 = w_ada.shape
    b = c.shape[0]
    rows = 16
    c_pad = jnp.zeros((rows, d), F32).at[:b].set(c)
    tn = 1536
    out = pl.pallas_call(
        _ada_kernel,
        out_shape=jax.ShapeDtypeStruct((depth, rows, n6), F32),
        grid=(depth, n6 // tn),
        in_specs=[
            pl.BlockSpec((rows, d), lambda l, j: (0, 0)),
            pl.BlockSpec((1, d, tn), lambda l, j: (l, 0, j)),
            pl.BlockSpec((1, 1, tn), lambda l, j: (l, 0, j)),
        ],
        out_specs=pl.BlockSpec((1, rows, tn), lambda l, j: (l, 0, j)),
        compiler_params=_cparams(2),
        name="ada_mod",
    )(c_pad, w_ada, b_ada.reshape(depth, 1, n6))
    return out[:, :b]


MOD_SH1, MOD_SC1, MOD_G1, MOD_SH2, MOD_SC2, MOD_G2 = range(6)


def _mod_spec(layer, chunk):
    return pl.BlockSpec((None, 1, None, 1, D_MODEL),
                        lambda bi, *_: (layer, bi, chunk, 0, 0))


def _rms_mod(xf, g, sc, sh):
    ms = jnp.mean(xf * xf, axis=-1, keepdims=True)
    return xf * lax.rsqrt(ms + EPS) * g * (1.0 + sc) + sh


def _mix_in_kernel(x_ref, sc_ref, sh_ref, g_ref, w_ref, gq_ref, gk_ref, ones_ref,
                   z_ref, qt_ref, k_ref, v_ref, wbf_ref):
    first = jnp.logical_and(pl.program_id(0) == 0, pl.program_id(1) == 0)

    @pl.when(first)
    def _():
        wbf_ref[...] = w_ref[...].astype(BF16)

    h = _rms_mod(x_ref[0], g_ref[...], sc_ref[0], sh_ref[0]).astype(BF16)
    proj = jnp.dot(h, wbf_ref[...], preferred_element_type=F32)

    a = proj[:, :D_CONV]
    gate = proj[:, D_CONV:2 * D_CONV]
    z_ref[0] = (a * jax.nn.sigmoid(gate)).astype(BF16)

    def head_norm(t, g):
        sq = (t * t).astype(BF16)
        ss = jnp.concatenate(
            [jnp.dot(sq[:, c:c + ATT_W], ones_ref[...], preferred_element_type=F32)
             for c in range(0, D_ATT, ATT_W)], axis=1)
        return t * lax.rsqrt(ss * (1.0 / HEAD_DIM) + EPS) * g

    o = 2 * D_CONV
    q = head_norm(proj[:, o:o + D_ATT], gq_ref[...])
    k_ref[0] = head_norm(proj[:, o + D_ATT:o + 2 * D_ATT], gk_ref[...]).astype(BF16)
    v_ref[0] = proj[:, o + 2 * D_ATT:].astype(BF16)
    for cidx in range(q.shape[0] // ATT_Q):
        rows = slice(cidx * ATT_Q, (cidx + 1) * ATT_Q)
        qt_ref[0, cidx] = q[rows, :].T.astype(BF16)


def _mix_in(x, mod5, g, w_in, layer, gq, gk, tm):
    b, s, d = x.shape
    ones_bd = (jnp.arange(ATT_W)[:, None] // HEAD_DIM
               == jnp.arange(ATT_W)[None, :] // HEAD_DIM).astype(BF16)
    gq_t = (jnp.tile(gq, N_HEADS) * (HEAD_DIM ** -0.5)).reshape(1, D_ATT)
    gk_t = jnp.tile(gk, N_HEADS).reshape(1, D_ATT)
    row = lambda bi, ti: (bi, ti, 0)
    const2 = lambda bi, ti: (0, 0)
    return pl.pallas_call(
        _mix_in_kernel,
        out_shape=(
            jax.ShapeDtypeStruct((b, s, D_CONV), BF16),
            jax.ShapeDtypeStruct((b, s // ATT_Q, D_ATT, ATT_Q), BF16),
            jax.ShapeDtypeStruct((b, s, D_ATT), BF16),
            jax.ShapeDtypeStruct((b, s, D_ATT), BF16),
        ),
        grid=(b, s // tm),
        in_specs=[
            pl.BlockSpec((1, tm, d), row),
            _mod_spec(layer, MOD_SC1),
            _mod_spec(layer, MOD_SH1),
            pl.BlockSpec((1, d), const2),
            pl.BlockSpec((None, d, D_IN_COLS), lambda bi, ti: (layer, 0, 0),
                         pipeline_mode=pl.Buffered(1)),
            pl.BlockSpec((1, D_ATT), const2),
            pl.BlockSpec((1, D_ATT), const2),
            pl.BlockSpec((ATT_W, ATT_W), const2),
        ],
        out_specs=(
            pl.BlockSpec((1, tm, D_CONV), row),
            pl.BlockSpec((1, tm // ATT_Q, D_ATT, ATT_Q), lambda bi, ti: (bi, ti, 0, 0)),
            pl.BlockSpec((1, tm, D_ATT), row),
            pl.BlockSpec((1, tm, D_ATT), row),
        ),
        scratch_shapes=[pltpu.VMEM((d, D_IN_COLS), BF16)],
        compiler_params=_cparams(2),
        name="mix_in",
    )(x, mod5, mod5, g.reshape(1, d), w_in, gq_t, gk_t, ones_bd)


def _conv_kernel(zc_ref, zp_ref, w_ref, cb_ref, lg_ref, lb_ref, o_ref, win_ref, sh_ref,
                 acc_ref):
    tt = zc_ref.shape[1]
    t = pl.program_id(1)
    halo = zp_ref[0].astype(F32)
    win_ref[0:CONV_HALO, :] = jnp.where(t == 0, 0.0, halo)
    win_ref[CONV_HALO:, :] = zc_ref[0].astype(F32)
    span = tt + CONV_HALO - SUBLANES
    for sft in range(1, SUBLANES):
        sh_ref[sft - 1, 0:span, :] = win_ref[sft:sft + span, :]
    base = CONV_HALO - (CONV_WIDTH - 1)
    tiles = CONV_ROWS // SUBLANES

    for g in range(tt // CONV_ROWS):
        r = g * CONV_ROWS
        acc = jnp.zeros((tiles, SUBLANES, D_CONV), F32) + cb_ref[...]
        for j in range(CONV_WIDTH):
            whole, sft = divmod(base + j, SUBLANES)
            start = r + whole * SUBLANES
            if sft == 0:
                tap = win_ref[start:start + CONV_ROWS, :]
            else:
                tap = sh_ref[sft - 1, start:start + CONV_ROWS, :]
            acc = acc + tap.reshape(tiles, SUBLANES, D_CONV) * w_ref[j]
        acc_ref[r:r + CONV_ROWS, :] = acc.reshape(CONV_ROWS, D_CONV)
    acc = acc_ref[...]
    mu = jnp.mean(acc, axis=-1, keepdims=True)
    xc = acc - mu
    var = jnp.mean(xc * xc, axis=-1, keepdims=True)
    y = xc * lax.rsqrt(var + EPS) * lg_ref[...] + lb_ref[...]
    o_ref[0] = _silu(y).astype(BF16)


def _conv_branch(z, conv_w, conv_b, ln_g, ln_b, tt):
    b, s, c = z.shape
    hb = tt // CONV_HALO
    w_tiles = jnp.broadcast_to(conv_w.reshape(CONV_WIDTH, 1, c), (CONV_WIDTH, SUBLANES, c))
    const2 = lambda bi, ti: (0, 0)
    return pl.pallas_call(
        _conv_kernel,
        out_shape=jax.ShapeDtypeStruct((b, s, c), BF16),
        grid=(b, s // tt),
        in_specs=[
            pl.BlockSpec((1, tt, c), lambda bi, ti: (bi, ti, 0)),
            pl.BlockSpec((1, CONV_HALO, c),
                         lambda bi, ti: (bi, jnp.maximum(ti * hb - 1, 0), 0)),
            pl.BlockSpec((CONV_WIDTH, SUBLANES, c), lambda bi, ti: (0, 0, 0)),
            pl.BlockSpec((1, c), const2),
            pl.BlockSpec((1, c), const2),
            pl.BlockSpec((1, c), const2),
        ],
        out_specs=pl.BlockSpec((1, tt, c), lambda bi, ti: (bi, ti, 0)),
        scratch_shapes=[pltpu.VMEM((tt + CONV_HALO, c), F32),
                        pltpu.VMEM((SUBLANES - 1, tt + CONV_HALO, c), F32),
                        pltpu.VMEM((tt, c), F32)],
        compiler_params=_cparams(2),
        name="conv_branch",
    )(z, z, w_tiles, conv_b.reshape(1, c),
      ln_g.reshape(1, c), ln_b.reshape(1, c))


def _attn_kernel(qt_ref, k_ref, v_ref, bias_ref, o_ref, kpad_ref, vpad_ref,
                 st0_ref, st1_ref, pb0_ref, pb1_ref, den0_ref, den1_ref):
    st_refs = (st0_ref, st1_ref)
    pb_refs = (pb0_ref, pb1_ref)
    den_refs = (den0_ref, den1_ref)
    s = k_ref.shape[1]
    kpad_ref[0:BAND_PAD, :] = jnp.zeros((BAND_PAD, ATT_W), BF16)
    kpad_ref[BAND_PAD:, :] = k_ref[0]
    vpad_ref[0:BAND_PAD, :] = jnp.zeros((BAND_PAD, ATT_W), BF16)
    vpad_ref[BAND_PAD:, :] = v_ref[0]

    iota = lambda shape, dim: lax.broadcasted_iota(jnp.int32, shape, dim)
    q_shift = ATT_Q.bit_length() - 1
    d_shift = HEAD_DIM.bit_length() - 1
    qb_mask = (iota((ATT_W, ATT_L), 0) >> d_shift) == (iota((ATT_W, ATT_L), 1) >> q_shift)
    key_row = lax.broadcasted_iota(jnp.int32, (ATT_BAND, ATT_L), 0)

    def scores(m, p):
        r0 = m * ATT_Q
        qt = qt_ref[0, m]
        qb = jnp.where(qb_mask, jnp.concatenate([qt] * ATT_HEADS, axis=1), 0)
        kb = kpad_ref[pl.ds(r0, ATT_BAND), :]
        st_refs[p][...] = jnp.dot(kb, qb.astype(BF16),
                                  preferred_element_type=F32)

    def softmax(m, p, masked):
        st = st_refs[p][...] + bias_ref[0]
        if masked:
            st = jnp.where(key_row >= BAND_PAD - m * ATT_Q, st, NEG_INF)
        mx = jnp.max(st, axis=0, keepdims=True)
        e = jnp.exp(st - mx)
        den_refs[p][...] = jnp.sum(e, axis=0, keepdims=True)
        pb_refs[p][...] = e.astype(BF16)

    def values(m, p):
        r0 = m * ATT_Q
        vb = vpad_ref[pl.ds(r0, ATT_BAND), :]
        o = lax.dot_general(pb_refs[p][...], vb, (((0,), (0,)), ((), ())),
                            preferred_element_type=F32)
        inv = jnp.transpose(jnp.broadcast_to(1.0 / den_refs[p][...], (SUBLANES, ATT_L)))[:, :1]
        y = jnp.concatenate(
            [o[h * ATT_Q:(h + 1) * ATT_Q, h * HEAD_DIM:(h + 1) * HEAD_DIM]
             * inv[h * ATT_Q:(h + 1) * ATT_Q] for h in range(ATT_HEADS)], axis=1)
        o_ref[0, pl.ds(r0, ATT_Q), :] = y.astype(BF16)

    n = s // ATT_Q
    n_masked = BAND_PAD // ATT_Q
    for m in range(n + 2):
        if m < n:
            scores(m, m % 2)
        if 1 <= m <= n:
            softmax(m - 1, (m - 1) % 2, m - 1 < n_masked)
        if m >= 2:
            values(m - 2, m % 2)


def _attn_bias_t(rel_bias):
    rb = rel_bias.astype(F32).reshape(-1, rel_bias.shape[-1])
    n_rows = rb.shape[0]
    nu = ATT_BAND + ATT_Q - 1
    n_low = BAND_PAD - MAX_REL + ATT_Q
    t = jnp.concatenate([jnp.repeat(rb[:, :1], n_low, axis=1),
                         rb[:, 1:1 + nu - n_low]], axis=1)
    tp = jnp.pad(t, ((0, 0), (0, 1)))
    flat = jnp.tile(tp, (1, ATT_Q + 1))[:, ATT_Q - 1:ATT_Q - 1 + ATT_Q * nu]
    bias = flat.reshape(n_rows, ATT_Q, nu)[:, :, :ATT_BAND]
    bias = bias.reshape(n_rows // ATT_HEADS, ATT_HEADS, ATT_Q, ATT_BAND)
    bias = bias.transpose(0, 3, 1, 2).reshape(n_rows // ATT_HEADS, ATT_BAND, ATT_L)
    r = jnp.arange(ATT_BAND)[:, None]
    qq = jnp.arange(ATT_L)[None, :] % ATT_Q
    first = (qq // CHUNK) * CHUNK
    valid = (r >= first) & (r < first + BAND_PAD + CHUNK)
    return jnp.where(valid[None], bias, NEG_INF)


def _attention(qt, kn, v, bias_t, layer):
    b, s, _ = kn.shape
    nck = s // ATT_Q
    assert nck >= 2, "the attention pipeline needs at least two query steps"
    return pl.pallas_call(
        _attn_kernel,
        out_shape=jax.ShapeDtypeStruct((b, s, D_ATT), BF16),
        grid=(b, ATT_GROUPS),
        in_specs=[
            pl.BlockSpec((1, nck, ATT_W, ATT_Q), lambda bi, gi: (bi, 0, gi, 0)),
            pl.BlockSpec((1, s, ATT_W), lambda bi, gi: (bi, 0, gi)),
            pl.BlockSpec((1, s, ATT_W), lambda bi, gi: (bi, 0, gi)),
            pl.BlockSpec((1, ATT_BAND, ATT_L), lambda bi, gi: (layer * ATT_GROUPS + gi, 0, 0)),
        ],
        out_specs=pl.BlockSpec((1, s, ATT_W), lambda bi, gi: (bi, 0, gi)),
        scratch_shapes=[
            pltpu.VMEM((s + BAND_PAD, ATT_W), BF16),
            pltpu.VMEM((s + BAND_PAD, ATT_W), BF16),
            pltpu.VMEM((ATT_BAND, ATT_L), F32), pltpu.VMEM((ATT_BAND, ATT_L), F32),
            pltpu.VMEM((ATT_BAND, ATT_L), BF16), pltpu.VMEM((ATT_BAND, ATT_L), BF16),
            pltpu.VMEM((1, ATT_L), F32), pltpu.VMEM((1, ATT_L), F32),
        ],
        compiler_params=_cparams(2),
        name="band_attention",
    )(qt, kn, v, bias_t)


def _mix_out_kernel(x_ref, yc_ref, ya_ref, w_ref, g1_ref, gf_ref, sc_ref, sh_ref,
                    wr_ref, br_ref, x1_ref, h_ref, info_ref, info_t_ref, cnt_ref,
                    wbf_ref, carry_ref):
    first = jnp.logical_and(pl.program_id(0) == 0, pl.program_id(1) == 0)

    @pl.when(first)
    def _():
        wbf_ref[...] = w_ref[...].astype(BF16)
        carry_ref[...] = jnp.zeros_like(carry_ref)

    y = jnp.dot(yc_ref[0], wbf_ref[0:D_CONV, :], preferred_element_type=F32)
    y = y + jnp.dot(ya_ref[0], wbf_ref[D_CONV:, :], preferred_element_type=F32)
    x1 = x_ref[0] + g1_ref[0] * y
    x1_ref[0] = x1
    h = _rms_mod(x1, gf_ref[...], sc_ref[0], sh_ref[0])
    h_ref[0] = _pack_bf16_pairs(h)
    logits = jnp.dot(h.astype(BF16), wr_ref[...], preferred_element_type=F32) + br_ref[...]
    _route_tile(logits, info_ref, info_t_ref, cnt_ref, carry_ref)


def _mix_out_routed(x, yc, ya, w_out, layer, mod5, gf, w_router, b_router, tm):
    b, s, d = x.shape
    nt = s // tm
    row = lambda bi, ti: (bi, ti, 0)
    const2 = lambda bi, ti: (0, 0)
    w_pad = jnp.zeros((d, LANE_PAD_E), BF16).at[:, :N_EXPERTS].set(w_router.astype(BF16))
    b_pad = jnp.full((1, LANE_PAD_E), -jnp.inf, F32).at[0, :N_EXPERTS].set(
        b_router.astype(F32))
    return pl.pallas_call(
        _mix_out_kernel,
        out_shape=(jax.ShapeDtypeStruct((b, s, d), F32),
                   jax.ShapeDtypeStruct((b, s, d // 2), jnp.int32),
                   jax.ShapeDtypeStruct((b, s, LANE_PAD_E), F32),
                   jax.ShapeDtypeStruct((SUBLANES, b * s), F32),
                   jax.ShapeDtypeStruct((SUBLANES, LANE_PAD_E), F32)),
        grid=(b, nt),
        in_specs=[
            pl.BlockSpec((1, tm, d), row),
            pl.BlockSpec((1, tm, D_CONV), row),
            pl.BlockSpec((1, tm, D_ATT), row),
            pl.BlockSpec((None, d, d), lambda bi, ti: (layer, 0, 0),
                         pipeline_mode=pl.Buffered(1)),
            _mod_spec(layer, MOD_G1),
            pl.BlockSpec((1, d), const2),
            _mod_spec(layer, MOD_SC2),
            _mod_spec(layer, MOD_SH2),
            pl.BlockSpec((d, LANE_PAD_E), const2),
            pl.BlockSpec((1, LANE_PAD_E), const2),
        ],
        out_specs=(pl.BlockSpec((1, tm, d), row), pl.BlockSpec((1, tm, d // 2), row),
                   pl.BlockSpec((1, tm, LANE_PAD_E), row),
                   pl.BlockSpec((SUBLANES, tm), lambda bi, ti: (0, bi * nt + ti)),
                   pl.BlockSpec((SUBLANES, LANE_PAD_E), const2)),
        scratch_shapes=[pltpu.VMEM((d, d), BF16), pltpu.VMEM((SUBLANES, LANE_PAD_E), F32)],
        compiler_params=_cparams(2),
        name="mix_out",
    )(x, yc, ya, w_out, mod5, gf.reshape(1, d), mod5, mod5, w_pad, b_pad)


def _cast_ffn_kernel(wg_ref, wu_ref, wd_ref, wo_ref, wg3_ref, wu3_ref, wd3_ref, wob_ref):
    wg3_ref[0] = wg_ref[...].astype(BF16)
    wu3_ref[0] = wu_ref[...].astype(BF16)
    wd3_ref[0] = wd_ref[...].astype(BF16)

    @pl.when(pl.program_id(0) == 0)
    def _():
        wob_ref[...] = wo_ref[...].astype(BF16)


def _cast_ffn_weights(wg, wu, wd, w_out, layer, fc):
    d, ff = wg.shape
    nf = ff // fc
    return pl.pallas_call(
        _cast_ffn_kernel,
        out_shape=(jax.ShapeDtypeStruct((nf, d, fc), BF16),
                   jax.ShapeDtypeStruct((nf, d, fc), BF16),
                   jax.ShapeDtypeStruct((nf, fc, d), BF16),
                   jax.ShapeDtypeStruct((d, d), BF16)),
        grid=(nf,),
        in_specs=[
            pl.BlockSpec((d, fc), lambda f: (0, f)),
            pl.BlockSpec((d, fc), lambda f: (0, f)),
            pl.BlockSpec((fc, d), lambda f: (f, 0)),
            pl.BlockSpec((None, d, d), lambda f: (layer, 0, 0)),
        ],
        out_specs=(pl.BlockSpec((1, d, fc), lambda f: (f, 0, 0)),
                   pl.BlockSpec((1, d, fc), lambda f: (f, 0, 0)),
                   pl.BlockSpec((1, fc, d), lambda f: (f, 0, 0)),
                   pl.BlockSpec((d, d), lambda f: (0, 0))),
        compiler_params=_cparams(1),
        name="cast_ffn_weights",
    )(wg, wu, wd, w_out)


def _mix_ffn_kernel(x_ref, yc_ref, ya_ref, wo_ref, g1_ref, gf_ref, sc_ref, sh_ref, g2_ref,
                    wg_ref, wu_ref, wd_ref, o_ref, acc_ref):
    y = jnp.dot(yc_ref[0], wo_ref[0:D_CONV, :], preferred_element_type=F32)
    y = y + jnp.dot(ya_ref[0], wo_ref[D_CONV:, :], preferred_element_type=F32)
    x1 = x_ref[0] + g1_ref[0] * y
    h = _rms_mod(x1, gf_ref[...], sc_ref[0], sh_ref[0]).astype(BF16)
    for c in range(wg_ref.shape[0]):
        a = jnp.dot(h, wg_ref[c], preferred_element_type=F32)
        u = jnp.dot(h, wu_ref[c], preferred_element_type=F32)
        t = (_silu(a) * u).astype(BF16)
        dn = jnp.dot(t, wd_ref[c], preferred_element_type=F32)
        if c == 0:
            acc_ref[...] = dn
        else:
            acc_ref[...] += dn
    o_ref[0] = x1 + g2_ref[0] * acc_ref[...]


def _mix_out_dense_ffn(x, yc, ya, w_out, layer, mod5, gf, wg, wu, wd, tm, fc):
    b, s, d = x.shape
    wg3, wu3, wd3, wob = _cast_ffn_weights(wg, wu, wd, w_out, layer, fc)
    nf = wg3.shape[0]
    row = lambda bi, ti: (bi, ti, 0)
    const2 = lambda bi, ti: (0, 0)
    const3 = lambda bi, ti: (0, 0, 0)
    resident = pl.Buffered(1)
    return pl.pallas_call(
        _mix_ffn_kernel,
        out_shape=jax.ShapeDtypeStruct((b, s, d), F32),
        grid=(b, s // tm),
        in_specs=[
            pl.BlockSpec((1, tm, d), row),
            pl.BlockSpec((1, tm, D_CONV), row),
            pl.BlockSpec((1, tm, D_ATT), row),
            pl.BlockSpec((d, d), const2, pipeline_mode=resident),
            _mod_spec(layer, MOD_G1),
            pl.BlockSpec((1, d), const2),
            _mod_spec(layer, MOD_SC2),
            _mod_spec(layer, MOD_SH2),
            _mod_spec(layer, MOD_G2),
            pl.BlockSpec((nf, d, fc), const3, pipeline_mode=resident),
            pl.BlockSpec((nf, d, fc), const3, pipeline_mode=resident),
            pl.BlockSpec((nf, fc, d), const3, pipeline_mode=resident),
        ],
        out_specs=pl.BlockSpec((1, tm, d), row),
        scratch_shapes=[pltpu.VMEM((tm, d), F32)],
        compiler_params=_cparams(2),
        name="mix_out_dense_ffn",
    )(x, yc, ya, wob, mod5, gf.reshape(1, d), mod5, mod5, mod5, wg3, wu3, wd3)


def _route_tile(logits, info_ref, info_t_ref, cnt_ref, carry_ref):
    tr = logits.shape[0]
    lane = lax.broadcasted_iota(jnp.int32, (tr, LANE_PAD_E), 1).astype(F32)
    no_lane = float(LANE_PAD_E)
    v0 = jnp.max(logits, axis=-1, keepdims=True)
    i0 = jnp.min(jnp.where(logits == v0, lane, no_lane), axis=-1, keepdims=True)
    rest = jnp.where(lane == i0, -jnp.inf, logits)
    v1 = jnp.max(rest, axis=-1, keepdims=True)
    i1 = jnp.min(jnp.where(rest == v1, lane, no_lane), axis=-1, keepdims=True)
    e1 = jnp.exp(v1 - v0)
    w0 = 1.0 / (1.0 + e1)
    w1 = e1 / (1.0 + e1)
    oh0 = lane == i0
    oh1 = lane == i1
    cnt = jnp.where(jnp.logical_or(oh0, oh1), 1.0, 0.0)
    tri = (lax.broadcasted_iota(jnp.int32, (tr, tr), 1)
           < lax.broadcasted_iota(jnp.int32, (tr, tr), 0)).astype(BF16)
    before = jnp.dot(tri, cnt.astype(BF16), preferred_element_type=F32) + carry_ref[0:1, :]
    r0 = jnp.sum(jnp.where(oh0, before, 0.0), axis=-1, keepdims=True)
    r1 = jnp.sum(jnp.where(oh1, before, 0.0), axis=-1, keepdims=True)
    carry_ref[...] = carry_ref[...] + jnp.sum(cnt, axis=0, keepdims=True)
    cnt_ref[...] = carry_ref[...]
    info = jnp.where(lane == 0, i0, 0.0)
    info = jnp.where(lane == 1, i1, info)
    info = jnp.where(lane == 2, w0, info)
    info = jnp.where(lane == 3, w1, info)
    info = jnp.where(lane == 4, r0, info)
    info = jnp.where(lane == 5, r1, info)
    info_ref[0] = info
    for c in range(tr // LANES):
        blk = info[c * LANES:(c + 1) * LANES, :].T
        info_t_ref[:, c * LANES:(c + 1) * LANES] = blk[0:SUBLANES, :]


def _moe_kernel(te_ref, tn_ref, xs_ref, wg_ref, wu_ref, wd_ref, ys_ref,
                acc_ref, wgb_ref, wub_ref, wdb_ref):
    i = pl.program_id(0)
    f = pl.program_id(1)
    nhalf = tn_ref[i]
    nfull = nhalf // (MOE_SUB // MOE_HALF)

    @pl.when(f == 0)
    def _():
        acc_ref[...] = jnp.zeros_like(acc_ref)

    def block(r, rows, wgb, wub, wdb):
        xb = _unpack_bf16_pairs(xs_ref[pl.ds(r, rows), :]).astype(BF16)
        a = jnp.dot(xb, wgb, preferred_element_type=F32)
        u = jnp.dot(xb, wub, preferred_element_type=F32)
        t = (_silu(a) * u).astype(BF16)
        acc_ref[pl.ds(r, rows), :] += jnp.dot(t, wdb, preferred_element_type=F32)

    def split(units):
        out, r = [], 0
        for rows in (MOE_SUB,) * (units * MOE_HALF // MOE_SUB) + MOE_TAILS:
            if units * MOE_HALF - r >= rows:
                out.append((r, rows))
                r += rows
        return out

    for units in MOE_STATIC_UNITS:
        @pl.when(nhalf == units)
        def _(units=units):
            wgb, wub, wdb = (w[0].astype(BF16) for w in (wg_ref, wu_ref, wd_ref))
            for r, rows in split(units):
                block(r, rows, wgb, wub, wdb)

    other = nhalf > 0
    for units in MOE_STATIC_UNITS:
        other = jnp.logical_and(other, nhalf != units)

    @pl.when(other)
    def _():
        wgb_ref[...] = wg_ref[0].astype(BF16)
        wub_ref[...] = wu_ref[0].astype(BF16)
        wdb_ref[...] = wd_ref[0].astype(BF16)

        def sub(sidx, carry):
            block(pl.multiple_of(sidx * MOE_SUB, MOE_SUB), MOE_SUB,
                  wgb_ref[...], wub_ref[...], wdb_ref[...])
            return carry

        lax.fori_loop(0, nfull, sub, 0)
        done = nfull * MOE_SUB
        for rows in MOE_TAILS:
            units = rows // MOE_HALF

            @pl.when((nhalf & units) != 0)
            def _(rows=rows, units=units):
                higher = nhalf & (MOE_SUB // MOE_HALF - 1) & ~(2 * units - 1)
                block(pl.multiple_of(done + higher * MOE_HALF, MOE_HALF), rows,
                      wgb_ref[...], wub_ref[...], wdb_ref[...])

    @pl.when(f == pl.num_programs(1) - 1)
    def _():
        ys_ref[...] = _pack_bf16_pairs(acc_ref[...])


def _moe_ffn(xs, tile_e, tile_nsub, wg, wu, wd):
    rpad, dw = xs.shape
    d = 2 * dw
    ntiles = rpad // MOE_TILE
    ff = wg.shape[2]
    nf = ff // MOE_FC

    def fcol(i, f, tn):
        return jnp.where(tn[i] > 0, f, nf - 1)

    return pl.pallas_call(
        _moe_kernel,
        out_shape=jax.ShapeDtypeStruct((rpad, dw), jnp.int32),
        grid_spec=pltpu.PrefetchScalarGridSpec(
            num_scalar_prefetch=2,
            grid=(ntiles, nf),
            in_specs=[
                pl.BlockSpec((MOE_TILE, dw), lambda i, f, te, tn: (i, 0)),
                pl.BlockSpec((1, d, MOE_FC), lambda i, f, te, tn: (te[i], 0, fcol(i, f, tn))),
                pl.BlockSpec((1, d, MOE_FC), lambda i, f, te, tn: (te[i], 0, fcol(i, f, tn))),
                pl.BlockSpec((1, MOE_FC, d), lambda i, f, te, tn: (te[i], fcol(i, f, tn), 0)),
            ],
            out_specs=pl.BlockSpec((MOE_TILE, dw), lambda i, f, te, tn: (i, 0)),
            scratch_shapes=[
                pltpu.VMEM((MOE_TILE, d), F32),
                pltpu.VMEM((d, MOE_FC), BF16),
                pltpu.VMEM((d, MOE_FC), BF16),
                pltpu.VMEM((MOE_FC, d), BF16),
            ],
        ),
        compiler_params=_cparams(2),
        name="moe_ffn",
    )(tile_e, tile_nsub, xs, wg, wu, wd)


def _combine_kernel(x_ref, y0_ref, y1_ref, info_ref, g2_ref, o_ref):
    info = info_ref[0]
    w0 = info[:, 2:3]
    w1 = info[:, 3:4]
    f = w0 * _unpack_bf16_pairs(y0_ref[0, 0]) + w1 * _unpack_bf16_pairs(y1_ref[0, 0])
    o_ref[0] = x_ref[0] + g2_ref[0] * f


def _combine(x1, y01, info, mod5, layer, tm):
    b, s, d = x1.shape
    row = lambda bi, ti: (bi, ti, 0)
    return pl.pallas_call(
        _combine_kernel,
        out_shape=jax.ShapeDtypeStruct((b, s, d), F32),
        grid=(b, s // tm),
        in_specs=[
            pl.BlockSpec((1, tm, d), row),
            pl.BlockSpec((1, 1, tm, d // 2), lambda bi, ti: (0, bi, ti, 0)),
            pl.BlockSpec((1, 1, tm, d // 2), lambda bi, ti: (1, bi, ti, 0)),
            pl.BlockSpec((1, tm, LANE_PAD_E), row),
            _mod_spec(layer, MOD_G2),
        ],
        out_specs=pl.BlockSpec((1, tm, d), row),
        compiler_params=_cparams(2),
        name="moe_combine",
    )(x1, y01, y01, info, mod5)


def _moe_layer(hp, info, info_t, cnt, x1, mod5, layer, wg, wu, wd, tm):
    b, s, d = x1.shape
    n = b * s
    e0 = info_t[0].astype(jnp.int32)
    e1 = info_t[1].astype(jnp.int32)
    r0 = info_t[4].astype(jnp.int32)
    r1 = info_t[5].astype(jnp.int32)
    counts = cnt[0, :N_EXPERTS].astype(jnp.int32)

    ntiles = (2 * n) // MOE_TILE + N_EXPERTS
    tiles_per_e = (counts + MOE_TILE - 1) // MOE_TILE
    tile_end = jnp.cumsum(tiles_per_e)
    tile_start = tile_end - tiles_per_e
    total = tile_end[-1]
    tidx = jnp.arange(ntiles, dtype=jnp.int32)
    live = tidx < total
    tclip = jnp.minimum(tidx, total - 1)
    tile_e = jnp.minimum(jnp.sum(tclip[:, None] >= tile_end[None, :], axis=1),
                         N_EXPERTS - 1).astype(jnp.int32)
    rows_left = counts[tile_e] - (tclip - tile_start[tile_e]) * MOE_TILE
    rows_here = jnp.clip(rows_left, 0, MOE_TILE)
    tile_nsub = jnp.where(live, (rows_here + MOE_HALF - 1) // MOE_HALF, 0).astype(jnp.int32)

    row_start = tile_start * MOE_TILE
    eid = jnp.arange(N_EXPERTS, dtype=jnp.int32)[None, :]
    pos0 = jnp.sum(jnp.where(e0[:, None] == eid, row_start[None, :], 0), axis=1) + r0
    pos1 = jnp.sum(jnp.where(e1[:, None] == eid, row_start[None, :], 0), axis=1) + r1

    xs = _sc_scatter_rows2(hp.reshape(n, d // 2), pos0, pos1, ntiles * MOE_TILE)
    ys = _moe_ffn(xs, tile_e, tile_nsub, wg, wu, wd)
    y01 = _sc_gather_rows(ys, jnp.concatenate([pos0, pos1]))
    return _combine(x1, y01.reshape(2, b, s, d // 2), info, mod5, layer, tm)


def kernel(x, c, w_ada, b_ada, norm_mix_g, norm_ffn_g, w_in, w_out, conv_w, conv_b,
           conv_ln_g, conv_ln_b, q_norm_g, k_norm_g, rel_bias, ffn_w_gate, ffn_w_up,
           ffn_w_down, moe_w_router, moe_b_router, moe_w_gate, moe_w_up, moe_w_down):
    b, s, d = x.shape
    depth = w_ada.shape[0]
    tm = min(1024, s)
    mod5 = _ada_mod(c, w_ada, b_ada).reshape(depth, b, 6, 1, d)
    bias_t = _attn_bias_t(rel_bias)
    for l in range(depth):
        z, qt, kn, v = _mix_in(x, mod5, norm_mix_g[l], w_in, l,
                               q_norm_g[l], k_norm_g[l], tm)
        yc = _conv_branch(z, conv_w[l], conv_b[l], conv_ln_g[l], conv_ln_b[l], tm)
        ya = _attention(qt, kn, v, bias_t, l)
        i = l // 2
        if l % 2 == 0:
            x = _mix_out_dense_ffn(x, yc, ya, w_out, l, mod5, norm_ffn_g[l],
                                   ffn_w_gate[i], ffn_w_up[i], ffn_w_down[i],
                                   min(1024, s), 256)
        else:
            x1, hp, info, info_t, cnt = _mix_out_routed(
                x, yc, ya, w_out, l, mod5, norm_ffn_g[l],
                moe_w_router[i], moe_b_router[i], tm)
            x = _moe_layer(hp, info, info_t, cnt, x1, mod5, l,
                           moe_w_gate[i], moe_w_up[i], moe_w_down[i], tm)
    return x
```

```python
import functools

import jax
import jax.numpy as jnp
from jax import lax
from jax.experimental import pallas as pl
from jax.experimental.pallas import tpu as pltpu
from jax.experimental.pallas import tpu_sc as plsc

F32 = jnp.float32
BF16 = jnp.bfloat16

D_MODEL = 1024
CHUNK = 64
N_PREV_CHUNKS = 8
BAND_PAD = N_PREV_CHUNKS * CHUNK
D_CONV = 512
D_ATT = 512
HEAD_DIM = 64
N_HEADS = 8
CONV_WIDTH = 31
MAX_REL = 128
D_IN_COLS = 2 * D_CONV + 3 * D_ATT
N_EXPERTS = 8
EPS = 1e-6
NEG_INF = -1e30

LANES = 128
SUBLANES = 8
VMEM_LIMIT_BYTES = 56 * 1024 * 1024

ATT_HEADS = 4
ATT_GROUPS = N_HEADS // ATT_HEADS
ATT_W = ATT_HEADS * HEAD_DIM
ATT_Q = 2 * CHUNK
ATT_BAND = BAND_PAD + ATT_Q
ATT_L = ATT_HEADS * ATT_Q

CONV_HALO = 32
CONV_ROWS = 32
LANE_PAD_E = LANES

MOE_HALF = 256
MOE_SUB = 4 * MOE_HALF
MOE_TAILS = (2 * MOE_HALF, MOE_HALF)
MOE_TILE = 9 * MOE_HALF
MOE_STATIC_UNITS = (9, 8, 7)
MOE_FC = 512


def _cparams(n_axes, vmem=VMEM_LIMIT_BYTES):
    return pltpu.CompilerParams(
        dimension_semantics=("arbitrary",) * n_axes, vmem_limit_bytes=vmem)


def _silu(v):
    return v * jax.nn.sigmoid(v)


def _pack_bf16_pairs(v):
    w = v.shape[1] // 2
    bits = lax.bitcast_convert_type(v.astype(BF16).astype(F32), jnp.uint32)
    packed = (bits[:, w:] & jnp.uint32(0xFFFF0000)) | (bits[:, :w] >> 16)
    return lax.bitcast_convert_type(packed, jnp.int32)


def _unpack_bf16_pairs(p):
    bits = lax.bitcast_convert_type(p, jnp.uint32)
    lo = lax.bitcast_convert_type(bits << 16, F32)
    hi = lax.bitcast_convert_type(bits & jnp.uint32(0xFFFF0000), F32)
    return jnp.concatenate([lo, hi], axis=1)


SC_CORES = 2
SC_SUBCORES = 16
SC_WORKERS = SC_CORES * SC_SUBCORES
SC_CHUNK = 64


def _sc_worker_id():
    return lax.axis_index("s") * SC_CORES + lax.axis_index("c")


def _sc_mesh():
    return plsc.VectorSubcoreMesh(core_axis_name="c", subcore_axis_name="s")


def _sc_gather_rows(table, idx):
    _, w = table.shape
    b = idx.shape[0]
    per_w = b // SC_WORKERS
    nch = per_w // SC_CHUNK

    def body(table_hbm, idx_hbm, out_hbm, idx_v, rows_v, gsem, wsem):
        wid = _sc_worker_id()
        base = wid * per_w
        pltpu.sync_copy(idx_hbm.at[wid], idx_v)
        gathers = [None] * nch
        writes = [None] * nch
        gathers[0] = pltpu.async_copy(table_hbm.at[idx_v.at[0]], rows_v.at[0], gsem.at[0])
        for c in range(nch):
            slot = c % 2
            gathers[c].wait()
            if c + 1 < nch:
                if c >= 1:
                    writes[c - 1].wait()
                gathers[c + 1] = pltpu.async_copy(
                    table_hbm.at[idx_v.at[c + 1]], rows_v.at[1 - slot], gsem.at[1 - slot])
            writes[c] = pltpu.async_copy(
                rows_v.at[slot], out_hbm.at[pl.ds(base + c * SC_CHUNK, SC_CHUNK)], wsem.at[slot])
        if nch >= 2:
            writes[nch - 2].wait()
        writes[nch - 1].wait()

    call = pl.kernel(
        body, mesh=_sc_mesh(),
        out_type=jax.ShapeDtypeStruct((b, w), jnp.int32),
        scratch_types=[pltpu.VMEM((nch, SC_CHUNK), jnp.int32),
                       pltpu.VMEM((2, SC_CHUNK, w), jnp.int32),
                       pltpu.SemaphoreType.DMA((2,)), pltpu.SemaphoreType.DMA((2,))],
        name="sc_gather_rows")
    return call(table, idx.reshape(SC_WORKERS, nch, SC_CHUNK))


def _sc_scatter_rows2(src, idx0, idx1, rows_out):
    n, w = src.shape
    per_w = n // SC_WORKERS
    nch = per_w // SC_CHUNK

    def body(src_hbm, i0_hbm, i1_hbm, out_hbm, i0_v, i1_v, rows_v, rsem, wsem):
        wid = _sc_worker_id()
        base = wid * per_w
        pltpu.sync_copy(i0_hbm.at[wid], i0_v)
        pltpu.sync_copy(i1_hbm.at[wid], i1_v)
        reads = [None] * nch
        writes = [None] * nch
        reads[0] = pltpu.async_copy(src_hbm.at[pl.ds(base, SC_CHUNK)], rows_v.at[0], rsem.at[0])
        for c in range(nch):
            slot = c % 2
            reads[c].wait()
            if c + 1 < nch:
                if c >= 1:
                    for wr in writes[c - 1]:
                        wr.wait()
                reads[c + 1] = pltpu.async_copy(
                    src_hbm.at[pl.ds(base + (c + 1) * SC_CHUNK, SC_CHUNK)],
                    rows_v.at[1 - slot], rsem.at[1 - slot])
            writes[c] = (
                pltpu.async_copy(rows_v.at[slot], out_hbm.at[i0_v.at[c]], wsem.at[slot, 0]),
                pltpu.async_copy(rows_v.at[slot], out_hbm.at[i1_v.at[c]], wsem.at[slot, 1]),
            )
        for c in range(max(nch - 2, 0), nch):
            for wr in writes[c]:
                wr.wait()

    call = pl.kernel(
        body, mesh=_sc_mesh(),
        out_type=jax.ShapeDtypeStruct((rows_out, w), jnp.int32),
        scratch_types=[pltpu.VMEM((nch, SC_CHUNK), jnp.int32),
                       pltpu.VMEM((nch, SC_CHUNK), jnp.int32),
                       pltpu.VMEM((2, SC_CHUNK, w), jnp.int32),
                       pltpu.SemaphoreType.DMA((2,)), pltpu.SemaphoreType.DMA((2, 2))],
        name="sc_scatter_rows")
    shape3 = (SC_WORKERS, nch, SC_CHUNK)
    return call(src, idx0.reshape(shape3), idx1.reshape(shape3))


def _ada_kernel(c_ref, w_ref, b_ref, o_ref):
    ca = _silu(c_ref[...]).astype(BF16)
    w = w_ref[0].astype(BF16)
    o_ref[0] = jnp.dot(ca, w, preferred_element_type=F32) + b_ref[0]


def _ada_mod(c, w_ada, b_ada):
    depth, d, n6 = w_ada.shape
    b = c.shape[0]
    rows = 16
    c_pad = jnp.zeros((rows, d), F32).at[:b].set(c)
    tn = 1536
    out = pl.pallas_call(
        _ada_kernel,
        out_shape=jax.ShapeDtypeStruct((depth, rows, n6), F32),
        grid=(depth, n6 // tn),
        in_specs=[
            pl.BlockSpec((rows, d), lambda l, j: (0, 0)),
            pl.BlockSpec((1, d, tn), lambda l, j: (l, 0, j)),
            pl.BlockSpec((1, 1, tn), lambda l, j: (l, 0, j)),
        ],
        out_specs=pl.BlockSpec((1, rows, tn), lambda l, j: (l, 0, j)),
        compiler_params=_cparams(2),
        name="ada_mod",
    )(c_pad, w_ada, b_ada.reshape(depth, 1, n6))
    return out[:, :b]


MOD_SH1, MOD_SC1, MOD_G1, MOD_SH2, MOD_SC2, MOD_G2 = range(6)


def _mod_spec(layer, chunk):
    return pl.BlockSpec((None, 1, None, 1, D_MODEL),
                        lambda bi, *_: (layer, bi, chunk, 0, 0))


def _rms_mod(xf, g, sc, sh):
    ms = jnp.mean(xf * xf, axis=-1, keepdims=True)
    return xf * lax.rsqrt(ms + EPS) * g * (1.0 + sc) + sh


def _mix_in_kernel(x_ref, sc_ref, sh_ref, g_ref, w_ref, gq_ref, gk_ref, ones_ref,
                   z_ref, qt_ref, k_ref, v_ref, wbf_ref):
    first = jnp.logical_and(pl.program_id(0) == 0, pl.program_id(1) == 0)

    @pl.when(first)
    def _():
        wbf_ref[...] = w_ref[...].astype(BF16)

    h = _rms_mod(x_ref[0], g_ref[...], sc_ref[0], sh_ref[0]).astype(BF16)
    proj = jnp.dot(h, wbf_ref[...], preferred_element_type=F32)

    a = proj[:, :D_CONV]
    gate = proj[:, D_CONV:2 * D_CONV]
    z_ref[0] = (a * jax.nn.sigmoid(gate)).astype(BF16)

    def head_norm(t, g):
        sq = (t * t).astype(BF16)
        ss = jnp.concatenate(
            [jnp.dot(sq[:, c:c + ATT_W], ones_ref[...], preferred_element_type=F32)
             for c in range(0, D_ATT, ATT_W)], axis=1)
        return t * lax.rsqrt(ss * (1.0 / HEAD_DIM) + EPS) * g

    o = 2 * D_CONV
    q = head_norm(proj[:, o:o + D_ATT], gq_ref[...])
    k_ref[0] = head_norm(proj[:, o + D_ATT:o + 2 * D_ATT], gk_ref[...]).astype(BF16)
    v_ref[0] = proj[:, o + 2 * D_ATT:].astype(BF16)
    for cidx in range(q.shape[0] // ATT_Q):
        rows = slice(cidx * ATT_Q, (cidx + 1) * ATT_Q)
        qt_ref[0, cidx] = q[rows, :].T.astype(BF16)


def _mix_in(x, mod5, g, w_in, layer, gq, gk, tm):
    b, s, d = x.shape
    ones_bd = (jnp.arange(ATT_W)[:, None] // HEAD_DIM
               == jnp.arange(ATT_W)[None, :] // HEAD_DIM).astype(BF16)
    gq_t = (jnp.tile(gq, N_HEADS) * (HEAD_DIM ** -0.5)).reshape(1, D_ATT)
    gk_t = jnp.tile(gk, N_HEADS).reshape(1, D_ATT)
    row = lambda bi, ti: (bi, ti, 0)
    const2 = lambda bi, ti: (0, 0)
    return pl.pallas_call(
        _mix_in_kernel,
        out_shape=(
            jax.ShapeDtypeStruct((b, s, D_CONV), BF16),
            jax.ShapeDtypeStruct((b, s // ATT_Q, D_ATT, ATT_Q), BF16),
            jax.ShapeDtypeStruct((b, s, D_ATT), BF16),
            jax.ShapeDtypeStruct((b, s, D_ATT), BF16),
        ),
        grid=(b, s // tm),
        in_specs=[
            pl.BlockSpec((1, tm, d), row),
            _mod_spec(layer, MOD_SC1),
            _mod_spec(layer, MOD_SH1),
            pl.BlockSpec((1, d), const2),
            pl.BlockSpec((None, d, D_IN_COLS), lambda bi, ti: (layer, 0, 0),
                         pipeline_mode=pl.Buffered(1)),
            pl.BlockSpec((1, D_ATT), const2),
            pl.BlockSpec((1, D_ATT), const2),
            pl.BlockSpec((ATT_W, ATT_W), const2),
        ],
        out_specs=(
            pl.BlockSpec((1, tm, D_CONV), row),
            pl.BlockSpec((1, tm // ATT_Q, D_ATT, ATT_Q), lambda bi, ti: (bi, ti, 0, 0)),
            pl.BlockSpec((1, tm, D_ATT), row),
            pl.BlockSpec((1, tm, D_ATT), row),
        ),
        scratch_shapes=[pltpu.VMEM((d, D_IN_COLS), BF16)],
        compiler_params=_cparams(2),
        name="mix_in",
    )(x, mod5, mod5, g.reshape(1, d), w_in, gq_t, gk_t, ones_bd)


def _conv_kernel(zc_ref, zp_ref, w_ref, cb_ref, lg_ref, lb_ref, o_ref, win_ref, sh_ref,
                 acc_ref):
    tt = zc_ref.shape[1]
    t = pl.program_id(1)
    halo = zp_ref[0].astype(F32)
    win_ref[0:CONV_HALO, :] = jnp.where(t == 0, 0.0, halo)
    win_ref[CONV_HALO:, :] = zc_ref[0].astype(F32)
    span = tt + CONV_HALO - SUBLANES
    for sft in range(1, SUBLANES):
        sh_ref[sft - 1, 0:span, :] = win_ref[sft:sft + span, :]
    base = CONV_HALO - (CONV_WIDTH - 1)
    tiles = CONV_ROWS // SUBLANES

    for g in range(tt // CONV_ROWS):
        r = g * CONV_ROWS
        acc = jnp.zeros((tiles, SUBLANES, D_CONV), F32) + cb_ref[...]
        for j in range(CONV_WIDTH):
            whole, sft = divmod(base + j, SUBLANES)
            start = r + whole * SUBLANES
            if sft == 0:
                tap = win_ref[start:start + CONV_ROWS, :]
            else:
                tap = sh_ref[sft - 1, start:start + CONV_ROWS, :]
            acc = acc + tap.reshape(tiles, SUBLANES, D_CONV) * w_ref[j]
        acc_ref[r:r + CONV_ROWS, :] = acc.reshape(CONV_ROWS, D_CONV)
    acc = acc_ref[...]
    mu = jnp.mean(acc, axis=-1, keepdims=True)
    xc = acc - mu
    var = jnp.mean(xc * xc, axis=-1, keepdims=True)
    y = xc * lax.rsqrt(var + EPS) * lg_ref[...] + lb_ref[...]
    o_ref[0] = _silu(y).astype(BF16)


def _conv_branch(z, conv_w, conv_b, ln_g, ln_b, tt):
    b, s, c = z.shape
    hb = tt // CONV_HALO
    w_tiles = jnp.broadcast_to(conv_w.reshape(CONV_WIDTH, 1, c), (CONV_WIDTH, SUBLANES, c))
    const2 = lambda bi, ti: (0, 0)
    return pl.pallas_call(
        _conv_kernel,
        out_shape=jax.ShapeDtypeStruct((b, s, c), BF16),
        grid=(b, s // tt),
        in_specs=[
            pl.BlockSpec((1, tt, c), lambda bi, ti: (bi, ti, 0)),
            pl.BlockSpec((1, CONV_HALO, c),
                         lambda bi, ti: (bi, jnp.maximum(ti * hb - 1, 0), 0)),
            pl.BlockSpec((CONV_WIDTH, SUBLANES, c), lambda bi, ti: (0, 0, 0)),
            pl.BlockSpec((1, c), const2),
            pl.BlockSpec((1, c), const2),
            pl.BlockSpec((1, c), const2),
        ],
        out_specs=pl.BlockSpec((1, tt, c), lambda bi, ti: (bi, ti, 0)),
        scratch_shapes=[pltpu.VMEM((tt + CONV_HALO, c), F32),
                        pltpu.VMEM((SUBLANES - 1, tt + CONV_HALO, c), F32),
                        pltpu.VMEM((tt, c), F32)],
        compiler_params=_cparams(2),
        name="conv_branch",
    )(z, z, w_tiles, conv_b.reshape(1, c),
      ln_g.reshape(1, c), ln_b.reshape(1, c))


def _attn_kernel(qt_ref, k_ref, v_ref, bias_ref, o_ref, kpad_ref, vpad_ref,
                 st0_ref, st1_ref, pb0_ref, pb1_ref, den0_ref, den1_ref):
    st_refs = (st0_ref, st1_ref)
    pb_refs = (pb0_ref, pb1_ref)
    den_refs = (den0_ref, den1_ref)
    s = k_ref.shape[1]
    kpad_ref[0:BAND_PAD, :] = jnp.zeros((BAND_PAD, ATT_W), BF16)
    kpad_ref[BAND_PAD:, :] = k_ref[0]
    vpad_ref[0:BAND_PAD, :] = jnp.zeros((BAND_PAD, ATT_W), BF16)
    vpad_ref[BAND_PAD:, :] = v_ref[0]

    iota = lambda shape, dim: lax.broadcasted_iota(jnp.int32, shape, dim)
    q_shift = ATT_Q.bit_length() - 1
    d_shift = HEAD_DIM.bit_length() - 1
    qb_mask = (iota((ATT_W, ATT_L), 0) >> d_shift) == (iota((ATT_W, ATT_L), 1) >> q_shift)
    key_row = lax.broadcasted_iota(jnp.int32, (ATT_BAND, ATT_L), 0)

    def scores(m, p):
        r0 = m * ATT_Q
        qt = qt_ref[0, m]
        qb = jnp.where(qb_mask, jnp.concatenate([qt] * ATT_HEADS, axis=1), 0)
        kb = kpad_ref[pl.ds(r0, ATT_BAND), :]
        st_refs[p][...] = jnp.dot(kb, qb.astype(BF16),
                                  preferred_element_type=F32)

    def softmax(m, p, masked):
        st = st_refs[p][...] + bias_ref[0]
        if masked:
            st = jnp.where(key_row >= BAND_PAD - m * ATT_Q, st, NEG_INF)
        mx = jnp.max(st, axis=0, keepdims=True)
        e = jnp.exp(st - mx)
        den_refs[p][...] = jnp.sum(e, axis=0, keepdims=True)
        pb_refs[p][...] = e.astype(BF16)

    def values(m, p):
        r0 = m * ATT_Q
        vb = vpad_ref[pl.ds(r0, ATT_BAND), :]
        o = lax.dot_general(pb_refs[p][...], vb, (((0,), (0,)), ((), ())),
                            preferred_element_type=F32)
        inv = jnp.transpose(jnp.broadcast_to(1.0 / den_refs[p][...], (SUBLANES, ATT_L)))[:, :1]
        y = jnp.concatenate(
            [o[h * ATT_Q:(h + 1) * ATT_Q, h * HEAD_DIM:(h + 1) * HEAD_DIM]
             * inv[h * ATT_Q:(h + 1) * ATT_Q] for h in range(ATT_HEADS)], axis=1)
        o_ref[0, pl.ds(r0, ATT_Q), :] = y.astype(BF16)

    n = s // ATT_Q
    n_masked = BAND_PAD // ATT_Q
    for m in range(n + 2):
        if m < n:
            scores(m, m % 2)
        if 1 <= m <= n:
            softmax(m - 1, (m - 1) % 2, m - 1 < n_masked)
        if m >= 2:
            values(m - 2, m % 2)


def _attn_bias_t(rel_bias):
    rb = rel_bias.astype(F32).reshape(-1, rel_bias.shape[-1])
    n_rows = rb.shape[0]
    nu = ATT_BAND + ATT_Q - 1
    n_low = BAND_PAD - MAX_REL + ATT_Q
    t = jnp.concatenate([jnp.repeat(rb[:, :1], n_low, axis=1),
                         rb[:, 1:1 + nu - n_low]], axis=1)
    n_win = ATT_BAND // ATT_Q + 1
    win = jnp.pad(t, ((0, 0), (0, n_win * ATT_Q - nu))).reshape(n_rows, n_win, ATT_Q)[:, :, ::-1]
    n_tab = n_rows // ATT_HEADS
    return pl.pallas_call(
        _bias_kernel,
        out_shape=jax.ShapeDtypeStruct((n_tab, ATT_BAND, ATT_L), F32),
        grid=(n_tab,),
        in_specs=[pl.BlockSpec((1, ATT_HEADS * n_win, ATT_Q), lambda i: (i, 0, 0))],
        out_specs=pl.BlockSpec((1, ATT_BAND, ATT_L), lambda i: (i, 0, 0)),
        compiler_params=_cparams(1),
        name="attn_bias",
    )(win.reshape(n_tab, ATT_HEADS * n_win, ATT_Q))


def _bias_kernel(w_ref, o_ref):
    n_win = ATT_BAND // ATT_Q + 1
    rr = lax.broadcasted_iota(jnp.int32, (ATT_Q, ATT_Q), 0)
    qq = lax.broadcasted_iota(jnp.int32, (ATT_Q, ATT_Q), 1)
    first = (qq // CHUNK) * CHUNK
    for h in range(ATT_HEADS):
        circ = [pltpu.roll(jnp.broadcast_to(w_ref[0, h * n_win + a:h * n_win + a + 1, :],
                                            (ATT_Q, ATT_Q)), 0, 1, stride=1, stride_axis=0)
                for a in range(n_win)]
        for a in range(n_win - 1):
            blk = jnp.where(rr > qq, circ[a + 1], circ[a])
            r = a * ATT_Q + rr
            valid = (r >= first) & (r < first + BAND_PAD + CHUNK)
            o_ref[0, a * ATT_Q:(a + 1) * ATT_Q, h * ATT_Q:(h + 1) * ATT_Q] = (
                jnp.where(valid, blk, NEG_INF))


def _attention(qt, kn, v, bias_t, layer):
    b, s, _ = kn.shape
    nck = s // ATT_Q
    assert nck >= 2, "the attention pipeline needs at least two query steps"
    return pl.pallas_call(
        _attn_kernel,
        out_shape=jax.ShapeDtypeStruct((b, s, D_ATT), BF16),
        grid=(b, ATT_GROUPS),
        in_specs=[
            pl.BlockSpec((1, nck, ATT_W, ATT_Q), lambda bi, gi: (bi, 0, gi, 0)),
            pl.BlockSpec((1, s, ATT_W), lambda bi, gi: (bi, 0, gi)),
            pl.BlockSpec((1, s, ATT_W), lambda bi, gi: (bi, 0, gi)),
            pl.BlockSpec((1, ATT_BAND, ATT_L), lambda bi, gi: (layer * ATT_GROUPS + gi, 0, 0)),
        ],
        out_specs=pl.BlockSpec((1, s, ATT_W), lambda bi, gi: (bi, 0, gi)),
        scratch_shapes=[
            pltpu.VMEM((s + BAND_PAD, ATT_W), BF16),
            pltpu.VMEM((s + BAND_PAD, ATT_W), BF16),
            pltpu.VMEM((ATT_BAND, ATT_L), F32), pltpu.VMEM((ATT_BAND, ATT_L), F32),
            pltpu.VMEM((ATT_BAND, ATT_L), BF16), pltpu.VMEM((ATT_BAND, ATT_L), BF16),
            pltpu.VMEM((1, ATT_L), F32), pltpu.VMEM((1, ATT_L), F32),
        ],
        compiler_params=_cparams(2),
        name="band_attention",
    )(qt, kn, v, bias_t)


def _mix_out_kernel(x_ref, yc_ref, ya_ref, w_ref, g1_ref, gf_ref, sc_ref, sh_ref,
                    wr_ref, br_ref, x1_ref, h_ref, info_ref, info_t_ref, cnt_ref,
                    wbf_ref, carry_ref):
    first = jnp.logical_and(pl.program_id(0) == 0, pl.program_id(1) == 0)

    @pl.when(first)
    def _():
        wbf_ref[...] = w_ref[...].astype(BF16)
        carry_ref[...] = jnp.zeros_like(carry_ref)

    y = jnp.dot(yc_ref[0], wbf_ref[0:D_CONV, :], preferred_element_type=F32)
    y = y + jnp.dot(ya_ref[0], wbf_ref[D_CONV:, :], preferred_element_type=F32)
    x1 = x_ref[0] + g1_ref[0] * y
    x1_ref[0] = x1
    h = _rms_mod(x1, gf_ref[...], sc_ref[0], sh_ref[0])
    h_ref[0] = _pack_bf16_pairs(h)
    logits = jnp.dot(h.astype(BF16), wr_ref[...], preferred_element_type=F32) + br_ref[...]
    _route_tile(logits, info_ref, info_t_ref, cnt_ref, carry_ref)


def _mix_out_routed(x, yc, ya, w_out, layer, mod5, gf, w_router, b_router, tm):
    b, s, d = x.shape
    nt = s // tm
    row = lambda bi, ti: (bi, ti, 0)
    const2 = lambda bi, ti: (0, 0)
    w_pad = jnp.zeros((d, LANE_PAD_E), BF16).at[:, :N_EXPERTS].set(w_router.astype(BF16))
    b_pad = jnp.full((1, LANE_PAD_E), -jnp.inf, F32).at[0, :N_EXPERTS].set(
        b_router.astype(F32))
    return pl.pallas_call(
        _mix_out_kernel,
        out_shape=(jax.ShapeDtypeStruct((b, s, d), F32),
                   jax.ShapeDtypeStruct((b, s, d // 2), jnp.int32),
                   jax.ShapeDtypeStruct((b, s, LANE_PAD_E), F32),
                   jax.ShapeDtypeStruct((SUBLANES, b * s), F32),
                   jax.ShapeDtypeStruct((SUBLANES, LANE_PAD_E), F32)),
        grid=(b, nt),
        in_specs=[
            pl.BlockSpec((1, tm, d), row),
            pl.BlockSpec((1, tm, D_CONV), row),
            pl.BlockSpec((1, tm, D_ATT), row),
            pl.BlockSpec((None, d, d), lambda bi, ti: (layer, 0, 0),
                         pipeline_mode=pl.Buffered(1)),
            _mod_spec(layer, MOD_G1),
            pl.BlockSpec((1, d), const2),
            _mod_spec(layer, MOD_SC2),
            _mod_spec(layer, MOD_SH2),
            pl.BlockSpec((d, LANE_PAD_E), const2),
            pl.BlockSpec((1, LANE_PAD_E), const2),
        ],
        out_specs=(pl.BlockSpec((1, tm, d), row), pl.BlockSpec((1, tm, d // 2), row),
                   pl.BlockSpec((1, tm, LANE_PAD_E), row),
                   pl.BlockSpec((SUBLANES, tm), lambda bi, ti: (0, bi * nt + ti)),
                   pl.BlockSpec((SUBLANES, LANE_PAD_E), const2)),
        scratch_shapes=[pltpu.VMEM((d, d), BF16), pltpu.VMEM((SUBLANES, LANE_PAD_E), F32)],
        compiler_params=_cparams(2),
        name="mix_out",
    )(x, yc, ya, w_out, mod5, gf.reshape(1, d), mod5, mod5, w_pad, b_pad)


def _cast_ffn_kernel(wg_ref, wu_ref, wd_ref, wo_ref, wg3_ref, wu3_ref, wd3_ref, wob_ref):
    wg3_ref[0] = wg_ref[...].astype(BF16)
    wu3_ref[0] = wu_ref[...].astype(BF16)
    wd3_ref[0] = wd_ref[...].astype(BF16)

    @pl.when(pl.program_id(0) == 0)
    def _():
        wob_ref[...] = wo_ref[...].astype(BF16)


def _cast_ffn_weights(wg, wu, wd, w_out, layer, fc):
    d, ff = wg.shape
    nf = ff // fc
    return pl.pallas_call(
        _cast_ffn_kernel,
        out_shape=(jax.ShapeDtypeStruct((nf, d, fc), BF16),
                   jax.ShapeDtypeStruct((nf, d, fc), BF16),
                   jax.ShapeDtypeStruct((nf, fc, d), BF16),
                   jax.ShapeDtypeStruct((d, d), BF16)),
        grid=(nf,),
        in_specs=[
            pl.BlockSpec((d, fc), lambda f: (0, f)),
            pl.BlockSpec((d, fc), lambda f: (0, f)),
            pl.BlockSpec((fc, d), lambda f: (f, 0)),
            pl.BlockSpec((None, d, d), lambda f: (layer, 0, 0)),
        ],
        out_specs=(pl.BlockSpec((1, d, fc), lambda f: (f, 0, 0)),
                   pl.BlockSpec((1, d, fc), lambda f: (f, 0, 0)),
                   pl.BlockSpec((1, fc, d), lambda f: (f, 0, 0)),
                   pl.BlockSpec((d, d), lambda f: (0, 0))),
        compiler_params=_cparams(1),
        name="cast_ffn_weights",
    )(wg, wu, wd, w_out)


def _mix_ffn_kernel(x_ref, yc_ref, ya_ref, wo_ref, g1_ref, gf_ref, sc_ref, sh_ref, g2_ref,
                    wg_ref, wu_ref, wd_ref, o_ref, acc_ref):
    y = jnp.dot(yc_ref[0], wo_ref[0:D_CONV, :], preferred_element_type=F32)
    y = y + jnp.dot(ya_ref[0], wo_ref[D_CONV:, :], preferred_element_type=F32)
    x1 = x_ref[0] + g1_ref[0] * y
    h = _rms_mod(x1, gf_ref[...], sc_ref[0], sh_ref[0]).astype(BF16)
    for c in range(wg_ref.shape[0]):
        a = jnp.dot(h, wg_ref[c], preferred_element_type=F32)
        u = jnp.dot(h, wu_ref[c], preferred_element_type=F32)
        t = (_silu(a) * u).astype(BF16)
        dn = jnp.dot(t, wd_ref[c], preferred_element_type=F32)
        if c == 0:
            acc_ref[...] = dn
        else:
            acc_ref[...] += dn
    o_ref[0] = x1 + g2_ref[0] * acc_ref[...]


def _mix_out_dense_ffn(x, yc, ya, w_out, layer, mod5, gf, wg, wu, wd, tm, fc):
    b, s, d = x.shape
    wg3, wu3, wd3, wob = _cast_ffn_weights(wg, wu, wd, w_out, layer, fc)
    nf = wg3.shape[0]
    row = lambda bi, ti: (bi, ti, 0)
    const2 = lambda bi, ti: (0, 0)
    const3 = lambda bi, ti: (0, 0, 0)
    resident = pl.Buffered(1)
    return pl.pallas_call(
        _mix_ffn_kernel,
        out_shape=jax.ShapeDtypeStruct((b, s, d), F32),
        grid=(b, s // tm),
        in_specs=[
            pl.BlockSpec((1, tm, d), row),
            pl.BlockSpec((1, tm, D_CONV), row),
            pl.BlockSpec((1, tm, D_ATT), row),
            pl.BlockSpec((d, d), const2, pipeline_mode=resident),
            _mod_spec(layer, MOD_G1),
            pl.BlockSpec((1, d), const2),
            _mod_spec(layer, MOD_SC2),
            _mod_spec(layer, MOD_SH2),
            _mod_spec(layer, MOD_G2),
            pl.BlockSpec((nf, d, fc), const3, pipeline_mode=resident),
            pl.BlockSpec((nf, d, fc), const3, pipeline_mode=resident),
            pl.BlockSpec((nf, fc, d), const3, pipeline_mode=resident),
        ],
        out_specs=pl.BlockSpec((1, tm, d), row),
        scratch_shapes=[pltpu.VMEM((tm, d), F32)],
        compiler_params=_cparams(2),
        name="mix_out_dense_ffn",
    )(x, yc, ya, wob, mod5, gf.reshape(1, d), mod5, mod5, mod5, wg3, wu3, wd3)


def _route_tile(logits, info_ref, info_t_ref, cnt_ref, carry_ref):
    tr = logits.shape[0]
    lane = lax.broadcasted_iota(jnp.int32, (tr, LANE_PAD_E), 1).astype(F32)
    no_lane = float(LANE_PAD_E)
    v0 = jnp.max(logits, axis=-1, keepdims=True)
    i0 = jnp.min(jnp.where(logits == v0, lane, no_lane), axis=-1, keepdims=True)
    rest = jnp.where(lane == i0, -jnp.inf, logits)
    v1 = jnp.max(rest, axis=-1, keepdims=True)
    i1 = jnp.min(jnp.where(rest == v1, lane, no_lane), axis=-1, keepdims=True)
    e1 = jnp.exp(v1 - v0)
    w0 = 1.0 / (1.0 + e1)
    w1 = e1 / (1.0 + e1)
    oh0 = lane == i0
    oh1 = lane == i1
    cnt = jnp.where(jnp.logical_or(oh0, oh1), 1.0, 0.0)
    tri = (lax.broadcasted_iota(jnp.int32, (tr, tr), 1)
           < lax.broadcasted_iota(jnp.int32, (tr, tr), 0)).astype(BF16)
    before = jnp.dot(tri, cnt.astype(BF16), preferred_element_type=F32) + carry_ref[0:1, :]
    r0 = jnp.sum(jnp.where(oh0, before, 0.0), axis=-1, keepdims=True)
    r1 = jnp.sum(jnp.where(oh1, before, 0.0), axis=-1, keepdims=True)
    carry_ref[...] = carry_ref[...] + jnp.sum(cnt, axis=0, keepdims=True)
    cnt_ref[...] = carry_ref[...]
    info = jnp.where(lane == 0, i0, 0.0)
    info = jnp.where(lane == 1, i1, info)
    info = jnp.where(lane == 2, w0, info)
    info = jnp.where(lane == 3, w1, info)
    info = jnp.where(lane == 4, r0, info)
    info = jnp.where(lane == 5, r1, info)
    info_ref[0] = info
    for c in range(tr // LANES):
        blk = info[c * LANES:(c + 1) * LANES, :].T
        info_t_ref[:, c * LANES:(c + 1) * LANES] = blk[0:SUBLANES, :]


def _moe_kernel(te_ref, tn_ref, xs_ref, wg_ref, wu_ref, wd_ref, ys_ref,
                acc_ref, wgb_ref, wub_ref, wdb_ref):
    i = pl.program_id(0)
    f = pl.program_id(1)
    nhalf = tn_ref[i]
    nfull = nhalf // (MOE_SUB // MOE_HALF)

    @pl.when(f == 0)
    def _():
        acc_ref[...] = jnp.zeros_like(acc_ref)

    def block(r, rows, wgb, wub, wdb):
        xb = _unpack_bf16_pairs(xs_ref[pl.ds(r, rows), :]).astype(BF16)
        a = jnp.dot(xb, wgb, preferred_element_type=F32)
        u = jnp.dot(xb, wub, preferred_element_type=F32)
        t = (_silu(a) * u).astype(BF16)
        acc_ref[pl.ds(r, rows), :] += jnp.dot(t, wdb, preferred_element_type=F32)

    def split(units):
        out, r = [], 0
        for rows in (MOE_SUB,) * (units * MOE_HALF // MOE_SUB) + MOE_TAILS:
            if units * MOE_HALF - r >= rows:
                out.append((r, rows))
                r += rows
        return out

    for units in MOE_STATIC_UNITS:
        @pl.when(nhalf == units)
        def _(units=units):
            wgb, wub, wdb = (w[0].astype(BF16) for w in (wg_ref, wu_ref, wd_ref))
            for r, rows in split(units):
                block(r, rows, wgb, wub, wdb)

    other = nhalf > 0
    for units in MOE_STATIC_UNITS:
        other = jnp.logical_and(other, nhalf != units)

    @pl.when(other)
    def _():
        wgb_ref[...] = wg_ref[0].astype(BF16)
        wub_ref[...] = wu_ref[0].astype(BF16)
        wdb_ref[...] = wd_ref[0].astype(BF16)

        def sub(sidx, carry):
            block(pl.multiple_of(sidx * MOE_SUB, MOE_SUB), MOE_SUB,
                  wgb_ref[...], wub_ref[...], wdb_ref[...])
            return carry

        lax.fori_loop(0, nfull, sub, 0)
        done = nfull * MOE_SUB
        for rows in MOE_TAILS:
            units = rows // MOE_HALF

            @pl.when((nhalf & units) != 0)
            def _(rows=rows, units=units):
                higher = nhalf & (MOE_SUB // MOE_HALF - 1) & ~(2 * units - 1)
                block(pl.multiple_of(done + higher * MOE_HALF, MOE_HALF), rows,
                      wgb_ref[...], wub_ref[...], wdb_ref[...])

    @pl.when(f == pl.num_programs(1) - 1)
    def _():
        ys_ref[...] = _pack_bf16_pairs(acc_ref[...])


def _moe_ffn(xs, tile_e, tile_nsub, wg, wu, wd):
    rpad, dw = xs.shape
    d = 2 * dw
    ntiles = rpad // MOE_TILE
    ff = wg.shape[2]
    nf = ff // MOE_FC

    def fcol(i, f, tn):
        return jnp.where(tn[i] > 0, f, nf - 1)

    return pl.pallas_call(
        _moe_kernel,
        out_shape=jax.ShapeDtypeStruct((rpad, dw), jnp.int32),
        grid_spec=pltpu.PrefetchScalarGridSpec(
            num_scalar_prefetch=2,
            grid=(ntiles, nf),
            in_specs=[
                pl.BlockSpec((MOE_TILE, dw), lambda i, f, te, tn: (i, 0)),
                pl.BlockSpec((1, d, MOE_FC), lambda i, f, te, tn: (te[i], 0, fcol(i, f, tn))),
                pl.BlockSpec((1, d, MOE_FC), lambda i, f, te, tn: (te[i], 0, fcol(i, f, tn))),
                pl.BlockSpec((1, MOE_FC, d), lambda i, f, te, tn: (te[i], fcol(i, f, tn), 0)),
            ],
            out_specs=pl.BlockSpec((MOE_TILE, dw), lambda i, f, te, tn: (i, 0)),
            scratch_shapes=[
                pltpu.VMEM((MOE_TILE, d), F32),
                pltpu.VMEM((d, MOE_FC), BF16),
                pltpu.VMEM((d, MOE_FC), BF16),
                pltpu.VMEM((MOE_FC, d), BF16),
            ],
        ),
        compiler_params=_cparams(2),
        name="moe_ffn",
    )(tile_e, tile_nsub, xs, wg, wu, wd)


def _combine_kernel(x_ref, y0_ref, y1_ref, info_ref, g2_ref, o_ref):
    info = info_ref[0]
    w0 = info[:, 2:3]
    w1 = info[:, 3:4]
    f = w0 * _unpack_bf16_pairs(y0_ref[0, 0]) + w1 * _unpack_bf16_pairs(y1_ref[0, 0])
    o_ref[0] = x_ref[0] + g2_ref[0] * f


def _combine(x1, y01, info, mod5, layer, tm):
    b, s, d = x1.shape
    row = lambda bi, ti: (bi, ti, 0)
    return pl.pallas_call(
        _combine_kernel,
        out_shape=jax.ShapeDtypeStruct((b, s, d), F32),
        grid=(b, s // tm),
        in_specs=[
            pl.BlockSpec((1, tm, d), row),
            pl.BlockSpec((1, 1, tm, d // 2), lambda bi, ti: (0, bi, ti, 0)),
            pl.BlockSpec((1, 1, tm, d // 2), lambda bi, ti: (1, bi, ti, 0)),
            pl.BlockSpec((1, tm, LANE_PAD_E), row),
            _mod_spec(layer, MOD_G2),
        ],
        out_specs=pl.BlockSpec((1, tm, d), row),
        compiler_params=_cparams(2),
        name="moe_combine",
    )(x1, y01, y01, info, mod5)


def _moe_layer(hp, info, info_t, cnt, x1, mod5, layer, wg, wu, wd, tm):
    b, s, d = x1.shape
    n = b * s
    e0 = info_t[0].astype(jnp.int32)
    e1 = info_t[1].astype(jnp.int32)
    r0 = info_t[4].astype(jnp.int32)
    r1 = info_t[5].astype(jnp.int32)
    counts = cnt[0, :N_EXPERTS].astype(jnp.int32)

    ntiles = (2 * n) // MOE_TILE + N_EXPERTS
    tiles_per_e = (counts + MOE_TILE - 1) // MOE_TILE
    tile_end = jnp.cumsum(tiles_per_e)
    tile_start = tile_end - tiles_per_e
    total = tile_end[-1]
    tidx = jnp.arange(ntiles, dtype=jnp.int32)
    live = tidx < total
    tclip = jnp.minimum(tidx, total - 1)
    tile_e = jnp.minimum(jnp.sum(tclip[:, None] >= tile_end[None, :], axis=1),
                         N_EXPERTS - 1).astype(jnp.int32)
    rows_left = counts[tile_e] - (tclip - tile_start[tile_e]) * MOE_TILE
    rows_here = jnp.clip(rows_left, 0, MOE_TILE)
    tile_nsub = jnp.where(live, (rows_here + MOE_HALF - 1) // MOE_HALF, 0).astype(jnp.int32)

    row_start = tile_start * MOE_TILE
    eid = jnp.arange(N_EXPERTS, dtype=jnp.int32)[None, :]
    pos0 = jnp.sum(jnp.where(e0[:, None] == eid, row_start[None, :], 0), axis=1) + r0
    pos1 = jnp.sum(jnp.where(e1[:, None] == eid, row_start[None, :], 0), axis=1) + r1

    xs = _sc_scatter_rows2(hp.reshape(n, d // 2), pos0, pos1, ntiles * MOE_TILE)
    ys = _moe_ffn(xs, tile_e, tile_nsub, wg, wu, wd)
    y01 = _sc_gather_rows(ys, jnp.concatenate([pos0, pos1]))
    return _combine(x1, y01.reshape(2, b, s, d // 2), info, mod5, layer, tm)


def kernel(x, c, w_ada, b_ada, norm_mix_g, norm_ffn_g, w_in, w_out, conv_w, conv_b,
           conv_ln_g, conv_ln_b, q_norm_g, k_norm_g, rel_bias, ffn_w_gate, ffn_w_up,
           ffn_w_down, moe_w_router, moe_b_router, moe_w_gate, moe_w_up, moe_w_down):
    b, s, d = x.shape
    depth = w_ada.shape[0]
    tm = min(1024, s)
    mod5 = _ada_mod(c, w_ada, b_ada).reshape(depth, b, 6, 1, d)
    bias_t = _attn_bias_t(rel_bias)
    for l in range(depth):
        z, qt, kn, v = _mix_in(x, mod5, norm_mix_g[l], w_in, l,
                               q_norm_g[l], k_norm_g[l], tm)
        yc = _conv_branch(z, conv_w[l], conv_b[l], conv_ln_g[l], conv_ln_b[l], tm)
        ya = _attention(qt, kn, v, bias_t, l)
        i = l // 2
        if l % 2 == 0:
            x = _mix_out_dense_ffn(x, yc, ya, w_out, l, mod5, norm_ffn_g[l],
                                   ffn_w_gate[i], ffn_w_up[i], ffn_w_down[i],
                                   min(1024, s), 256)
        else:
            x1, hp, info, info_t, cnt = _mix_out_routed(
                x, yc, ya, w_out, l, mod5, norm_ffn_g[l],
                moe_w_router[i], moe_b_router[i], tm)
            x = _moe_layer(hp, info, info_t, cnt, x1, mod5, l,
                           moe_w_gate[i], moe_w_up[i], moe_w_down[i], tm)
    return x
```
